```python
import jax, jax.numpy as jnp
from jax import lax
import numpy as np

D_MODEL = 2048
BATCH = 4
SEQ = 2048
DEPTH = 4

N_A = DEPTH // 2
N_B = DEPTH - N_A
N_DENSE = (DEPTH + 1) // 2
N_MOE = DEPTH // 2

EPS = 1e-6
NEG = -1e30
FORCED_SCORE = 1e6
Q_BLOCK = 128

MLA_HEADS = 16
Q_LORA = 512
KV_LORA = 512
QK_NOPE = 128
QK_ROPE = 64
V_HEAD = 128
ROPE_THETA = 10000.0

NSA_HEADS = 16
NSA_GROUPS = 4
NSA_HPG = NSA_HEADS // NSA_GROUPS
NSA_DK = 128
NSA_DV = 128
CMP_BLOCK = 32
CMP_STRIDE = 16
CMP_HIDDEN = 128
SEL_BLOCK = 64
SEL_TOPN = 16
WINDOW = 512
SEL_Q_CHUNK = 32

D_FF = 5632
N_EXPERTS = 8
TOP_K = 2

kernel_name = 'yoco_mla_nsa_moe_adaln'


def _rmsnorm(x, g):
    xf = x.astype(jnp.float32)
    y = xf * lax.rsqrt(jnp.mean(xf * xf, axis=-1, keepdims=True) + EPS)
    return (y * g.astype(jnp.float32)).astype(x.dtype)


def _modulate(h, shift, scale):
    return h * (1 + scale[:, None, :]) + shift[:, None, :]


def _rope(x, pos):
    d = x.shape[-1]
    inv = ROPE_THETA ** (-jnp.arange(0, d, 2, dtype=jnp.float32) / d)
    ang = pos.astype(jnp.float32)[:, None] * inv[None, :]
    cos = jnp.cos(ang)[None, :, None, :].astype(x.dtype)
    sin = jnp.sin(ang)[None, :, None, :].astype(x.dtype)
    x1, x2 = x[..., : d // 2], x[..., d // 2:]
    return jnp.concatenate([x1 * cos - x2 * sin, x1 * sin + x2 * cos], axis=-1)


def _alibi_slopes(n):
    return jnp.asarray(2.0 ** (-8.0 * np.arange(1, n + 1) / n), jnp.float32)


def _cmp_to_sel_matrix(n_cmp, n_sel):
    tok = np.arange(n_cmp)[:, None] * CMP_STRIDE + np.arange(CMP_BLOCK)[None, :]
    blk = tok // SEL_BLOCK
    m = (blk[:, :, None] == np.arange(n_sel)[None, None, :]).sum(axis=1) / CMP_BLOCK
    return jnp.asarray(m, jnp.float32)


def _mla_mixer(h, w_in, g_q, g_kv, w_uq, w_ukv, w_o):
    B, S, _ = h.shape
    H = MLA_HEADS
    a = h @ w_in
    c_q = _rmsnorm(a[..., :Q_LORA], g_q)
    c_kv = _rmsnorm(a[..., Q_LORA:Q_LORA + KV_LORA], g_kv)
    pos = jnp.arange(S)
    k_pe = _rope(a[..., Q_LORA + KV_LORA:][:, :, None, :], pos)[:, :, 0, :]
    q = (c_q @ w_uq).reshape(B, S, H, QK_NOPE + QK_ROPE)
    q_nope = q[..., :QK_NOPE]
    q_pe = _rope(q[..., QK_NOPE:], pos)
    kv = (c_kv @ w_ukv).reshape(B, S, H, QK_NOPE + V_HEAD)
    k_nope = kv[..., :QK_NOPE]
    v = kv[..., QK_NOPE:]
    scale = (QK_NOPE + QK_ROPE) ** -0.5

    def block(i):
        t0 = i * Q_BLOCK
        qn = lax.dynamic_slice_in_dim(q_nope, t0, Q_BLOCK, axis=1)
        qp = lax.dynamic_slice_in_dim(q_pe, t0, Q_BLOCK, axis=1)
        s = (jnp.einsum('bqhd,bkhd->bhqk', qn, k_nope)
             + jnp.einsum('bqhr,bkr->bhqk', qp, k_pe)).astype(jnp.float32) * scale
        causal = pos[None, :] <= (t0 + jnp.arange(Q_BLOCK))[:, None]
        s = jnp.where(causal[None, None], s, NEG)
        p = jax.nn.softmax(s, axis=-1).astype(v.dtype)
        return jnp.einsum('bhqk,bkhd->bqhd', p, v)

    o = lax.map(block, jnp.arange(S // Q_BLOCK))
    o = jnp.moveaxis(o, 0, 1).reshape(B, S, H * V_HEAD)
    return o @ w_o


def _nsa_shared_kv(hkv, w_kv, pos_k, pos_v, w_k1, w_k2, w_v1, w_v2):
    B, S, _ = hkv.shape
    G, DK = NSA_GROUPS, NSA_DK
    kv = (hkv @ w_kv).reshape(B, S, 6, G, DK)
    k_c, v_c, k_s, v_s, k_w, v_w = [kv[:, :, i] for i in range(6)]
    n_cmp = (S - CMP_BLOCK) // CMP_STRIDE + 1
    idx = np.arange(n_cmp)[:, None] * CMP_STRIDE + np.arange(CMP_BLOCK)[None, :]

    def compress(t, pe, w1, w2):
        blk = t[:, idx] + pe[None, None, :, None, :]
        blk = jnp.moveaxis(blk, 3, 2).reshape(B, n_cmp, G, CMP_BLOCK * DK)
        return jax.nn.silu(blk @ w1) @ w2

    k_cmp = compress(k_c, pos_k, w_k1, w_k2)
    v_cmp = compress(v_c, pos_v, w_v1, w_v2)
    return (k_cmp, v_cmp, k_s, v_s, k_w, v_w)


def _nsa_mixer(h, w_in, w_o, k_cmp, v_cmp, k_slc, v_slc, k_win, v_win):
    B, S, _ = h.shape
    G, HP, DK = NSA_GROUPS, NSA_HPG, NSA_DK
    a = h @ w_in
    q = a[..., :NSA_HEADS * DK].reshape(B, S, G, HP, DK)
    gates = jax.nn.sigmoid(a[..., NSA_HEADS * DK:].astype(jnp.float32)).reshape(B, S, G, HP, 3)
    scale = DK ** -0.5
    slopes = _alibi_slopes(NSA_HEADS).reshape(G, HP)
    pos = jnp.arange(S)

    n_cmp = k_cmp.shape[1]
    cmp_end = jnp.arange(n_cmp) * CMP_STRIDE + (CMP_BLOCK - 1)
    dist = pos[:, None] - cmp_end[None, :]
    s = jnp.einsum('btghd,bngd->btghn', q, k_cmp).astype(jnp.float32) * scale
    s = s - slopes[None, None, :, :, None] * dist.astype(jnp.float32)[None, :, None, None, :]
    s = jnp.where((dist >= 0)[None, :, None, None, :], s, NEG)
    has_cmp = (pos >= CMP_BLOCK - 1).astype(jnp.float32)
    p_cmp = jax.nn.softmax(s, axis=-1) * has_cmp[None, :, None, None, None]
    o_cmp = jnp.einsum('btghn,bngd->btghd', p_cmp.astype(v_cmp.dtype), v_cmp)

    n_sel = S // SEL_BLOCK
    top_n = min(SEL_TOPN, n_sel)
    imp = jnp.einsum('btghn,nj->btgj', p_cmp, _cmp_to_sel_matrix(n_cmp, n_sel))
    blk_t = pos // SEL_BLOCK
    j = jnp.arange(n_sel)
    forced = (j[None, :] == 0) | (j[None, :] == blk_t[:, None]) | (j[None, :] == blk_t[:, None] - 1)
    future = j[None, :] > blk_t[:, None]
    imp = jnp.where(forced[None, :, None, :], FORCED_SCORE, imp)
    imp = jnp.where(future[None, :, None, :], -1.0, imp)
    _, sel_idx = lax.top_k(imp, top_n)

    k_blk = k_slc.reshape(B, n_sel, SEL_BLOCK, G, DK).transpose(0, 3, 1, 2, 4)
    v_blk = v_slc.reshape(B, n_sel, SEL_BLOCK, G, NSA_DV).transpose(0, 3, 1, 2, 4)
    b_ix = jnp.arange(B)[:, None, None, None]
    g_ix = jnp.arange(G)[None, None, :, None]
    off = jnp.arange(SEL_BLOCK)

    def sel_chunk(i):
        t0 = i * SEL_Q_CHUNK
        qc = lax.dynamic_slice_in_dim(q, t0, SEL_Q_CHUNK, axis=1)
        ic = lax.dynamic_slice_in_dim(sel_idx, t0, SEL_Q_CHUNK, axis=1)
        kg = k_blk[b_ix, g_ix, ic]
        vg = v_blk[b_ix, g_ix, ic]
        tq = t0 + jnp.arange(SEL_Q_CHUNK)
        d = tq[None, :, None, None, None] - (ic[..., None] * SEL_BLOCK + off)
        sc = jnp.einsum('bcghd,bcgnld->bcghnl', qc, kg).astype(jnp.float32) * scale
        sc = sc - slopes[None, None, :, :, None, None] * d.astype(jnp.float32)[:, :, :, None]
        sc = jnp.where((d >= 0)[:, :, :, None], sc, NEG)
        p = jax.nn.softmax(sc.reshape(B, SEL_Q_CHUNK, G, HP, top_n * SEL_BLOCK), axis=-1)
        p = p.reshape(B, SEL_Q_CHUNK, G, HP, top_n, SEL_BLOCK).astype(vg.dtype)
        return jnp.einsum('bcghnl,bcgnld->bcghd', p, vg)

    o_sel = lax.map(sel_chunk, jnp.arange(S // SEL_Q_CHUNK))
    o_sel = jnp.moveaxis(o_sel, 0, 1).reshape(B, S, G, HP, NSA_DV)

    k_pad = jnp.pad(k_win, ((0, 0), (WINDOW, 0), (0, 0), (0, 0)))
    v_pad = jnp.pad(v_win, ((0, 0), (WINDOW, 0), (0, 0), (0, 0)))
    span = Q_BLOCK + WINDOW

    def win_block(i):
        t0 = i * Q_BLOCK
        qb = lax.dynamic_slice_in_dim(q, t0, Q_BLOCK, axis=1)
        kb = lax.dynamic_slice_in_dim(k_pad, t0, span, axis=1)
        vb = lax.dynamic_slice_in_dim(v_pad, t0, span, axis=1)
        tq = t0 + jnp.arange(Q_BLOCK)
        ks = t0 - WINDOW + jnp.arange(span)
        d = tq[:, None] - ks[None, :]
        valid = (d >= 0) & (d < WINDOW) & (ks[None, :] >= 0)
        sc = jnp.einsum('bqghd,bkgd->bqghk', qb, kb).astype(jnp.float32) * scale
        sc = sc - slopes[None, None, :, :, None] * d.astype(jnp.float32)[None, :, None, None, :]
        sc = jnp.where(valid[None, :, None, None, :], sc, NEG)
        p = jax.nn.softmax(sc, axis=-1).astype(vb.dtype)
        return jnp.einsum('bqghk,bkgd->bqghd', p, vb)

    o_win = lax.map(win_block, jnp.arange(S // Q_BLOCK))
    o_win = jnp.moveaxis(o_win, 0, 1).reshape(B, S, G, HP, NSA_DV)

    g = gates.astype(o_cmp.dtype)
    o = g[..., 0:1] * o_cmp + g[..., 1:2] * o_sel + g[..., 2:3] * o_win
    return o.reshape(B, S, NSA_HEADS * NSA_DV) @ w_o


def _swiglu(h, w_gate, w_up, w_down):
    return (jax.nn.silu(h @ w_gate) * (h @ w_up)) @ w_down


def _moe(h, w_router, b_router, w_gate, w_up, w_down):
    logits = (h @ w_router + b_router).astype(jnp.float32)
    top_v, top_i = lax.top_k(logits, TOP_K)
    wts = jax.nn.softmax(top_v, axis=-1)
    gate = jnp.sum(jax.nn.one_hot(top_i, N_EXPERTS, dtype=jnp.float32) * wts[..., None], axis=-2)
    gate = gate.astype(h.dtype)
    out = jnp.zeros_like(h)
    for e in range(N_EXPERTS):
        out = out + gate[..., e:e + 1] * _swiglu(h, w_gate[e], w_up[e], w_down[e])
    return out


def setup_inputs(seed: int = 0) -> dict:
    key = jax.random.key(seed)
    ks = jax.random.split(key, 33)
    f32 = jnp.float32

    def nrm(k, shape, scale):
        return jax.random.normal(k, shape, f32) * scale

    D = D_MODEL
    G, DK = NSA_GROUPS, NSA_DK
    return {
        'x': nrm(ks[0], (BATCH, SEQ, D), 1.0),
        'c': nrm(ks[1], (BATCH, D), 1.0),
        'ada_w': nrm(ks[2], (DEPTH, D, 6 * D), 0.5 * D ** -0.5),
        'ada_b': nrm(ks[3], (DEPTH, 6 * D), 0.01),
        'norm1_g': 1.0 + nrm(ks[4], (DEPTH, D), 0.01),
        'norm2_g': 1.0 + nrm(ks[5], (DEPTH, D), 0.01),
        'mla_w_in': nrm(ks[6], (N_A, D, Q_LORA + KV_LORA + QK_ROPE), D ** -0.5),
        'mla_g_q': 1.0 + nrm(ks[7], (N_A, Q_LORA), 0.01),
        'mla_g_kv': 1.0 + nrm(ks[8], (N_A, KV_LORA), 0.01),
        'mla_w_uq': nrm(ks[9], (N_A, Q_LORA, MLA_HEADS * (QK_NOPE + QK_ROPE)), Q_LORA ** -0.5),
        'mla_w_ukv': nrm(ks[10], (N_A, KV_LORA, MLA_HEADS * (QK_NOPE + V_HEAD)), KV_LORA ** -0.5),
        'mla_w_o': nrm(ks[11], (N_A, MLA_HEADS * V_HEAD, D), (MLA_HEADS * V_HEAD) ** -0.5),
        'kv_ada_w': nrm(ks[12], (D, 2 * D), 0.5 * D ** -0.5),
        'kv_ada_b': nrm(ks[13], (2 * D,), 0.01),
        'kv_norm_g': 1.0 + nrm(ks[14], (D,), 0.01),
        'nsa_w_kv': nrm(ks[15], (D, 6 * G * DK), D ** -0.5),
        'cmp_pos_k': nrm(ks[16], (CMP_BLOCK, DK), 0.5),
        'cmp_pos_v': nrm(ks[17], (CMP_BLOCK, DK), 0.5),
        'cmp_k_w1': nrm(ks[18], (CMP_BLOCK * DK, CMP_HIDDEN), (CMP_BLOCK * DK) ** -0.5),
        'cmp_k_w2': nrm(ks[19], (CMP_HIDDEN, DK), CMP_HIDDEN ** -0.5),
        'cmp_v_w1': nrm(ks[20], (CMP_BLOCK * DK, CMP_HIDDEN), (CMP_BLOCK * DK) ** -0.5),
        'cmp_v_w2': nrm(ks[21], (CMP_HIDDEN, NSA_DV), CMP_HIDDEN ** -0.5),
        'nsa_w_in': nrm(ks[22], (N_B, D, NSA_HEADS * DK + 3 * NSA_HEADS), D ** -0.5),
        'nsa_w_o': nrm(ks[23], (N_B, NSA_HEADS * NSA_DV, D), (NSA_HEADS * NSA_DV) ** -0.5),
        'ffn_w_gate': nrm(ks[24], (N_DENSE, D, D_FF), D ** -0.5),
        'ffn_w_up': nrm(ks[25], (N_DENSE, D, D_FF), D ** -0.5),
        'ffn_w_down': nrm(ks[26], (N_DENSE, D_FF, D), D_FF ** -0.5),
        'moe_w_router': nrm(ks[27], (N_MOE, D, N_EXPERTS), D ** -0.5),
        'moe_b_router': nrm(ks[28], (N_MOE, N_EXPERTS), 0.01),
        'moe_w_gate': nrm(ks[29], (N_MOE, N_EXPERTS, D, D_FF), D ** -0.5),
        'moe_w_up': nrm(ks[30], (N_MOE, N_EXPERTS, D, D_FF), D ** -0.5),
        'moe_w_down': nrm(ks[31], (N_MOE, N_EXPERTS, D_FF, D), D_FF ** -0.5),
        'final_g': 1.0 + nrm(ks[32], (D,), 0.01),
    }


def reference(x, c, ada_w, ada_b, norm1_g, norm2_g, mla_w_in, mla_g_q, mla_g_kv, mla_w_uq,
              mla_w_ukv, mla_w_o, kv_ada_w, kv_ada_b, kv_norm_g, nsa_w_kv, cmp_pos_k, cmp_pos_v,
              cmp_k_w1, cmp_k_w2, cmp_v_w1, cmp_v_w2, nsa_w_in, nsa_w_o, ffn_w_gate, ffn_w_up,
              ffn_w_down, moe_w_router, moe_b_router, moe_w_gate, moe_w_up, moe_w_down, final_g):
    c_act = jax.nn.silu(c)
    shared = None
    for l in range(DEPTH):
        mod = c_act @ ada_w[l] + ada_b[l]
        sh1, sc1, g1, sh2, sc2, g2 = jnp.split(mod, 6, axis=-1)
        h = _modulate(_rmsnorm(x, norm1_g[l]), sh1, sc1)
        if l < N_A:
            y = _mla_mixer(h, mla_w_in[l], mla_g_q[l], mla_g_kv[l], mla_w_uq[l], mla_w_ukv[l], mla_w_o[l])
        else:
            jb = l - N_A
            y = _nsa_mixer(h, nsa_w_in[jb], nsa_w_o[jb], *shared)
        x = x + g1[:, None, :] * y
        h = _modulate(_rmsnorm(x, norm2_g[l]), sh2, sc2)
        if l % 2 == 0:
            y = _swiglu(h, ffn_w_gate[l // 2], ffn_w_up[l // 2], ffn_w_down[l // 2])
        else:
            y = _moe(h, moe_w_router[l // 2], moe_b_router[l // 2], moe_w_gate[l // 2],
                     moe_w_up[l // 2], moe_w_down[l // 2])
        x = x + g2[:, None, :] * y
        if l == N_A - 1:
            shk, sck = jnp.split(c_act @ kv_ada_w + kv_ada_b, 2, axis=-1)
            hkv = _modulate(_rmsnorm(x, kv_norm_g), shk, sck)
            shared = _nsa_shared_kv(hkv, nsa_w_kv, cmp_pos_k, cmp_pos_v,
                                    cmp_k_w1, cmp_k_w2, cmp_v_w1, cmp_v_w2)
    return _rmsnorm(x, final_g)
```

```python
import functools

import numpy as np
import jax
import jax.numpy as jnp
from jax import lax
from jax.experimental import pallas as pl
from jax.experimental.pallas import tpu as pltpu

F32 = jnp.float32
BF16 = jnp.bfloat16

EPS = 1e-6
NEG = -1e30
FORCED_SCORE = 1e6

MLA_HEADS = 16
QK_NOPE = 128
QK_ROPE = 64
V_HEAD = 128
ROPE_THETA = 10000.0

NSA_HEADS = 16
NSA_GROUPS = 4
NSA_HPG = NSA_HEADS // NSA_GROUPS
NSA_DK = 128
CMP_BLOCK = 32
CMP_STRIDE = 16
SEL_BLOCK = 64
SEL_TOPN = 16
WINDOW = 512

N_EXPERTS = 8
LANES = 128
VMEM_LIMIT = 56 * 1024 * 1024

MOE_TM = 512


def _cparams(n_axes):
    return pltpu.CompilerParams(
        dimension_semantics=("arbitrary",) * n_axes, vmem_limit_bytes=VMEM_LIMIT)


def _mod_kernel(c_ref, w_ref, b_ref, o_ref):
    c = c_ref[...]
    ca = (c * jax.nn.sigmoid(c)).astype(BF16)
    o_ref[...] = jnp.dot(ca, w_ref[...].astype(BF16), preferred_element_type=F32) + b_ref[...]


def _modulation(c_pad, w, b, tn=1024):
    L, D, N = w.shape
    return pl.pallas_call(
        _mod_kernel,
        grid=(L, N // tn),
        in_specs=[
            pl.BlockSpec((8, D), lambda l, n: (0, 0)),
            pl.BlockSpec((None, D, tn), lambda l, n: (l, 0, n)),
            pl.BlockSpec((None, 1, tn), lambda l, n: (l, 0, n)),
        ],
        out_specs=pl.BlockSpec((None, 8, tn), lambda l, n: (l, 0, n)),
        out_shape=jax.ShapeDtypeStruct((L, 8, N), F32),
        compiler_params=_cparams(2),
        name="adaln_mod",
    )(c_pad, w, b.reshape(L, 1, N))


def _rms(x, g):
    return x * lax.rsqrt(jnp.mean(x * x, axis=-1, keepdims=True) + EPS) * g


def _norm_kernel(x_ref, g_ref, *refs, modulate, route):
    if modulate:
        sh_ref, sc_ref, *refs = refs
    if route:
        wr_ref, br_ref, *refs = refs
    h = _rms(x_ref[...], g_ref[...])
    if modulate:
        h = h * (1.0 + sc_ref[...]) + sh_ref[...]
    refs[0][...] = h.astype(refs[0].dtype)
    if route:
        logits = jnp.dot(h, wr_ref[...], preferred_element_type=F32,
                         precision=lax.Precision.HIGHEST) + br_ref[...]
        lane = lax.broadcasted_iota(jnp.int32, logits.shape, 1).astype(F32)
        logits = jnp.where(lane < N_EXPERTS, logits, -jnp.inf)
        v1 = jnp.max(logits, axis=-1, keepdims=True)
        i1 = jnp.min(jnp.where(logits == v1, lane, float(LANES)), axis=-1, keepdims=True)
        rest = jnp.where(lane == i1, -jnp.inf, logits)
        v2 = jnp.max(rest, axis=-1, keepdims=True)
        i2 = jnp.min(jnp.where(rest == v2, lane, float(LANES)), axis=-1, keepdims=True)
        e = jnp.exp(v2 - v1)
        w1 = 1.0 / (1.0 + e)
        w2 = e / (1.0 + e)
        refs[1][...] = jnp.where(lane == 0, i1, jnp.where(lane == 1, i2, jnp.where(
            lane == 2, w1, jnp.where(lane == 3, w2, 0.0))))


def _norm(x2, g, B, S, shift=None, scale=None, router=None, out_dtype=BF16, ts=512):
    T, D = x2.shape
    nst = S // ts
    modulate = shift is not None
    route = router is not None
    args = [x2, g.reshape(1, D)]
    in_specs = [
        pl.BlockSpec((ts, D), lambda b, s: (b * nst + s, 0)),
        pl.BlockSpec((1, D), lambda b, s: (0, 0)),
    ]
    if modulate:
        for arr, l, j in (shift, scale):
            args.append(arr)
            in_specs.append(pl.BlockSpec((None, None, None, 1, D),
                                         lambda b, s, l=l, j=j: (l, b, j, 0, 0)))
    out_shape = [jax.ShapeDtypeStruct((T, D), out_dtype)]
    out_specs = [pl.BlockSpec((ts, D), lambda b, s: (b * nst + s, 0))]
    if route:
        wr, br = router
        args += [wr, br]
        in_specs += [pl.BlockSpec((D, LANES), lambda b, s: (0, 0)),
                     pl.BlockSpec((1, LANES), lambda b, s: (0, 0))]
        out_shape.append(jax.ShapeDtypeStruct((T, LANES), F32))
        out_specs.append(pl.BlockSpec((ts, LANES), lambda b, s: (b * nst + s, 0)))
    outs = pl.pallas_call(
        functools.partial(_norm_kernel, modulate=modulate, route=route),
        grid=(B, nst),
        in_specs=in_specs,
        out_specs=out_specs,
        out_shape=out_shape,
        compiler_params=_cparams(2),
        name="norm_mod",
    )(*args)
    return outs if route else outs[0]


def _gmm_kernel(te_ref, tv_ref, a_ref, *refs, mode):
    m = pl.program_id(1)
    first = jnp.logical_or(m == 0, te_ref[m] != te_ref[jnp.maximum(m - 1, 0)])
    valid = tv_ref[m] != 0
    if mode == "swiglu":
        wg_ref, wu_ref, o_ref, wgb, wub = refs

        @pl.when(first)
        def _():
            wgb[...] = wg_ref[...].astype(BF16)
            wub[...] = wu_ref[...].astype(BF16)

        @pl.when(valid)
        def _():
            a = a_ref[...]
            g = jnp.dot(a, wgb[...], preferred_element_type=F32)
            u = jnp.dot(a, wub[...], preferred_element_type=F32)
            o_ref[...] = (g * jax.nn.sigmoid(g) * u).astype(o_ref.dtype)
    else:
        if mode == "cast":
            w_ref, o_ref, wb = refs
        elif mode == "residual":
            w_ref, x_ref, gate_ref, o_ref, wb = refs
        else:
            w_ref, rs_ref, o_ref, wb = refs

        @pl.when(first)
        def _():
            wb[...] = w_ref[...].astype(BF16)

        @pl.when(valid)
        def _():
            acc = jnp.dot(a_ref[...], wb[...], preferred_element_type=F32)
            if mode == "residual":
                acc = x_ref[...] + gate_ref[...] * acc
            elif mode == "rowscale":
                acc = rs_ref[...] * acc
            o_ref[...] = acc.astype(o_ref.dtype)

    @pl.when(jnp.logical_not(valid))
    def _():
        o_ref[...] = jnp.zeros_like(o_ref)


def _gmm(a, ws, l, *, mode, tm, tn, n_out, out_dtype, tile_expert=None, tile_valid=None,
         n_off=0, xres=None, gate=None, seq=None, rowscale=None):
    M, K = a.shape
    mt = M // tm
    if tile_expert is None:
        tile_expert = jnp.zeros((mt,), jnp.int32)
        tile_valid = jnp.ones((mt,), jnp.int32)
    w_spec = pl.BlockSpec((None, None, K, tn), lambda n, m, te, tv: (l, te[m], 0, n + n_off))
    args = [a] + list(ws)
    in_specs = [pl.BlockSpec((tm, K), lambda n, m, te, tv: (m, 0))] + [w_spec] * len(ws)
    aliases = {}
    if mode == "residual":
        garr, gl, gj = gate
        per_b = seq // tm
        aliases = {2 + len(args): 0}
        args += [xres, garr]
        in_specs += [
            pl.BlockSpec((tm, tn), lambda n, m, te, tv: (m, n)),
            pl.BlockSpec((None, None, None, 1, tn),
                         lambda n, m, te, tv: (gl, m // per_b, gj, 0, n)),
        ]
    elif mode == "rowscale":
        args.append(rowscale)
        in_specs.append(pl.BlockSpec((tm, 1), lambda n, m, te, tv: (m, 0)))
    return pl.pallas_call(
        functools.partial(_gmm_kernel, mode=mode),
        grid_spec=pltpu.PrefetchScalarGridSpec(
            num_scalar_prefetch=2,
            grid=(n_out // tn, mt),
            in_specs=in_specs,
            out_specs=pl.BlockSpec((tm, tn), lambda n, m, te, tv: (m, n)),
            scratch_shapes=[pltpu.VMEM((K, tn), BF16) for _ in ws],
        ),
        out_shape=jax.ShapeDtypeStruct((M, n_out), out_dtype),
        input_output_aliases=aliases,
        compiler_params=_cparams(2),
        name="gmm_" + mode,
    )(tile_expert, tile_valid, *args)


def _rope_pairs(v, cos, s1, s2):
    return v * cos + pltpu.roll(v, LANES - QK_ROPE // 2, 1) * s1 + pltpu.roll(v, QK_ROPE // 2, 1) * s2


def _mla_in_kernel(a_ref, w_ref, gq_ref, gkv_ref, cos_ref, s1_ref, s2_ref,
                   cq_ref, ckv_ref, kpe_ref, wb, *, q_lora, kv_lora):
    @pl.when(pl.program_id(0) == 0)
    def _():
        wb[...] = w_ref[...].astype(BF16)

    acc = jnp.dot(a_ref[...], wb[...], preferred_element_type=F32)
    cq_ref[...] = _rms(acc[:, :q_lora], gq_ref[...]).astype(BF16)
    ckv_ref[...] = _rms(acc[:, q_lora:q_lora + kv_lora], gkv_ref[...]).astype(BF16)
    v = acc[:, q_lora + kv_lora:]
    kpe_ref[...] = _rope_pairs(v, cos_ref[...], s1_ref[...], s2_ref[...]).astype(BF16)


def _mla_in(h, w_pad, g_q, g_kv, rope_tabs, S, tm=512):
    T, D = h.shape
    q_lora, kv_lora = g_q.shape[0], g_kv.shape[0]
    n_all = w_pad.shape[1]
    nst = S // tm
    tab_spec = pl.BlockSpec((tm, LANES), lambda i: (i % nst, 0))
    return pl.pallas_call(
        functools.partial(_mla_in_kernel, q_lora=q_lora, kv_lora=kv_lora),
        grid=(T // tm,),
        in_specs=[
            pl.BlockSpec((tm, D), lambda i: (i, 0)),
            pl.BlockSpec((D, n_all), lambda i: (0, 0)),
            pl.BlockSpec((1, q_lora), lambda i: (0, 0)),
            pl.BlockSpec((1, kv_lora), lambda i: (0, 0)),
            tab_spec, tab_spec, tab_spec,
        ],
        out_specs=[
            pl.BlockSpec((tm, q_lora), lambda i: (i, 0)),
            pl.BlockSpec((tm, kv_lora), lambda i: (i, 0)),
            pl.BlockSpec((tm, LANES), lambda i: (i, 0)),
        ],
        out_shape=[
            jax.ShapeDtypeStruct((T, q_lora), BF16),
            jax.ShapeDtypeStruct((T, kv_lora), BF16),
            jax.ShapeDtypeStruct((T, LANES), BF16),
        ],
        scratch_shapes=[pltpu.VMEM((D, n_all), BF16)],
        compiler_params=_cparams(1),
        name="mla_in",
    )(h, w_pad, g_q.reshape(1, -1), g_kv.reshape(1, -1), *rope_tabs)


def _mla_attn_kernel(qn_ref, qp_ref, kn_ref, v_ref, kpe_ref, cos_ref, s1_ref, s2_ref, o_ref,
                     m_sc, l_sc, acc_sc, *, tq, scale):
    qi = pl.program_id(2)
    qp = _rope_pairs(qp_ref[...].astype(F32), cos_ref[...], s1_ref[...], s2_ref[...])
    lane = lax.broadcasted_iota(jnp.int32, qp.shape, 1)
    row = lax.broadcasted_iota(jnp.int32, (tq, tq), 0)
    col = lax.broadcasted_iota(jnp.int32, (tq, tq), 1)
    for hh in range(2):
        keep = (lane < QK_ROPE) if hh == 0 else (lane >= QK_ROPE)
        q = jnp.concatenate(
            [qn_ref[:, hh * QK_NOPE:(hh + 1) * QK_NOPE], jnp.where(keep, qp, 0.0).astype(BF16)],
            axis=1)
        m_sc[...] = jnp.full_like(m_sc, NEG)
        l_sc[...] = jnp.zeros_like(l_sc)
        acc_sc[...] = jnp.zeros_like(acc_sc)

        def step(j, diagonal, q=q, hh=hh):
            ks = pl.multiple_of(j * tq, tq)
            k = jnp.concatenate(
                [kn_ref[pl.ds(ks, tq), hh * QK_NOPE:(hh + 1) * QK_NOPE], kpe_ref[pl.ds(ks, tq), :]],
                axis=1)
            s = lax.dot_general(q, k, (((1,), (1,)), ((), ())),
                                preferred_element_type=F32) * scale
            if diagonal:
                s = jnp.where(col <= row, s, NEG)
            m_old = m_sc[...]
            m_new = jnp.maximum(m_old, jnp.max(s, axis=-1, keepdims=True))
            alpha = jnp.exp(m_old - m_new)
            p = jnp.exp(s - m_new)
            l_sc[...] = alpha * l_sc[...] + jnp.sum(p, axis=-1, keepdims=True)
            vv = v_ref[pl.ds(ks, tq), hh * V_HEAD:(hh + 1) * V_HEAD]
            acc_sc[...] = alpha * acc_sc[...] + jnp.dot(p.astype(BF16), vv,
                                                        preferred_element_type=F32)
            m_sc[...] = m_new

        def body(j, carry):
            step(j, False)
            return carry

        lax.fori_loop(0, qi, body, 0)
        step(qi, True)
        o_ref[:, hh * V_HEAD:(hh + 1) * V_HEAD] = (acc_sc[...] / l_sc[...]).astype(o_ref.dtype)


def _mla_attn(q_all, kv_all, kpe, rope_tabs, B, S, tq=512):
    T = q_all.shape[0]
    H = MLA_HEADS
    nqt = S // tq
    pair_w = 2 * QK_NOPE
    n_pairs = H // 2
    tab_spec = pl.BlockSpec((tq, LANES), lambda b, p, i: (i, 0))
    return pl.pallas_call(
        functools.partial(_mla_attn_kernel, tq=tq, scale=(QK_NOPE + QK_ROPE) ** -0.5),
        grid=(B, n_pairs, nqt),
        in_specs=[
            pl.BlockSpec((tq, pair_w), lambda b, p, i: (b * nqt + i, p)),
            pl.BlockSpec((tq, LANES), lambda b, p, i: (b * nqt + i, (H * QK_NOPE) // LANES + p)),
            pl.BlockSpec((S, pair_w), lambda b, p, i: (b, p)),
            pl.BlockSpec((S, pair_w), lambda b, p, i: (b, n_pairs + p)),
            pl.BlockSpec((S, LANES), lambda b, p, i: (b, 0)),
            tab_spec, tab_spec, tab_spec,
        ],
        out_specs=pl.BlockSpec((tq, pair_w), lambda b, p, i: (b * nqt + i, p)),
        out_shape=jax.ShapeDtypeStruct((T, H * V_HEAD), BF16),
        scratch_shapes=[pltpu.VMEM((tq, 1), F32), pltpu.VMEM((tq, 1), F32),
                        pltpu.VMEM((tq, V_HEAD), F32)],
        compiler_params=_cparams(3),
        name="mla_attn",
    )(q_all, q_all, kv_all, kv_all, kpe, *rope_tabs)


def _nsa_compress_kernel(t_ref, pe_ref, w1_ref, w2_ref, o_ref, *, n_chunk):
    half = CMP_BLOCK // 2
    pe = pe_ref[...]

    def chunk_rows(off):
        cols = []
        for l in range(half):
            x = t_ref[pl.ds(l, n_chunk, stride=CMP_STRIDE), :] + pe[off + l:off + l + 1, :]
            cols.append(x.astype(BF16))
        return jnp.concatenate(cols, axis=1)

    w1 = w1_ref[...].astype(BF16)
    kdim = half * NSA_DK
    p0 = jnp.dot(chunk_rows(0), w1[:kdim], preferred_element_type=F32)
    p1 = jnp.dot(chunk_rows(half), w1[kdim:], preferred_element_type=F32)
    pre = p0 + pltpu.roll(p1, n_chunk - 1, 0)
    hid = (pre * jax.nn.sigmoid(pre)).astype(BF16)
    out = jnp.dot(hid, w2_ref[...].astype(BF16), preferred_element_type=F32)
    rown = lax.broadcasted_iota(jnp.int32, out.shape, 0)
    o_ref[...] = jnp.where(rown < n_chunk - 1, out, 0.0).astype(o_ref.dtype)


def _nsa_compress(kc, pe, w1, w2, B, S):
    G = NSA_GROUPS
    n_chunk = S // CMP_STRIDE
    return pl.pallas_call(
        functools.partial(_nsa_compress_kernel, n_chunk=n_chunk),
        grid=(B, 2, G),
        in_specs=[
            pl.BlockSpec((S, NSA_DK), lambda b, i, g: (b, i * G + g)),
            pl.BlockSpec((None, CMP_BLOCK, NSA_DK), lambda b, i, g: (i, 0, 0)),
            pl.BlockSpec((None, CMP_BLOCK * NSA_DK, NSA_DK), lambda b, i, g: (i, 0, 0)),
            pl.BlockSpec((None, NSA_DK, NSA_DK), lambda b, i, g: (i, 0, 0)),
        ],
        out_specs=pl.BlockSpec((None, None, None, n_chunk, NSA_DK), lambda b, i, g: (b, i, g, 0, 0)),
        out_shape=jax.ShapeDtypeStruct((B, 2, G, n_chunk, NSA_DK), BF16),
        compiler_params=_cparams(3),
        name="nsa_compress",
    )(kc, pe, w1, w2)


def _nsa_attn_kernel(q_ref, gt_ref, ks_ref, vs_ref, kw_ref, vw_ref, kc_ref, vc_ref, slope_ref,
                     mt_ref, e_ref, o_ref, m_sc, l_sc, acc_sc, *, tq, tk, n_cmp, n_sel):
    qi = pl.program_id(2)
    t0 = qi * tq
    HP = NSA_HPG
    R = HP * tq
    scale = NSA_DK ** -0.5
    qb = q_ref[...]
    qs = jnp.concatenate([qb[:, h * NSA_DK:(h + 1) * NSA_DK] for h in range(HP)], axis=0)
    slope = slope_ref[...]
    rowi = lax.broadcasted_iota(jnp.int32, (R, 1), 0)
    tcol = (t0 + rowi % tq).astype(F32)

    def dot_nt(a, b):
        return lax.dot_general(a, b, (((1,), (1,)), ((), ())), preferred_element_type=F32)

    ncol = lax.broadcasted_iota(jnp.int32, (1, kc_ref.shape[0]), 1)
    dist = tcol - (ncol * CMP_STRIDE + (CMP_BLOCK - 1)).astype(F32)
    valid = jnp.logical_and(dist >= 0, ncol < n_cmp)
    s = jnp.where(valid, dot_nt(qs, kc_ref[...]) * scale - slope * dist, NEG)
    p = jnp.where(valid, jnp.exp(s - jnp.max(s, axis=-1, keepdims=True)), 0.0)
    l = jnp.sum(p, axis=-1, keepdims=True)
    p_cmp = jnp.where(tcol >= CMP_BLOCK - 1, p / l, 0.0)
    o_cmp = jnp.dot(p_cmp.astype(BF16), vc_ref[...], preferred_element_type=F32)

    ps = p_cmp[0:tq]
    for h in range(1, HP):
        ps = ps + p_cmp[h * tq:(h + 1) * tq]
    ps_hi = ps.astype(BF16)
    ps_lo = (ps - ps_hi.astype(F32)).astype(BF16)
    imp = dot_nt(mt_ref[...], ps_hi) + dot_nt(mt_ref[...], ps_lo)
    jrow = lax.broadcasted_iota(jnp.int32, (n_sel, tq), 0)
    blk_t = (t0 + lax.broadcasted_iota(jnp.int32, (n_sel, tq), 1)) // SEL_BLOCK
    forced = (jrow == 0) | (jrow == blk_t) | (jrow == blk_t - 1)
    imp = jnp.where(forced, FORCED_SCORE, imp)
    imp = jnp.where(jrow > blk_t, -1.0, imp)
    rank = jnp.zeros((n_sel, tq), F32)
    for k in range(n_sel):
        rk = imp[k:k + 1, :]
        beats = (rk > imp) | ((rk == imp) & (jrow > k))
        rank = rank + jnp.where(beats, 1.0, 0.0)
    sel_t = jnp.where(rank < SEL_TOPN, 1.0, 0.0)
    sel_t = jnp.concatenate([sel_t, jnp.zeros((LANES - n_sel, tq), F32)], axis=0)
    sel = sel_t.T.astype(BF16)

    m_sc[...] = jnp.full_like(m_sc, NEG)
    l_sc[...] = jnp.zeros_like(l_sc)
    acc_sc[...] = jnp.zeros_like(acc_sc)

    def body(j, carry):
        ks = pl.multiple_of(j * tk, tk)
        kpos = (ks + lax.broadcasted_iota(jnp.int32, (1, tk), 1)).astype(F32)
        d = tcol - kpos
        mk = jnp.dot(sel, e_ref[j], preferred_element_type=F32)
        mk = jnp.concatenate([mk] * HP, axis=0)
        ok = jnp.logical_and(d >= 0, mk > 0.5)
        sc = jnp.where(ok, dot_nt(qs, ks_ref[pl.ds(ks, tk), :]) * scale - slope * d, NEG)
        m_old = m_sc[...]
        m_new = jnp.maximum(m_old, jnp.max(sc, axis=-1, keepdims=True))
        alpha = jnp.exp(m_old - m_new)
        pp = jnp.where(ok, jnp.exp(sc - m_new), 0.0)
        l_sc[...] = alpha * l_sc[...] + jnp.sum(pp, axis=-1, keepdims=True)
        acc_sc[...] = alpha * acc_sc[...] + jnp.dot(
            pp.astype(BF16), vs_ref[pl.ds(ks, tk), :], preferred_element_type=F32)
        m_sc[...] = m_new
        return carry

    lax.fori_loop(0, (t0 + tq + tk - 1) // tk, body, 0)
    o_sel = acc_sc[...] / l_sc[...]

    span = WINDOW + tq
    ws = pl.multiple_of(jnp.maximum(t0 - WINDOW, 0), tq)
    kpos = (ws + lax.broadcasted_iota(jnp.int32, (1, span), 1)).astype(F32)
    d = tcol - kpos
    ok = jnp.logical_and(d >= 0, d < WINDOW)
    sw = jnp.where(ok, dot_nt(qs, kw_ref[pl.ds(ws, span), :]) * scale - slope * d, NEG)
    pw = jnp.where(ok, jnp.exp(sw - jnp.max(sw, axis=-1, keepdims=True)), 0.0)
    lw = jnp.sum(pw, axis=-1, keepdims=True)
    o_win = jnp.dot(pw.astype(BF16), vw_ref[pl.ds(ws, span), :], preferred_element_type=F32) / lw

    gt = jax.nn.sigmoid(gt_ref[...])

    def gcol(i):
        return jnp.concatenate([gt[:, i * HP + h:i * HP + h + 1] for h in range(HP)], axis=0)

    o = gcol(0) * o_cmp + gcol(1) * o_sel + gcol(2) * o_win
    o_ref[...] = jnp.concatenate([o[h * tq:(h + 1) * tq] for h in range(HP)], axis=1).astype(o_ref.dtype)


def _nsa_attn(q, gates, kvb, kvc, slopes, mt, e3, B, S, tq=128, tk=256):
    T = q.shape[0]
    G, DK, HP = NSA_GROUPS, NSA_DK, NSA_HPG
    nqt = S // tq
    n_chunk = kvc.shape[3]
    n_cmp = (S - CMP_BLOCK) // CMP_STRIDE + 1
    n_sel = S // SEL_BLOCK

    def kv_spec(i):
        return pl.BlockSpec((S, DK), lambda b, g, t, i=i: (b, i * G + g))

    def cmp_spec(i):
        return pl.BlockSpec((None, None, None, n_chunk, DK), lambda b, g, t, i=i: (b, i, g, 0, 0))

    return pl.pallas_call(
        functools.partial(_nsa_attn_kernel, tq=tq, tk=tk, n_cmp=n_cmp, n_sel=n_sel),
        grid=(B, G, nqt),
        in_specs=[
            pl.BlockSpec((tq, HP * DK), lambda b, g, t: (b * nqt + t, g)),
            pl.BlockSpec((tq, LANES), lambda b, g, t: (b * nqt + t, g)),
            kv_spec(0), kv_spec(1), kv_spec(2), kv_spec(3),
            cmp_spec(0), cmp_spec(1),
            pl.BlockSpec((None, HP * tq, 1), lambda b, g, t: (g, 0, 0)),
            pl.BlockSpec(mt.shape, lambda b, g, t: (0, 0)),
            pl.BlockSpec(e3.shape, lambda b, g, t: (0, 0, 0)),
        ],
        out_specs=pl.BlockSpec((tq, HP * DK), lambda b, g, t: (b * nqt + t, g)),
        out_shape=jax.ShapeDtypeStruct((T, NSA_HEADS * DK), BF16),
        scratch_shapes=[pltpu.VMEM((HP * tq, 1), F32), pltpu.VMEM((HP * tq, 1), F32),
                        pltpu.VMEM((HP * tq, DK), F32)],
        compiler_params=_cparams(3),
        name="nsa_attn",
    )(q, gates, kvb, kvb, kvb, kvb, kvc, kvc, slopes, mt, e3)


def _row_copy(src_hbm, dst_vmem, sem, src_row, dst_row):
    return pltpu.make_async_copy(src_hbm.at[pl.ds(src_row, 1), :], dst_vmem.at[pl.ds(dst_row, 1), :], sem)


def _dispatch_kernel(idx_ref, src_hbm, o_ref, buf, sem, *, tg):
    base = pl.program_id(0) * tg

    def start(r, c):
        _row_copy(src_hbm, buf, sem, idx_ref[base + r], r).start()
        return c

    def wait(r, c):
        _row_copy(src_hbm, buf, sem, 0, r).wait()
        return c

    lax.fori_loop(0, tg, start, 0)
    lax.fori_loop(0, tg, wait, 0)
    o_ref[...] = buf[...].astype(o_ref.dtype)


def _dispatch(h32, src_token, tg=256):
    T, D = h32.shape
    P = src_token.shape[0]
    return pl.pallas_call(
        functools.partial(_dispatch_kernel, tg=tg),
        grid_spec=pltpu.PrefetchScalarGridSpec(
            num_scalar_prefetch=1,
            grid=(P // tg,),
            in_specs=[pl.BlockSpec(memory_space=pl.ANY)],
            out_specs=pl.BlockSpec((tg, D), lambda i, idx: (i, 0)),
            scratch_shapes=[pltpu.VMEM((tg, D), F32), pltpu.SemaphoreType.DMA(())],
        ),
        out_shape=jax.ShapeDtypeStruct((P, D), BF16),
        compiler_params=_cparams(1),
        name="moe_dispatch",
    )(src_token, h32)


def _combine_kernel(pos_ref, y_hbm, x_ref, gate_ref, o_ref, buf0, buf1, sem, *, tg, n_tok):
    base = pl.program_id(0) * tg

    def start(r, c):
        _row_copy(y_hbm, buf0, sem, pos_ref[base + r], r).start()
        _row_copy(y_hbm, buf1, sem, pos_ref[n_tok + base + r], r).start()
        return c

    def wait(r, c):
        _row_copy(y_hbm, buf0, sem, 0, r).wait()
        _row_copy(y_hbm, buf1, sem, 0, r).wait()
        return c

    lax.fori_loop(0, tg, start, 0)
    lax.fori_loop(0, tg, wait, 0)
    o_ref[...] = x_ref[...] + gate_ref[...] * (buf0[...] + buf1[...])


def _combine(x2, y_sorted, pos, gate, S, tg=256):
    T, D = x2.shape
    garr, gl, gj = gate
    per_b = S // tg
    return pl.pallas_call(
        functools.partial(_combine_kernel, tg=tg, n_tok=T),
        grid_spec=pltpu.PrefetchScalarGridSpec(
            num_scalar_prefetch=1,
            grid=(T // tg,),
            in_specs=[
                pl.BlockSpec(memory_space=pl.ANY),
                pl.BlockSpec((tg, D), lambda i, pos: (i, 0)),
                pl.BlockSpec((None, None, None, 1, D), lambda i, pos: (gl, i // per_b, gj, 0, 0)),
            ],
            out_specs=pl.BlockSpec((tg, D), lambda i, pos: (i, 0)),
            scratch_shapes=[pltpu.VMEM((tg, D), F32), pltpu.VMEM((tg, D), F32),
                            pltpu.SemaphoreType.DMA(())],
        ),
        out_shape=jax.ShapeDtypeStruct((T, D), F32),
        input_output_aliases={2: 0},
        compiler_params=_cparams(1),
        name="moe_combine",
    )(pos, y_sorted, x2, garr)


def _moe_plan(route, T, tm):
    E = N_EXPERTS
    n_tiles = (2 * T) // tm + E
    P = n_tiles * tm
    e_pair = jnp.concatenate([route[:, 0], route[:, 1]]).astype(jnp.int32)
    w_pair = jnp.concatenate([route[:, 2], route[:, 3]])
    onehot = (e_pair[:, None] == jnp.arange(E, dtype=jnp.int32)[None, :]).astype(jnp.int32)
    csum = jnp.cumsum(onehot, axis=0)
    rank = jnp.sum(onehot * (csum - 1), axis=1)
    counts = csum[-1]
    padded = ((counts + tm - 1) // tm) * tm
    ends = jnp.cumsum(padded)
    pos = (ends - padded)[e_pair] + rank
    tok = jnp.concatenate([jnp.arange(T, dtype=jnp.int32)] * 2)
    src_token = jnp.zeros((P,), jnp.int32).at[pos].set(tok)
    w_sorted = jnp.zeros((P,), F32).at[pos].set(w_pair)
    tile_start = jnp.arange(n_tiles, dtype=jnp.int32) * tm
    tile_valid = (tile_start < ends[-1]).astype(jnp.int32)
    tile_expert = jnp.sum((tile_start[:, None] >= ends[None, :]).astype(jnp.int32), axis=1)
    last_valid = tile_expert[jnp.maximum(ends[-1] // tm - 1, 0)]
    tile_expert = jnp.where(tile_valid == 1, tile_expert, last_valid).astype(jnp.int32)
    return pos.astype(jnp.int32), src_token, w_sorted.reshape(P, 1), tile_expert, tile_valid


def _rope_tables(S):
    d = QK_ROPE
    inv = ROPE_THETA ** (-jnp.arange(0, d, 2, dtype=F32) / d)
    ang = jnp.arange(S).astype(F32)[:, None] * inv[None, :]
    cos, sin = jnp.cos(ang), jnp.sin(ang)
    z = jnp.zeros_like(sin)
    return (jnp.concatenate([cos, cos, cos, cos], axis=1),
            jnp.concatenate([-sin, z, -sin, z], axis=1),
            jnp.concatenate([z, sin, z, sin], axis=1))


def _nsa_tables(S, tq, tk):
    n_cmp = (S - CMP_BLOCK) // CMP_STRIDE + 1
    n_sel = S // SEL_BLOCK
    n_chunk = S // CMP_STRIDE
    tok = np.arange(n_cmp)[:, None] * CMP_STRIDE + np.arange(CMP_BLOCK)[None, :]
    blk = tok // SEL_BLOCK
    m = (blk[:, :, None] == np.arange(n_sel)[None, None, :]).sum(axis=1) / CMP_BLOCK
    mt = np.zeros((n_sel, n_chunk), np.float32)
    mt[:, :n_cmp] = m.T
    key_blk = np.arange(S) // SEL_BLOCK
    e = (np.arange(LANES)[:, None] == key_blk[None, :]).astype(np.float32)
    e3 = e.reshape(LANES, S // tk, tk).transpose(1, 0, 2)
    slopes = (2.0 ** (-8.0 * np.arange(1, NSA_HEADS + 1) / NSA_HEADS)).astype(np.float32)
    slopes = np.repeat(slopes.reshape(NSA_GROUPS, NSA_HPG), tq, axis=1)[..., None]
    return jnp.asarray(slopes, F32), jnp.asarray(mt, BF16), jnp.asarray(e3, BF16)


def kernel(x, c, ada_w, ada_b, norm1_g, norm2_g, mla_w_in, mla_g_q, mla_g_kv, mla_w_uq, mla_w_ukv, mla_w_o, kv_ada_w, kv_ada_b, kv_norm_g, nsa_w_kv, cmp_pos_k, cmp_pos_v, cmp_k_w1, cmp_k_w2, cmp_v_w1, cmp_v_w2, nsa_w_in, nsa_w_o, ffn_w_gate, ffn_w_up, ffn_w_down, moe_w_router, moe_b_router, moe_w_gate, moe_w_up, moe_w_down, final_g):
    B, S, D = x.shape
    T = B * S
    depth = ada_w.shape[0]
    n_a = mla_w_in.shape[0]
    H = MLA_HEADS
    G, HP, DK = NSA_GROUPS, NSA_HPG, NSA_DK
    d_ff = ffn_w_gate.shape[-1]
    nsa_tq, nsa_tk = 128, 256

    c_pad = jnp.zeros((8, D), F32).at[:B].set(c)
    mod = _modulation(c_pad, ada_w, ada_b)[:, :B].reshape(depth, B, 6, 1, D)
    kv_mod = _modulation(c_pad, kv_ada_w[None], kv_ada_b[None])[:, :B].reshape(1, B, 2, 1, D)

    rope_tabs = _rope_tables(S)
    slopes, mt, e3 = _nsa_tables(S, nsa_tq, nsa_tk)

    ffn_wg = ffn_w_gate[:, None]
    ffn_wu = ffn_w_up[:, None]
    ffn_wd = ffn_w_down[:, None]

    x2 = x.reshape(T, D)
    shared = None
    for l in range(depth):
        h = _norm(x2, norm1_g[l], B, S, shift=(mod, l, 0), scale=(mod, l, 1))
        if l < n_a:
            w_in = mla_w_in[l]
            q_lora, kv_lora = mla_g_q.shape[1], mla_g_kv.shape[1]
            w_pad = jnp.concatenate([w_in, w_in[:, q_lora + kv_lora:]], axis=1)
            cq, ckv, kpe = _mla_in(h, w_pad, mla_g_q[l], mla_g_kv[l], rope_tabs, S)
            wq = mla_w_uq[l].reshape(q_lora, H, QK_NOPE + QK_ROPE)
            wq = jnp.concatenate([wq[:, :, :QK_NOPE].reshape(q_lora, H * QK_NOPE),
                                  wq[:, :, QK_NOPE:].reshape(q_lora, H * QK_ROPE)], axis=1)
            wkv = mla_w_ukv[l].reshape(kv_lora, H, QK_NOPE + V_HEAD)
            wkv = jnp.concatenate([wkv[:, :, :QK_NOPE].reshape(kv_lora, H * QK_NOPE),
                                   wkv[:, :, QK_NOPE:].reshape(kv_lora, H * V_HEAD)], axis=1)
            q_all = _gmm(cq, [wq[None, None]], 0, mode="cast", tm=2048, tn=1024,
                         n_out=wq.shape[1], out_dtype=BF16)
            kv_all = _gmm(ckv, [wkv[None, None]], 0, mode="cast", tm=2048, tn=1024,
                          n_out=wkv.shape[1], out_dtype=BF16)
            o = _mla_attn(q_all, kv_all, kpe, rope_tabs, B, S)
            x2 = _gmm(o, [mla_w_o[:, None]], l, mode="residual", tm=1024, tn=512, n_out=D,
                      out_dtype=F32, xres=x2, gate=(mod, l, 2), seq=S)
        else:
            jb = l - n_a
            w_in = nsa_w_in[jb]
            q = _gmm(h, [nsa_w_in[:, None]], jb, mode="cast", tm=1024, tn=512,
                     n_out=NSA_HEADS * DK, out_dtype=BF16)
            wg = w_in[:, NSA_HEADS * DK:].reshape(D, G, HP, 3).transpose(0, 1, 3, 2)
            wg = jnp.pad(wg.reshape(D, G, 3 * HP), ((0, 0), (0, 0), (0, LANES - 3 * HP)))
            gates = _gmm(h, [wg.reshape(1, 1, D, G * LANES)], 0, mode="cast", tm=1024, tn=512,
                         n_out=G * LANES, out_dtype=F32)
            kvb, kvc = shared
            o = _nsa_attn(q, gates, kvb, kvc, slopes, mt, e3, B, S, tq=nsa_tq, tk=nsa_tk)
            x2 = _gmm(o, [nsa_w_o[:, None]], jb, mode="residual", tm=1024, tn=512, n_out=D,
                      out_dtype=F32, xres=x2, gate=(mod, l, 2), seq=S)

        if l % 2 == 0:
            h = _norm(x2, norm2_g[l], B, S, shift=(mod, l, 3), scale=(mod, l, 4))
            hid = _gmm(h, [ffn_wg, ffn_wu], l // 2, mode="swiglu", tm=1024, tn=512, n_out=d_ff,
                       out_dtype=BF16)
            x2 = _gmm(hid, [ffn_wd], l // 2, mode="residual", tm=512, tn=512, n_out=D,
                      out_dtype=F32, xres=x2, gate=(mod, l, 5), seq=S)
        else:
            li = l // 2
            wr = jnp.pad(moe_w_router[li], ((0, 0), (0, LANES - N_EXPERTS)))
            br = jnp.pad(moe_b_router[li], (0, LANES - N_EXPERTS)).reshape(1, LANES)
            h32, route = _norm(x2, norm2_g[l], B, S, shift=(mod, l, 3), scale=(mod, l, 4),
                               router=(wr, br), out_dtype=F32)
            pos, src_token, w_sorted, tile_expert, tile_valid = _moe_plan(route, T, MOE_TM)
            hs = _dispatch(h32, src_token)
            hid = _gmm(hs, [moe_w_gate, moe_w_up], li, mode="swiglu", tm=MOE_TM, tn=512,
                       n_out=d_ff, out_dtype=BF16, tile_expert=tile_expert, tile_valid=tile_valid)
            ys = _gmm(hid, [moe_w_down], li, mode="rowscale", tm=MOE_TM, tn=512, n_out=D,
                      out_dtype=F32, tile_expert=tile_expert, tile_valid=tile_valid,
                      rowscale=w_sorted)
            x2 = _combine(x2, ys, pos, (mod, l, 5), S)

        if l == n_a - 1:
            hkv = _norm(x2, kv_norm_g, B, S, shift=(kv_mod, 0, 0), scale=(kv_mod, 0, 1))
            w_kv = nsa_w_kv[None, None]
            kc = _gmm(hkv, [w_kv], 0, mode="cast", tm=1024, tn=512, n_out=2 * G * DK,
                      out_dtype=F32)
            kvb = _gmm(hkv, [w_kv], 0, mode="cast", tm=1024, tn=512, n_out=4 * G * DK,
                       out_dtype=BF16, n_off=(2 * G * DK) // 512)
            kvc = _nsa_compress(kc, jnp.stack([cmp_pos_k, cmp_pos_v]),
                                jnp.stack([cmp_k_w1, cmp_v_w1]), jnp.stack([cmp_k_w2, cmp_v_w2]),
                                B, S)
            shared = (kvb, kvc)

    out = _norm(x2, final_g, B, S, out_dtype=F32)
    return out.reshape(B, S, D)
```

```python
import functools

import numpy as np
import jax
import jax.numpy as jnp
from jax import lax
from jax.experimental import pallas as pl
from jax.experimental.pallas import tpu as pltpu

F32 = jnp.float32
BF16 = jnp.bfloat16

EPS = 1e-6
NEG = -1e30
FORCED_SCORE = 1e6
LOG2E = 1.4426950408889634

MLA_HEADS = 16
QK_NOPE = 128
QK_ROPE = 64
V_HEAD = 128
ROPE_THETA = 10000.0

NSA_HEADS = 16
NSA_GROUPS = 4
NSA_HPG = NSA_HEADS // NSA_GROUPS
NSA_DK = 128
CMP_BLOCK = 32
CMP_STRIDE = 16
SEL_BLOCK = 64
SEL_TOPN = 16
WINDOW = 512

N_EXPERTS = 8
LANES = 128
VMEM_LIMIT = 56 * 1024 * 1024

MOE_TM = 512

SEL_LANES = 32
POS_HI_LANE = 32
POS_LO_LANE = 35
MASK_BIG = 2.0 ** 30


def _cparams(n_axes):
    return pltpu.CompilerParams(
        dimension_semantics=("arbitrary",) * n_axes, vmem_limit_bytes=VMEM_LIMIT)


def _dot_nt(a, b):
    return lax.dot_general(a, b, (((1,), (1,)), ((), ())), preferred_element_type=F32)


def _mod_kernel(c_ref, w_ref, b_ref, o_ref):
    c = c_ref[...]
    ca = (c * jax.nn.sigmoid(c)).astype(BF16)
    o_ref[...] = jnp.dot(ca, w_ref[...].astype(BF16), preferred_element_type=F32) + b_ref[...]


def _modulation(c_pad, w, b, tn=1024):
    L, D, N = w.shape
    return pl.pallas_call(
        _mod_kernel,
        grid=(L, N // tn),
        in_specs=[
            pl.BlockSpec((8, D), lambda l, n: (0, 0)),
            pl.BlockSpec((None, D, tn), lambda l, n: (l, 0, n)),
            pl.BlockSpec((None, 1, tn), lambda l, n: (l, 0, n)),
        ],
        out_specs=pl.BlockSpec((None, 8, tn), lambda l, n: (l, 0, n)),
        out_shape=jax.ShapeDtypeStruct((L, 8, N), F32),
        compiler_params=_cparams(2),
        name="adaln_mod",
    )(c_pad, w, b.reshape(L, 1, N))


def _rms(x, g):
    return x * lax.rsqrt(jnp.mean(x * x, axis=-1, keepdims=True) + EPS) * g


def _norm_kernel(x_ref, g_ref, *refs, modulate, route):
    if modulate:
        sh_ref, sc_ref, *refs = refs
    if route:
        wr_ref, br_ref, *refs = refs
    h = _rms(x_ref[...], g_ref[...])
    if modulate:
        h = h * (1.0 + sc_ref[...]) + sh_ref[...]
    refs[0][...] = h.astype(refs[0].dtype)
    if route:
        logits = jnp.dot(h, wr_ref[...], preferred_element_type=F32,
                         precision=lax.Precision.HIGHEST) + br_ref[...]
        lane = lax.broadcasted_iota(jnp.int32, logits.shape, 1).astype(F32)
        logits = jnp.where(lane < N_EXPERTS, logits, -jnp.inf)
        v1 = jnp.max(logits, axis=-1, keepdims=True)
        i1 = jnp.min(jnp.where(logits == v1, lane, float(LANES)), axis=-1, keepdims=True)
        rest = jnp.where(lane == i1, -jnp.inf, logits)
        v2 = jnp.max(rest, axis=-1, keepdims=True)
        i2 = jnp.min(jnp.where(rest == v2, lane, float(LANES)), axis=-1, keepdims=True)
        e = jnp.exp(v2 - v1)
        w1 = 1.0 / (1.0 + e)
        w2 = e / (1.0 + e)
        refs[1][...] = jnp.where(lane == 0, i1, jnp.where(lane == 1, i2, jnp.where(
            lane == 2, w1, jnp.where(lane == 3, w2, 0.0))))


def _norm(x2, g, B, S, shift=None, scale=None, router=None, out_dtype=BF16, ts=512):
    T, D = x2.shape
    nst = S // ts
    modulate = shift is not None
    route = router is not None
    args = [x2, g.reshape(1, D)]
    in_specs = [
        pl.BlockSpec((ts, D), lambda b, s: (b * nst + s, 0)),
        pl.BlockSpec((1, D), lambda b, s: (0, 0)),
    ]
    if modulate:
        for arr, l, j in (shift, scale):
            args.append(arr)
            in_specs.append(pl.BlockSpec((None, None, None, 1, D),
                                         lambda b, s, l=l, j=j: (l, b, j, 0, 0)))
    out_shape = [jax.ShapeDtypeStruct((T, D), out_dtype)]
    out_specs = [pl.BlockSpec((ts, D), lambda b, s: (b * nst + s, 0))]
    if route:
        wr, br = router
        args += [wr, br]
        in_specs += [pl.BlockSpec((D, LANES), lambda b, s: (0, 0)),
                     pl.BlockSpec((1, LANES), lambda b, s: (0, 0))]
        out_shape.append(jax.ShapeDtypeStruct((T, LANES), F32))
        out_specs.append(pl.BlockSpec((ts, LANES), lambda b, s: (b * nst + s, 0)))
    outs = pl.pallas_call(
        functools.partial(_norm_kernel, modulate=modulate, route=route),
        grid=(B, nst),
        in_specs=in_specs,
        out_specs=out_specs,
        out_shape=out_shape,
        compiler_params=_cparams(2),
        name="norm_mod",
    )(*args)
    return outs if route else outs[0]


def _gmm_kernel(te_ref, tv_ref, a_ref, *refs, mode, out_scale):
    m = pl.program_id(1)
    first = jnp.logical_or(m == 0, te_ref[m] != te_ref[jnp.maximum(m - 1, 0)])
    valid = tv_ref[m] != 0
    if mode == "swiglu":
        wg_ref, wu_ref, o_ref, wgb, wub = refs

        @pl.when(first)
        def _():
            wgb[...] = wg_ref[...].astype(BF16)
            wub[...] = wu_ref[...].astype(BF16)

        @pl.when(valid)
        def _():
            a = a_ref[...]
            g = jnp.dot(a, wgb[...], preferred_element_type=F32)
            u = jnp.dot(a, wub[...], preferred_element_type=F32)
            o_ref[...] = (g * jax.nn.sigmoid(g) * u).astype(o_ref.dtype)
    else:
        if mode == "cast":
            w_ref, o_ref, wb = refs
        elif mode == "residual":
            w_ref, x_ref, gate_ref, o_ref, wb = refs
        else:
            w_ref, rs_ref, o_ref, wb = refs

        @pl.when(first)
        def _():
            wb[...] = w_ref[...].astype(BF16)

        @pl.when(valid)
        def _():
            acc = jnp.dot(a_ref[...], wb[...], preferred_element_type=F32)
            if out_scale is not None:
                acc = acc * out_scale
            if mode == "residual":
                acc = x_ref[...] + gate_ref[...] * acc
            elif mode == "rowscale":
                acc = rs_ref[...] * acc
            o_ref[...] = acc.astype(o_ref.dtype)

    @pl.when(jnp.logical_not(valid))
    def _():
        o_ref[...] = jnp.zeros_like(o_ref)


def _gmm(a, ws, l, *, mode, tm, tn, n_out, out_dtype, tile_expert=None, tile_valid=None,
         n_off=0, xres=None, gate=None, seq=None, rowscale=None, out_scale=None):
    M, K = a.shape
    mt = M // tm
    if tile_expert is None:
        tile_expert = jnp.zeros((mt,), jnp.int32)
        tile_valid = jnp.ones((mt,), jnp.int32)
    w_spec = pl.BlockSpec((None, None, K, tn), lambda n, m, te, tv: (l, te[m], 0, n + n_off))
    args = [a] + list(ws)
    in_specs = [pl.BlockSpec((tm, K), lambda n, m, te, tv: (m, 0))] + [w_spec] * len(ws)
    aliases = {}
    if mode == "residual":
        garr, gl, gj = gate
        per_b = seq // tm
        aliases = {2 + len(args): 0}
        args += [xres, garr]
        in_specs += [
            pl.BlockSpec((tm, tn), lambda n, m, te, tv: (m, n)),
            pl.BlockSpec((None, None, None, 1, tn),
                         lambda n, m, te, tv: (gl, m // per_b, gj, 0, n)),
        ]
    elif mode == "rowscale":
        args.append(rowscale)
        in_specs.append(pl.BlockSpec((tm, 1), lambda n, m, te, tv: (m, 0)))
    return pl.pallas_call(
        functools.partial(_gmm_kernel, mode=mode, out_scale=out_scale),
        grid_spec=pltpu.PrefetchScalarGridSpec(
            num_scalar_prefetch=2,
            grid=(n_out // tn, mt),
            in_specs=in_specs,
            out_specs=pl.BlockSpec((tm, tn), lambda n, m, te, tv: (m, n)),
            scratch_shapes=[pltpu.VMEM((K, tn), BF16) for _ in ws],
        ),
        out_shape=jax.ShapeDtypeStruct((M, n_out), out_dtype),
        input_output_aliases=aliases,
        compiler_params=_cparams(2),
        name="gmm_" + mode,
    )(tile_expert, tile_valid, *args)


def _rope_pairs(v, cos, s1, s2):
    return v * cos + pltpu.roll(v, LANES - QK_ROPE // 2, 1) * s1 + pltpu.roll(v, QK_ROPE // 2, 1) * s2


def _mla_in_kernel(a_ref, w_ref, gq_ref, gkv_ref, cos_ref, s1_ref, s2_ref,
                   cq_ref, ckv_ref, kpe_ref, wb, *, q_lora, kv_lora):
    @pl.when(pl.program_id(0) == 0)
    def _():
        wb[...] = w_ref[...].astype(BF16)

    acc = jnp.dot(a_ref[...], wb[...], preferred_element_type=F32)
    cq_ref[...] = _rms(acc[:, :q_lora], gq_ref[...]).astype(BF16)
    ckv_ref[...] = _rms(acc[:, q_lora:q_lora + kv_lora], gkv_ref[...]).astype(BF16)
    v = acc[:, q_lora + kv_lora:]
    kpe_ref[...] = _rope_pairs(v, cos_ref[...], s1_ref[...], s2_ref[...]).astype(BF16)


def _mla_in(h, w_pad, g_q, g_kv, rope_tabs, S, tm=512):
    T, D = h.shape
    q_lora, kv_lora = g_q.shape[0], g_kv.shape[0]
    n_all = w_pad.shape[1]
    nst = S // tm
    tab_spec = pl.BlockSpec((tm, LANES), lambda i: (i % nst, 0))
    return pl.pallas_call(
        functools.partial(_mla_in_kernel, q_lora=q_lora, kv_lora=kv_lora),
        grid=(T // tm,),
        in_specs=[
            pl.BlockSpec((tm, D), lambda i: (i, 0)),
            pl.BlockSpec((D, n_all), lambda i: (0, 0)),
            pl.BlockSpec((1, q_lora), lambda i: (0, 0)),
            pl.BlockSpec((1, kv_lora), lambda i: (0, 0)),
            tab_spec, tab_spec, tab_spec,
        ],
        out_specs=[
            pl.BlockSpec((tm, q_lora), lambda i: (i, 0)),
            pl.BlockSpec((tm, kv_lora), lambda i: (i, 0)),
            pl.BlockSpec((tm, LANES), lambda i: (i, 0)),
        ],
        out_shape=[
            jax.ShapeDtypeStruct((T, q_lora), BF16),
            jax.ShapeDtypeStruct((T, kv_lora), BF16),
            jax.ShapeDtypeStruct((T, LANES), BF16),
        ],
        scratch_shapes=[pltpu.VMEM((D, n_all), BF16)],
        compiler_params=_cparams(1),
        name="mla_in",
    )(h, w_pad, g_q.reshape(1, -1), g_kv.reshape(1, -1), *rope_tabs)


def _mla_attn_kernel(qn_ref, qp_ref, kn_ref, v_ref, kpe_ref, cos_ref, s1_ref, s2_ref, o_ref,
                     m_sc, l_sc, acc_sc, *, tq):
    qi = pl.program_id(2)
    qp = _rope_pairs(qp_ref[...].astype(F32), cos_ref[...], s1_ref[...], s2_ref[...])
    lane = lax.broadcasted_iota(jnp.int32, qp.shape, 1)
    qs = []
    for hh in range(2):
        keep = (lane < QK_ROPE) if hh == 0 else (lane >= QK_ROPE)
        qs.append(jnp.concatenate(
            [qn_ref[:, hh * QK_NOPE:(hh + 1) * QK_NOPE], jnp.where(keep, qp, 0.0).astype(BF16)],
            axis=1))
    m_sc[...] = jnp.full_like(m_sc, NEG)
    l_sc[...] = jnp.zeros_like(l_sc)
    acc_sc[...] = jnp.zeros_like(acc_sc)

    def step(j, diagonal):
        ks = pl.multiple_of(j * tq, tq)
        kpe = kpe_ref[pl.ds(ks, tq), :]
        for hh in range(2):
            k = jnp.concatenate([kn_ref[pl.ds(ks, tq), hh * QK_NOPE:(hh + 1) * QK_NOPE], kpe],
                                axis=1)
            s = _dot_nt(qs[hh], k)
            if diagonal:
                row = lax.broadcasted_iota(jnp.int32, (tq, tq), 0)
                col = lax.broadcasted_iota(jnp.int32, (tq, tq), 1)
                s = jnp.where(col <= row, s, NEG)
            m_old = m_sc[hh]
            m_new = jnp.maximum(m_old, jnp.max(s, axis=-1, keepdims=True))
            alpha = jnp.exp2(m_old - m_new)
            p = jnp.exp2(s - jnp.concatenate([m_new] * (tq // LANES), axis=1))
            l_sc[hh] = alpha * l_sc[hh] + jnp.sum(p, axis=-1, keepdims=True)
            vv = v_ref[pl.ds(ks, tq), hh * V_HEAD:(hh + 1) * V_HEAD]
            acc_sc[hh] = alpha * acc_sc[hh] + jnp.dot(p.astype(BF16), vv,
                                                      preferred_element_type=F32)
            m_sc[hh] = m_new

    def body(j, carry):
        step(j, False)
        return carry

    lax.fori_loop(0, qi, body, 0)
    step(qi, True)
    for hh in range(2):
        o_ref[:, hh * V_HEAD:(hh + 1) * V_HEAD] = (acc_sc[hh] / l_sc[hh]).astype(o_ref.dtype)


def _mla_attn(q_all, kv_all, kpe, rope_tabs, B, S, tq=512):
    T = q_all.shape[0]
    H = MLA_HEADS
    nqt = S // tq
    pair_w = 2 * QK_NOPE
    n_pairs = H // 2
    tab_spec = pl.BlockSpec((tq, LANES), lambda b, p, i: (i, 0))
    return pl.pallas_call(
        functools.partial(_mla_attn_kernel, tq=tq),
        grid=(B, n_pairs, nqt),
        in_specs=[
            pl.BlockSpec((tq, pair_w), lambda b, p, i: (b * nqt + i, p)),
            pl.BlockSpec((tq, LANES), lambda b, p, i: (b * nqt + i, (H * QK_NOPE) // LANES + p)),
            pl.BlockSpec((S, pair_w), lambda b, p, i: (b, p)),
            pl.BlockSpec((S, pair_w), lambda b, p, i: (b, n_pairs + p)),
            pl.BlockSpec((S, LANES), lambda b, p, i: (b, 0)),
            tab_spec, tab_spec, tab_spec,
        ],
        out_specs=pl.BlockSpec((tq, pair_w), lambda b, p, i: (b * nqt + i, p)),
        out_shape=jax.ShapeDtypeStruct((T, H * V_HEAD), BF16),
        scratch_shapes=[pltpu.VMEM((2, tq, LANES), F32), pltpu.VMEM((2, tq, LANES), F32),
                        pltpu.VMEM((2, tq, V_HEAD), F32)],
        compiler_params=_cparams(3),
        name="mla_attn",
    )(q_all, q_all, kv_all, kv_all, kpe, *rope_tabs)


def _nsa_compress_kernel(t_ref, pe_ref, w1_ref, w2_ref, o_ref, *, n_chunk):
    half = CMP_BLOCK // 2
    pe = pe_ref[...]

    def chunk_rows(off):
        cols = []
        for l in range(half):
            x = t_ref[pl.ds(l, n_chunk, stride=CMP_STRIDE), :] + pe[off + l:off + l + 1, :]
            cols.append(x.astype(BF16))
        return jnp.concatenate(cols, axis=1)

    w1 = w1_ref[...].astype(BF16)
    kdim = half * NSA_DK
    p0 = jnp.dot(chunk_rows(0), w1[:kdim], preferred_element_type=F32)
    p1 = jnp.dot(chunk_rows(half), w1[kdim:], preferred_element_type=F32)
    pre = p0 + pltpu.roll(p1, n_chunk - 1, 0)
    hid = (pre * jax.nn.sigmoid(pre)).astype(BF16)
    out = jnp.dot(hid, w2_ref[...].astype(BF16), preferred_element_type=F32)
    rown = lax.broadcasted_iota(jnp.int32, out.shape, 0)
    o_ref[...] = jnp.where(rown < n_chunk - 1, out, 0.0).astype(o_ref.dtype)


def _nsa_compress(kc, pe, w1, w2, B, S):
    G = NSA_GROUPS
    n_chunk = S // CMP_STRIDE
    return pl.pallas_call(
        functools.partial(_nsa_compress_kernel, n_chunk=n_chunk),
        grid=(B, 2, G),
        in_specs=[
            pl.BlockSpec((S, NSA_DK), lambda b, i, g: (b, i * G + g)),
            pl.BlockSpec((None, CMP_BLOCK, NSA_DK), lambda b, i, g: (i, 0, 0)),
            pl.BlockSpec((None, CMP_BLOCK * NSA_DK, NSA_DK), lambda b, i, g: (i, 0, 0)),
            pl.BlockSpec((None, NSA_DK, NSA_DK), lambda b, i, g: (i, 0, 0)),
        ],
        out_specs=pl.BlockSpec((None, None, None, n_chunk, NSA_DK), lambda b, i, g: (b, i, g, 0, 0)),
        out_shape=jax.ShapeDtypeStruct((B, 2, G, n_chunk, NSA_DK), BF16),
        compiler_params=_cparams(3),
        name="nsa_compress",
    )(kc, pe, w1, w2)


def _nsa_attn_kernel(q_ref, gt_ref, ks_ref, vs_ref, kw_ref, vw_ref, kc_ref, vc_ref, qtab_ref,
                     ktab_ref, ctab_ref, mt_ref, o_ref, m_sc, l_sc, acc_sc,
                     *, tq, tk, n_cmp, n_sel):
    qi = pl.program_id(2)
    t0 = qi * tq
    HP = NSA_HPG
    R = HP * tq
    qb = q_ref[...]
    qs = jnp.concatenate([qb[:, h * NSA_DK:(h + 1) * NSA_DK] for h in range(HP)], axis=0)
    qtab = qtab_ref[...]
    qa = jnp.concatenate([qs, qtab], axis=1)
    rowi = lax.broadcasted_iota(jnp.int32, (R, 1), 0)
    tcol = (t0 + rowi % tq).astype(F32)

    ncol = lax.broadcasted_iota(jnp.int32, (1, kc_ref.shape[0]), 1)
    dist = tcol - (ncol * CMP_STRIDE + (CMP_BLOCK - 1)).astype(F32)
    valid = jnp.logical_and(dist >= 0, ncol < n_cmp)
    kc = jnp.concatenate([kc_ref[...], ctab_ref[...]], axis=1)
    s = jnp.where(valid, _dot_nt(qa, kc), NEG)
    p = jnp.exp2(s - jnp.max(s, axis=-1, keepdims=True))
    l = jnp.sum(p, axis=-1, keepdims=True)
    p_cmp = jnp.where(tcol >= CMP_BLOCK - 1, p / l, 0.0)
    o_cmp = jnp.dot(p_cmp.astype(BF16), vc_ref[...], preferred_element_type=F32)

    ps = p_cmp[0:tq]
    for h in range(1, HP):
        ps = ps + p_cmp[h * tq:(h + 1) * tq]
    ps_hi = ps.astype(BF16)
    ps_lo = (ps - ps_hi.astype(F32)).astype(BF16)
    imp = _dot_nt(mt_ref[...], ps_hi) + _dot_nt(mt_ref[...], ps_lo)
    jrow = lax.broadcasted_iota(jnp.int32, (n_sel, tq), 0)
    blk_t = (t0 + lax.broadcasted_iota(jnp.int32, (n_sel, tq), 1)) // SEL_BLOCK
    forced = (jrow == 0) | (jrow == blk_t) | (jrow == blk_t - 1)
    imp = jnp.where(forced, FORCED_SCORE, imp)
    imp = jnp.where(jrow > blk_t, -1.0, imp)
    rank = jnp.zeros((n_sel, tq), F32)
    for k in range(n_sel):
        rk = imp[k:k + 1, :]
        beats = (rk > imp) | ((rk == imp) & (jrow > k))
        rank = rank + jnp.where(beats, 1.0, 0.0)
    sel_t = jnp.where(rank < SEL_TOPN, 1.0, 0.0)
    sel_t = jnp.concatenate([sel_t, jnp.zeros((LANES - n_sel, tq), F32)], axis=0)
    sel_neg = ((sel_t.T - 1.0) * MASK_BIG).astype(BF16)
    lane = lax.broadcasted_iota(jnp.int32, (R, LANES), 1)
    qx = jnp.where(lane < SEL_LANES, jnp.concatenate([sel_neg] * HP, axis=0), qtab)
    qsel = jnp.concatenate([qs, qx], axis=1)

    m_sc[...] = jnp.full_like(m_sc, NEG)
    l_sc[...] = jnp.zeros_like(l_sc)
    acc_sc[...] = jnp.zeros_like(acc_sc)

    def tile(j, causal):
        ks = pl.multiple_of(j * tk, tk)
        k = jnp.concatenate([ks_ref[pl.ds(ks, tk), :], ktab_ref[pl.ds(ks, tk), :]], axis=1)
        sc = _dot_nt(qsel, k)
        if causal:
            kpos = (ks + lax.broadcasted_iota(jnp.int32, (1, tk), 1)).astype(F32)
            sc = jnp.where(kpos <= tcol, sc, NEG)
        m_old = m_sc[...]
        m_new = jnp.maximum(m_old, jnp.max(sc, axis=-1, keepdims=True))
        alpha = jnp.exp2(m_old - m_new)
        pp = jnp.exp2(sc - jnp.concatenate([m_new] * (tk // LANES), axis=1))
        l_sc[...] = alpha * l_sc[...] + jnp.sum(pp, axis=-1, keepdims=True)
        acc_sc[...] = alpha * acc_sc[...] + jnp.dot(
            pp.astype(BF16), vs_ref[pl.ds(ks, tk), :], preferred_element_type=F32)
        m_sc[...] = m_new

    def body(j, carry):
        tile(j, False)
        return carry

    n_full = t0 // tk
    lax.fori_loop(0, n_full, body, 0)
    tile(n_full, True)
    o_sel = acc_sc[...] / l_sc[...]

    span = WINDOW + tq
    ws = pl.multiple_of(jnp.maximum(t0 - WINDOW, 0), tq)
    kpos = (ws + lax.broadcasted_iota(jnp.int32, (1, span), 1)).astype(F32)
    d = tcol - kpos
    ok = jnp.logical_and(d >= 0, d < WINDOW)
    kw = jnp.concatenate([kw_ref[pl.ds(ws, span), :], ktab_ref[pl.ds(ws, span), :]], axis=1)
    sw = jnp.where(ok, _dot_nt(qa, kw), NEG)
    pw = jnp.exp2(sw - jnp.max(sw, axis=-1, keepdims=True))
    lw = jnp.sum(pw, axis=-1, keepdims=True)
    o_win = jnp.dot(pw.astype(BF16), vw_ref[pl.ds(ws, span), :], preferred_element_type=F32) / lw

    gt = jax.nn.sigmoid(gt_ref[...])

    def gcol(i):
        return jnp.concatenate([gt[:, i * HP + h:i * HP + h + 1] for h in range(HP)], axis=0)

    o = gcol(0) * o_cmp + gcol(1) * o_sel + gcol(2) * o_win
    o_ref[...] = jnp.concatenate([o[h * tq:(h + 1) * tq] for h in range(HP)], axis=1).astype(o_ref.dtype)


def _nsa_attn(q, gates, kvb, kvc, tabs, B, S, tq=128, tk=256):
    T = q.shape[0]
    G, DK, HP = NSA_GROUPS, NSA_DK, NSA_HPG
    qtab, ktab, ctab, mt = tabs
    nqt = S // tq
    n_chunk = kvc.shape[3]
    n_cmp = (S - CMP_BLOCK) // CMP_STRIDE + 1
    n_sel = S // SEL_BLOCK

    def kv_spec(i):
        return pl.BlockSpec((S, DK), lambda b, g, t, i=i: (b, i * G + g))

    def cmp_spec(i):
        return pl.BlockSpec((None, None, None, n_chunk, DK), lambda b, g, t, i=i: (b, i, g, 0, 0))

    return pl.pallas_call(
        functools.partial(_nsa_attn_kernel, tq=tq, tk=tk, n_cmp=n_cmp, n_sel=n_sel),
        grid=(B, G, nqt),
        in_specs=[
            pl.BlockSpec((tq, HP * DK), lambda b, g, t: (b * nqt + t, g)),
            pl.BlockSpec((tq, LANES), lambda b, g, t: (b * nqt + t, g)),
            kv_spec(0), kv_spec(1), kv_spec(2), kv_spec(3),
            cmp_spec(0), cmp_spec(1),
            pl.BlockSpec((None, HP * tq, LANES), lambda b, g, t: (g, 0, 0)),
            pl.BlockSpec(ktab.shape, lambda b, g, t: (0, 0)),
            pl.BlockSpec(ctab.shape, lambda b, g, t: (0, 0)),
            pl.BlockSpec(mt.shape, lambda b, g, t: (0, 0)),
        ],
        out_specs=pl.BlockSpec((tq, HP * DK), lambda b, g, t: (b * nqt + t, g)),
        out_shape=jax.ShapeDtypeStruct((T, NSA_HEADS * DK), BF16),
        scratch_shapes=[pltpu.VMEM((HP * tq, LANES), F32), pltpu.VMEM((HP * tq, LANES), F32),
                        pltpu.VMEM((HP * tq, DK), F32)],
        compiler_params=_cparams(3),
        name="nsa_attn",
    )(q, gates, kvb, kvb, kvb, kvb, kvc, kvc, qtab, ktab, ctab, mt)


def _row_copy(src_hbm, dst_vmem, sem, src_row, dst_row):
    return pltpu.make_async_copy(src_hbm.at[pl.ds(src_row, 1), :], dst_vmem.at[pl.ds(dst_row, 1), :], sem)


def _dispatch_kernel(idx_ref, src_hbm, o_ref, buf, sem, *, tg, n_steps):
    i = pl.program_id(0)
    slot = i % 2

    def issue(step, slot_):
        def start(r, c):
            _row_copy(src_hbm, buf.at[slot_], sem.at[slot_], idx_ref[step * tg + r], r).start()
            return c

        lax.fori_loop(0, tg, start, 0, unroll=8)

    @pl.when(i == 0)
    def _():
        issue(0, 0)

    @pl.when(i + 1 < n_steps)
    def _():
        issue(i + 1, 1 - slot)

    def wait(r, c):
        _row_copy(src_hbm, buf.at[slot], sem.at[slot], 0, r).wait()
        return c

    lax.fori_loop(0, tg, wait, 0, unroll=8)
    o_ref[...] = buf[slot].astype(o_ref.dtype)


def _dispatch(h32, src_token, tg=256):
    T, D = h32.shape
    P = src_token.shape[0]
    n_steps = P // tg
    return pl.pallas_call(
        functools.partial(_dispatch_kernel, tg=tg, n_steps=n_steps),
        grid_spec=pltpu.PrefetchScalarGridSpec(
            num_scalar_prefetch=1,
            grid=(n_steps,),
            in_specs=[pl.BlockSpec(memory_space=pl.ANY)],
            out_specs=pl.BlockSpec((tg, D), lambda i, idx: (i, 0)),
            scratch_shapes=[pltpu.VMEM((2, tg, D), F32), pltpu.SemaphoreType.DMA((2,))],
        ),
        out_shape=jax.ShapeDtypeStruct((P, D), BF16),
        compiler_params=_cparams(1),
        name="moe_dispatch",
    )(src_token, h32)


def _combine_kernel(pos_ref, y_hbm, x_ref, gate_ref, o_ref, buf0, buf1, sem, *, tg, n_tok):
    base = pl.program_id(0) * tg

    def start(r, c):
        _row_copy(y_hbm, buf0, sem, pos_ref[base + r], r).start()
        _row_copy(y_hbm, buf1, sem, pos_ref[n_tok + base + r], r).start()
        return c

    def wait(r, c):
        _row_copy(y_hbm, buf0, sem, 0, r).wait()
        _row_copy(y_hbm, buf1, sem, 0, r).wait()
        return c

    lax.fori_loop(0, tg, start, 0, unroll=8)
    lax.fori_loop(0, tg, wait, 0, unroll=8)
    o_ref[...] = x_ref[...] + gate_ref[...] * (buf0[...] + buf1[...])


def _combine(x2, y_sorted, pos, gate, S, tg=256):
    T, D = x2.shape
    garr, gl, gj = gate
    per_b = S // tg
    return pl.pallas_call(
        functools.partial(_combine_kernel, tg=tg, n_tok=T),
        grid_spec=pltpu.PrefetchScalarGridSpec(
            num_scalar_prefetch=1,
            grid=(T // tg,),
            in_specs=[
                pl.BlockSpec(memory_space=pl.ANY),
                pl.BlockSpec((tg, D), lambda i, pos: (i, 0)),
                pl.BlockSpec((None, None, None, 1, D), lambda i, pos: (gl, i // per_b, gj, 0, 0)),
            ],
            out_specs=pl.BlockSpec((tg, D), lambda i, pos: (i, 0)),
            scratch_shapes=[pltpu.VMEM((tg, D), F32), pltpu.VMEM((tg, D), F32),
                            pltpu.SemaphoreType.DMA(())],
        ),
        out_shape=jax.ShapeDtypeStruct((T, D), F32),
        input_output_aliases={2: 0},
        compiler_params=_cparams(1),
        name="moe_combine",
    )(pos, y_sorted, x2, garr)


def _moe_plan(route, T, tm):
    E = N_EXPERTS
    n_tiles = (2 * T) // tm + E
    P = n_tiles * tm
    e_pair = jnp.concatenate([route[:, 0], route[:, 1]]).astype(jnp.int32)
    w_pair = jnp.concatenate([route[:, 2], route[:, 3]])
    onehot = (e_pair[:, None] == jnp.arange(E, dtype=jnp.int32)[None, :]).astype(jnp.int32)
    csum = jnp.cumsum(onehot, axis=0)
    rank = jnp.sum(onehot * (csum - 1), axis=1)
    counts = csum[-1]
    padded = ((counts + tm - 1) // tm) * tm
    ends = jnp.cumsum(padded)
    pos = (ends - padded)[e_pair] + rank
    tok = jnp.concatenate([jnp.arange(T, dtype=jnp.int32)] * 2)
    src_token = jnp.zeros((P,), jnp.int32).at[pos].set(tok)
    w_sorted = jnp.zeros((P,), F32).at[pos].set(w_pair)
    tile_start = jnp.arange(n_tiles, dtype=jnp.int32) * tm
    tile_valid = (tile_start < ends[-1]).astype(jnp.int32)
    tile_expert = jnp.sum((tile_start[:, None] >= ends[None, :]).astype(jnp.int32), axis=1)
    last_valid = tile_expert[jnp.maximum(ends[-1] // tm - 1, 0)]
    tile_expert = jnp.where(tile_valid == 1, tile_expert, last_valid).astype(jnp.int32)
    return pos.astype(jnp.int32), src_token, w_sorted.reshape(P, 1), tile_expert, tile_valid


def _rope_tables(S):
    d = QK_ROPE
    inv = ROPE_THETA ** (-jnp.arange(0, d, 2, dtype=F32) / d)
    ang = jnp.arange(S).astype(F32)[:, None] * inv[None, :]
    cos, sin = jnp.cos(ang), jnp.sin(ang)
    z = jnp.zeros_like(sin)
    return (jnp.concatenate([cos, cos, cos, cos], axis=1),
            jnp.concatenate([-sin, z, -sin, z], axis=1),
            jnp.concatenate([z, sin, z, sin], axis=1))


def _pos_columns(pos):
    tab = np.zeros((pos.shape[0], LANES), np.float32)
    tab[:, POS_HI_LANE:POS_HI_LANE + 3] = (LANES * (pos // LANES))[:, None]
    tab[:, POS_LO_LANE:POS_LO_LANE + 3] = (pos % LANES)[:, None]
    return tab


def _nsa_tables(S, tq):
    n_cmp = (S - CMP_BLOCK) // CMP_STRIDE + 1
    n_sel = S // SEL_BLOCK
    n_chunk = S // CMP_STRIDE
    tok = np.arange(n_cmp)[:, None] * CMP_STRIDE + np.arange(CMP_BLOCK)[None, :]
    blk = tok // SEL_BLOCK
    m = (blk[:, :, None] == np.arange(n_sel)[None, None, :]).sum(axis=1) / CMP_BLOCK
    mt = np.zeros((n_sel, n_chunk), np.float32)
    mt[:, :n_cmp] = m.T
    keys = np.arange(S)
    ktab = _pos_columns(keys)
    ktab[:, :SEL_LANES] = (keys[:, None] // SEL_BLOCK == np.arange(SEL_LANES)[None, :])
    ctab = _pos_columns(np.arange(n_chunk) * CMP_STRIDE + (CMP_BLOCK - 1))
    slopes = jnp.asarray(2.0 ** (-8.0 * np.arange(1, NSA_HEADS + 1) / NSA_HEADS), F32)
    a = slopes * LOG2E
    a_hi = a.astype(BF16)
    r1 = a - a_hi.astype(F32)
    a_mid = r1.astype(BF16)
    a_lo = (r1 - a_mid.astype(F32)).astype(BF16)
    pieces = jnp.stack([a_hi, a_mid, a_lo], axis=-1)
    qrow = jnp.zeros((NSA_HEADS, LANES), BF16)
    qrow = qrow.at[:, POS_HI_LANE:POS_HI_LANE + 3].set(pieces)
    qrow = qrow.at[:, POS_LO_LANE:POS_LO_LANE + 3].set(pieces)
    qtab = jnp.repeat(qrow.reshape(NSA_GROUPS, NSA_HPG, LANES), tq, axis=1)
    return qtab, jnp.asarray(ktab, BF16), jnp.asarray(ctab, BF16), jnp.asarray(mt, BF16)


def kernel(x, c, ada_w, ada_b, norm1_g, norm2_g, mla_w_in, mla_g_q, mla_g_kv, mla_w_uq, mla_w_ukv, mla_w_o, kv_ada_w, kv_ada_b, kv_norm_g, nsa_w_kv, cmp_pos_k, cmp_pos_v, cmp_k_w1, cmp_k_w2, cmp_v_w1, cmp_v_w2, nsa_w_in, nsa_w_o, ffn_w_gate, ffn_w_up, ffn_w_down, moe_w_router, moe_b_router, moe_w_gate, moe_w_up, moe_w_down, final_g):
    B, S, D = x.shape
    T = B * S
    depth = ada_w.shape[0]
    n_a = mla_w_in.shape[0]
    H = MLA_HEADS
    G, HP, DK = NSA_GROUPS, NSA_HPG, NSA_DK
    d_ff = ffn_w_gate.shape[-1]
    nsa_tq = 128

    c_pad = jnp.zeros((8, D), F32).at[:B].set(c)
    mod = _modulation(c_pad, ada_w, ada_b)[:, :B].reshape(depth, B, 6, 1, D)
    kv_mod = _modulation(c_pad, kv_ada_w[None], kv_ada_b[None])[:, :B].reshape(1, B, 2, 1, D)

    rope_tabs = _rope_tables(S)
    nsa_tabs = _nsa_tables(S, nsa_tq)

    ffn_wg = ffn_w_gate[:, None]
    ffn_wu = ffn_w_up[:, None]
    ffn_wd = ffn_w_down[:, None]

    x2 = x.reshape(T, D)
    shared = None
    for l in range(depth):
        h = _norm(x2, norm1_g[l], B, S, shift=(mod, l, 0), scale=(mod, l, 1))
        if l < n_a:
            w_in = mla_w_in[l]
            q_lora, kv_lora = mla_g_q.shape[1], mla_g_kv.shape[1]
            w_pad = jnp.concatenate([w_in, w_in[:, q_lora + kv_lora:]], axis=1)
            cq, ckv, kpe = _mla_in(h, w_pad, mla_g_q[l], mla_g_kv[l], rope_tabs, S)
            wq = mla_w_uq[l].reshape(q_lora, H, QK_NOPE + QK_ROPE)
            wq = jnp.concatenate([wq[:, :, :QK_NOPE].reshape(q_lora, H * QK_NOPE),
                                  wq[:, :, QK_NOPE:].reshape(q_lora, H * QK_ROPE)], axis=1)
            wkv = mla_w_ukv[l].reshape(kv_lora, H, QK_NOPE + V_HEAD)
            wkv = jnp.concatenate([wkv[:, :, :QK_NOPE].reshape(kv_lora, H * QK_NOPE),
                                   wkv[:, :, QK_NOPE:].reshape(kv_lora, H * V_HEAD)], axis=1)
            q_all = _gmm(cq, [wq[None, None]], 0, mode="cast", tm=2048, tn=1024,
                         n_out=wq.shape[1], out_dtype=BF16,
                         out_scale=(QK_NOPE + QK_ROPE) ** -0.5 * LOG2E)
            kv_all = _gmm(ckv, [wkv[None, None]], 0, mode="cast", tm=2048, tn=1024,
                          n_out=wkv.shape[1], out_dtype=BF16)
            o = _mla_attn(q_all, kv_all, kpe, rope_tabs, B, S)
            x2 = _gmm(o, [mla_w_o[:, None]], l, mode="residual", tm=1024, tn=512, n_out=D,
                      out_dtype=F32, xres=x2, gate=(mod, l, 2), seq=S)
        else:
            jb = l - n_a
            w_in = nsa_w_in[jb]
            q = _gmm(h, [nsa_w_in[:, None]], jb, mode="cast", tm=1024, tn=512,
                     n_out=NSA_HEADS * DK, out_dtype=BF16, out_scale=DK ** -0.5 * LOG2E)
            wg = w_in[:, NSA_HEADS * DK:].reshape(D, G, HP, 3).transpose(0, 1, 3, 2)
            wg = jnp.pad(wg.reshape(D, G, 3 * HP), ((0, 0), (0, 0), (0, LANES - 3 * HP)))
            gates = _gmm(h, [wg.reshape(1, 1, D, G * LANES)], 0, mode="cast", tm=1024, tn=512,
                         n_out=G * LANES, out_dtype=F32)
            kvb, kvc = shared
            o = _nsa_attn(q, gates, kvb, kvc, nsa_tabs, B, S, tq=nsa_tq)
            x2 = _gmm(o, [nsa_w_o[:, None]], jb, mode="residual", tm=1024, tn=512, n_out=D,
                      out_dtype=F32, xres=x2, gate=(mod, l, 2), seq=S)

        if l % 2 == 0:
            h = _norm(x2, norm2_g[l], B, S, shift=(mod, l, 3), scale=(mod, l, 4))
            hid = _gmm(h, [ffn_wg, ffn_wu], l // 2, mode="swiglu", tm=1024, tn=512, n_out=d_ff,
                       out_dtype=BF16)
            x2 = _gmm(hid, [ffn_wd], l // 2, mode="residual", tm=512, tn=512, n_out=D,
                      out_dtype=F32, xres=x2, gate=(mod, l, 5), seq=S)
        else:
            li = l // 2
            wr = jnp.pad(moe_w_router[li], ((0, 0), (0, LANES - N_EXPERTS)))
            br = jnp.pad(moe_b_router[li], (0, LANES - N_EXPERTS)).reshape(1, LANES)
            h32, route = _norm(x2, norm2_g[l], B, S, shift=(mod, l, 3), scale=(mod, l, 4),
                               router=(wr, br), out_dtype=F32)
            pos, src_token, w_sorted, tile_expert, tile_valid = _moe_plan(route, T, MOE_TM)
            hs = _dispatch(h32, src_token)
            hid = _gmm(hs, [moe_w_gate, moe_w_up], li, mode="swiglu", tm=MOE_TM, tn=512,
                       n_out=d_ff, out_dtype=BF16, tile_expert=tile_expert, tile_valid=tile_valid)
            ys = _gmm(hid, [moe_w_down], li, mode="rowscale", tm=MOE_TM, tn=512, n_out=D,
                      out_dtype=F32, tile_expert=tile_expert, tile_valid=tile_valid,
                      rowscale=w_sorted)
            x2 = _combine(x2, ys, pos, (mod, l, 5), S)

        if l == n_a - 1:
            hkv = _norm(x2, kv_norm_g, B, S, shift=(kv_mod, 0, 0), scale=(kv_mod, 0, 1))
            w_kv = nsa_w_kv[None, None]
            kc = _gmm(hkv, [w_kv], 0, mode="cast", tm=1024, tn=512, n_out=2 * G * DK,
                      out_dtype=F32)
            kvb = _gmm(hkv, [w_kv], 0, mode="cast", tm=1024, tn=512, n_out=4 * G * DK,
                       out_dtype=BF16, n_off=(2 * G * DK) // 512)
            kvc = _nsa_compress(kc, jnp.stack([cmp_pos_k, cmp_pos_v]),
                                jnp.stack([cmp_k_w1, cmp_v_w1]), jnp.stack([cmp_k_w2, cmp_v_w2]),
                                B, S)
            shared = (kvb, kvc)

    out = _norm(x2, final_g, B, S, out_dtype=F32)
    return out.reshape(B, S, D)
```

```python
import functools

import numpy as np
import jax
import jax.numpy as jnp
from jax import lax
from jax.experimental import pallas as pl
from jax.experimental.pallas import tpu as pltpu

F32 = jnp.float32
BF16 = jnp.bfloat16

EPS = 1e-6
NEG = -1e30
FORCED_SCORE = 1e6
LOG2E = 1.4426950408889634

MLA_HEADS = 16
QK_NOPE = 128
QK_ROPE = 64
V_HEAD = 128
ROPE_THETA = 10000.0

NSA_HEADS = 16
NSA_GROUPS = 4
NSA_HPG = NSA_HEADS // NSA_GROUPS
NSA_DK = 128
CMP_BLOCK = 32
CMP_STRIDE = 16
SEL_BLOCK = 64
SEL_TOPN = 16
WINDOW = 512

N_EXPERTS = 8
LANES = 128
VMEM_LIMIT = 56 * 1024 * 1024

MOE_TM = 512

SEL_LANES = 32
POS_HI_LANE = 32
POS_LO_LANE = 35
MASK_BIG = 2.0 ** 30


def _cparams(n_axes):
    return pltpu.CompilerParams(
        dimension_semantics=("arbitrary",) * n_axes, vmem_limit_bytes=VMEM_LIMIT)


def _dot_nt(a, b):
    return lax.dot_general(a, b, (((1,), (1,)), ((), ())), preferred_element_type=F32)


def _mod_kernel(c_ref, w_ref, b_ref, o_ref):
    c = c_ref[...]
    ca = (c * jax.nn.sigmoid(c)).astype(BF16)
    o_ref[...] = jnp.dot(ca, w_ref[...].astype(BF16), preferred_element_type=F32) + b_ref[...]


def _modulation(c_pad, w, b, tn=1024):
    L, D, N = w.shape
    return pl.pallas_call(
        _mod_kernel,
        grid=(L, N // tn),
        in_specs=[
            pl.BlockSpec((8, D), lambda l, n: (0, 0)),
            pl.BlockSpec((None, D, tn), lambda l, n: (l, 0, n)),
            pl.BlockSpec((None, 1, tn), lambda l, n: (l, 0, n)),
        ],
        out_specs=pl.BlockSpec((None, 8, tn), lambda l, n: (l, 0, n)),
        out_shape=jax.ShapeDtypeStruct((L, 8, N), F32),
        compiler_params=_cparams(2),
        name="adaln_mod",
    )(c_pad, w, b.reshape(L, 1, N))


def _rms(x, g):
    return x * lax.rsqrt(jnp.mean(x * x, axis=-1, keepdims=True) + EPS) * g


_HI16 = 0xFFFF0000


def _pack_bf16_halves(h):
    half = h.shape[1] // 2
    bits = lax.bitcast_convert_type(h.astype(BF16).astype(F32), jnp.uint32)
    return (bits[:, :half] >> 16) | (bits[:, half:] & jnp.uint32(_HI16))


def _unpack_bf16_halves(a):
    lo = lax.bitcast_convert_type(a << 16, F32).astype(BF16)
    hi = lax.bitcast_convert_type(a & jnp.uint32(_HI16), F32).astype(BF16)
    return jnp.concatenate([lo, hi], axis=1)


def _norm_kernel(x_ref, g_ref, *refs, modulate, route):
    if modulate:
        sh_ref, sc_ref, *refs = refs
    if route:
        wr_ref, br_ref, *refs = refs
    h = _rms(x_ref[...], g_ref[...])
    if modulate:
        h = h * (1.0 + sc_ref[...]) + sh_ref[...]
    if route:
        refs[0][...] = _pack_bf16_halves(h)
    else:
        refs[0][...] = h.astype(refs[0].dtype)
    if route:
        logits = jnp.dot(h, wr_ref[...], preferred_element_type=F32,
                         precision=lax.Precision.HIGHEST) + br_ref[...]
        lane = lax.broadcasted_iota(jnp.int32, logits.shape, 1).astype(F32)
        logits = jnp.where(lane < N_EXPERTS, logits, -jnp.inf)
        v1 = jnp.max(logits, axis=-1, keepdims=True)
        i1 = jnp.min(jnp.where(logits == v1, lane, float(LANES)), axis=-1, keepdims=True)
        rest = jnp.where(lane == i1, -jnp.inf, logits)
        v2 = jnp.max(rest, axis=-1, keepdims=True)
        i2 = jnp.min(jnp.where(rest == v2, lane, float(LANES)), axis=-1, keepdims=True)
        e = jnp.exp(v2 - v1)
        w1 = 1.0 / (1.0 + e)
        w2 = e / (1.0 + e)
        refs[1][...] = jnp.where(lane == 0, i1, jnp.where(lane == 1, i2, jnp.where(
            lane == 2, w1, jnp.where(lane == 3, w2, 0.0))))


def _norm(x2, g, B, S, shift=None, scale=None, router=None, out_dtype=BF16, ts=512):
    T, D = x2.shape
    nst = S // ts
    modulate = shift is not None
    route = router is not None
    args = [x2, g.reshape(1, D)]
    in_specs = [
        pl.BlockSpec((ts, D), lambda b, s: (b * nst + s, 0)),
        pl.BlockSpec((1, D), lambda b, s: (0, 0)),
    ]
    if modulate:
        for arr, l, j in (shift, scale):
            args.append(arr)
            in_specs.append(pl.BlockSpec((None, None, None, 1, D),
                                         lambda b, s, l=l, j=j: (l, b, j, 0, 0)))
    d_out = D // 2 if route else D
    out_shape = [jax.ShapeDtypeStruct((T, d_out), jnp.uint32 if route else out_dtype)]
    out_specs = [pl.BlockSpec((ts, d_out), lambda b, s: (b * nst + s, 0))]
    if route:
        wr, br = router
        args += [wr, br]
        in_specs += [pl.BlockSpec((D, LANES), lambda b, s: (0, 0)),
                     pl.BlockSpec((1, LANES), lambda b, s: (0, 0))]
        out_shape.append(jax.ShapeDtypeStruct((T, LANES), F32))
        out_specs.append(pl.BlockSpec((ts, LANES), lambda b, s: (b * nst + s, 0)))
    outs = pl.pallas_call(
        functools.partial(_norm_kernel, modulate=modulate, route=route),
        grid=(B, nst),
        in_specs=in_specs,
        out_specs=out_specs,
        out_shape=out_shape,
        compiler_params=_cparams(2),
        name="norm_mod",
    )(*args)
    return outs if route else outs[0]


def _gmm_kernel(te_ref, tv_ref, a_ref, *refs, mode, out_scale, a_packed):
    m = pl.program_id(1)
    first = jnp.logical_or(m == 0, te_ref[m] != te_ref[jnp.maximum(m - 1, 0)])
    valid = tv_ref[m] != 0
    if mode == "swiglu":
        wg_ref, wu_ref, o_ref, wgb, wub = refs

        @pl.when(first)
        def _():
            wgb[...] = wg_ref[...].astype(BF16)
            wub[...] = wu_ref[...].astype(BF16)

        @pl.when(valid)
        def _():
            a = _unpack_bf16_halves(a_ref[...]) if a_packed else a_ref[...]
            g = jnp.dot(a, wgb[...], preferred_element_type=F32)
            u = jnp.dot(a, wub[...], preferred_element_type=F32)
            o_ref[...] = (g * jax.nn.sigmoid(g) * u).astype(o_ref.dtype)
    else:
        if mode == "cast":
            w_ref, o_ref, wb = refs
        else:
            w_ref, x_ref, gate_ref, o_ref, wb = refs

        @pl.when(first)
        def _():
            wb[...] = w_ref[...].astype(BF16)

        @pl.when(valid)
        def _():
            acc = jnp.dot(a_ref[...], wb[...], preferred_element_type=F32)
            if out_scale is not None:
                acc = acc * out_scale
            if mode == "residual":
                acc = x_ref[...] + gate_ref[...] * acc
            o_ref[...] = acc.astype(o_ref.dtype)

    @pl.when(jnp.logical_not(valid))
    def _():
        o_ref[...] = jnp.zeros_like(o_ref)


def _gmm(a, ws, l, *, mode, tm, tn, n_out, out_dtype, tile_expert=None, tile_valid=None,
         n_off=0, xres=None, gate=None, seq=None, out_scale=None, a_packed=False):
    M, ka = a.shape
    K = ws[0].shape[2]
    mt = M // tm
    if tile_expert is None:
        tile_expert = jnp.zeros((mt,), jnp.int32)
        tile_valid = jnp.ones((mt,), jnp.int32)
    w_spec = pl.BlockSpec((None, None, K, tn), lambda n, m, te, tv: (l, te[m], 0, n + n_off))
    args = [a] + list(ws)
    in_specs = [pl.BlockSpec((tm, ka), lambda n, m, te, tv: (m, 0))] + [w_spec] * len(ws)
    aliases = {}
    if mode == "residual":
        garr, gl, gj = gate
        per_b = seq // tm
        aliases = {2 + len(args): 0}
        args += [xres, garr]
        in_specs += [
            pl.BlockSpec((tm, tn), lambda n, m, te, tv: (m, n)),
            pl.BlockSpec((None, None, None, 1, tn),
                         lambda n, m, te, tv: (gl, m // per_b, gj, 0, n)),
        ]
    return pl.pallas_call(
        functools.partial(_gmm_kernel, mode=mode, out_scale=out_scale, a_packed=a_packed),
        grid_spec=pltpu.PrefetchScalarGridSpec(
            num_scalar_prefetch=2,
            grid=(n_out // tn, mt),
            in_specs=in_specs,
            out_specs=pl.BlockSpec((tm, tn), lambda n, m, te, tv: (m, n)),
            scratch_shapes=[pltpu.VMEM((K, tn), BF16) for _ in ws],
        ),
        out_shape=jax.ShapeDtypeStruct((M, n_out), out_dtype),
        input_output_aliases=aliases,
        compiler_params=_cparams(2),
        name="gmm_" + mode,
    )(tile_expert, tile_valid, *args)


def _rope_pairs(v, cos, s1, s2):
    return v * cos + pltpu.roll(v, LANES - QK_ROPE // 2, 1) * s1 + pltpu.roll(v, QK_ROPE // 2, 1) * s2


def _mla_in_kernel(a_ref, w_ref, gq_ref, gkv_ref, cos_ref, s1_ref, s2_ref,
                   cq_ref, ckv_ref, kpe_ref, wb, *, q_lora, kv_lora):
    @pl.when(pl.program_id(0) == 0)
    def _():
        wb[...] = w_ref[...].astype(BF16)

    acc = jnp.dot(a_ref[...], wb[...], preferred_element_type=F32)
    cq_ref[...] = _rms(acc[:, :q_lora], gq_ref[...]).astype(BF16)
    ckv_ref[...] = _rms(acc[:, q_lora:q_lora + kv_lora], gkv_ref[...]).astype(BF16)
    v = acc[:, q_lora + kv_lora:]
    kpe_ref[...] = _rope_pairs(v, cos_ref[...], s1_ref[...], s2_ref[...]).astype(BF16)


def _mla_in(h, w_pad, g_q, g_kv, rope_tabs, S, tm=512):
    T, D = h.shape
    q_lora, kv_lora = g_q.shape[0], g_kv.shape[0]
    n_all = w_pad.shape[1]
    nst = S // tm
    tab_spec = pl.BlockSpec((tm, LANES), lambda i: (i % nst, 0))
    return pl.pallas_call(
        functools.partial(_mla_in_kernel, q_lora=q_lora, kv_lora=kv_lora),
        grid=(T // tm,),
        in_specs=[
            pl.BlockSpec((tm, D), lambda i: (i, 0)),
            pl.BlockSpec((D, n_all), lambda i: (0, 0)),
            pl.BlockSpec((1, q_lora), lambda i: (0, 0)),
            pl.BlockSpec((1, kv_lora), lambda i: (0, 0)),
            tab_spec, tab_spec, tab_spec,
        ],
        out_specs=[
            pl.BlockSpec((tm, q_lora), lambda i: (i, 0)),
            pl.BlockSpec((tm, kv_lora), lambda i: (i, 0)),
            pl.BlockSpec((tm, LANES), lambda i: (i, 0)),
        ],
        out_shape=[
            jax.ShapeDtypeStruct((T, q_lora), BF16),
            jax.ShapeDtypeStruct((T, kv_lora), BF16),
            jax.ShapeDtypeStruct((T, LANES), BF16),
        ],
        scratch_shapes=[pltpu.VMEM((D, n_all), BF16)],
        compiler_params=_cparams(1),
        name="mla_in",
    )(h, w_pad, g_q.reshape(1, -1), g_kv.reshape(1, -1), *rope_tabs)


def _mla_attn_kernel(qn_ref, qp_ref, kn_ref, v_ref, kpe_ref, cos_ref, s1_ref, s2_ref, o_ref,
                     m_sc, l_sc, acc_sc, *, tq):
    qi = pl.program_id(2)
    qp = _rope_pairs(qp_ref[...].astype(F32), cos_ref[...], s1_ref[...], s2_ref[...])
    lane = lax.broadcasted_iota(jnp.int32, qp.shape, 1)
    qs = []
    for hh in range(2):
        keep = (lane < QK_ROPE) if hh == 0 else (lane >= QK_ROPE)
        qs.append(jnp.concatenate(
            [qn_ref[:, hh * QK_NOPE:(hh + 1) * QK_NOPE], jnp.where(keep, qp, 0.0).astype(BF16)],
            axis=1))
    m_sc[...] = jnp.full_like(m_sc, NEG)
    l_sc[...] = jnp.zeros_like(l_sc)
    acc_sc[...] = jnp.zeros_like(acc_sc)

    def step(j, diagonal):
        ks = pl.multiple_of(j * tq, tq)
        kpe = kpe_ref[pl.ds(ks, tq), :]
        for hh in range(2):
            k = jnp.concatenate([kn_ref[pl.ds(ks, tq), hh * QK_NOPE:(hh + 1) * QK_NOPE], kpe],
                                axis=1)
            s = _dot_nt(qs[hh], k)
            if diagonal:
                row = lax.broadcasted_iota(jnp.int32, (tq, tq), 0)
                col = lax.broadcasted_iota(jnp.int32, (tq, tq), 1)
                s = jnp.where(col <= row, s, NEG)
            m_old = m_sc[hh]
            m_new = jnp.maximum(m_old, jnp.max(s, axis=-1, keepdims=True))
            alpha = jnp.exp2(m_old - m_new)
            p = jnp.exp2(s - jnp.concatenate([m_new] * (tq // LANES), axis=1))
            l_sc[hh] = alpha * l_sc[hh] + jnp.sum(p, axis=-1, keepdims=True)
            vv = v_ref[pl.ds(ks, tq), hh * V_HEAD:(hh + 1) * V_HEAD]
            acc_sc[hh] = alpha * acc_sc[hh] + jnp.dot(p.astype(BF16), vv,
                                                      preferred_element_type=F32)
            m_sc[hh] = m_new

    def body(j, carry):
        step(j, False)
        return carry

    lax.fori_loop(0, qi, body, 0)
    step(qi, True)
    for hh in range(2):
        o_ref[:, hh * V_HEAD:(hh + 1) * V_HEAD] = (acc_sc[hh] / l_sc[hh]).astype(o_ref.dtype)


def _mla_attn(q_all, kv_all, kpe, rope_tabs, B, S, tq=512):
    T = q_all.shape[0]
    H = MLA_HEADS
    nqt = S // tq
    pair_w = 2 * QK_NOPE
    n_pairs = H // 2
    tab_spec = pl.BlockSpec((tq, LANES), lambda b, p, i: (i, 0))
    return pl.pallas_call(
        functools.partial(_mla_attn_kernel, tq=tq),
        grid=(B, n_pairs, nqt),
        in_specs=[
            pl.BlockSpec((tq, pair_w), lambda b, p, i: (b * nqt + i, p)),
            pl.BlockSpec((tq, LANES), lambda b, p, i: (b * nqt + i, (H * QK_NOPE) // LANES + p)),
            pl.BlockSpec((S, pair_w), lambda b, p, i: (b, p)),
            pl.BlockSpec((S, pair_w), lambda b, p, i: (b, n_pairs + p)),
            pl.BlockSpec((S, LANES), lambda b, p, i: (b, 0)),
            tab_spec, tab_spec, tab_spec,
        ],
        out_specs=pl.BlockSpec((tq, pair_w), lambda b, p, i: (b * nqt + i, p)),
        out_shape=jax.ShapeDtypeStruct((T, H * V_HEAD), BF16),
        scratch_shapes=[pltpu.VMEM((2, tq, LANES), F32), pltpu.VMEM((2, tq, LANES), F32),
                        pltpu.VMEM((2, tq, V_HEAD), F32)],
        compiler_params=_cparams(3),
        name="mla_attn",
    )(q_all, q_all, kv_all, kv_all, kpe, *rope_tabs)


def _nsa_compress_kernel(t_ref, pe_ref, w1_ref, w2_ref, o_ref, *, n_chunk):
    half = CMP_BLOCK // 2
    pe = pe_ref[...]

    def chunk_rows(off):
        cols = []
        for l in range(half):
            x = t_ref[pl.ds(l, n_chunk, stride=CMP_STRIDE), :] + pe[off + l:off + l + 1, :]
            cols.append(x.astype(BF16))
        return jnp.concatenate(cols, axis=1)

    w1 = w1_ref[...].astype(BF16)
    kdim = half * NSA_DK
    p0 = jnp.dot(chunk_rows(0), w1[:kdim], preferred_element_type=F32)
    p1 = jnp.dot(chunk_rows(half), w1[kdim:], preferred_element_type=F32)
    pre = p0 + pltpu.roll(p1, n_chunk - 1, 0)
    hid = (pre * jax.nn.sigmoid(pre)).astype(BF16)
    out = jnp.dot(hid, w2_ref[...].astype(BF16), preferred_element_type=F32)
    rown = lax.broadcasted_iota(jnp.int32, out.shape, 0)
    o_ref[...] = jnp.where(rown < n_chunk - 1, out, 0.0).astype(o_ref.dtype)


def _nsa_compress(kc, pe, w1, w2, B, S):
    G = NSA_GROUPS
    n_chunk = S // CMP_STRIDE
    return pl.pallas_call(
        functools.partial(_nsa_compress_kernel, n_chunk=n_chunk),
        grid=(B, 2, G),
        in_specs=[
            pl.BlockSpec((S, NSA_DK), lambda b, i, g: (b, i * G + g)),
            pl.BlockSpec((None, CMP_BLOCK, NSA_DK), lambda b, i, g: (i, 0, 0)),
            pl.BlockSpec((None, CMP_BLOCK * NSA_DK, NSA_DK), lambda b, i, g: (i, 0, 0)),
            pl.BlockSpec((None, NSA_DK, NSA_DK), lambda b, i, g: (i, 0, 0)),
        ],
        out_specs=pl.BlockSpec((None, None, None, n_chunk, NSA_DK), lambda b, i, g: (b, i, g, 0, 0)),
        out_shape=jax.ShapeDtypeStruct((B, 2, G, n_chunk, NSA_DK), BF16),
        compiler_params=_cparams(3),
        name="nsa_compress",
    )(kc, pe, w1, w2)


def _nsa_attn_kernel(q_ref, gt_ref, ks_ref, vs_ref, kw_ref, vw_ref, kc_ref, vc_ref, qtab_ref,
                     ktab_ref, ctab_ref, mt_ref, o_ref, m_sc, l_sc, acc_sc,
                     *, tq, tk, n_cmp, n_sel):
    qi = pl.program_id(2)
    t0 = qi * tq
    HP = NSA_HPG
    R = HP * tq
    qb = q_ref[...]
    qs = jnp.concatenate([qb[:, h * NSA_DK:(h + 1) * NSA_DK] for h in range(HP)], axis=0)
    qtab = qtab_ref[...]
    qa = jnp.concatenate([qs, qtab], axis=1)
    rowi = lax.broadcasted_iota(jnp.int32, (R, 1), 0)
    tcol = (t0 + rowi % tq).astype(F32)

    ncol = lax.broadcasted_iota(jnp.int32, (1, kc_ref.shape[0]), 1)
    dist = tcol - (ncol * CMP_STRIDE + (CMP_BLOCK - 1)).astype(F32)
    valid = jnp.logical_and(dist >= 0, ncol < n_cmp)
    kc = jnp.concatenate([kc_ref[...], ctab_ref[...]], axis=1)
    s = jnp.where(valid, _dot_nt(qa, kc), NEG)
    p = jnp.exp2(s - jnp.max(s, axis=-1, keepdims=True))
    l = jnp.sum(p, axis=-1, keepdims=True)
    p_cmp = jnp.where(tcol >= CMP_BLOCK - 1, p / l, 0.0)
    o_cmp = jnp.dot(p_cmp.astype(BF16), vc_ref[...], preferred_element_type=F32)

    ps = p_cmp[0:tq]
    for h in range(1, HP):
        ps = ps + p_cmp[h * tq:(h + 1) * tq]
    ps_hi = ps.astype(BF16)
    ps_lo = (ps - ps_hi.astype(F32)).astype(BF16)
    imp = _dot_nt(mt_ref[...], ps_hi) + _dot_nt(mt_ref[...], ps_lo)
    jrow = lax.broadcasted_iota(jnp.int32, (n_sel, tq), 0)
    blk_t = (t0 + lax.broadcasted_iota(jnp.int32, (n_sel, tq), 1)) // SEL_BLOCK
    forced = (jrow == 0) | (jrow == blk_t) | (jrow == blk_t - 1)
    imp = jnp.where(forced, FORCED_SCORE, imp)
    imp = jnp.where(jrow > blk_t, -1.0, imp)
    rank = jnp.zeros((n_sel, tq), F32)
    for k in range(n_sel):
        rk = imp[k:k + 1, :]
        beats = (rk > imp) | ((rk == imp) & (jrow > k))
        rank = rank + jnp.where(beats, 1.0, 0.0)
    sel_t = jnp.where(rank < SEL_TOPN, 1.0, 0.0)
    sel_t = jnp.concatenate([sel_t, jnp.zeros((LANES - n_sel, tq), F32)], axis=0)
    sel_neg = ((sel_t.T - 1.0) * MASK_BIG).astype(BF16)
    lane = lax.broadcasted_iota(jnp.int32, (R, LANES), 1)
    qx = jnp.where(lane < SEL_LANES, jnp.concatenate([sel_neg] * HP, axis=0), qtab)
    qsel = jnp.concatenate([qs, qx], axis=1)

    m_sc[...] = jnp.full_like(m_sc, NEG)
    l_sc[...] = jnp.zeros_like(l_sc)
    acc_sc[...] = jnp.zeros_like(acc_sc)

    def tile(j, causal):
        ks = pl.multiple_of(j * tk, tk)
        k = jnp.concatenate([ks_ref[pl.ds(ks, tk), :], ktab_ref[pl.ds(ks, tk), :]], axis=1)
        sc = _dot_nt(qsel, k)
        if causal:
            kpos = (ks + lax.broadcasted_iota(jnp.int32, (1, tk), 1)).astype(F32)
            sc = jnp.where(kpos <= tcol, sc, NEG)
        m_old = m_sc[...]
        m_new = jnp.maximum(m_old, jnp.max(sc, axis=-1, keepdims=True))
        alpha = jnp.exp2(m_old - m_new)
        pp = jnp.exp2(sc - jnp.concatenate([m_new] * (tk // LANES), axis=1))
        l_sc[...] = alpha * l_sc[...] + jnp.sum(pp, axis=-1, keepdims=True)
        acc_sc[...] = alpha * acc_sc[...] + jnp.dot(
            pp.astype(BF16), vs_ref[pl.ds(ks, tk), :], preferred_element_type=F32)
        m_sc[...] = m_new

    def body(j, carry):
        tile(j, False)
        return carry

    n_full = t0 // tk
    lax.fori_loop(0, n_full, body, 0)
    tile(n_full, True)
    o_sel = acc_sc[...] / l_sc[...]

    span = WINDOW + tq
    ws = pl.multiple_of(jnp.maximum(t0 - WINDOW, 0), tq)
    kpos = (ws + lax.broadcasted_iota(jnp.int32, (1, span), 1)).astype(F32)
    d = tcol - kpos
    ok = jnp.logical_and(d >= 0, d < WINDOW)
    kw = jnp.concatenate([kw_ref[pl.ds(ws, span), :], ktab_ref[pl.ds(ws, span), :]], axis=1)
    sw = jnp.where(ok, _dot_nt(qa, kw), NEG)
    pw = jnp.exp2(sw - jnp.max(sw, axis=-1, keepdims=True))
    lw = jnp.sum(pw, axis=-1, keepdims=True)
    o_win = jnp.dot(pw.astype(BF16), vw_ref[pl.ds(ws, span), :], preferred_element_type=F32) / lw

    gt = jax.nn.sigmoid(gt_ref[...])

    def gcol(i):
        return jnp.concatenate([gt[:, i * HP + h:i * HP + h + 1] for h in range(HP)], axis=0)

    o = gcol(0) * o_cmp + gcol(1) * o_sel + gcol(2) * o_win
    o_ref[...] = jnp.concatenate([o[h * tq:(h + 1) * tq] for h in range(HP)], axis=1).astype(o_ref.dtype)


def _nsa_attn(q, gates, kvb, kvc, tabs, B, S, tq=128, tk=256):
    T = q.shape[0]
    G, DK, HP = NSA_GROUPS, NSA_DK, NSA_HPG
    qtab, ktab, ctab, mt = tabs
    nqt = S // tq
    n_chunk = kvc.shape[3]
    n_cmp = (S - CMP_BLOCK) // CMP_STRIDE + 1
    n_sel = S // SEL_BLOCK

    def kv_spec(i):
        return pl.BlockSpec((S, DK), lambda b, g, t, i=i: (b, i * G + g))

    def cmp_spec(i):
        return pl.BlockSpec((None, None, None, n_chunk, DK), lambda b, g, t, i=i: (b, i, g, 0, 0))

    return pl.pallas_call(
        functools.partial(_nsa_attn_kernel, tq=tq, tk=tk, n_cmp=n_cmp, n_sel=n_sel),
        grid=(B, G, nqt),
        in_specs=[
            pl.BlockSpec((tq, HP * DK), lambda b, g, t: (b * nqt + t, g)),
            pl.BlockSpec((tq, LANES), lambda b, g, t: (b * nqt + t, g)),
            kv_spec(0), kv_spec(1), kv_spec(2), kv_spec(3),
            cmp_spec(0), cmp_spec(1),
            pl.BlockSpec((None, HP * tq, LANES), lambda b, g, t: (g, 0, 0)),
            pl.BlockSpec(ktab.shape, lambda b, g, t: (0, 0)),
            pl.BlockSpec(ctab.shape, lambda b, g, t: (0, 0)),
            pl.BlockSpec(mt.shape, lambda b, g, t: (0, 0)),
        ],
        out_specs=pl.BlockSpec((tq, HP * DK), lambda b, g, t: (b * nqt + t, g)),
        out_shape=jax.ShapeDtypeStruct((T, NSA_HEADS * DK), BF16),
        scratch_shapes=[pltpu.VMEM((HP * tq, LANES), F32), pltpu.VMEM((HP * tq, LANES), F32),
                        pltpu.VMEM((HP * tq, DK), F32)],
        compiler_params=_cparams(3),
        name="nsa_attn",
    )(q, gates, kvb, kvb, kvb, kvb, kvc, kvc, qtab, ktab, ctab, mt)


def _row_copy(src_hbm, dst_vmem, sem, src_row, dst_row):
    return pltpu.make_async_copy(src_hbm.at[pl.ds(src_row, 1), :], dst_vmem.at[pl.ds(dst_row, 1), :], sem)


def _dispatch_kernel(pos_ref, src_hbm, init_hbm, out_hbm, sem, *, chunk, n_tok, n_steps):
    del init_hbm
    i = pl.program_id(0)
    base = i * chunk

    def start(r, c):
        t = base + r
        _row_copy(src_hbm, out_hbm, sem, t, pos_ref[t]).start()
        _row_copy(src_hbm, out_hbm, sem, t, pos_ref[n_tok + t]).start()
        return c

    def wait(r, c):
        _row_copy(src_hbm, out_hbm, sem, 0, 0).wait()
        return c

    lax.fori_loop(0, chunk, start, 0, unroll=8)

    @pl.when(i > 0)
    def _():
        lax.fori_loop(0, 2 * chunk, wait, 0, unroll=8)

    @pl.when(i == n_steps - 1)
    def _():
        lax.fori_loop(0, 2 * chunk, wait, 0, unroll=8)


def _dispatch(h_packed, pos, n_rows, chunk=512):
    T, W = h_packed.shape
    n_steps = T // chunk
    init = jnp.zeros((n_rows, W), h_packed.dtype)
    return pl.pallas_call(
        functools.partial(_dispatch_kernel, chunk=chunk, n_tok=T, n_steps=n_steps),
        grid_spec=pltpu.PrefetchScalarGridSpec(
            num_scalar_prefetch=1,
            grid=(n_steps,),
            in_specs=[pl.BlockSpec(memory_space=pl.ANY), pl.BlockSpec(memory_space=pl.ANY)],
            out_specs=pl.BlockSpec(memory_space=pl.ANY),
            scratch_shapes=[pltpu.SemaphoreType.DMA(())],
        ),
        out_shape=jax.ShapeDtypeStruct((n_rows, W), h_packed.dtype),
        input_output_aliases={2: 0},
        compiler_params=_cparams(1),
        name="moe_dispatch",
    )(pos, h_packed, init)


def _combine_kernel(pos_ref, y_hbm, x_ref, gate_ref, rt_ref, o_ref, buf, sem,
                    *, tg, n_tok, n_steps):
    i = pl.program_id(0)
    slot = i % 2

    def issue(step, slot_):
        def start(r, c):
            t = step * tg + r
            _row_copy(y_hbm, buf.at[slot_, 0], sem.at[slot_], pos_ref[t], r).start()
            _row_copy(y_hbm, buf.at[slot_, 1], sem.at[slot_], pos_ref[n_tok + t], r).start()
            return c

        lax.fori_loop(0, tg, start, 0, unroll=8)

    @pl.when(i == 0)
    def _():
        issue(0, 0)

    @pl.when(i + 1 < n_steps)
    def _():
        issue(i + 1, 1 - slot)

    def wait(r, c):
        _row_copy(y_hbm, buf.at[slot, 0], sem.at[slot], 0, 0).wait()
        return c

    lax.fori_loop(0, 2 * tg, wait, 0, unroll=8)
    rt = rt_ref[...]
    y = rt[:, 2:3] * buf[slot, 0] + rt[:, 3:4] * buf[slot, 1]
    o_ref[...] = x_ref[...] + gate_ref[...] * y


def _combine(x2, y_sorted, pos, route, gate, S, tg=256):
    T, D = x2.shape
    garr, gl, gj = gate
    per_b = S // tg
    n_steps = T // tg
    return pl.pallas_call(
        functools.partial(_combine_kernel, tg=tg, n_tok=T, n_steps=n_steps),
        grid_spec=pltpu.PrefetchScalarGridSpec(
            num_scalar_prefetch=1,
            grid=(n_steps,),
            in_specs=[
                pl.BlockSpec(memory_space=pl.ANY),
                pl.BlockSpec((tg, D), lambda i, pos: (i, 0)),
                pl.BlockSpec((None, None, None, 1, D), lambda i, pos: (gl, i // per_b, gj, 0, 0)),
                pl.BlockSpec((tg, LANES), lambda i, pos: (i, 0)),
            ],
            out_specs=pl.BlockSpec((tg, D), lambda i, pos: (i, 0)),
            scratch_shapes=[pltpu.VMEM((2, 2, tg, D), F32), pltpu.SemaphoreType.DMA((2,))],
        ),
        out_shape=jax.ShapeDtypeStruct((T, D), F32),
        input_output_aliases={2: 0},
        compiler_params=_cparams(1),
        name="moe_combine",
    )(pos, y_sorted, x2, garr, route)


def _moe_plan(route, T, tm):
    E = N_EXPERTS
    n_tiles = (2 * T) // tm + E
    P = n_tiles * tm
    e_pair = jnp.concatenate([route[:, 0], route[:, 1]]).astype(jnp.int32)
    onehot = (e_pair[:, None] == jnp.arange(E, dtype=jnp.int32)[None, :]).astype(jnp.int32)
    csum = jnp.cumsum(onehot, axis=0)
    rank = jnp.sum(onehot * (csum - 1), axis=1)
    counts = csum[-1]
    padded = ((counts + tm - 1) // tm) * tm
    ends = jnp.cumsum(padded)
    pos = (ends - padded)[e_pair] + rank
    tile_start = jnp.arange(n_tiles, dtype=jnp.int32) * tm
    tile_valid = (tile_start < ends[-1]).astype(jnp.int32)
    tile_expert = jnp.sum((tile_start[:, None] >= ends[None, :]).astype(jnp.int32), axis=1)
    last_valid = tile_expert[jnp.maximum(ends[-1] // tm - 1, 0)]
    tile_expert = jnp.where(tile_valid == 1, tile_expert, last_valid).astype(jnp.int32)
    return pos.astype(jnp.int32), P, tile_expert, tile_valid


def _rope_tables(S):
    d = QK_ROPE
    inv = ROPE_THETA ** (-jnp.arange(0, d, 2, dtype=F32) / d)
    ang = jnp.arange(S).astype(F32)[:, None] * inv[None, :]
    cos, sin = jnp.cos(ang), jnp.sin(ang)
    z = jnp.zeros_like(sin)
    return (jnp.concatenate([cos, cos, cos, cos], axis=1),
            jnp.concatenate([-sin, z, -sin, z], axis=1),
            jnp.concatenate([z, sin, z, sin], axis=1))


def _pos_columns(pos):
    tab = np.zeros((pos.shape[0], LANES), np.float32)
    tab[:, POS_HI_LANE:POS_HI_LANE + 3] = (LANES * (pos // LANES))[:, None]
    tab[:, POS_LO_LANE:POS_LO_LANE + 3] = (pos % LANES)[:, None]
    return tab


def _nsa_tables(S, tq):
    n_cmp = (S - CMP_BLOCK) // CMP_STRIDE + 1
    n_sel = S // SEL_BLOCK
    n_chunk = S // CMP_STRIDE
    tok = np.arange(n_cmp)[:, None] * CMP_STRIDE + np.arange(CMP_BLOCK)[None, :]
    blk = tok // SEL_BLOCK
    m = (blk[:, :, None] == np.arange(n_sel)[None, None, :]).sum(axis=1) / CMP_BLOCK
    mt = np.zeros((n_sel, n_chunk), np.float32)
    mt[:, :n_cmp] = m.T
    keys = np.arange(S)
    ktab = _pos_columns(keys)
    ktab[:, :SEL_LANES] = (keys[:, None] // SEL_BLOCK == np.arange(SEL_LANES)[None, :])
    ctab = _pos_columns(np.arange(n_chunk) * CMP_STRIDE + (CMP_BLOCK - 1))
    slopes = jnp.asarray(2.0 ** (-8.0 * np.arange(1, NSA_HEADS + 1) / NSA_HEADS), F32)
    a = slopes * LOG2E
    a_hi = a.astype(BF16)
    r1 = a - a_hi.astype(F32)
    a_mid = r1.astype(BF16)
    a_lo = (r1 - a_mid.astype(F32)).astype(BF16)
    pieces = jnp.stack([a_hi, a_mid, a_lo], axis=-1)
    qrow = jnp.zeros((NSA_HEADS, LANES), BF16)
    qrow = qrow.at[:, POS_HI_LANE:POS_HI_LANE + 3].set(pieces)
    qrow = qrow.at[:, POS_LO_LANE:POS_LO_LANE + 3].set(pieces)
    qtab = jnp.repeat(qrow.reshape(NSA_GROUPS, NSA_HPG, LANES), tq, axis=1)
    return qtab, jnp.asarray(ktab, BF16), jnp.asarray(ctab, BF16), jnp.asarray(mt, BF16)


def kernel(x, c, ada_w, ada_b, norm1_g, norm2_g, mla_w_in, mla_g_q, mla_g_kv, mla_w_uq, mla_w_ukv, mla_w_o, kv_ada_w, kv_ada_b, kv_norm_g, nsa_w_kv, cmp_pos_k, cmp_pos_v, cmp_k_w1, cmp_k_w2, cmp_v_w1, cmp_v_w2, nsa_w_in, nsa_w_o, ffn_w_gate, ffn_w_up, ffn_w_down, moe_w_router, moe_b_router, moe_w_gate, moe_w_up, moe_w_down, final_g):
    B, S, D = x.shape
    T = B * S
    depth = ada_w.shape[0]
    n_a = mla_w_in.shape[0]
    H = MLA_HEADS
    G, HP, DK = NSA_GROUPS, NSA_HPG, NSA_DK
    d_ff = ffn_w_gate.shape[-1]
    nsa_tq = 256

    c_pad = jnp.zeros((8, D), F32).at[:B].set(c)
    mod = _modulation(c_pad, ada_w, ada_b)[:, :B].reshape(depth, B, 6, 1, D)
    kv_mod = _modulation(c_pad, kv_ada_w[None], kv_ada_b[None])[:, :B].reshape(1, B, 2, 1, D)

    rope_tabs = _rope_tables(S)
    nsa_tabs = _nsa_tables(S, nsa_tq)

    ffn_wg = ffn_w_gate[:, None]
    ffn_wu = ffn_w_up[:, None]
    ffn_wd = ffn_w_down[:, None]

    x2 = x.reshape(T, D)
    shared = None
    for l in range(depth):
        h = _norm(x2, norm1_g[l], B, S, shift=(mod, l, 0), scale=(mod, l, 1))
        if l < n_a:
            w_in = mla_w_in[l]
            q_lora, kv_lora = mla_g_q.shape[1], mla_g_kv.shape[1]
            w_pad = jnp.concatenate([w_in, w_in[:, q_lora + kv_lora:]], axis=1)
            cq, ckv, kpe = _mla_in(h, w_pad, mla_g_q[l], mla_g_kv[l], rope_tabs, S)
            wq = mla_w_uq[l].reshape(q_lora, H, QK_NOPE + QK_ROPE)
            wq = jnp.concatenate([wq[:, :, :QK_NOPE].reshape(q_lora, H * QK_NOPE),
                                  wq[:, :, QK_NOPE:].reshape(q_lora, H * QK_ROPE)], axis=1)
            wkv = mla_w_ukv[l].reshape(kv_lora, H, QK_NOPE + V_HEAD)
            wkv = jnp.concatenate([wkv[:, :, :QK_NOPE].reshape(kv_lora, H * QK_NOPE),
                                   wkv[:, :, QK_NOPE:].reshape(kv_lora, H * V_HEAD)], axis=1)
            q_all = _gmm(cq, [wq[None, None]], 0, mode="cast", tm=2048, tn=1024,
                         n_out=wq.shape[1], out_dtype=BF16,
                         out_scale=(QK_NOPE + QK_ROPE) ** -0.5 * LOG2E)
            kv_all = _gmm(ckv, [wkv[None, None]], 0, mode="cast", tm=2048, tn=1024,
                          n_out=wkv.shape[1], out_dtype=BF16)
            o = _mla_attn(q_all, kv_all, kpe, rope_tabs, B, S)
            x2 = _gmm(o, [mla_w_o[:, None]], l, mode="residual", tm=1024, tn=512, n_out=D,
                      out_dtype=F32, xres=x2, gate=(mod, l, 2), seq=S)
        else:
            jb = l - n_a
            w_in = nsa_w_in[jb]
            q = _gmm(h, [nsa_w_in[:, None]], jb, mode="cast", tm=1024, tn=512,
                     n_out=NSA_HEADS * DK, out_dtype=BF16, out_scale=DK ** -0.5 * LOG2E)
            wg = w_in[:, NSA_HEADS * DK:].reshape(D, G, HP, 3).transpose(0, 1, 3, 2)
            wg = jnp.pad(wg.reshape(D, G, 3 * HP), ((0, 0), (0, 0), (0, LANES - 3 * HP)))
            gates = _gmm(h, [wg.reshape(1, 1, D, G * LANES)], 0, mode="cast", tm=1024, tn=512,
                         n_out=G * LANES, out_dtype=F32)
            kvb, kvc = shared
            o = _nsa_attn(q, gates, kvb, kvc, nsa_tabs, B, S, tq=nsa_tq)
            x2 = _gmm(o, [nsa_w_o[:, None]], jb, mode="residual", tm=1024, tn=512, n_out=D,
                      out_dtype=F32, xres=x2, gate=(mod, l, 2), seq=S)

        if l % 2 == 0:
            h = _norm(x2, norm2_g[l], B, S, shift=(mod, l, 3), scale=(mod, l, 4))
            hid = _gmm(h, [ffn_wg, ffn_wu], l // 2, mode="swiglu", tm=1024, tn=512, n_out=d_ff,
                       out_dtype=BF16)
            x2 = _gmm(hid, [ffn_wd], l // 2, mode="residual", tm=512, tn=512, n_out=D,
                      out_dtype=F32, xres=x2, gate=(mod, l, 5), seq=S)
        else:
            li = l // 2
            wr = jnp.pad(moe_w_router[li], ((0, 0), (0, LANES - N_EXPERTS)))
            br = jnp.pad(moe_b_router[li], (0, LANES - N_EXPERTS)).reshape(1, LANES)
            hp, route = _norm(x2, norm2_g[l], B, S, shift=(mod, l, 3), scale=(mod, l, 4),
                              router=(wr, br))
            pos, n_rows, tile_expert, tile_valid = _moe_plan(route, T, MOE_TM)
            hs = _dispatch(hp, pos, n_rows)
            hid = _gmm(hs, [moe_w_gate, moe_w_up], li, mode="swiglu", tm=MOE_TM, tn=512,
                       n_out=d_ff, out_dtype=BF16, tile_expert=tile_expert, tile_valid=tile_valid,
                       a_packed=True)
            ys = _gmm(hid, [moe_w_down], li, mode="cast", tm=MOE_TM, tn=512, n_out=D,
                      out_dtype=F32, tile_expert=tile_expert, tile_valid=tile_valid)
            x2 = _combine(x2, ys, pos, route, (mod, l, 5), S)

        if l == n_a - 1:
            hkv = _norm(x2, kv_norm_g, B, S, shift=(kv_mod, 0, 0), scale=(kv_mod, 0, 1))
            w_kv = nsa_w_kv[None, None]
            kc = _gmm(hkv, [w_kv], 0, mode="cast", tm=1024, tn=512, n_out=2 * G * DK,
                      out_dtype=F32)
            kvb = _gmm(hkv, [w_kv], 0, mode="cast", tm=1024, tn=512, n_out=4 * G * DK,
                       out_dtype=BF16, n_off=(2 * G * DK) // 512)
            kvc = _nsa_compress(kc, jnp.stack([cmp_pos_k, cmp_pos_v]),
                                jnp.stack([cmp_k_w1, cmp_v_w1]), jnp.stack([cmp_k_w2, cmp_v_w2]),
                                B, S)
            shared = (kvb, kvc)

    out = _norm(x2, final_g, B, S, out_dtype=F32)
    return out.reshape(B, S, D)
```

```python
import functools

import numpy as np
import jax
import jax.numpy as jnp
from jax import lax
from jax.experimental import pallas as pl
from jax.experimental.pallas import tpu as pltpu

F32 = jnp.float32
BF16 = jnp.bfloat16

EPS = 1e-6
NEG = -1e30
FORCED_SCORE = 1e6
LOG2E = 1.4426950408889634

MLA_HEADS = 16
QK_NOPE = 128
QK_ROPE = 64
V_HEAD = 128
ROPE_THETA = 10000.0

NSA_HEADS = 16
NSA_GROUPS = 4
NSA_HPG = NSA_HEADS // NSA_GROUPS
NSA_DK = 128
CMP_BLOCK = 32
CMP_STRIDE = 16
SEL_BLOCK = 64
SEL_TOPN = 16
WINDOW = 512

N_EXPERTS = 8
LANES = 128
VMEM_LIMIT = 56 * 1024 * 1024

MOE_TM = 512

SEL_LANES = 32
POS_HI_LANE = 32
POS_LO_LANE = 35
MASK_BIG = 2.0 ** 30


def _cparams(n_axes):
    return pltpu.CompilerParams(
        dimension_semantics=("arbitrary",) * n_axes, vmem_limit_bytes=VMEM_LIMIT)


def _dot_nt(a, b):
    return lax.dot_general(a, b, (((1,), (1,)), ((), ())), preferred_element_type=F32)


def _mod_kernel(c_ref, w_ref, b_ref, o_ref):
    c = c_ref[...]
    ca = (c * jax.nn.sigmoid(c)).astype(BF16)
    o_ref[...] = jnp.dot(ca, w_ref[...].astype(BF16), preferred_element_type=F32) + b_ref[...]


def _modulation(c_pad, w, b, tn=1024):
    L, D, N = w.shape
    return pl.pallas_call(
        _mod_kernel,
        grid=(L, N // tn),
        in_specs=[
            pl.BlockSpec((8, D), lambda l, n: (0, 0)),
            pl.BlockSpec((None, D, tn), lambda l, n: (l, 0, n)),
            pl.BlockSpec((None, 1, tn), lambda l, n: (l, 0, n)),
        ],
        out_specs=pl.BlockSpec((None, 8, tn), lambda l, n: (l, 0, n)),
        out_shape=jax.ShapeDtypeStruct((L, 8, N), F32),
        compiler_params=_cparams(2),
        name="adaln_mod",
    )(c_pad, w, b.reshape(L, 1, N))


def _rms(x, g):
    return x * lax.rsqrt(jnp.mean(x * x, axis=-1, keepdims=True) + EPS) * g


_HI16 = 0xFFFF0000


def _pack_bf16_halves(h):
    half = h.shape[1] // 2
    bits = lax.bitcast_convert_type(h.astype(BF16).astype(F32), jnp.uint32)
    return (bits[:, :half] >> 16) | (bits[:, half:] & jnp.uint32(_HI16))


def _unpack_bf16_halves(a):
    lo = lax.bitcast_convert_type(a << 16, F32).astype(BF16)
    hi = lax.bitcast_convert_type(a & jnp.uint32(_HI16), F32).astype(BF16)
    return jnp.concatenate([lo, hi], axis=1)


def _norm_kernel(x_ref, g_ref, *refs, modulate, route):
    if modulate:
        sh_ref, sc_ref, *refs = refs
    if route:
        wr_ref, br_ref, *refs = refs
    h = _rms(x_ref[...], g_ref[...])
    if modulate:
        h = h * (1.0 + sc_ref[...]) + sh_ref[...]
    if route:
        refs[0][...] = _pack_bf16_halves(h)
    else:
        refs[0][...] = h.astype(refs[0].dtype)
    if route:
        logits = jnp.dot(h, wr_ref[...], preferred_element_type=F32,
                         precision=lax.Precision.HIGHEST) + br_ref[...]
        lane = lax.broadcasted_iota(jnp.int32, logits.shape, 1).astype(F32)
        logits = jnp.where(lane < N_EXPERTS, logits, -jnp.inf)
        v1 = jnp.max(logits, axis=-1, keepdims=True)
        i1 = jnp.min(jnp.where(logits == v1, lane, float(LANES)), axis=-1, keepdims=True)
        rest = jnp.where(lane == i1, -jnp.inf, logits)
        v2 = jnp.max(rest, axis=-1, keepdims=True)
        i2 = jnp.min(jnp.where(rest == v2, lane, float(LANES)), axis=-1, keepdims=True)
        e = jnp.exp(v2 - v1)
        w1 = 1.0 / (1.0 + e)
        w2 = e / (1.0 + e)
        refs[1][...] = jnp.where(lane == 0, i1, jnp.where(lane == 1, i2, jnp.where(
            lane == 2, w1, jnp.where(lane == 3, w2, 0.0))))


def _norm(x2, g, B, S, shift=None, scale=None, router=None, out_dtype=BF16, ts=512):
    T, D = x2.shape
    nst = S // ts
    modulate = shift is not None
    route = router is not None
    args = [x2, g.reshape(1, D)]
    in_specs = [
        pl.BlockSpec((ts, D), lambda b, s: (b * nst + s, 0)),
        pl.BlockSpec((1, D), lambda b, s: (0, 0)),
    ]
    if modulate:
        for arr, l, j in (shift, scale):
            args.append(arr)
            in_specs.append(pl.BlockSpec((None, None, None, 1, D),
                                         lambda b, s, l=l, j=j: (l, b, j, 0, 0)))
    d_out = D // 2 if route else D
    out_shape = [jax.ShapeDtypeStruct((T, d_out), jnp.uint32 if route else out_dtype)]
    out_specs = [pl.BlockSpec((ts, d_out), lambda b, s: (b * nst + s, 0))]
    if route:
        wr, br = router
        args += [wr, br]
        in_specs += [pl.BlockSpec((D, LANES), lambda b, s: (0, 0)),
                     pl.BlockSpec((1, LANES), lambda b, s: (0, 0))]
        out_shape.append(jax.ShapeDtypeStruct((T, LANES), F32))
        out_specs.append(pl.BlockSpec((ts, LANES), lambda b, s: (b * nst + s, 0)))
    outs = pl.pallas_call(
        functools.partial(_norm_kernel, modulate=modulate, route=route),
        grid=(B, nst),
        in_specs=in_specs,
        out_specs=out_specs,
        out_shape=out_shape,
        compiler_params=_cparams(2),
        name="norm_mod",
    )(*args)
    return outs if route else outs[0]


def _gmm_kernel(te_ref, tv_ref, a_ref, *refs, mode, out_scale, a_packed):
    m = pl.program_id(1)
    first = jnp.logical_or(m == 0, te_ref[m] != te_ref[jnp.maximum(m - 1, 0)])
    valid = tv_ref[m] != 0
    if mode == "swiglu":
        wg_ref, wu_ref, o_ref, wgb, wub = refs

        @pl.when(first)
        def _():
            wgb[...] = wg_ref[...].astype(BF16)
            wub[...] = wu_ref[...].astype(BF16)

        @pl.when(valid)
        def _():
            a = _unpack_bf16_halves(a_ref[...]) if a_packed else a_ref[...]
            g = jnp.dot(a, wgb[...], preferred_element_type=F32)
            u = jnp.dot(a, wub[...], preferred_element_type=F32)
            o_ref[...] = (g * jax.nn.sigmoid(g) * u).astype(o_ref.dtype)
    else:
        if mode == "cast":
            w_ref, o_ref, wb = refs
        else:
            w_ref, x_ref, gate_ref, o_ref, wb = refs

        @pl.when(first)
        def _():
            wb[...] = w_ref[...].astype(BF16)

        @pl.when(valid)
        def _():
            acc = jnp.dot(a_ref[...], wb[...], preferred_element_type=F32)
            if out_scale is not None:
                acc = acc * out_scale
            if mode == "residual":
                acc = x_ref[...] + gate_ref[...] * acc
            o_ref[...] = acc.astype(o_ref.dtype)

    @pl.when(jnp.logical_not(valid))
    def _():
        o_ref[...] = jnp.zeros_like(o_ref)


def _gmm(a, ws, l, *, mode, tm, tn, n_out, out_dtype, tile_expert=None, tile_valid=None,
         n_off=0, xres=None, gate=None, seq=None, out_scale=None, a_packed=False):
    M, ka = a.shape
    K = ws[0].shape[2]
    mt = M // tm
    if tile_expert is None:
        tile_expert = jnp.zeros((mt,), jnp.int32)
        tile_valid = jnp.ones((mt,), jnp.int32)
    w_spec = pl.BlockSpec((None, None, K, tn), lambda n, m, te, tv: (l, te[m], 0, n + n_off))
    args = [a] + list(ws)
    in_specs = [pl.BlockSpec((tm, ka), lambda n, m, te, tv: (m, 0))] + [w_spec] * len(ws)
    aliases = {}
    if mode == "residual":
        garr, gl, gj = gate
        per_b = seq // tm
        aliases = {2 + len(args): 0}
        args += [xres, garr]
        in_specs += [
            pl.BlockSpec((tm, tn), lambda n, m, te, tv: (m, n)),
            pl.BlockSpec((None, None, None, 1, tn),
                         lambda n, m, te, tv: (gl, m // per_b, gj, 0, n)),
        ]
    return pl.pallas_call(
        functools.partial(_gmm_kernel, mode=mode, out_scale=out_scale, a_packed=a_packed),
        grid_spec=pltpu.PrefetchScalarGridSpec(
            num_scalar_prefetch=2,
            grid=(n_out // tn, mt),
            in_specs=in_specs,
            out_specs=pl.BlockSpec((tm, tn), lambda n, m, te, tv: (m, n)),
            scratch_shapes=[pltpu.VMEM((K, tn), BF16) for _ in ws],
        ),
        out_shape=jax.ShapeDtypeStruct((M, n_out), out_dtype),
        input_output_aliases=aliases,
        compiler_params=_cparams(2),
        name="gmm_" + mode,
    )(tile_expert, tile_valid, *args)


def _rope_pairs(v, cos, s1, s2):
    return v * cos + pltpu.roll(v, LANES - QK_ROPE // 2, 1) * s1 + pltpu.roll(v, QK_ROPE // 2, 1) * s2


def _mla_in_kernel(a_ref, w_ref, gq_ref, gkv_ref, cos_ref, s1_ref, s2_ref,
                   cq_ref, ckv_ref, kpe_ref, wb, *, q_lora, kv_lora):
    @pl.when(pl.program_id(0) == 0)
    def _():
        wb[...] = w_ref[...].astype(BF16)

    acc = jnp.dot(a_ref[...], wb[...], preferred_element_type=F32)
    cq_ref[...] = _rms(acc[:, :q_lora], gq_ref[...]).astype(BF16)
    ckv_ref[...] = _rms(acc[:, q_lora:q_lora + kv_lora], gkv_ref[...]).astype(BF16)
    v = acc[:, q_lora + kv_lora:]
    kpe_ref[...] = _rope_pairs(v, cos_ref[...], s1_ref[...], s2_ref[...]).astype(BF16)


def _mla_in(h, w_pad, g_q, g_kv, rope_tabs, S, tm=512):
    T, D = h.shape
    q_lora, kv_lora = g_q.shape[0], g_kv.shape[0]
    n_all = w_pad.shape[1]
    nst = S // tm
    tab_spec = pl.BlockSpec((tm, LANES), lambda i: (i % nst, 0))
    return pl.pallas_call(
        functools.partial(_mla_in_kernel, q_lora=q_lora, kv_lora=kv_lora),
        grid=(T // tm,),
        in_specs=[
            pl.BlockSpec((tm, D), lambda i: (i, 0)),
            pl.BlockSpec((D, n_all), lambda i: (0, 0)),
            pl.BlockSpec((1, q_lora), lambda i: (0, 0)),
            pl.BlockSpec((1, kv_lora), lambda i: (0, 0)),
            tab_spec, tab_spec, tab_spec,
        ],
        out_specs=[
            pl.BlockSpec((tm, q_lora), lambda i: (i, 0)),
            pl.BlockSpec((tm, kv_lora), lambda i: (i, 0)),
            pl.BlockSpec((tm, LANES), lambda i: (i, 0)),
        ],
        out_shape=[
            jax.ShapeDtypeStruct((T, q_lora), BF16),
            jax.ShapeDtypeStruct((T, kv_lora), BF16),
            jax.ShapeDtypeStruct((T, LANES), BF16),
        ],
        scratch_shapes=[pltpu.VMEM((D, n_all), BF16)],
        compiler_params=_cparams(1),
        name="mla_in",
    )(h, w_pad, g_q.reshape(1, -1), g_kv.reshape(1, -1), *rope_tabs)


def _mla_attn_kernel(qn_ref, qp_ref, kn_ref, v_ref, kpe_ref, cos_ref, s1_ref, s2_ref, o_ref,
                     m_sc, l_sc, acc_sc, *, tq):
    qi = pl.program_id(2)
    qp = _rope_pairs(qp_ref[...].astype(F32), cos_ref[...], s1_ref[...], s2_ref[...])
    lane = lax.broadcasted_iota(jnp.int32, qp.shape, 1)
    qs = []
    for hh in range(2):
        keep = (lane < QK_ROPE) if hh == 0 else (lane >= QK_ROPE)
        qs.append(jnp.concatenate(
            [qn_ref[:, hh * QK_NOPE:(hh + 1) * QK_NOPE], jnp.where(keep, qp, 0.0).astype(BF16)],
            axis=1))
    m_sc[...] = jnp.full_like(m_sc, NEG)
    l_sc[...] = jnp.zeros_like(l_sc)
    acc_sc[...] = jnp.zeros_like(acc_sc)

    def step(j, diagonal):
        ks = pl.multiple_of(j * tq, tq)
        kpe = kpe_ref[pl.ds(ks, tq), :]
        for hh in range(2):
            k = jnp.concatenate([kn_ref[pl.ds(ks, tq), hh * QK_NOPE:(hh + 1) * QK_NOPE], kpe],
                                axis=1)
            s = _dot_nt(qs[hh], k)
            if diagonal:
                row = lax.broadcasted_iota(jnp.int32, (tq, tq), 0)
                col = lax.broadcasted_iota(jnp.int32, (tq, tq), 1)
                s = jnp.where(col <= row, s, NEG)
            m_old = m_sc[hh]
            m_new = jnp.maximum(m_old, jnp.max(s, axis=-1, keepdims=True))
            alpha = jnp.exp2(m_old - m_new)
            p = jnp.exp2(s - jnp.concatenate([m_new] * (tq // LANES), axis=1))
            l_sc[hh] = alpha * l_sc[hh] + jnp.sum(p, axis=-1, keepdims=True)
            vv = v_ref[pl.ds(ks, tq), hh * V_HEAD:(hh + 1) * V_HEAD]
            acc_sc[hh] = alpha * acc_sc[hh] + jnp.dot(p.astype(BF16), vv,
                                                      preferred_element_type=F32)
            m_sc[hh] = m_new

    def body(j, carry):
        step(j, False)
        return carry

    lax.fori_loop(0, qi, body, 0)
    step(qi, True)
    for hh in range(2):
        o_ref[:, hh * V_HEAD:(hh + 1) * V_HEAD] = (acc_sc[hh] / l_sc[hh]).astype(o_ref.dtype)


def _mla_attn(q_all, kv_all, kpe, rope_tabs, B, S, tq=512):
    T = q_all.shape[0]
    H = MLA_HEADS
    nqt = S // tq
    pair_w = 2 * QK_NOPE
    n_pairs = H // 2
    tab_spec = pl.BlockSpec((tq, LANES), lambda b, p, i: (i, 0))
    return pl.pallas_call(
        functools.partial(_mla_attn_kernel, tq=tq),
        grid=(B, n_pairs, nqt),
        in_specs=[
            pl.BlockSpec((tq, pair_w), lambda b, p, i: (b * nqt + i, p)),
            pl.BlockSpec((tq, LANES), lambda b, p, i: (b * nqt + i, (H * QK_NOPE) // LANES + p)),
            pl.BlockSpec((S, pair_w), lambda b, p, i: (b, p)),
            pl.BlockSpec((S, pair_w), lambda b, p, i: (b, n_pairs + p)),
            pl.BlockSpec((S, LANES), lambda b, p, i: (b, 0)),
            tab_spec, tab_spec, tab_spec,
        ],
        out_specs=pl.BlockSpec((tq, pair_w), lambda b, p, i: (b * nqt + i, p)),
        out_shape=jax.ShapeDtypeStruct((T, H * V_HEAD), BF16),
        scratch_shapes=[pltpu.VMEM((2, tq, LANES), F32), pltpu.VMEM((2, tq, LANES), F32),
                        pltpu.VMEM((2, tq, V_HEAD), F32)],
        compiler_params=_cparams(3),
        name="mla_attn",
    )(q_all, q_all, kv_all, kv_all, kpe, *rope_tabs)


def _nsa_compress_kernel(t_ref, pe_ref, w1_ref, w2_ref, o_ref, *, n_chunk):
    half = CMP_BLOCK // 2
    pe = pe_ref[...]

    def chunk_rows(off):
        cols = []
        for l in range(half):
            x = t_ref[pl.ds(l, n_chunk, stride=CMP_STRIDE), :] + pe[off + l:off + l + 1, :]
            cols.append(x.astype(BF16))
        return jnp.concatenate(cols, axis=1)

    w1 = w1_ref[...].astype(BF16)
    kdim = half * NSA_DK
    p0 = jnp.dot(chunk_rows(0), w1[:kdim], preferred_element_type=F32)
    p1 = jnp.dot(chunk_rows(half), w1[kdim:], preferred_element_type=F32)
    pre = p0 + pltpu.roll(p1, n_chunk - 1, 0)
    hid = (pre * jax.nn.sigmoid(pre)).astype(BF16)
    out = jnp.dot(hid, w2_ref[...].astype(BF16), preferred_element_type=F32)
    rown = lax.broadcasted_iota(jnp.int32, out.shape, 0)
    o_ref[...] = jnp.where(rown < n_chunk - 1, out, 0.0).astype(o_ref.dtype)


def _nsa_compress(kc, pe, w1, w2, B, S):
    G = NSA_GROUPS
    n_chunk = S // CMP_STRIDE
    return pl.pallas_call(
        functools.partial(_nsa_compress_kernel, n_chunk=n_chunk),
        grid=(B, 2, G),
        in_specs=[
            pl.BlockSpec((S, NSA_DK), lambda b, i, g: (b, i * G + g)),
            pl.BlockSpec((None, CMP_BLOCK, NSA_DK), lambda b, i, g: (i, 0, 0)),
            pl.BlockSpec((None, CMP_BLOCK * NSA_DK, NSA_DK), lambda b, i, g: (i, 0, 0)),
            pl.BlockSpec((None, NSA_DK, NSA_DK), lambda b, i, g: (i, 0, 0)),
        ],
        out_specs=pl.BlockSpec((None, None, None, n_chunk, NSA_DK), lambda b, i, g: (b, i, g, 0, 0)),
        out_shape=jax.ShapeDtypeStruct((B, 2, G, n_chunk, NSA_DK), BF16),
        compiler_params=_cparams(3),
        name="nsa_compress",
    )(kc, pe, w1, w2)


def _nsa_attn_kernel(q_ref, gt_ref, ks_ref, vs_ref, kw_ref, vw_ref, kc_ref, vc_ref, qtab_ref,
                     ktab_ref, ctab_ref, mt_ref, o_ref, m_sc, l_sc, acc_sc,
                     *, tq, tk, n_cmp, n_sel):
    qi = pl.program_id(2)
    t0 = qi * tq
    HP = NSA_HPG
    R = HP * tq
    qb = q_ref[...]
    qs = jnp.concatenate([qb[:, h * NSA_DK:(h + 1) * NSA_DK] for h in range(HP)], axis=0)
    qtab = qtab_ref[...]
    qa = jnp.concatenate([qs, qtab], axis=1)
    rowi = lax.broadcasted_iota(jnp.int32, (R, 1), 0)
    tcol = (t0 + rowi % tq).astype(F32)

    ncol = lax.broadcasted_iota(jnp.int32, (1, kc_ref.shape[0]), 1)
    dist = tcol - (ncol * CMP_STRIDE + (CMP_BLOCK - 1)).astype(F32)
    valid = jnp.logical_and(dist >= 0, ncol < n_cmp)
    kc = jnp.concatenate([kc_ref[...], ctab_ref[...]], axis=1)
    s = jnp.where(valid, _dot_nt(qa, kc), NEG)
    p = jnp.exp2(s - jnp.max(s, axis=-1, keepdims=True))
    l = jnp.sum(p, axis=-1, keepdims=True)
    p_cmp = jnp.where(tcol >= CMP_BLOCK - 1, p / l, 0.0)
    o_cmp = jnp.dot(p_cmp.astype(BF16), vc_ref[...], preferred_element_type=F32)

    ps = p_cmp[0:tq]
    for h in range(1, HP):
        ps = ps + p_cmp[h * tq:(h + 1) * tq]
    ps_hi = ps.astype(BF16)
    ps_lo = (ps - ps_hi.astype(F32)).astype(BF16)
    imp = _dot_nt(mt_ref[...], ps_hi) + _dot_nt(mt_ref[...], ps_lo)
    jrow = lax.broadcasted_iota(jnp.int32, (n_sel, tq), 0)
    blk_t = (t0 + lax.broadcasted_iota(jnp.int32, (n_sel, tq), 1)) // SEL_BLOCK
    forced = (jrow == 0) | (jrow == blk_t) | (jrow == blk_t - 1)
    imp = jnp.where(forced, FORCED_SCORE, imp)
    imp = jnp.where(jrow > blk_t, -1.0, imp)
    rank = jnp.zeros((n_sel, tq), F32)
    for k in range(n_sel):
        rk = imp[k:k + 1, :]
        beats = (rk > imp) | ((rk == imp) & (jrow > k))
        rank = rank + jnp.where(beats, 1.0, 0.0)
    sel_t = jnp.where(rank < SEL_TOPN, 1.0, 0.0)
    sel_t = jnp.concatenate([sel_t, jnp.zeros((LANES - n_sel, tq), F32)], axis=0)
    sel_neg = ((sel_t.T - 1.0) * MASK_BIG).astype(BF16)
    lane = lax.broadcasted_iota(jnp.int32, (R, LANES), 1)
    qx = jnp.where(lane < SEL_LANES, jnp.concatenate([sel_neg] * HP, axis=0), qtab)
    qsel = jnp.concatenate([qs, qx], axis=1)

    m_sc[...] = jnp.full_like(m_sc, NEG)
    l_sc[...] = jnp.zeros_like(l_sc)
    acc_sc[...] = jnp.zeros_like(acc_sc)

    def tile(j, causal):
        ks = pl.multiple_of(j * tk, tk)
        k = jnp.concatenate([ks_ref[pl.ds(ks, tk), :], ktab_ref[pl.ds(ks, tk), :]], axis=1)
        sc = _dot_nt(qsel, k)
        if causal:
            kpos = (ks + lax.broadcasted_iota(jnp.int32, (1, tk), 1)).astype(F32)
            sc = jnp.where(kpos <= tcol, sc, NEG)
        m_old = m_sc[...]
        m_new = jnp.maximum(m_old, jnp.max(sc, axis=-1, keepdims=True))
        alpha = jnp.exp2(m_old - m_new)
        pp = jnp.exp2(sc - jnp.concatenate([m_new] * (tk // LANES), axis=1))
        l_sc[...] = alpha * l_sc[...] + jnp.sum(pp, axis=-1, keepdims=True)
        acc_sc[...] = alpha * acc_sc[...] + jnp.dot(
            pp.astype(BF16), vs_ref[pl.ds(ks, tk), :], preferred_element_type=F32)
        m_sc[...] = m_new

    def body(j, carry):
        tile(j, False)
        return carry

    n_full = t0 // tk
    lax.fori_loop(0, n_full, body, 0)
    tile(n_full, True)
    o_sel = acc_sc[...] / l_sc[...]

    span = WINDOW + tq
    ws = pl.multiple_of(jnp.maximum(t0 - WINDOW, 0), tq)
    kpos = (ws + lax.broadcasted_iota(jnp.int32, (1, span), 1)).astype(F32)
    d = tcol - kpos
    ok = jnp.logical_and(d >= 0, d < WINDOW)
    kw = jnp.concatenate([kw_ref[pl.ds(ws, span), :], ktab_ref[pl.ds(ws, span), :]], axis=1)
    sw = jnp.where(ok, _dot_nt(qa, kw), NEG)
    pw = jnp.exp2(sw - jnp.max(sw, axis=-1, keepdims=True))
    lw = jnp.sum(pw, axis=-1, keepdims=True)
    o_win = jnp.dot(pw.astype(BF16), vw_ref[pl.ds(ws, span), :], preferred_element_type=F32) / lw

    gt = jax.nn.sigmoid(gt_ref[...])

    def gcol(i):
        return jnp.concatenate([gt[:, i * HP + h:i * HP + h + 1] for h in range(HP)], axis=0)

    o = gcol(0) * o_cmp + gcol(1) * o_sel + gcol(2) * o_win
    o_ref[...] = jnp.concatenate([o[h * tq:(h + 1) * tq] for h in range(HP)], axis=1).astype(o_ref.dtype)


def _nsa_attn(q, gates, kvb, kvc, tabs, B, S, tq=128, tk=256):
    T = q.shape[0]
    G, DK, HP = NSA_GROUPS, NSA_DK, NSA_HPG
    qtab, ktab, ctab, mt = tabs
    nqt = S // tq
    n_chunk = kvc.shape[3]
    n_cmp = (S - CMP_BLOCK) // CMP_STRIDE + 1
    n_sel = S // SEL_BLOCK

    def kv_spec(i):
        return pl.BlockSpec((S, DK), lambda b, g, t, i=i: (b, i * G + g))

    def cmp_spec(i):
        return pl.BlockSpec((None, None, None, n_chunk, DK), lambda b, g, t, i=i: (b, i, g, 0, 0))

    return pl.pallas_call(
        functools.partial(_nsa_attn_kernel, tq=tq, tk=tk, n_cmp=n_cmp, n_sel=n_sel),
        grid=(B, G, nqt),
        in_specs=[
            pl.BlockSpec((tq, HP * DK), lambda b, g, t: (b * nqt + t, g)),
            pl.BlockSpec((tq, LANES), lambda b, g, t: (b * nqt + t, g)),
            kv_spec(0), kv_spec(1), kv_spec(2), kv_spec(3),
            cmp_spec(0), cmp_spec(1),
            pl.BlockSpec((None, HP * tq, LANES), lambda b, g, t: (g, 0, 0)),
            pl.BlockSpec(ktab.shape, lambda b, g, t: (0, 0)),
            pl.BlockSpec(ctab.shape, lambda b, g, t: (0, 0)),
            pl.BlockSpec(mt.shape, lambda b, g, t: (0, 0)),
        ],
        out_specs=pl.BlockSpec((tq, HP * DK), lambda b, g, t: (b * nqt + t, g)),
        out_shape=jax.ShapeDtypeStruct((T, NSA_HEADS * DK), BF16),
        scratch_shapes=[pltpu.VMEM((HP * tq, LANES), F32), pltpu.VMEM((HP * tq, LANES), F32),
                        pltpu.VMEM((HP * tq, DK), F32)],
        compiler_params=_cparams(3),
        name="nsa_attn",
    )(q, gates, kvb, kvb, kvb, kvb, kvc, kvc, qtab, ktab, ctab, mt)


def _row_copy(src_hbm, dst_vmem, sem, src_row, dst_row):
    return pltpu.make_async_copy(src_hbm.at[pl.ds(src_row, 1), :], dst_vmem.at[pl.ds(dst_row, 1), :], sem)


def _dispatch_kernel(pos_ref, src_ref, init_hbm, out_hbm, sem, *, chunk, n_tok):
    del init_hbm
    base = pl.program_id(0) * chunk

    def start(r, c):
        t = base + r
        _row_copy(src_ref, out_hbm, sem, r, pos_ref[t]).start()
        _row_copy(src_ref, out_hbm, sem, r, pos_ref[n_tok + t]).start()
        return c

    def wait(r, c):
        _row_copy(src_ref, out_hbm, sem, 0, 0).wait()
        return c

    lax.fori_loop(0, chunk, start, 0, unroll=8)
    lax.fori_loop(0, 2 * chunk, wait, 0, unroll=8)


def _dispatch(h_packed, pos, n_rows, chunk=512):
    T, W = h_packed.shape
    n_steps = T // chunk
    init = jnp.zeros((n_rows, W), h_packed.dtype)
    return pl.pallas_call(
        functools.partial(_dispatch_kernel, chunk=chunk, n_tok=T),
        grid_spec=pltpu.PrefetchScalarGridSpec(
            num_scalar_prefetch=1,
            grid=(n_steps,),
            in_specs=[pl.BlockSpec((chunk, W), lambda i, pos: (i, 0)),
                      pl.BlockSpec(memory_space=pl.ANY)],
            out_specs=pl.BlockSpec(memory_space=pl.ANY),
            scratch_shapes=[pltpu.SemaphoreType.DMA(())],
        ),
        out_shape=jax.ShapeDtypeStruct((n_rows, W), h_packed.dtype),
        input_output_aliases={2: 0},
        compiler_params=_cparams(1),
        name="moe_dispatch",
    )(pos, h_packed, init)


def _combine_kernel(pos_ref, y_hbm, x_ref, gate_ref, rt_ref, o_ref, buf, sem,
                    *, tg, n_tok, n_steps):
    i = pl.program_id(0)
    slot = i % 2

    def issue(step, slot_):
        def start(r, c):
            t = step * tg + r
            _row_copy(y_hbm, buf.at[slot_, 0], sem.at[slot_], pos_ref[t], r).start()
            _row_copy(y_hbm, buf.at[slot_, 1], sem.at[slot_], pos_ref[n_tok + t], r).start()
            return c

        lax.fori_loop(0, tg, start, 0, unroll=8)

    @pl.when(i == 0)
    def _():
        issue(0, 0)

    @pl.when(i + 1 < n_steps)
    def _():
        issue(i + 1, 1 - slot)

    def wait(r, c):
        _row_copy(y_hbm, buf.at[slot, 0], sem.at[slot], 0, 0).wait()
        return c

    lax.fori_loop(0, 2 * tg, wait, 0, unroll=8)
    rt = rt_ref[...]
    y = rt[:, 2:3] * buf[slot, 0] + rt[:, 3:4] * buf[slot, 1]
    o_ref[...] = x_ref[...] + gate_ref[...] * y


def _combine(x2, y_sorted, pos, route, gate, S, tg=256):
    T, D = x2.shape
    garr, gl, gj = gate
    per_b = S // tg
    n_steps = T // tg
    return pl.pallas_call(
        functools.partial(_combine_kernel, tg=tg, n_tok=T, n_steps=n_steps),
        grid_spec=pltpu.PrefetchScalarGridSpec(
            num_scalar_prefetch=1,
            grid=(n_steps,),
            in_specs=[
                pl.BlockSpec(memory_space=pl.ANY),
                pl.BlockSpec((tg, D), lambda i, pos: (i, 0)),
                pl.BlockSpec((None, None, None, 1, D), lambda i, pos: (gl, i // per_b, gj, 0, 0)),
                pl.BlockSpec((tg, LANES), lambda i, pos: (i, 0)),
            ],
            out_specs=pl.BlockSpec((tg, D), lambda i, pos: (i, 0)),
            scratch_shapes=[pltpu.VMEM((2, 2, tg, D), F32), pltpu.SemaphoreType.DMA((2,))],
        ),
        out_shape=jax.ShapeDtypeStruct((T, D), F32),
        input_output_aliases={2: 0},
        compiler_params=_cparams(1),
        name="moe_combine",
    )(pos, y_sorted, x2, garr, route)


def _moe_plan(route, T, tm):
    E = N_EXPERTS
    n_tiles = (2 * T) // tm + E
    P = n_tiles * tm
    e_pair = jnp.concatenate([route[:, 0], route[:, 1]]).astype(jnp.int32)
    onehot = (e_pair[:, None] == jnp.arange(E, dtype=jnp.int32)[None, :]).astype(jnp.int32)
    csum = jnp.cumsum(onehot, axis=0)
    rank = jnp.sum(onehot * (csum - 1), axis=1)
    counts = csum[-1]
    padded = ((counts + tm - 1) // tm) * tm
    ends = jnp.cumsum(padded)
    pos = (ends - padded)[e_pair] + rank
    tile_start = jnp.arange(n_tiles, dtype=jnp.int32) * tm
    tile_valid = (tile_start < ends[-1]).astype(jnp.int32)
    tile_expert = jnp.sum((tile_start[:, None] >= ends[None, :]).astype(jnp.int32), axis=1)
    last_valid = tile_expert[jnp.maximum(ends[-1] // tm - 1, 0)]
    tile_expert = jnp.where(tile_valid == 1, tile_expert, last_valid).astype(jnp.int32)
    return pos.astype(jnp.int32), P, tile_expert, tile_valid


def _rope_tables(S):
    d = QK_ROPE
    inv = ROPE_THETA ** (-jnp.arange(0, d, 2, dtype=F32) / d)
    ang = jnp.arange(S).astype(F32)[:, None] * inv[None, :]
    cos, sin = jnp.cos(ang), jnp.sin(ang)
    z = jnp.zeros_like(sin)
    return (jnp.concatenate([cos, cos, cos, cos], axis=1),
            jnp.concatenate([-sin, z, -sin, z], axis=1),
            jnp.concatenate([z, sin, z, sin], axis=1))


def _pos_columns(pos):
    tab = np.zeros((pos.shape[0], LANES), np.float32)
    tab[:, POS_HI_LANE:POS_HI_LANE + 3] = (LANES * (pos // LANES))[:, None]
    tab[:, POS_LO_LANE:POS_LO_LANE + 3] = (pos % LANES)[:, None]
    return tab


def _nsa_tables(S, tq):
    n_cmp = (S - CMP_BLOCK) // CMP_STRIDE + 1
    n_sel = S // SEL_BLOCK
    n_chunk = S // CMP_STRIDE
    tok = np.arange(n_cmp)[:, None] * CMP_STRIDE + np.arange(CMP_BLOCK)[None, :]
    blk = tok // SEL_BLOCK
    m = (blk[:, :, None] == np.arange(n_sel)[None, None, :]).sum(axis=1) / CMP_BLOCK
    mt = np.zeros((n_sel, n_chunk), np.float32)
    mt[:, :n_cmp] = m.T
    keys = np.arange(S)
    ktab = _pos_columns(keys)
    ktab[:, :SEL_LANES] = (keys[:, None] // SEL_BLOCK == np.arange(SEL_LANES)[None, :])
    ctab = _pos_columns(np.arange(n_chunk) * CMP_STRIDE + (CMP_BLOCK - 1))
    slopes = jnp.asarray(2.0 ** (-8.0 * np.arange(1, NSA_HEADS + 1) / NSA_HEADS), F32)
    a = slopes * LOG2E
    a_hi = a.astype(BF16)
    r1 = a - a_hi.astype(F32)
    a_mid = r1.astype(BF16)
    a_lo = (r1 - a_mid.astype(F32)).astype(BF16)
    pieces = jnp.stack([a_hi, a_mid, a_lo], axis=-1)
    qrow = jnp.zeros((NSA_HEADS, LANES), BF16)
    qrow = qrow.at[:, POS_HI_LANE:POS_HI_LANE + 3].set(pieces)
    qrow = qrow.at[:, POS_LO_LANE:POS_LO_LANE + 3].set(pieces)
    qtab = jnp.repeat(qrow.reshape(NSA_GROUPS, NSA_HPG, LANES), tq, axis=1)
    return qtab, jnp.asarray(ktab, BF16), jnp.asarray(ctab, BF16), jnp.asarray(mt, BF16)


def kernel(x, c, ada_w, ada_b, norm1_g, norm2_g, mla_w_in, mla_g_q, mla_g_kv, mla_w_uq, mla_w_ukv, mla_w_o, kv_ada_w, kv_ada_b, kv_norm_g, nsa_w_kv, cmp_pos_k, cmp_pos_v, cmp_k_w1, cmp_k_w2, cmp_v_w1, cmp_v_w2, nsa_w_in, nsa_w_o, ffn_w_gate, ffn_w_up, ffn_w_down, moe_w_router, moe_b_router, moe_w_gate, moe_w_up, moe_w_down, final_g):
    B, S, D = x.shape
    T = B * S
    depth = ada_w.shape[0]
    n_a = mla_w_in.shape[0]
    H = MLA_HEADS
    G, HP, DK = NSA_GROUPS, NSA_HPG, NSA_DK
    d_ff = ffn_w_gate.shape[-1]
    nsa_tq = 256

    c_pad = jnp.zeros((8, D), F32).at[:B].set(c)
    mod = _modulation(c_pad, ada_w, ada_b)[:, :B].reshape(depth, B, 6, 1, D)
    kv_mod = _modulation(c_pad, kv_ada_w[None], kv_ada_b[None])[:, :B].reshape(1, B, 2, 1, D)

    rope_tabs = _rope_tables(S)
    nsa_tabs = _nsa_tables(S, nsa_tq)

    ffn_wg = ffn_w_gate[:, None]
    ffn_wu = ffn_w_up[:, None]
    ffn_wd = ffn_w_down[:, None]

    x2 = x.reshape(T, D)
    shared = None
    for l in range(depth):
        h = _norm(x2, norm1_g[l], B, S, shift=(mod, l, 0), scale=(mod, l, 1))
        if l < n_a:
            w_in = mla_w_in[l]
            q_lora, kv_lora = mla_g_q.shape[1], mla_g_kv.shape[1]
            w_pad = jnp.concatenate([w_in, w_in[:, q_lora + kv_lora:]], axis=1)
            cq, ckv, kpe = _mla_in(h, w_pad, mla_g_q[l], mla_g_kv[l], rope_tabs, S)
            wq = mla_w_uq[l].reshape(q_lora, H, QK_NOPE + QK_ROPE)
            wq = jnp.concatenate([wq[:, :, :QK_NOPE].reshape(q_lora, H * QK_NOPE),
                                  wq[:, :, QK_NOPE:].reshape(q_lora, H * QK_ROPE)], axis=1)
            wkv = mla_w_ukv[l].reshape(kv_lora, H, QK_NOPE + V_HEAD)
            wkv = jnp.concatenate([wkv[:, :, :QK_NOPE].reshape(kv_lora, H * QK_NOPE),
                                   wkv[:, :, QK_NOPE:].reshape(kv_lora, H * V_HEAD)], axis=1)
            q_all = _gmm(cq, [wq[None, None]], 0, mode="cast", tm=2048, tn=1024,
                         n_out=wq.shape[1], out_dtype=BF16,
                         out_scale=(QK_NOPE + QK_ROPE) ** -0.5 * LOG2E)
            kv_all = _gmm(ckv, [wkv[None, None]], 0, mode="cast", tm=2048, tn=1024,
                          n_out=wkv.shape[1], out_dtype=BF16)
            o = _mla_attn(q_all, kv_all, kpe, rope_tabs, B, S)
            x2 = _gmm(o, [mla_w_o[:, None]], l, mode="residual", tm=1024, tn=512, n_out=D,
                      out_dtype=F32, xres=x2, gate=(mod, l, 2), seq=S)
        else:
            jb = l - n_a
            w_in = nsa_w_in[jb]
            q = _gmm(h, [nsa_w_in[:, None]], jb, mode="cast", tm=1024, tn=512,
                     n_out=NSA_HEADS * DK, out_dtype=BF16, out_scale=DK ** -0.5 * LOG2E)
            wg = w_in[:, NSA_HEADS * DK:].reshape(D, G, HP, 3).transpose(0, 1, 3, 2)
            wg = jnp.pad(wg.reshape(D, G, 3 * HP), ((0, 0), (0, 0), (0, LANES - 3 * HP)))
            gates = _gmm(h, [wg.reshape(1, 1, D, G * LANES)], 0, mode="cast", tm=1024, tn=512,
                         n_out=G * LANES, out_dtype=F32)
            kvb, kvc = shared
            o = _nsa_attn(q, gates, kvb, kvc, nsa_tabs, B, S, tq=nsa_tq)
            x2 = _gmm(o, [nsa_w_o[:, None]], jb, mode="residual", tm=1024, tn=512, n_out=D,
                      out_dtype=F32, xres=x2, gate=(mod, l, 2), seq=S)

        if l % 2 == 0:
            h = _norm(x2, norm2_g[l], B, S, shift=(mod, l, 3), scale=(mod, l, 4))
            hid = _gmm(h, [ffn_wg, ffn_wu], l // 2, mode="swiglu", tm=1024, tn=512, n_out=d_ff,
                       out_dtype=BF16)
            x2 = _gmm(hid, [ffn_wd], l // 2, mode="residual", tm=512, tn=512, n_out=D,
                      out_dtype=F32, xres=x2, gate=(mod, l, 5), seq=S)
        else:
            li = l // 2
            wr = jnp.pad(moe_w_router[li], ((0, 0), (0, LANES - N_EXPERTS)))
            br = jnp.pad(moe_b_router[li], (0, LANES - N_EXPERTS)).reshape(1, LANES)
            hp, route = _norm(x2, norm2_g[l], B, S, shift=(mod, l, 3), scale=(mod, l, 4),
                              router=(wr, br))
            pos, n_rows, tile_expert, tile_valid = _moe_plan(route, T, MOE_TM)
            hs = _dispatch(hp, pos, n_rows)
            hid = _gmm(hs, [moe_w_gate, moe_w_up], li, mode="swiglu", tm=MOE_TM, tn=512,
                       n_out=d_ff, out_dtype=BF16, tile_expert=tile_expert, tile_valid=tile_valid,
                       a_packed=True)
            ys = _gmm(hid, [moe_w_down], li, mode="cast", tm=MOE_TM, tn=512, n_out=D,
                      out_dtype=F32, tile_expert=tile_expert, tile_valid=tile_valid)
            x2 = _combine(x2, ys, pos, route, (mod, l, 5), S)

        if l == n_a - 1:
            hkv = _norm(x2, kv_norm_g, B, S, shift=(kv_mod, 0, 0), scale=(kv_mod, 0, 1))
            w_kv = nsa_w_kv[None, None]
            kc = _gmm(hkv, [w_kv], 0, mode="cast", tm=1024, tn=512, n_out=2 * G * DK,
                      out_dtype=F32)
            kvb = _gmm(hkv, [w_kv], 0, mode="cast", tm=1024, tn=512, n_out=4 * G * DK,
                       out_dtype=BF16, n_off=(2 * G * DK) // 512)
            kvc = _nsa_compress(kc, jnp.stack([cmp_pos_k, cmp_pos_v]),
                                jnp.stack([cmp_k_w1, cmp_v_w1]), jnp.stack([cmp_k_w2, cmp_v_w2]),
                                B, S)
            shared = (kvb, kvc)

    out = _norm(x2, final_g, B, S, out_dtype=F32)
    return out.reshape(B, S, D)
```

```python
import functools

import numpy as np
import jax
import jax.numpy as jnp
from jax import lax
from jax.experimental import pallas as pl
from jax.experimental.pallas import tpu as pltpu

F32 = jnp.float32
BF16 = jnp.bfloat16

EPS = 1e-6
NEG = -1e30
FORCED_SCORE = 1e6
LOG2E = 1.4426950408889634

MLA_HEADS = 16
QK_NOPE = 128
QK_ROPE = 64
V_HEAD = 128
ROPE_THETA = 10000.0

NSA_HEADS = 16
NSA_GROUPS = 4
NSA_HPG = NSA_HEADS // NSA_GROUPS
NSA_DK = 128
CMP_BLOCK = 32
CMP_STRIDE = 16
SEL_BLOCK = 64
SEL_TOPN = 16
WINDOW = 512

N_EXPERTS = 8
LANES = 128
VMEM_LIMIT = 56 * 1024 * 1024

MOE_TM = 512

SEL_LANES = 32
POS_HI_LANE = 32
POS_LO_LANE = 35
MASK_BIG = 2.0 ** 30


def _cparams(n_axes):
    return pltpu.CompilerParams(
        dimension_semantics=("arbitrary",) * n_axes, vmem_limit_bytes=VMEM_LIMIT)


def _dot_nt(a, b):
    return lax.dot_general(a, b, (((1,), (1,)), ((), ())), preferred_element_type=F32)


def _mod_kernel(c_ref, w_ref, b_ref, o_ref):
    c = c_ref[...]
    ca = (c * jax.nn.sigmoid(c)).astype(BF16)
    o_ref[...] = jnp.dot(ca, w_ref[...].astype(BF16), preferred_element_type=F32) + b_ref[...]


def _modulation(c_pad, w, b, tn=1024):
    L, D, N = w.shape
    return pl.pallas_call(
        _mod_kernel,
        grid=(L, N // tn),
        in_specs=[
            pl.BlockSpec((8, D), lambda l, n: (0, 0)),
            pl.BlockSpec((None, D, tn), lambda l, n: (l, 0, n)),
            pl.BlockSpec((None, 1, tn), lambda l, n: (l, 0, n)),
        ],
        out_specs=pl.BlockSpec((None, 8, tn), lambda l, n: (l, 0, n)),
        out_shape=jax.ShapeDtypeStruct((L, 8, N), F32),
        compiler_params=_cparams(2),
        name="adaln_mod",
    )(c_pad, w, b.reshape(L, 1, N))


def _rms(x, g):
    return x * lax.rsqrt(jnp.mean(x * x, axis=-1, keepdims=True) + EPS) * g


_HI16 = 0xFFFF0000


def _pack_bf16_halves(h):
    half = h.shape[1] // 2
    bits = lax.bitcast_convert_type(h.astype(BF16).astype(F32), jnp.uint32)
    return (bits[:, :half] >> 16) | (bits[:, half:] & jnp.uint32(_HI16))


def _unpack_bf16_halves(a):
    lo = lax.bitcast_convert_type(a << 16, F32).astype(BF16)
    hi = lax.bitcast_convert_type(a & jnp.uint32(_HI16), F32).astype(BF16)
    return jnp.concatenate([lo, hi], axis=1)


def _norm_kernel(x_ref, g_ref, *refs, modulate, route):
    if modulate:
        sh_ref, sc_ref, *refs = refs
    if route:
        wr_ref, br_ref, *refs = refs
    h = _rms(x_ref[...], g_ref[...])
    if modulate:
        h = h * (1.0 + sc_ref[...]) + sh_ref[...]
    if route:
        refs[0][...] = _pack_bf16_halves(h)
    else:
        refs[0][...] = h.astype(refs[0].dtype)
    if route:
        logits = jnp.dot(h, wr_ref[...], preferred_element_type=F32,
                         precision=lax.Precision.HIGHEST) + br_ref[...]
        lane = lax.broadcasted_iota(jnp.int32, logits.shape, 1).astype(F32)
        logits = jnp.where(lane < N_EXPERTS, logits, -jnp.inf)
        v1 = jnp.max(logits, axis=-1, keepdims=True)
        i1 = jnp.min(jnp.where(logits == v1, lane, float(LANES)), axis=-1, keepdims=True)
        rest = jnp.where(lane == i1, -jnp.inf, logits)
        v2 = jnp.max(rest, axis=-1, keepdims=True)
        i2 = jnp.min(jnp.where(rest == v2, lane, float(LANES)), axis=-1, keepdims=True)
        e = jnp.exp(v2 - v1)
        w1 = 1.0 / (1.0 + e)
        w2 = e / (1.0 + e)
        refs[1][...] = jnp.where(lane == 0, i1, jnp.where(lane == 1, i2, jnp.where(
            lane == 2, w1, jnp.where(lane == 3, w2, 0.0))))


def _norm(x2, g, B, S, shift=None, scale=None, router=None, out_dtype=BF16, ts=512):
    T, D = x2.shape
    nst = S // ts
    modulate = shift is not None
    route = router is not None
    args = [x2, g.reshape(1, D)]
    in_specs = [
        pl.BlockSpec((ts, D), lambda b, s: (b * nst + s, 0)),
        pl.BlockSpec((1, D), lambda b, s: (0, 0)),
    ]
    if modulate:
        for arr, l, j in (shift, scale):
            args.append(arr)
            in_specs.append(pl.BlockSpec((None, None, None, 1, D),
                                         lambda b, s, l=l, j=j: (l, b, j, 0, 0)))
    d_out = D // 2 if route else D
    out_shape = [jax.ShapeDtypeStruct((T, d_out), jnp.uint32 if route else out_dtype)]
    out_specs = [pl.BlockSpec((ts, d_out), lambda b, s: (b * nst + s, 0))]
    if route:
        wr, br = router
        args += [wr, br]
        in_specs += [pl.BlockSpec((D, LANES), lambda b, s: (0, 0)),
                     pl.BlockSpec((1, LANES), lambda b, s: (0, 0))]
        out_shape.append(jax.ShapeDtypeStruct((T, LANES), F32))
        out_specs.append(pl.BlockSpec((ts, LANES), lambda b, s: (b * nst + s, 0)))
    outs = pl.pallas_call(
        functools.partial(_norm_kernel, modulate=modulate, route=route),
        grid=(B, nst),
        in_specs=in_specs,
        out_specs=out_specs,
        out_shape=out_shape,
        compiler_params=_cparams(2),
        name="norm_mod",
    )(*args)
    return outs if route else outs[0]


def _gmm_kernel(te_ref, tv_ref, nx_ref, a_ref, *refs, mode, out_scale, a_packed, n_w, layer,
                n_off, tn, n_n):
    n = pl.program_id(0)
    m = pl.program_id(1)
    w_hbm, refs = refs[:n_w], refs[n_w:]
    if mode == "residual":
        x_ref, gate_ref, *refs = refs
    o_ref, *refs = refs
    wst, wb, sem = refs[:n_w], refs[n_w:2 * n_w], refs[2 * n_w]
    first = jnp.logical_or(m == 0, te_ref[m] != te_ref[jnp.maximum(m - 1, 0)])
    valid = tv_ref[m] != 0

    def w_copy(i, e, nn):
        col = pl.multiple_of((nn + n_off) * tn, tn)
        return pltpu.make_async_copy(w_hbm[i].at[layer, e, :, pl.ds(col, tn)], wst[i], sem.at[i])

    @pl.when(jnp.logical_and(n == 0, m == 0))
    def _():
        for i in range(n_w):
            w_copy(i, te_ref[0], 0).start()

    @pl.when(first)
    def _():
        for i in range(n_w):
            w_copy(i, te_ref[m], n).wait()
            wb[i][...] = wst[i][...].astype(BF16)
        nm = nx_ref[m]
        same_sweep = nm >= 0
        e_next = jnp.where(same_sweep, te_ref[jnp.maximum(nm, 0)], te_ref[0])
        n_next = jnp.where(same_sweep, n, n + 1)

        @pl.when(jnp.logical_or(same_sweep, n + 1 < n_n))
        def _():
            for i in range(n_w):
                w_copy(i, e_next, n_next).start()

    @pl.when(valid)
    def _():
        a = _unpack_bf16_halves(a_ref[...]) if a_packed else a_ref[...]
        if mode == "swiglu":
            g = jnp.dot(a, wb[0][...], preferred_element_type=F32)
            u = jnp.dot(a, wb[1][...], preferred_element_type=F32)
            acc = g * jax.nn.sigmoid(g) * u
        else:
            acc = jnp.dot(a, wb[0][...], preferred_element_type=F32)
        if out_scale is not None:
            acc = acc * out_scale
        if mode == "residual":
            acc = x_ref[...] + gate_ref[...] * acc
        o_ref[...] = acc.astype(o_ref.dtype)

    @pl.when(jnp.logical_not(valid))
    def _():
        o_ref[...] = jnp.zeros_like(o_ref)


def _gmm(a, ws, l, *, mode, tm, tn, n_out, out_dtype, tile_expert=None, tile_valid=None,
         n_off=0, xres=None, gate=None, seq=None, out_scale=None, a_packed=False):
    M, ka = a.shape
    K = ws[0].shape[2]
    mt = M // tm
    n_w = len(ws)
    n_n = n_out // tn
    if tile_expert is None:
        tile_expert = jnp.zeros((mt,), jnp.int32)
        tile_valid = jnp.ones((mt,), jnp.int32)
    idx = jnp.arange(mt, dtype=jnp.int32)
    later = ((idx[None, :] > idx[:, None]) & (tile_valid[None, :] != 0)
             & (tile_expert[None, :] != tile_expert[:, None]))
    next_run = jnp.where(jnp.any(later, axis=1), jnp.argmax(later, axis=1), -1).astype(jnp.int32)
    args = [a] + list(ws)
    in_specs = [pl.BlockSpec((tm, ka), lambda n, m, te, tv, nx: (m, 0))]
    in_specs += [pl.BlockSpec(memory_space=pl.ANY)] * n_w
    aliases = {}
    if mode == "residual":
        garr, gl, gj = gate
        per_b = seq // tm
        aliases = {3 + len(args): 0}
        args += [xres, garr]
        in_specs += [
            pl.BlockSpec((tm, tn), lambda n, m, te, tv, nx: (m, n)),
            pl.BlockSpec((None, None, None, 1, tn),
                         lambda n, m, te, tv, nx: (gl, m // per_b, gj, 0, n)),
        ]
    return pl.pallas_call(
        functools.partial(_gmm_kernel, mode=mode, out_scale=out_scale, a_packed=a_packed,
                          n_w=n_w, layer=l, n_off=n_off, tn=tn, n_n=n_n),
        grid_spec=pltpu.PrefetchScalarGridSpec(
            num_scalar_prefetch=3,
            grid=(n_n, mt),
            in_specs=in_specs,
            out_specs=pl.BlockSpec((tm, tn), lambda n, m, te, tv, nx: (m, n)),
            scratch_shapes=([pltpu.VMEM((K, tn), F32) for _ in ws]
                            + [pltpu.VMEM((K, tn), BF16) for _ in ws]
                            + [pltpu.SemaphoreType.DMA((n_w,))]),
        ),
        out_shape=jax.ShapeDtypeStruct((M, n_out), out_dtype),
        input_output_aliases=aliases,
        compiler_params=_cparams(2),
        name="gmm_" + mode,
    )(tile_expert, tile_valid, next_run, *args)


def _rope_pairs(v, cos, s1, s2):
    return v * cos + pltpu.roll(v, LANES - QK_ROPE // 2, 1) * s1 + pltpu.roll(v, QK_ROPE // 2, 1) * s2


def _mla_in_kernel(a_ref, w_ref, gq_ref, gkv_ref, cos_ref, s1_ref, s2_ref,
                   cq_ref, ckv_ref, kpe_ref, wb, *, q_lora, kv_lora):
    @pl.when(pl.program_id(0) == 0)
    def _():
        wb[...] = w_ref[...].astype(BF16)

    acc = jnp.dot(a_ref[...], wb[...], preferred_element_type=F32)
    cq_ref[...] = _rms(acc[:, :q_lora], gq_ref[...]).astype(BF16)
    ckv_ref[...] = _rms(acc[:, q_lora:q_lora + kv_lora], gkv_ref[...]).astype(BF16)
    v = acc[:, q_lora + kv_lora:]
    kpe_ref[...] = _rope_pairs(v, cos_ref[...], s1_ref[...], s2_ref[...]).astype(BF16)


def _mla_in(h, w_pad, g_q, g_kv, rope_tabs, S, tm=512):
    T, D = h.shape
    q_lora, kv_lora = g_q.shape[0], g_kv.shape[0]
    n_all = w_pad.shape[1]
    nst = S // tm
    tab_spec = pl.BlockSpec((tm, LANES), lambda i: (i % nst, 0))
    return pl.pallas_call(
        functools.partial(_mla_in_kernel, q_lora=q_lora, kv_lora=kv_lora),
        grid=(T // tm,),
        in_specs=[
            pl.BlockSpec((tm, D), lambda i: (i, 0)),
            pl.BlockSpec((D, n_all), lambda i: (0, 0)),
            pl.BlockSpec((1, q_lora), lambda i: (0, 0)),
            pl.BlockSpec((1, kv_lora), lambda i: (0, 0)),
            tab_spec, tab_spec, tab_spec,
        ],
        out_specs=[
            pl.BlockSpec((tm, q_lora), lambda i: (i, 0)),
            pl.BlockSpec((tm, kv_lora), lambda i: (i, 0)),
            pl.BlockSpec((tm, LANES), lambda i: (i, 0)),
        ],
        out_shape=[
            jax.ShapeDtypeStruct((T, q_lora), BF16),
            jax.ShapeDtypeStruct((T, kv_lora), BF16),
            jax.ShapeDtypeStruct((T, LANES), BF16),
        ],
        scratch_shapes=[pltpu.VMEM((D, n_all), BF16)],
        compiler_params=_cparams(1),
        name="mla_in",
    )(h, w_pad, g_q.reshape(1, -1), g_kv.reshape(1, -1), *rope_tabs)


def _mla_attn_kernel(qn_ref, qp_ref, kn_ref, v_ref, kpe_ref, cos_ref, s1_ref, s2_ref, o_ref,
                     m_sc, acc_sc, *, tq):
    qi = pl.program_id(2)
    qp = _rope_pairs(qp_ref[...].astype(F32), cos_ref[...], s1_ref[...], s2_ref[...])
    lane = lax.broadcasted_iota(jnp.int32, qp.shape, 1)
    qs = []
    for hh in range(2):
        keep = (lane < QK_ROPE) if hh == 0 else (lane >= QK_ROPE)
        qs.append(jnp.concatenate(
            [qn_ref[:, hh * QK_NOPE:(hh + 1) * QK_NOPE], jnp.where(keep, qp, 0.0).astype(BF16)],
            axis=1))
    m_sc[...] = jnp.full_like(m_sc, NEG)
    acc_sc[...] = jnp.zeros_like(acc_sc)
    ones = jnp.ones((tq, LANES), BF16)

    def step(j, diagonal):
        ks = pl.multiple_of(j * tq, tq)
        kpe = kpe_ref[pl.ds(ks, tq), :]
        for hh in range(2):
            k = jnp.concatenate([kn_ref[pl.ds(ks, tq), hh * QK_NOPE:(hh + 1) * QK_NOPE], kpe],
                                axis=1)
            s = _dot_nt(qs[hh], k)
            if diagonal:
                row = lax.broadcasted_iota(jnp.int32, (tq, tq), 0)
                col = lax.broadcasted_iota(jnp.int32, (tq, tq), 1)
                s = jnp.where(col <= row, s, NEG)
            m_old = m_sc[hh]
            m_new = jnp.maximum(m_old, jnp.max(s, axis=-1, keepdims=True))
            alpha = jnp.exp2(m_old - m_new)
            p = jnp.exp2(s - jnp.concatenate([m_new] * (tq // LANES), axis=1))
            vv = jnp.concatenate([v_ref[pl.ds(ks, tq), hh * V_HEAD:(hh + 1) * V_HEAD], ones],
                                 axis=1)
            acc_sc[hh] = (jnp.concatenate([alpha, alpha], axis=1) * acc_sc[hh]
                          + jnp.dot(p.astype(BF16), vv, preferred_element_type=F32))
            m_sc[hh] = m_new

    def body(j, carry):
        step(j, False)
        return carry

    lax.fori_loop(0, qi, body, 0)
    step(qi, True)
    for hh in range(2):
        acc = acc_sc[hh]
        o_ref[:, hh * V_HEAD:(hh + 1) * V_HEAD] = (acc[:, :V_HEAD] / acc[:, V_HEAD:]).astype(o_ref.dtype)


def _mla_attn(q_all, kv_all, kpe, rope_tabs, B, S, tq=512):
    T = q_all.shape[0]
    H = MLA_HEADS
    nqt = S // tq
    pair_w = 2 * QK_NOPE
    n_pairs = H // 2
    tab_spec = pl.BlockSpec((tq, LANES), lambda b, p, i: (i, 0))
    return pl.pallas_call(
        functools.partial(_mla_attn_kernel, tq=tq),
        grid=(B, n_pairs, nqt),
        in_specs=[
            pl.BlockSpec((tq, pair_w), lambda b, p, i: (b * nqt + i, p)),
            pl.BlockSpec((tq, LANES), lambda b, p, i: (b * nqt + i, (H * QK_NOPE) // LANES + p)),
            pl.BlockSpec((S, pair_w), lambda b, p, i: (b, p)),
            pl.BlockSpec((S, pair_w), lambda b, p, i: (b, n_pairs + p)),
            pl.BlockSpec((S, LANES), lambda b, p, i: (b, 0)),
            tab_spec, tab_spec, tab_spec,
        ],
        out_specs=pl.BlockSpec((tq, pair_w), lambda b, p, i: (b * nqt + i, p)),
        out_shape=jax.ShapeDtypeStruct((T, H * V_HEAD), BF16),
        scratch_shapes=[pltpu.VMEM((2, tq, LANES), F32),
                        pltpu.VMEM((2, tq, V_HEAD + LANES), F32)],
        compiler_params=_cparams(3),
        name="mla_attn",
    )(q_all, q_all, kv_all, kv_all, kpe, *rope_tabs)


def _nsa_compress_kernel(t_ref, pe_ref, w1_ref, w2_ref, o_ref, *, n_chunk):
    half = CMP_BLOCK // 2
    pe = pe_ref[...]

    def chunk_rows(off):
        cols = []
        for l in range(half):
            x = t_ref[pl.ds(l, n_chunk, stride=CMP_STRIDE), :] + pe[off + l:off + l + 1, :]
            cols.append(x.astype(BF16))
        return jnp.concatenate(cols, axis=1)

    w1 = w1_ref[...].astype(BF16)
    kdim = half * NSA_DK
    p0 = jnp.dot(chunk_rows(0), w1[:kdim], preferred_element_type=F32)
    p1 = jnp.dot(chunk_rows(half), w1[kdim:], preferred_element_type=F32)
    pre = p0 + pltpu.roll(p1, n_chunk - 1, 0)
    hid = (pre * jax.nn.sigmoid(pre)).astype(BF16)
    out = jnp.dot(hid, w2_ref[...].astype(BF16), preferred_element_type=F32)
    rown = lax.broadcasted_iota(jnp.int32, out.shape, 0)
    o_ref[...] = jnp.where(rown < n_chunk - 1, out, 0.0).astype(o_ref.dtype)


def _nsa_compress(kc, pe, w1, w2, B, S):
    G = NSA_GROUPS
    n_chunk = S // CMP_STRIDE
    return pl.pallas_call(
        functools.partial(_nsa_compress_kernel, n_chunk=n_chunk),
        grid=(B, 2, G),
        in_specs=[
            pl.BlockSpec((S, NSA_DK), lambda b, i, g: (b, i * G + g)),
            pl.BlockSpec((None, CMP_BLOCK, NSA_DK), lambda b, i, g: (i, 0, 0)),
            pl.BlockSpec((None, CMP_BLOCK * NSA_DK, NSA_DK), lambda b, i, g: (i, 0, 0)),
            pl.BlockSpec((None, NSA_DK, NSA_DK), lambda b, i, g: (i, 0, 0)),
        ],
        out_specs=pl.BlockSpec((None, None, None, n_chunk, NSA_DK), lambda b, i, g: (b, i, g, 0, 0)),
        out_shape=jax.ShapeDtypeStruct((B, 2, G, n_chunk, NSA_DK), BF16),
        compiler_params=_cparams(3),
        name="nsa_compress",
    )(kc, pe, w1, w2)


def _nsa_attn_kernel(q_ref, gt_ref, ks_ref, vs_ref, kw_ref, vw_ref, kc_ref, vc_ref, qtab_ref,
                     ktab_ref, ctab_ref, mt_ref, o_ref, m_sc, acc_sc,
                     *, tq, tk, n_cmp, n_sel):
    qi = pl.program_id(2)
    t0 = qi * tq
    HP = NSA_HPG
    R = HP * tq
    qb = q_ref[...]
    qs = jnp.concatenate([qb[:, h * NSA_DK:(h + 1) * NSA_DK] for h in range(HP)], axis=0)
    qtab = qtab_ref[...]
    qa = jnp.concatenate([qs, qtab], axis=1)
    rowi = lax.broadcasted_iota(jnp.int32, (R, 1), 0)
    tcol = (t0 + rowi % tq).astype(F32)

    ncol = lax.broadcasted_iota(jnp.int32, (1, kc_ref.shape[0]), 1)
    dist = tcol - (ncol * CMP_STRIDE + (CMP_BLOCK - 1)).astype(F32)
    valid = jnp.logical_and(dist >= 0, ncol < n_cmp)
    kc = jnp.concatenate([kc_ref[...], ctab_ref[...]], axis=1)
    s = jnp.where(valid, _dot_nt(qa, kc), NEG)
    p = jnp.exp2(s - jnp.max(s, axis=-1, keepdims=True))
    l = jnp.sum(p, axis=-1, keepdims=True)
    p_cmp = jnp.where(tcol >= CMP_BLOCK - 1, p / l, 0.0)
    o_cmp = jnp.dot(p_cmp.astype(BF16), vc_ref[...], preferred_element_type=F32)

    ps = p_cmp[0:tq]
    for h in range(1, HP):
        ps = ps + p_cmp[h * tq:(h + 1) * tq]
    ps_hi = ps.astype(BF16)
    ps_lo = (ps - ps_hi.astype(F32)).astype(BF16)
    imp = _dot_nt(mt_ref[...], ps_hi) + _dot_nt(mt_ref[...], ps_lo)
    jrow = lax.broadcasted_iota(jnp.int32, (n_sel, tq), 0)
    blk_t = (t0 + lax.broadcasted_iota(jnp.int32, (n_sel, tq), 1)) // SEL_BLOCK
    forced = (jrow == 0) | (jrow == blk_t) | (jrow == blk_t - 1)
    imp = jnp.where(forced, FORCED_SCORE, imp)
    imp = jnp.where(jrow > blk_t, -1.0, imp)
    rank = jnp.zeros((n_sel, tq), F32)
    for k in range(n_sel):
        rk = imp[k:k + 1, :]
        beats = (rk > imp) | ((rk == imp) & (jrow > k))
        rank = rank + jnp.where(beats, 1.0, 0.0)
    sel_t = jnp.where(rank < SEL_TOPN, 1.0, 0.0)
    sel_t = jnp.concatenate([sel_t, jnp.zeros((LANES - n_sel, tq), F32)], axis=0)
    sel_neg = ((sel_t.T - 1.0) * MASK_BIG).astype(BF16)
    lane = lax.broadcasted_iota(jnp.int32, (R, LANES), 1)
    qx = jnp.where(lane < SEL_LANES, jnp.concatenate([sel_neg] * HP, axis=0), qtab)
    qsel = jnp.concatenate([qs, qx], axis=1)

    m_sc[...] = jnp.full_like(m_sc, NEG)
    acc_sc[...] = jnp.zeros_like(acc_sc)
    ones = jnp.ones((tk, LANES), BF16)

    def tile(j, causal):
        ks = pl.multiple_of(j * tk, tk)
        k = jnp.concatenate([ks_ref[pl.ds(ks, tk), :], ktab_ref[pl.ds(ks, tk), :]], axis=1)
        sc = _dot_nt(qsel, k)
        if causal:
            kpos = (ks + lax.broadcasted_iota(jnp.int32, (1, tk), 1)).astype(F32)
            sc = jnp.where(kpos <= tcol, sc, NEG)
        m_old = m_sc[...]
        m_new = jnp.maximum(m_old, jnp.max(sc, axis=-1, keepdims=True))
        alpha = jnp.exp2(m_old - m_new)
        pp = jnp.exp2(sc - jnp.concatenate([m_new] * (tk // LANES), axis=1))
        vv = jnp.concatenate([vs_ref[pl.ds(ks, tk), :], ones], axis=1)
        acc_sc[...] = (jnp.concatenate([alpha, alpha], axis=1) * acc_sc[...]
                       + jnp.dot(pp.astype(BF16), vv, preferred_element_type=F32))
        m_sc[...] = m_new

    def body(j, carry):
        tile(j, False)
        return carry

    n_full = t0 // tk
    lax.fori_loop(0, n_full, body, 0)
    tile(n_full, True)
    acc = acc_sc[...]
    o_sel = acc[:, :NSA_DK] / acc[:, NSA_DK:]

    span = WINDOW + tq
    ws = pl.multiple_of(jnp.maximum(t0 - WINDOW, 0), tq)
    kpos = (ws + lax.broadcasted_iota(jnp.int32, (1, span), 1)).astype(F32)
    d = tcol - kpos
    ok = jnp.logical_and(d >= 0, d < WINDOW)
    kw = jnp.concatenate([kw_ref[pl.ds(ws, span), :], ktab_ref[pl.ds(ws, span), :]], axis=1)
    sw = jnp.where(ok, _dot_nt(qa, kw), NEG)
    pw = jnp.exp2(sw - jnp.max(sw, axis=-1, keepdims=True))
    lw = jnp.sum(pw, axis=-1, keepdims=True)
    o_win = jnp.dot(pw.astype(BF16), vw_ref[pl.ds(ws, span), :], preferred_element_type=F32) / lw

    gt = jax.nn.sigmoid(gt_ref[...])

    def gcol(i):
        return jnp.concatenate([gt[:, i * HP + h:i * HP + h + 1] for h in range(HP)], axis=0)

    o = gcol(0) * o_cmp + gcol(1) * o_sel + gcol(2) * o_win
    o_ref[...] = jnp.concatenate([o[h * tq:(h + 1) * tq] for h in range(HP)], axis=1).astype(o_ref.dtype)


def _nsa_attn(q, gates, kvb, kvc, tabs, B, S, tq=128, tk=256):
    T = q.shape[0]
    G, DK, HP = NSA_GROUPS, NSA_DK, NSA_HPG
    qtab, ktab, ctab, mt = tabs
    nqt = S // tq
    n_chunk = kvc.shape[3]
    n_cmp = (S - CMP_BLOCK) // CMP_STRIDE + 1
    n_sel = S // SEL_BLOCK

    def kv_spec(i):
        return pl.BlockSpec((S, DK), lambda b, g, t, i=i: (b, i * G + g))

    def cmp_spec(i):
        return pl.BlockSpec((None, None, None, n_chunk, DK), lambda b, g, t, i=i: (b, i, g, 0, 0))

    return pl.pallas_call(
        functools.partial(_nsa_attn_kernel, tq=tq, tk=tk, n_cmp=n_cmp, n_sel=n_sel),
        grid=(B, G, nqt),
        in_specs=[
            pl.BlockSpec((tq, HP * DK), lambda b, g, t: (b * nqt + t, g)),
            pl.BlockSpec((tq, LANES), lambda b, g, t: (b * nqt + t, g)),
            kv_spec(0), kv_spec(1), kv_spec(2), kv_spec(3),
            cmp_spec(0), cmp_spec(1),
            pl.BlockSpec((None, HP * tq, LANES), lambda b, g, t: (g, 0, 0)),
            pl.BlockSpec(ktab.shape, lambda b, g, t: (0, 0)),
            pl.BlockSpec(ctab.shape, lambda b, g, t: (0, 0)),
            pl.BlockSpec(mt.shape, lambda b, g, t: (0, 0)),
        ],
        out_specs=pl.BlockSpec((tq, HP * DK), lambda b, g, t: (b * nqt + t, g)),
        out_shape=jax.ShapeDtypeStruct((T, NSA_HEADS * DK), BF16),
        scratch_shapes=[pltpu.VMEM((HP * tq, LANES), F32),
                        pltpu.VMEM((HP * tq, DK + LANES), F32)],
        compiler_params=_cparams(3),
        name="nsa_attn",
    )(q, gates, kvb, kvb, kvb, kvb, kvc, kvc, qtab, ktab, ctab, mt)


def _row_copy(src_hbm, dst_vmem, sem, src_row, dst_row):
    return pltpu.make_async_copy(src_hbm.at[pl.ds(src_row, 1), :], dst_vmem.at[pl.ds(dst_row, 1), :], sem)


def _dispatch_kernel(pos_ref, src_ref, init_hbm, out_hbm, sem, *, chunk, n_tok):
    del init_hbm
    base = pl.program_id(0) * chunk

    def start(r, c):
        t = base + r
        _row_copy(src_ref, out_hbm, sem, r, pos_ref[t]).start()
        _row_copy(src_ref, out_hbm, sem, r, pos_ref[n_tok + t]).start()
        return c

    def wait(r, c):
        _row_copy(src_ref, out_hbm, sem, 0, 0).wait()
        return c

    lax.fori_loop(0, chunk, start, 0, unroll=8)
    lax.fori_loop(0, 2 * chunk, wait, 0, unroll=8)


def _dispatch(h_packed, pos, n_rows, chunk=512):
    T, W = h_packed.shape
    n_steps = T // chunk
    init = jnp.zeros((n_rows, W), h_packed.dtype)
    return pl.pallas_call(
        functools.partial(_dispatch_kernel, chunk=chunk, n_tok=T),
        grid_spec=pltpu.PrefetchScalarGridSpec(
            num_scalar_prefetch=1,
            grid=(n_steps,),
            in_specs=[pl.BlockSpec((chunk, W), lambda i, pos: (i, 0)),
                      pl.BlockSpec(memory_space=pl.ANY)],
            out_specs=pl.BlockSpec(memory_space=pl.ANY),
            scratch_shapes=[pltpu.SemaphoreType.DMA(())],
        ),
        out_shape=jax.ShapeDtypeStruct((n_rows, W), h_packed.dtype),
        input_output_aliases={2: 0},
        compiler_params=_cparams(1),
        name="moe_dispatch",
    )(pos, h_packed, init)


def _combine_kernel(pos_ref, y_hbm, x_ref, gate_ref, rt_ref, o_ref, buf, sem,
                    *, tg, n_tok, n_steps):
    i = pl.program_id(0)
    slot = i % 2

    def issue(step, slot_):
        def start(r, c):
            t = step * tg + r
            _row_copy(y_hbm, buf.at[slot_, 0], sem.at[slot_], pos_ref[t], r).start()
            _row_copy(y_hbm, buf.at[slot_, 1], sem.at[slot_], pos_ref[n_tok + t], r).start()
            return c

        lax.fori_loop(0, tg, start, 0, unroll=8)

    @pl.when(i == 0)
    def _():
        issue(0, 0)

    @pl.when(i + 1 < n_steps)
    def _():
        issue(i + 1, 1 - slot)

    def wait(r, c):
        _row_copy(y_hbm, buf.at[slot, 0], sem.at[slot], 0, 0).wait()
        return c

    lax.fori_loop(0, 2 * tg, wait, 0, unroll=8)
    rt = rt_ref[...]
    y = rt[:, 2:3] * buf[slot, 0] + rt[:, 3:4] * buf[slot, 1]
    o_ref[...] = x_ref[...] + gate_ref[...] * y


def _combine(x2, y_sorted, pos, route, gate, S, tg=256):
    T, D = x2.shape
    garr, gl, gj = gate
    per_b = S // tg
    n_steps = T // tg
    return pl.pallas_call(
        functools.partial(_combine_kernel, tg=tg, n_tok=T, n_steps=n_steps),
        grid_spec=pltpu.PrefetchScalarGridSpec(
            num_scalar_prefetch=1,
            grid=(n_steps,),
            in_specs=[
                pl.BlockSpec(memory_space=pl.ANY),
                pl.BlockSpec((tg, D), lambda i, pos: (i, 0)),
                pl.BlockSpec((None, None, None, 1, D), lambda i, pos: (gl, i // per_b, gj, 0, 0)),
                pl.BlockSpec((tg, LANES), lambda i, pos: (i, 0)),
            ],
            out_specs=pl.BlockSpec((tg, D), lambda i, pos: (i, 0)),
            scratch_shapes=[pltpu.VMEM((2, 2, tg, D), F32), pltpu.SemaphoreType.DMA((2,))],
        ),
        out_shape=jax.ShapeDtypeStruct((T, D), F32),
        input_output_aliases={2: 0},
        compiler_params=_cparams(1),
        name="moe_combine",
    )(pos, y_sorted, x2, garr, route)


def _moe_plan(route, T, tm):
    E = N_EXPERTS
    n_tiles = (2 * T) // tm + E
    P = n_tiles * tm
    e_pair = jnp.concatenate([route[:, 0], route[:, 1]]).astype(jnp.int32)
    onehot = (e_pair[:, None] == jnp.arange(E, dtype=jnp.int32)[None, :]).astype(jnp.int32)
    csum = jnp.cumsum(onehot, axis=0)
    rank = jnp.sum(onehot * (csum - 1), axis=1)
    counts = csum[-1]
    padded = ((counts + tm - 1) // tm) * tm
    ends = jnp.cumsum(padded)
    pos = (ends - padded)[e_pair] + rank
    tile_start = jnp.arange(n_tiles, dtype=jnp.int32) * tm
    tile_valid = (tile_start < ends[-1]).astype(jnp.int32)
    tile_expert = jnp.sum((tile_start[:, None] >= ends[None, :]).astype(jnp.int32), axis=1)
    last_valid = tile_expert[jnp.maximum(ends[-1] // tm - 1, 0)]
    tile_expert = jnp.where(tile_valid == 1, tile_expert, last_valid).astype(jnp.int32)
    return pos.astype(jnp.int32), P, tile_expert, tile_valid


def _rope_tables(S):
    d = QK_ROPE
    inv = ROPE_THETA ** (-jnp.arange(0, d, 2, dtype=F32) / d)
    ang = jnp.arange(S).astype(F32)[:, None] * inv[None, :]
    cos, sin = jnp.cos(ang), jnp.sin(ang)
    z = jnp.zeros_like(sin)
    return (jnp.concatenate([cos, cos, cos, cos], axis=1),
            jnp.concatenate([-sin, z, -sin, z], axis=1),
            jnp.concatenate([z, sin, z, sin], axis=1))


def _pos_columns(pos):
    tab = np.zeros((pos.shape[0], LANES), np.float32)
    tab[:, POS_HI_LANE:POS_HI_LANE + 3] = (LANES * (pos // LANES))[:, None]
    tab[:, POS_LO_LANE:POS_LO_LANE + 3] = (pos % LANES)[:, None]
    return tab


def _nsa_tables(S, tq):
    n_cmp = (S - CMP_BLOCK) // CMP_STRIDE + 1
    n_sel = S // SEL_BLOCK
    n_chunk = S // CMP_STRIDE
    tok = np.arange(n_cmp)[:, None] * CMP_STRIDE + np.arange(CMP_BLOCK)[None, :]
    blk = tok // SEL_BLOCK
    m = (blk[:, :, None] == np.arange(n_sel)[None, None, :]).sum(axis=1) / CMP_BLOCK
    mt = np.zeros((n_sel, n_chunk), np.float32)
    mt[:, :n_cmp] = m.T
    keys = np.arange(S)
    ktab = _pos_columns(keys)
    ktab[:, :SEL_LANES] = (keys[:, None] // SEL_BLOCK == np.arange(SEL_LANES)[None, :])
    ctab = _pos_columns(np.arange(n_chunk) * CMP_STRIDE + (CMP_BLOCK - 1))
    slopes = jnp.asarray(2.0 ** (-8.0 * np.arange(1, NSA_HEADS + 1) / NSA_HEADS), F32)
    a = slopes * LOG2E
    a_hi = a.astype(BF16)
    r1 = a - a_hi.astype(F32)
    a_mid = r1.astype(BF16)
    a_lo = (r1 - a_mid.astype(F32)).astype(BF16)
    pieces = jnp.stack([a_hi, a_mid, a_lo], axis=-1)
    qrow = jnp.zeros((NSA_HEADS, LANES), BF16)
    qrow = qrow.at[:, POS_HI_LANE:POS_HI_LANE + 3].set(pieces)
    qrow = qrow.at[:, POS_LO_LANE:POS_LO_LANE + 3].set(pieces)
    qtab = jnp.repeat(qrow.reshape(NSA_GROUPS, NSA_HPG, LANES), tq, axis=1)
    return qtab, jnp.asarray(ktab, BF16), jnp.asarray(ctab, BF16), jnp.asarray(mt, BF16)


def kernel(x, c, ada_w, ada_b, norm1_g, norm2_g, mla_w_in, mla_g_q, mla_g_kv, mla_w_uq, mla_w_ukv, mla_w_o, kv_ada_w, kv_ada_b, kv_norm_g, nsa_w_kv, cmp_pos_k, cmp_pos_v, cmp_k_w1, cmp_k_w2, cmp_v_w1, cmp_v_w2, nsa_w_in, nsa_w_o, ffn_w_gate, ffn_w_up, ffn_w_down, moe_w_router, moe_b_router, moe_w_gate, moe_w_up, moe_w_down, final_g):
    B, S, D = x.shape
    T = B * S
    depth = ada_w.shape[0]
    n_a = mla_w_in.shape[0]
    H = MLA_HEADS
    G, HP, DK = NSA_GROUPS, NSA_HPG, NSA_DK
    d_ff = ffn_w_gate.shape[-1]
    nsa_tq = 256

    c_pad = jnp.zeros((8, D), F32).at[:B].set(c)
    mod = _modulation(c_pad, ada_w, ada_b)[:, :B].reshape(depth, B, 6, 1, D)
    kv_mod = _modulation(c_pad, kv_ada_w[None], kv_ada_b[None])[:, :B].reshape(1, B, 2, 1, D)

    rope_tabs = _rope_tables(S)
    nsa_tabs = _nsa_tables(S, nsa_tq)

    ffn_wg = ffn_w_gate[:, None]
    ffn_wu = ffn_w_up[:, None]
    ffn_wd = ffn_w_down[:, None]

    x2 = x.reshape(T, D)
    shared = None
    for l in range(depth):
        h = _norm(x2, norm1_g[l], B, S, shift=(mod, l, 0), scale=(mod, l, 1))
        if l < n_a:
            w_in = mla_w_in[l]
            q_lora, kv_lora = mla_g_q.shape[1], mla_g_kv.shape[1]
            w_pad = jnp.concatenate([w_in, w_in[:, q_lora + kv_lora:]], axis=1)
            cq, ckv, kpe = _mla_in(h, w_pad, mla_g_q[l], mla_g_kv[l], rope_tabs, S)
            wq = mla_w_uq[l].reshape(q_lora, H, QK_NOPE + QK_ROPE)
            wq = jnp.concatenate([wq[:, :, :QK_NOPE].reshape(q_lora, H * QK_NOPE),
                                  wq[:, :, QK_NOPE:].reshape(q_lora, H * QK_ROPE)], axis=1)
            wkv = mla_w_ukv[l].reshape(kv_lora, H, QK_NOPE + V_HEAD)
            wkv = jnp.concatenate([wkv[:, :, :QK_NOPE].reshape(kv_lora, H * QK_NOPE),
                                   wkv[:, :, QK_NOPE:].reshape(kv_lora, H * V_HEAD)], axis=1)
            q_all = _gmm(cq, [wq[None, None]], 0, mode="cast", tm=2048, tn=1024,
                         n_out=wq.shape[1], out_dtype=BF16,
                         out_scale=(QK_NOPE + QK_ROPE) ** -0.5 * LOG2E)
            kv_all = _gmm(ckv, [wkv[None, None]], 0, mode="cast", tm=2048, tn=1024,
                          n_out=wkv.shape[1], out_dtype=BF16)
            o = _mla_attn(q_all, kv_all, kpe, rope_tabs, B, S)
            x2 = _gmm(o, [mla_w_o[:, None]], l, mode="residual", tm=512, tn=D, n_out=D,
                      out_dtype=F32, xres=x2, gate=(mod, l, 2), seq=S)
        else:
            jb = l - n_a
            w_in = nsa_w_in[jb]
            q = _gmm(h, [nsa_w_in[:, None]], jb, mode="cast", tm=1024, tn=512,
                     n_out=NSA_HEADS * DK, out_dtype=BF16, out_scale=DK ** -0.5 * LOG2E)
            wg = w_in[:, NSA_HEADS * DK:].reshape(D, G, HP, 3).transpose(0, 1, 3, 2)
            wg = jnp.pad(wg.reshape(D, G, 3 * HP), ((0, 0), (0, 0), (0, LANES - 3 * HP)))
            gates = _gmm(h, [wg.reshape(1, 1, D, G * LANES)], 0, mode="cast", tm=1024, tn=512,
                         n_out=G * LANES, out_dtype=F32)
            kvb, kvc = shared
            o = _nsa_attn(q, gates, kvb, kvc, nsa_tabs, B, S, tq=nsa_tq)
            x2 = _gmm(o, [nsa_w_o[:, None]], jb, mode="residual", tm=512, tn=D, n_out=D,
                      out_dtype=F32, xres=x2, gate=(mod, l, 2), seq=S)

        if l % 2 == 0:
            h = _norm(x2, norm2_g[l], B, S, shift=(mod, l, 3), scale=(mod, l, 4))
            hid = _gmm(h, [ffn_wg, ffn_wu], l // 2, mode="swiglu", tm=1024, tn=512, n_out=d_ff,
                       out_dtype=BF16)
            x2 = _gmm(hid, [ffn_wd], l // 2, mode="residual", tm=256, tn=1024, n_out=D,
                      out_dtype=F32, xres=x2, gate=(mod, l, 5), seq=S)
        else:
            li = l // 2
            wr = jnp.pad(moe_w_router[li], ((0, 0), (0, LANES - N_EXPERTS)))
            br = jnp.pad(moe_b_router[li], (0, LANES - N_EXPERTS)).reshape(1, LANES)
            hp, route = _norm(x2, norm2_g[l], B, S, shift=(mod, l, 3), scale=(mod, l, 4),
                              router=(wr, br))
            pos, n_rows, tile_expert, tile_valid = _moe_plan(route, T, MOE_TM)
            hs = _dispatch(hp, pos, n_rows)
            hid = _gmm(hs, [moe_w_gate, moe_w_up], li, mode="swiglu", tm=MOE_TM, tn=512,
                       n_out=d_ff, out_dtype=BF16, tile_expert=tile_expert, tile_valid=tile_valid,
                       a_packed=True)
            ys = _gmm(hid, [moe_w_down], li, mode="cast", tm=MOE_TM, tn=1024, n_out=D,
                      out_dtype=F32, tile_expert=tile_expert, tile_valid=tile_valid)
            x2 = _combine(x2, ys, pos, route, (mod, l, 5), S)

        if l == n_a - 1:
            hkv = _norm(x2, kv_norm_g, B, S, shift=(kv_mod, 0, 0), scale=(kv_mod, 0, 1))
            w_kv = nsa_w_kv[None, None]
            kc = _gmm(hkv, [w_kv], 0, mode="cast", tm=1024, tn=512, n_out=2 * G * DK,
                      out_dtype=F32)
            kvb = _gmm(hkv, [w_kv], 0, mode="cast", tm=1024, tn=512, n_out=4 * G * DK,
                       out_dtype=BF16, n_off=(2 * G * DK) // 512)
            kvc = _nsa_compress(kc, jnp.stack([cmp_pos_k, cmp_pos_v]),
                                jnp.stack([cmp_k_w1, cmp_v_w1]), jnp.stack([cmp_k_w2, cmp_v_w2]),
                                B, S)
            shared = (kvb, kvc)

    out = _norm(x2, final_g, B, S, out_dtype=F32)
    return out.reshape(B, S, D)
```

```python
import functools

import numpy as np
import jax
import jax.numpy as jnp
from jax import lax
from jax.experimental import pallas as pl
from jax.experimental.pallas import tpu as pltpu

F32 = jnp.float32
BF16 = jnp.bfloat16

EPS = 1e-6
NEG = -1e30
FORCED_SCORE = 1e6
LOG2E = 1.4426950408889634

MLA_HEADS = 16
QK_NOPE = 128
QK_ROPE = 64
V_HEAD = 128
ROPE_THETA = 10000.0

NSA_HEADS = 16
NSA_GROUPS = 4
NSA_HPG = NSA_HEADS // NSA_GROUPS
NSA_DK = 128
CMP_BLOCK = 32
CMP_STRIDE = 16
SEL_BLOCK = 64
SEL_TOPN = 16
WINDOW = 512

N_EXPERTS = 8
LANES = 128
VMEM_LIMIT = 56 * 1024 * 1024

MOE_TM = 512

SEL_LANES = 32
POS_HI_LANE = 32
POS_LO_LANE = 35
MASK_BIG = 2.0 ** 30


def _cparams(n_axes):
    return pltpu.CompilerParams(
        dimension_semantics=("arbitrary",) * n_axes, vmem_limit_bytes=VMEM_LIMIT)


def _dot_nt(a, b):
    return lax.dot_general(a, b, (((1,), (1,)), ((), ())), preferred_element_type=F32)


def _mod_kernel(c_ref, w_ref, b_ref, o_ref):
    c = c_ref[...]
    ca = (c * jax.nn.sigmoid(c)).astype(BF16)
    o_ref[...] = jnp.dot(ca, w_ref[...].astype(BF16), preferred_element_type=F32) + b_ref[...]


def _modulation(c_pad, w, b, tn=1024):
    L, D, N = w.shape
    return pl.pallas_call(
        _mod_kernel,
        grid=(L, N // tn),
        in_specs=[
            pl.BlockSpec((8, D), lambda l, n: (0, 0)),
            pl.BlockSpec((None, D, tn), lambda l, n: (l, 0, n)),
            pl.BlockSpec((None, 1, tn), lambda l, n: (l, 0, n)),
        ],
        out_specs=pl.BlockSpec((None, 8, tn), lambda l, n: (l, 0, n)),
        out_shape=jax.ShapeDtypeStruct((L, 8, N), F32),
        compiler_params=_cparams(2),
        name="adaln_mod",
    )(c_pad, w, b.reshape(L, 1, N))


def _rms(x, g):
    return x * lax.rsqrt(jnp.mean(x * x, axis=-1, keepdims=True) + EPS) * g


_HI16 = 0xFFFF0000


def _pack_bf16_halves(h):
    half = h.shape[1] // 2
    bits = lax.bitcast_convert_type(h.astype(BF16).astype(F32), jnp.uint32)
    return (bits[:, :half] >> 16) | (bits[:, half:] & jnp.uint32(_HI16))


def _unpack_bf16_halves(a):
    lo = lax.bitcast_convert_type(a << 16, F32).astype(BF16)
    hi = lax.bitcast_convert_type(a & jnp.uint32(_HI16), F32).astype(BF16)
    return jnp.concatenate([lo, hi], axis=1)


def _norm_spec(g, shift=None, scale=None, router=None, out_dtype=BF16):
    return dict(g=g, shift=shift, scale=scale, router=router, out_dtype=out_dtype)


def _norm_cfg(ns):
    return (ns["shift"] is not None, ns["router"] is not None)


def _norm_io(ns, T, D, tm, batch_of, row_of):
    modulate, route = _norm_cfg(ns)
    args = [ns["g"].reshape(1, D)]
    in_specs = [pl.BlockSpec((1, D), lambda *a: (0, 0))]
    if modulate:
        for arr, l, j in (ns["shift"], ns["scale"]):
            args.append(arr)
            in_specs.append(pl.BlockSpec((None, None, None, 1, D),
                                         lambda *a, l=l, j=j: (l, batch_of(*a), j, 0, 0)))
    if route:
        wr, br = ns["router"]
        args += [wr, br]
        in_specs += [pl.BlockSpec((D, LANES), lambda *a: (0, 0)),
                     pl.BlockSpec((1, LANES), lambda *a: (0, 0))]
        out_shape = [jax.ShapeDtypeStruct((T, D // 2), jnp.uint32),
                     jax.ShapeDtypeStruct((T, LANES), F32)]
        out_specs = [pl.BlockSpec((tm, D // 2), lambda *a: (row_of(*a), 0)),
                     pl.BlockSpec((tm, LANES), lambda *a: (row_of(*a), 0))]
    else:
        out_shape = [jax.ShapeDtypeStruct((T, D), ns["out_dtype"])]
        out_specs = [pl.BlockSpec((tm, D), lambda *a: (row_of(*a), 0))]
    return args, in_specs, out_shape, out_specs


def _norm_counts(cfg):
    modulate, route = cfg
    return 1 + 2 * modulate + 2 * route, 1 + route


def _norm_apply(x, in_refs, out_refs, cfg):
    modulate, route = cfg
    g_ref, *rest = in_refs
    h = _rms(x, g_ref[...])
    if modulate:
        sh_ref, sc_ref, *rest = rest
        h = h * (1.0 + sc_ref[...]) + sh_ref[...]
    if not route:
        out_refs[0][...] = h.astype(out_refs[0].dtype)
        return
    wr_ref, br_ref = rest
    out_refs[0][...] = _pack_bf16_halves(h)
    logits = jnp.dot(h, wr_ref[...], preferred_element_type=F32,
                     precision=lax.Precision.HIGHEST) + br_ref[...]
    lane = lax.broadcasted_iota(jnp.int32, logits.shape, 1).astype(F32)
    logits = jnp.where(lane < N_EXPERTS, logits, -jnp.inf)
    v1 = jnp.max(logits, axis=-1, keepdims=True)
    i1 = jnp.min(jnp.where(logits == v1, lane, float(LANES)), axis=-1, keepdims=True)
    others = jnp.where(lane == i1, -jnp.inf, logits)
    v2 = jnp.max(others, axis=-1, keepdims=True)
    i2 = jnp.min(jnp.where(others == v2, lane, float(LANES)), axis=-1, keepdims=True)
    e = jnp.exp(v2 - v1)
    w1 = 1.0 / (1.0 + e)
    w2 = e / (1.0 + e)
    out_refs[1][...] = jnp.where(lane == 0, i1, jnp.where(lane == 1, i2, jnp.where(
        lane == 2, w1, jnp.where(lane == 3, w2, 0.0))))


def _apply_norms(x, refs_in, refs_out, cfgs):
    for cfg in cfgs:
        n_in, n_out = _norm_counts(cfg)
        _norm_apply(x, refs_in[:n_in], refs_out[:n_out], cfg)
        refs_in, refs_out = refs_in[n_in:], refs_out[n_out:]


def _norm_kernel(x_ref, *refs, cfg):
    n_in, _ = _norm_counts(cfg)
    _norm_apply(x_ref[...], refs[:n_in], refs[n_in:], cfg)


def _norm(x2, ns, B, S, ts=512):
    T, D = x2.shape
    nst = S // ts
    args, in_specs, out_shape, out_specs = _norm_io(
        ns, T, D, ts, lambda b, s: b, lambda b, s: b * nst + s)
    outs = pl.pallas_call(
        functools.partial(_norm_kernel, cfg=_norm_cfg(ns)),
        grid=(B, nst),
        in_specs=[pl.BlockSpec((ts, D), lambda b, s: (b * nst + s, 0))] + in_specs,
        out_specs=out_specs,
        out_shape=out_shape,
        compiler_params=_cparams(2),
        name="norm_mod",
    )(x2, *args)
    return outs if len(outs) > 1 else outs[0]


def _gmm_kernel(te_ref, tv_ref, nx_ref, a_ref, *refs, mode, out_scale, a_packed, n_w, layer,
                n_off, tn, n_n, norm_cfgs):
    n = pl.program_id(0)
    m = pl.program_id(1)
    w_hbm, refs = refs[:n_w], refs[n_w:]
    if mode == "residual":
        x_ref, gate_ref, *refs = refs
    n_norm_in = sum(_norm_counts(c)[0] for c in norm_cfgs)
    n_norm_out = sum(_norm_counts(c)[1] for c in norm_cfgs)
    norm_in, refs = refs[:n_norm_in], refs[n_norm_in:]
    o_ref, *refs = refs
    norm_out, refs = refs[:n_norm_out], refs[n_norm_out:]
    wst, wb, sem = refs[:n_w], refs[n_w:2 * n_w], refs[2 * n_w]
    first = jnp.logical_or(m == 0, te_ref[m] != te_ref[jnp.maximum(m - 1, 0)])
    valid = tv_ref[m] != 0

    def w_copy(i, e, nn):
        col = pl.multiple_of((nn + n_off) * tn, tn)
        return pltpu.make_async_copy(w_hbm[i].at[layer, e, :, pl.ds(col, tn)], wst[i], sem.at[i])

    @pl.when(jnp.logical_and(n == 0, m == 0))
    def _():
        for i in range(n_w):
            w_copy(i, te_ref[0], 0).start()

    @pl.when(first)
    def _():
        for i in range(n_w):
            w_copy(i, te_ref[m], n).wait()
            wb[i][...] = wst[i][...].astype(BF16)
        nm = nx_ref[m]
        same_sweep = nm >= 0
        e_next = jnp.where(same_sweep, te_ref[jnp.maximum(nm, 0)], te_ref[0])
        n_next = jnp.where(same_sweep, n, n + 1)

        @pl.when(jnp.logical_or(same_sweep, n + 1 < n_n))
        def _():
            for i in range(n_w):
                w_copy(i, e_next, n_next).start()

    @pl.when(valid)
    def _():
        a = _unpack_bf16_halves(a_ref[...]) if a_packed else a_ref[...]
        if mode == "swiglu":
            g = jnp.dot(a, wb[0][...], preferred_element_type=F32)
            u = jnp.dot(a, wb[1][...], preferred_element_type=F32)
            acc = g * jax.nn.sigmoid(g) * u
        else:
            acc = jnp.dot(a, wb[0][...], preferred_element_type=F32)
        if out_scale is not None:
            acc = acc * out_scale
        if mode == "residual":
            acc = x_ref[...] + gate_ref[...] * acc
        o_ref[...] = acc.astype(o_ref.dtype)
        _apply_norms(acc, norm_in, norm_out, norm_cfgs)

    @pl.when(jnp.logical_not(valid))
    def _():
        o_ref[...] = jnp.zeros_like(o_ref)


def _gmm(a, ws, l, *, mode, tm, tn, n_out, out_dtype, tile_expert=None, tile_valid=None,
         n_off=0, xres=None, gate=None, seq=None, out_scale=None, a_packed=False,
         alias_x=True, norms=None):
    M, ka = a.shape
    K = ws[0].shape[2]
    mt = M // tm
    n_w = len(ws)
    n_n = n_out // tn
    if tile_expert is None:
        tile_expert = jnp.zeros((mt,), jnp.int32)
        tile_valid = jnp.ones((mt,), jnp.int32)
    idx = jnp.arange(mt, dtype=jnp.int32)
    later = ((idx[None, :] > idx[:, None]) & (tile_valid[None, :] != 0)
             & (tile_expert[None, :] != tile_expert[:, None]))
    next_run = jnp.where(jnp.any(later, axis=1), jnp.argmax(later, axis=1), -1).astype(jnp.int32)
    args = [a] + list(ws)
    in_specs = [pl.BlockSpec((tm, ka), lambda n, m, te, tv, nx: (m, 0))]
    in_specs += [pl.BlockSpec(memory_space=pl.ANY)] * n_w
    aliases = {}
    if mode == "residual":
        garr, gl, gj = gate
        per_b = seq // tm
        if alias_x:
            aliases = {3 + len(args): 0}
        args += [xres, garr]
        in_specs += [
            pl.BlockSpec((tm, tn), lambda n, m, te, tv, nx: (m, n)),
            pl.BlockSpec((None, None, None, 1, tn),
                         lambda n, m, te, tv, nx: (gl, m // per_b, gj, 0, n)),
        ]
    out_shape = [jax.ShapeDtypeStruct((M, n_out), out_dtype)]
    out_specs = [pl.BlockSpec((tm, tn), lambda n, m, te, tv, nx: (m, n))]
    bare = norms is None
    norms = norms or []
    for ns in norms:
        assert tn == n_out
        n_args, n_in, n_shape, n_out_specs = _norm_io(
            ns, M, n_out, tm, lambda n, m, *_: m // (seq // tm), lambda n, m, *_: m)
        args += n_args
        in_specs += n_in
        out_shape += n_shape
        out_specs += n_out_specs
    outs = pl.pallas_call(
        functools.partial(_gmm_kernel, mode=mode, out_scale=out_scale, a_packed=a_packed,
                          n_w=n_w, layer=l, n_off=n_off, tn=tn, n_n=n_n,
                          norm_cfgs=tuple(_norm_cfg(ns) for ns in norms)),
        grid_spec=pltpu.PrefetchScalarGridSpec(
            num_scalar_prefetch=3,
            grid=(n_n, mt),
            in_specs=in_specs,
            out_specs=out_specs,
            scratch_shapes=([pltpu.VMEM((K, tn), F32) for _ in ws]
                            + [pltpu.VMEM((K, tn), BF16) for _ in ws]
                            + [pltpu.SemaphoreType.DMA((n_w,))]),
        ),
        out_shape=out_shape,
        input_output_aliases=aliases,
        compiler_params=_cparams(2),
        name="gmm_" + mode,
    )(tile_expert, tile_valid, next_run, *args)
    return outs[0] if bare else outs


def _rope_pairs(v, cos, s1, s2):
    return v * cos + pltpu.roll(v, LANES - QK_ROPE // 2, 1) * s1 + pltpu.roll(v, QK_ROPE // 2, 1) * s2


def _mla_in_kernel(a_ref, w_ref, gq_ref, gkv_ref, cos_ref, s1_ref, s2_ref,
                   cq_ref, ckv_ref, kpe_ref, wb, *, q_lora, kv_lora):
    @pl.when(pl.program_id(0) == 0)
    def _():
        wb[...] = w_ref[...].astype(BF16)

    acc = jnp.dot(a_ref[...], wb[...], preferred_element_type=F32)
    cq_ref[...] = _rms(acc[:, :q_lora], gq_ref[...]).astype(BF16)
    ckv_ref[...] = _rms(acc[:, q_lora:q_lora + kv_lora], gkv_ref[...]).astype(BF16)
    v = acc[:, q_lora + kv_lora:]
    kpe_ref[...] = _rope_pairs(v, cos_ref[...], s1_ref[...], s2_ref[...]).astype(BF16)


def _mla_in(h, w_pad, g_q, g_kv, rope_tabs, S, tm=512):
    T, D = h.shape
    q_lora, kv_lora = g_q.shape[0], g_kv.shape[0]
    n_all = w_pad.shape[1]
    nst = S // tm
    tab_spec = pl.BlockSpec((tm, LANES), lambda i: (i % nst, 0))
    return pl.pallas_call(
        functools.partial(_mla_in_kernel, q_lora=q_lora, kv_lora=kv_lora),
        grid=(T // tm,),
        in_specs=[
            pl.BlockSpec((tm, D), lambda i: (i, 0)),
            pl.BlockSpec((D, n_all), lambda i: (0, 0)),
            pl.BlockSpec((1, q_lora), lambda i: (0, 0)),
            pl.BlockSpec((1, kv_lora), lambda i: (0, 0)),
            tab_spec, tab_spec, tab_spec,
        ],
        out_specs=[
            pl.BlockSpec((tm, q_lora), lambda i: (i, 0)),
            pl.BlockSpec((tm, kv_lora), lambda i: (i, 0)),
            pl.BlockSpec((tm, LANES), lambda i: (i, 0)),
        ],
        out_shape=[
            jax.ShapeDtypeStruct((T, q_lora), BF16),
            jax.ShapeDtypeStruct((T, kv_lora), BF16),
            jax.ShapeDtypeStruct((T, LANES), BF16),
        ],
        scratch_shapes=[pltpu.VMEM((D, n_all), BF16)],
        compiler_params=_cparams(1),
        name="mla_in",
    )(h, w_pad, g_q.reshape(1, -1), g_kv.reshape(1, -1), *rope_tabs)


def _mla_attn_kernel(qn_ref, qp_ref, kn_ref, v_ref, kpe_ref, cos_ref, s1_ref, s2_ref, o_ref,
                     m_sc, l_sc, acc_sc, *, tq):
    qi = pl.program_id(2)
    qp = _rope_pairs(qp_ref[...].astype(F32), cos_ref[...], s1_ref[...], s2_ref[...])
    lane = lax.broadcasted_iota(jnp.int32, qp.shape, 1)
    qs = []
    for hh in range(2):
        keep = (lane < QK_ROPE) if hh == 0 else (lane >= QK_ROPE)
        qs.append(jnp.concatenate(
            [qn_ref[:, hh * QK_NOPE:(hh + 1) * QK_NOPE], jnp.where(keep, qp, 0.0).astype(BF16)],
            axis=1))
    m_sc[...] = jnp.full_like(m_sc, NEG)
    l_sc[...] = jnp.zeros_like(l_sc)
    acc_sc[...] = jnp.zeros_like(acc_sc)

    def step(j, diagonal):
        ks = pl.multiple_of(j * tq, tq)
        kpe = kpe_ref[pl.ds(ks, tq), :]
        for hh in range(2):
            k = jnp.concatenate([kn_ref[pl.ds(ks, tq), hh * QK_NOPE:(hh + 1) * QK_NOPE], kpe],
                                axis=1)
            s = _dot_nt(qs[hh], k)
            if diagonal:
                row = lax.broadcasted_iota(jnp.int32, (tq, tq), 0)
                col = lax.broadcasted_iota(jnp.int32, (tq, tq), 1)
                s = jnp.where(col <= row, s, NEG)
            m_old = m_sc[hh]
            m_new = jnp.maximum(m_old, jnp.max(s, axis=-1, keepdims=True))
            alpha = jnp.exp2(m_old - m_new)
            p = jnp.exp2(s - jnp.concatenate([m_new] * (tq // LANES), axis=1))
            l_sc[hh] = alpha * l_sc[hh] + jnp.sum(p, axis=-1, keepdims=True)
            vv = v_ref[pl.ds(ks, tq), hh * V_HEAD:(hh + 1) * V_HEAD]
            acc_sc[hh] = alpha * acc_sc[hh] + jnp.dot(p.astype(BF16), vv,
                                                      preferred_element_type=F32)
            m_sc[hh] = m_new

    def body(j, carry):
        step(j, False)
        return carry

    lax.fori_loop(0, qi, body, 0)
    step(qi, True)
    for hh in range(2):
        o_ref[:, hh * V_HEAD:(hh + 1) * V_HEAD] = (acc_sc[hh] / l_sc[hh]).astype(o_ref.dtype)


def _mla_attn(q_all, kv_all, kpe, rope_tabs, B, S, tq=512):
    T = q_all.shape[0]
    H = MLA_HEADS
    nqt = S // tq
    pair_w = 2 * QK_NOPE
    n_pairs = H // 2
    tab_spec = pl.BlockSpec((tq, LANES), lambda b, p, i: (i, 0))
    return pl.pallas_call(
        functools.partial(_mla_attn_kernel, tq=tq),
        grid=(B, n_pairs, nqt),
        in_specs=[
            pl.BlockSpec((tq, pair_w), lambda b, p, i: (b * nqt + i, p)),
            pl.BlockSpec((tq, LANES), lambda b, p, i: (b * nqt + i, (H * QK_NOPE) // LANES + p)),
            pl.BlockSpec((S, pair_w), lambda b, p, i: (b, p)),
            pl.BlockSpec((S, pair_w), lambda b, p, i: (b, n_pairs + p)),
            pl.BlockSpec((S, LANES), lambda b, p, i: (b, 0)),
            tab_spec, tab_spec, tab_spec,
        ],
        out_specs=pl.BlockSpec((tq, pair_w), lambda b, p, i: (b * nqt + i, p)),
        out_shape=jax.ShapeDtypeStruct((T, H * V_HEAD), BF16),
        scratch_shapes=[pltpu.VMEM((2, tq, LANES), F32), pltpu.VMEM((2, tq, LANES), F32),
                        pltpu.VMEM((2, tq, V_HEAD), F32)],
        compiler_params=_cparams(3),
        name="mla_attn",
    )(q_all, q_all, kv_all, kv_all, kpe, *rope_tabs)


def _nsa_compress_kernel(t_ref, pe_ref, w1_ref, w2_ref, o_ref, *, n_chunk):
    half = CMP_BLOCK // 2
    pe = pe_ref[...]

    def chunk_rows(off):
        cols = []
        for l in range(half):
            x = t_ref[pl.ds(l, n_chunk, stride=CMP_STRIDE), :] + pe[off + l:off + l + 1, :]
            cols.append(x.astype(BF16))
        return jnp.concatenate(cols, axis=1)

    w1 = w1_ref[...].astype(BF16)
    kdim = half * NSA_DK
    p0 = jnp.dot(chunk_rows(0), w1[:kdim], preferred_element_type=F32)
    p1 = jnp.dot(chunk_rows(half), w1[kdim:], preferred_element_type=F32)
    pre = p0 + pltpu.roll(p1, n_chunk - 1, 0)
    hid = (pre * jax.nn.sigmoid(pre)).astype(BF16)
    out = jnp.dot(hid, w2_ref[...].astype(BF16), preferred_element_type=F32)
    rown = lax.broadcasted_iota(jnp.int32, out.shape, 0)
    o_ref[...] = jnp.where(rown < n_chunk - 1, out, 0.0).astype(o_ref.dtype)


def _nsa_compress(kc, pe, w1, w2, B, S):
    G = NSA_GROUPS
    n_chunk = S // CMP_STRIDE
    return pl.pallas_call(
        functools.partial(_nsa_compress_kernel, n_chunk=n_chunk),
        grid=(B, 2, G),
        in_specs=[
            pl.BlockSpec((S, NSA_DK), lambda b, i, g: (b, i * G + g)),
            pl.BlockSpec((None, CMP_BLOCK, NSA_DK), lambda b, i, g: (i, 0, 0)),
            pl.BlockSpec((None, CMP_BLOCK * NSA_DK, NSA_DK), lambda b, i, g: (i, 0, 0)),
            pl.BlockSpec((None, NSA_DK, NSA_DK), lambda b, i, g: (i, 0, 0)),
        ],
        out_specs=pl.BlockSpec((None, None, None, n_chunk, NSA_DK), lambda b, i, g: (b, i, g, 0, 0)),
        out_shape=jax.ShapeDtypeStruct((B, 2, G, n_chunk, NSA_DK), BF16),
        compiler_params=_cparams(3),
        name="nsa_compress",
    )(kc, pe, w1, w2)


def _nsa_attn_kernel(q_ref, gt_ref, ks_ref, vs_ref, kw_ref, vw_ref, kc_ref, vc_ref, qtab_ref,
                     ktab_ref, ctab_ref, mt_ref, cbias_ref, wbias_ref, o_ref, m_sc, acc_sc,
                     *, tq, tk, n_sel):
    qi = pl.program_id(2)
    t0 = qi * tq
    HP = NSA_HPG
    R = HP * tq
    qb = q_ref[...]
    qs = jnp.concatenate([qb[:, h * NSA_DK:(h + 1) * NSA_DK] for h in range(HP)], axis=0)
    qtab = qtab_ref[...]
    qa = jnp.concatenate([qs, qtab], axis=1)
    rowi = lax.broadcasted_iota(jnp.int32, (R, 1), 0)
    tcol = (t0 + rowi % tq).astype(F32)

    kc = jnp.concatenate([kc_ref[...], ctab_ref[...]], axis=1)
    s = _dot_nt(qa, kc) + jnp.concatenate([cbias_ref[...]] * HP, axis=0)
    p = jnp.exp2(s - jnp.max(s, axis=-1, keepdims=True))
    l = jnp.sum(p, axis=-1, keepdims=True)
    p_cmp = jnp.where(tcol >= CMP_BLOCK - 1, p / l, 0.0)
    o_cmp = jnp.dot(p_cmp.astype(BF16), vc_ref[...], preferred_element_type=F32)

    ps = p_cmp[0:tq]
    for h in range(1, HP):
        ps = ps + p_cmp[h * tq:(h + 1) * tq]
    ps_hi = ps.astype(BF16)
    ps_lo = (ps - ps_hi.astype(F32)).astype(BF16)
    imp = _dot_nt(mt_ref[...], ps_hi) + _dot_nt(mt_ref[...], ps_lo)
    jrow = lax.broadcasted_iota(jnp.int32, (n_sel, tq), 0)
    blk_t = (t0 + lax.broadcasted_iota(jnp.int32, (n_sel, tq), 1)) // SEL_BLOCK
    forced = (jrow == 0) | (jrow == blk_t) | (jrow == blk_t - 1)
    imp = jnp.where(forced, FORCED_SCORE, imp)
    imp = jnp.where(jrow > blk_t, -1.0, imp)
    rank = jnp.zeros((n_sel, tq), F32)
    for k in range(n_sel):
        rk = imp[k:k + 1, :]
        beats = (rk > imp) | ((rk == imp) & (jrow > k))
        rank = rank + jnp.where(beats, 1.0, 0.0)
    sel_t = jnp.where(rank < SEL_TOPN, 1.0, 0.0)
    sel_t = jnp.concatenate([sel_t, jnp.zeros((LANES - n_sel, tq), F32)], axis=0)
    sel_neg = ((sel_t.T - 1.0) * MASK_BIG).astype(BF16)
    lane = lax.broadcasted_iota(jnp.int32, (R, LANES), 1)
    qx = jnp.where(lane < SEL_LANES, jnp.concatenate([sel_neg] * HP, axis=0), qtab)
    qsel = jnp.concatenate([qs, qx], axis=1)

    m_sc[...] = jnp.full_like(m_sc, NEG)
    acc_sc[...] = jnp.zeros_like(acc_sc)
    ones = jnp.ones((tk, LANES), BF16)

    def tile(j, causal):
        ks = pl.multiple_of(j * tk, tk)
        k = jnp.concatenate([ks_ref[pl.ds(ks, tk), :], ktab_ref[pl.ds(ks, tk), :]], axis=1)
        sc = _dot_nt(qsel, k)
        if causal:
            kpos = (ks + lax.broadcasted_iota(jnp.int32, (1, tk), 1)).astype(F32)
            sc = jnp.where(kpos <= tcol, sc, NEG)
        m_old = m_sc[...]
        m_new = jnp.maximum(m_old, jnp.max(sc, axis=-1, keepdims=True))
        alpha = jnp.exp2(m_old - m_new)
        pp = jnp.exp2(sc - jnp.concatenate([m_new] * (tk // LANES), axis=1))
        vv = jnp.concatenate([vs_ref[pl.ds(ks, tk), :], ones], axis=1)
        acc_sc[...] = (jnp.concatenate([alpha, alpha], axis=1) * acc_sc[...]
                       + jnp.dot(pp.astype(BF16), vv, preferred_element_type=F32))
        m_sc[...] = m_new

    def body(j, carry):
        tile(j, False)
        return carry

    n_full = t0 // tk
    lax.fori_loop(0, n_full, body, 0)
    tile(n_full, True)
    acc = acc_sc[...]
    o_sel = acc[:, :NSA_DK] / acc[:, NSA_DK:]

    span = WINDOW + tq
    ws = pl.multiple_of(jnp.maximum(t0 - WINDOW, 0), tq)
    wbias = wbias_ref[(t0 - ws) // tq]
    kw = jnp.concatenate([kw_ref[pl.ds(ws, span), :], ktab_ref[pl.ds(ws, span), :]], axis=1)
    sw = _dot_nt(qa, kw) + jnp.concatenate([wbias] * HP, axis=0)
    pw = jnp.exp2(sw - jnp.max(sw, axis=-1, keepdims=True))
    vw = jnp.concatenate([vw_ref[pl.ds(ws, span), :], jnp.ones((span, LANES), BF16)], axis=1)
    rw = jnp.dot(pw.astype(BF16), vw, preferred_element_type=F32)
    o_win = rw[:, :NSA_DK] / rw[:, NSA_DK:]

    gt = jax.nn.sigmoid(gt_ref[...])

    def gcol(i):
        return jnp.concatenate([gt[:, i * HP + h:i * HP + h + 1] for h in range(HP)], axis=0)

    o = gcol(0) * o_cmp + gcol(1) * o_sel + gcol(2) * o_win
    o_ref[...] = jnp.concatenate([o[h * tq:(h + 1) * tq] for h in range(HP)], axis=1).astype(o_ref.dtype)


def _nsa_attn(q, gates, kvb, kvc, tabs, B, S, tq=128, tk=256):
    T = q.shape[0]
    G, DK, HP = NSA_GROUPS, NSA_DK, NSA_HPG
    qtab, ktab, ctab, mt, cbias, wbias = tabs
    nqt = S // tq
    n_chunk = kvc.shape[3]
    n_sel = S // SEL_BLOCK

    def kv_spec(i):
        return pl.BlockSpec((S, DK), lambda b, g, t, i=i: (b, i * G + g))

    def cmp_spec(i):
        return pl.BlockSpec((None, None, None, n_chunk, DK), lambda b, g, t, i=i: (b, i, g, 0, 0))

    return pl.pallas_call(
        functools.partial(_nsa_attn_kernel, tq=tq, tk=tk, n_sel=n_sel),
        grid=(B, G, nqt),
        in_specs=[
            pl.BlockSpec((tq, HP * DK), lambda b, g, t: (b * nqt + t, g)),
            pl.BlockSpec((tq, LANES), lambda b, g, t: (b * nqt + t, g)),
            kv_spec(0), kv_spec(1), kv_spec(2), kv_spec(3),
            cmp_spec(0), cmp_spec(1),
            pl.BlockSpec((None, HP * tq, LANES), lambda b, g, t: (g, 0, 0)),
            pl.BlockSpec(ktab.shape, lambda b, g, t: (0, 0)),
            pl.BlockSpec(ctab.shape, lambda b, g, t: (0, 0)),
            pl.BlockSpec(mt.shape, lambda b, g, t: (0, 0)),
            pl.BlockSpec((tq, n_chunk), lambda b, g, t: (t, 0)),
            pl.BlockSpec(wbias.shape, lambda b, g, t: (0, 0, 0)),
        ],
        out_specs=pl.BlockSpec((tq, HP * DK), lambda b, g, t: (b * nqt + t, g)),
        out_shape=jax.ShapeDtypeStruct((T, NSA_HEADS * DK), BF16),
        scratch_shapes=[pltpu.VMEM((HP * tq, LANES), F32),
                        pltpu.VMEM((HP * tq, DK + LANES), F32)],
        compiler_params=_cparams(3),
        name="nsa_attn",
    )(q, gates, kvb, kvb, kvb, kvb, kvc, kvc, qtab, ktab, ctab, mt, cbias, wbias)


def _row_copy(src_hbm, dst_vmem, sem, src_row, dst_row):
    return pltpu.make_async_copy(src_hbm.at[pl.ds(src_row, 1), :], dst_vmem.at[pl.ds(dst_row, 1), :], sem)


def _dispatch_kernel(pos_ref, src_ref, init_hbm, out_hbm, sem, *, chunk, n_tok):
    del init_hbm
    base = pl.program_id(0) * chunk

    def start(r, c):
        t = base + r
        _row_copy(src_ref, out_hbm, sem, r, pos_ref[t]).start()
        _row_copy(src_ref, out_hbm, sem, r, pos_ref[n_tok + t]).start()
        return c

    def wait(r, c):
        _row_copy(src_ref, out_hbm, sem, 0, 0).wait()
        return c

    lax.fori_loop(0, chunk, start, 0, unroll=8)
    lax.fori_loop(0, 2 * chunk, wait, 0, unroll=8)


def _dispatch(h_packed, pos, n_rows, chunk=512):
    T, W = h_packed.shape
    n_steps = T // chunk
    init = jnp.zeros((n_rows, W), h_packed.dtype)
    return pl.pallas_call(
        functools.partial(_dispatch_kernel, chunk=chunk, n_tok=T),
        grid_spec=pltpu.PrefetchScalarGridSpec(
            num_scalar_prefetch=1,
            grid=(n_steps,),
            in_specs=[pl.BlockSpec((chunk, W), lambda i, pos: (i, 0)),
                      pl.BlockSpec(memory_space=pl.ANY)],
            out_specs=pl.BlockSpec(memory_space=pl.ANY),
            scratch_shapes=[pltpu.SemaphoreType.DMA(())],
        ),
        out_shape=jax.ShapeDtypeStruct((n_rows, W), h_packed.dtype),
        input_output_aliases={2: 0},
        compiler_params=_cparams(1),
        name="moe_dispatch",
    )(pos, h_packed, init)


def _combine_kernel(pos_ref, y_hbm, x_ref, gate_ref, rt_ref, *refs, tg, n_tok, n_steps,
                    norm_cfgs):
    n_norm_in = sum(_norm_counts(c)[0] for c in norm_cfgs)
    n_norm_out = sum(_norm_counts(c)[1] for c in norm_cfgs)
    norm_in, (o_ref, *refs) = refs[:n_norm_in], refs[n_norm_in:]
    norm_out, (buf, sem) = refs[:n_norm_out], refs[n_norm_out:]
    i = pl.program_id(0)
    slot = i % 2

    def issue(step, slot_):
        def start(r, c):
            t = step * tg + r
            _row_copy(y_hbm, buf.at[slot_, 0], sem.at[slot_], pos_ref[t], r).start()
            _row_copy(y_hbm, buf.at[slot_, 1], sem.at[slot_], pos_ref[n_tok + t], r).start()
            return c

        lax.fori_loop(0, tg, start, 0, unroll=8)

    @pl.when(i == 0)
    def _():
        issue(0, 0)

    @pl.when(i + 1 < n_steps)
    def _():
        issue(i + 1, 1 - slot)

    def wait(r, c):
        _row_copy(y_hbm, buf.at[slot, 0], sem.at[slot], 0, 0).wait()
        return c

    lax.fori_loop(0, 2 * tg, wait, 0, unroll=8)
    rt = rt_ref[...]
    y = rt[:, 2:3] * buf[slot, 0] + rt[:, 3:4] * buf[slot, 1]
    x_new = x_ref[...] + gate_ref[...] * y
    o_ref[...] = x_new
    _apply_norms(x_new, norm_in, norm_out, norm_cfgs)


def _combine(x2, y_sorted, pos, route, gate, S, norms=(), tg=256):
    T, D = x2.shape
    garr, gl, gj = gate
    per_b = S // tg
    n_steps = T // tg
    args = [y_sorted, x2, garr, route]
    in_specs = [
        pl.BlockSpec(memory_space=pl.ANY),
        pl.BlockSpec((tg, D), lambda i, pos: (i, 0)),
        pl.BlockSpec((None, None, None, 1, D), lambda i, pos: (gl, i // per_b, gj, 0, 0)),
        pl.BlockSpec((tg, LANES), lambda i, pos: (i, 0)),
    ]
    out_shape = [jax.ShapeDtypeStruct((T, D), F32)]
    out_specs = [pl.BlockSpec((tg, D), lambda i, pos: (i, 0))]
    for ns in norms:
        n_args, n_in, n_shape, n_out_specs = _norm_io(
            ns, T, D, tg, lambda i, *_: i // per_b, lambda i, *_: i)
        args += n_args
        in_specs += n_in
        out_shape += n_shape
        out_specs += n_out_specs
    outs = pl.pallas_call(
        functools.partial(_combine_kernel, tg=tg, n_tok=T, n_steps=n_steps,
                          norm_cfgs=tuple(_norm_cfg(ns) for ns in norms)),
        grid_spec=pltpu.PrefetchScalarGridSpec(
            num_scalar_prefetch=1,
            grid=(n_steps,),
            in_specs=in_specs,
            out_specs=out_specs,
            scratch_shapes=[pltpu.VMEM((2, 2, tg, D), F32), pltpu.SemaphoreType.DMA((2,))],
        ),
        out_shape=out_shape,
        input_output_aliases={2: 0},
        compiler_params=_cparams(1),
        name="moe_combine",
    )(pos, *args)
    return outs if norms else outs[0]


def _moe_plan(route, T, tm):
    E = N_EXPERTS
    n_tiles = (2 * T) // tm + E
    P = n_tiles * tm
    e_pair = jnp.concatenate([route[:, 0], route[:, 1]]).astype(jnp.int32)
    onehot = (e_pair[:, None] == jnp.arange(E, dtype=jnp.int32)[None, :]).astype(jnp.int32)
    csum = jnp.cumsum(onehot, axis=0)
    rank = jnp.sum(onehot * (csum - 1), axis=1)
    counts = csum[-1]
    padded = ((counts + tm - 1) // tm) * tm
    ends = jnp.cumsum(padded)
    pos = (ends - padded)[e_pair] + rank
    tile_start = jnp.arange(n_tiles, dtype=jnp.int32) * tm
    tile_valid = (tile_start < ends[-1]).astype(jnp.int32)
    tile_expert = jnp.sum((tile_start[:, None] >= ends[None, :]).astype(jnp.int32), axis=1)
    last_valid = tile_expert[jnp.maximum(ends[-1] // tm - 1, 0)]
    tile_expert = jnp.where(tile_valid == 1, tile_expert, last_valid).astype(jnp.int32)
    return pos.astype(jnp.int32), P, tile_expert, tile_valid


def _rope_tables(S):
    d = QK_ROPE
    inv = ROPE_THETA ** (-jnp.arange(0, d, 2, dtype=F32) / d)
    ang = jnp.arange(S).astype(F32)[:, None] * inv[None, :]
    cos, sin = jnp.cos(ang), jnp.sin(ang)
    z = jnp.zeros_like(sin)
    return (jnp.concatenate([cos, cos, cos, cos], axis=1),
            jnp.concatenate([-sin, z, -sin, z], axis=1),
            jnp.concatenate([z, sin, z, sin], axis=1))


def _pos_columns(pos):
    tab = np.zeros((pos.shape[0], LANES), np.float32)
    tab[:, POS_HI_LANE:POS_HI_LANE + 3] = (LANES * (pos // LANES))[:, None]
    tab[:, POS_LO_LANE:POS_LO_LANE + 3] = (pos % LANES)[:, None]
    return tab


def _nsa_tables(S, tq):
    n_cmp = (S - CMP_BLOCK) // CMP_STRIDE + 1
    n_sel = S // SEL_BLOCK
    n_chunk = S // CMP_STRIDE
    tok = np.arange(n_cmp)[:, None] * CMP_STRIDE + np.arange(CMP_BLOCK)[None, :]
    blk = tok // SEL_BLOCK
    m = (blk[:, :, None] == np.arange(n_sel)[None, None, :]).sum(axis=1) / CMP_BLOCK
    mt = np.zeros((n_sel, n_chunk), np.float32)
    mt[:, :n_cmp] = m.T
    keys = np.arange(S)
    ktab = _pos_columns(keys)
    ktab[:, :SEL_LANES] = (keys[:, None] // SEL_BLOCK == np.arange(SEL_LANES)[None, :])
    ctab = _pos_columns(np.arange(n_chunk) * CMP_STRIDE + (CMP_BLOCK - 1))
    slopes = jnp.asarray(2.0 ** (-8.0 * np.arange(1, NSA_HEADS + 1) / NSA_HEADS), F32)
    a = slopes * LOG2E
    a_hi = a.astype(BF16)
    r1 = a - a_hi.astype(F32)
    a_mid = r1.astype(BF16)
    a_lo = (r1 - a_mid.astype(F32)).astype(BF16)
    pieces = jnp.stack([a_hi, a_mid, a_lo], axis=-1)
    qrow = jnp.zeros((NSA_HEADS, LANES), BF16)
    qrow = qrow.at[:, POS_HI_LANE:POS_HI_LANE + 3].set(pieces)
    qrow = qrow.at[:, POS_LO_LANE:POS_LO_LANE + 3].set(pieces)
    qtab = jnp.repeat(qrow.reshape(NSA_GROUPS, NSA_HPG, LANES), tq, axis=1)
    cend = np.arange(n_chunk) * CMP_STRIDE + (CMP_BLOCK - 1)
    cvis = (keys[:, None] >= cend[None, :]) & (np.arange(n_chunk)[None, :] < n_cmp)
    cbias = np.where(cvis, 0.0, NEG).astype(np.float32)
    span = WINDOW + tq
    d = (np.arange(WINDOW // tq + 1)[:, None, None] * tq + np.arange(tq)[None, :, None]
         - np.arange(span)[None, None, :])
    wbias = np.where((d >= 0) & (d < WINDOW), 0.0, NEG).astype(np.float32)
    return (qtab, jnp.asarray(ktab, BF16), jnp.asarray(ctab, BF16), jnp.asarray(mt, BF16),
            jnp.asarray(cbias), jnp.asarray(wbias))


def kernel(x, c, ada_w, ada_b, norm1_g, norm2_g, mla_w_in, mla_g_q, mla_g_kv, mla_w_uq, mla_w_ukv, mla_w_o, kv_ada_w, kv_ada_b, kv_norm_g, nsa_w_kv, cmp_pos_k, cmp_pos_v, cmp_k_w1, cmp_k_w2, cmp_v_w1, cmp_v_w2, nsa_w_in, nsa_w_o, ffn_w_gate, ffn_w_up, ffn_w_down, moe_w_router, moe_b_router, moe_w_gate, moe_w_up, moe_w_down, final_g):
    B, S, D = x.shape
    T = B * S
    depth = ada_w.shape[0]
    n_a = mla_w_in.shape[0]
    H = MLA_HEADS
    G, HP, DK = NSA_GROUPS, NSA_HPG, NSA_DK
    d_ff = ffn_w_gate.shape[-1]
    nsa_tq = 256

    c_pad = jnp.zeros((8, D), F32).at[:B].set(c)
    mod = _modulation(c_pad, ada_w, ada_b)[:, :B].reshape(depth, B, 6, 1, D)
    kv_mod = _modulation(c_pad, kv_ada_w[None], kv_ada_b[None])[:, :B].reshape(1, B, 2, 1, D)

    rope_tabs = _rope_tables(S)
    nsa_tabs = _nsa_tables(S, nsa_tq)

    ffn_wg = ffn_w_gate[:, None]
    ffn_wu = ffn_w_up[:, None]
    ffn_wd = ffn_w_down[:, None]

    def norm1_spec(l):
        return _norm_spec(norm1_g[l], (mod, l, 0), (mod, l, 1))

    kv_spec = _norm_spec(kv_norm_g, (kv_mod, 0, 0), (kv_mod, 0, 1))
    final_spec = _norm_spec(final_g, out_dtype=F32)

    x2 = x.reshape(T, D)
    shared = None
    h = _norm(x2, norm1_spec(0), B, S)
    for l in range(depth):
        dense = l % 2 == 0
        if dense:
            norm2 = _norm_spec(norm2_g[l], (mod, l, 3), (mod, l, 4))
        else:
            wr = jnp.pad(moe_w_router[l // 2], ((0, 0), (0, LANES - N_EXPERTS)))
            br = jnp.pad(moe_b_router[l // 2], (0, LANES - N_EXPERTS)).reshape(1, LANES)
            norm2 = _norm_spec(norm2_g[l], (mod, l, 3), (mod, l, 4), router=(wr, br))
        fused2 = [norm2] if dense else []
        if l < n_a:
            w_in = mla_w_in[l]
            q_lora, kv_lora = mla_g_q.shape[1], mla_g_kv.shape[1]
            w_pad = jnp.concatenate([w_in, w_in[:, q_lora + kv_lora:]], axis=1)
            cq, ckv, kpe = _mla_in(h, w_pad, mla_g_q[l], mla_g_kv[l], rope_tabs, S)
            wq = mla_w_uq[l].reshape(q_lora, H, QK_NOPE + QK_ROPE)
            wq = jnp.concatenate([wq[:, :, :QK_NOPE].reshape(q_lora, H * QK_NOPE),
                                  wq[:, :, QK_NOPE:].reshape(q_lora, H * QK_ROPE)], axis=1)
            wkv = mla_w_ukv[l].reshape(kv_lora, H, QK_NOPE + V_HEAD)
            wkv = jnp.concatenate([wkv[:, :, :QK_NOPE].reshape(kv_lora, H * QK_NOPE),
                                   wkv[:, :, QK_NOPE:].reshape(kv_lora, H * V_HEAD)], axis=1)
            q_all = _gmm(cq, [wq[None, None]], 0, mode="cast", tm=2048, tn=1024,
                         n_out=wq.shape[1], out_dtype=BF16,
                         out_scale=(QK_NOPE + QK_ROPE) ** -0.5 * LOG2E)
            kv_all = _gmm(ckv, [wkv[None, None]], 0, mode="cast", tm=2048, tn=1024,
                          n_out=wkv.shape[1], out_dtype=BF16)
            o = _mla_attn(q_all, kv_all, kpe, rope_tabs, B, S)
            x2, *h2 = _gmm(o, [mla_w_o[:, None]], l, mode="residual", tm=512, tn=D, n_out=D,
                           out_dtype=F32, xres=x2, gate=(mod, l, 2), seq=S, norms=fused2,
                           alias_x=l > 0)
        else:
            jb = l - n_a
            w_in = nsa_w_in[jb]
            q = _gmm(h, [nsa_w_in[:, None]], jb, mode="cast", tm=1024, tn=512,
                     n_out=NSA_HEADS * DK, out_dtype=BF16, out_scale=DK ** -0.5 * LOG2E)
            wg = w_in[:, NSA_HEADS * DK:].reshape(D, G, HP, 3).transpose(0, 1, 3, 2)
            wg = jnp.pad(wg.reshape(D, G, 3 * HP), ((0, 0), (0, 0), (0, LANES - 3 * HP)))
            gates = _gmm(h, [wg.reshape(1, 1, D, G * LANES)], 0, mode="cast", tm=1024, tn=512,
                         n_out=G * LANES, out_dtype=F32)
            kvb, kvc = shared
            o = _nsa_attn(q, gates, kvb, kvc, nsa_tabs, B, S, tq=nsa_tq)
            x2, *h2 = _gmm(o, [nsa_w_o[:, None]], jb, mode="residual", tm=512, tn=D, n_out=D,
                           out_dtype=F32, xres=x2, gate=(mod, l, 2), seq=S, norms=fused2)

        after = [norm1_spec(l + 1) if l + 1 < depth else final_spec]
        if l == n_a - 1:
            after.append(kv_spec)
        if dense:
            hid = _gmm(h2[0], [ffn_wg, ffn_wu], l // 2, mode="swiglu", tm=1024, tn=512,
                       n_out=d_ff, out_dtype=BF16)
            x2 = _gmm(hid, [ffn_wd], l // 2, mode="residual", tm=256, tn=1024, n_out=D,
                      out_dtype=F32, xres=x2, gate=(mod, l, 5), seq=S)
            normed = [_norm(x2, ns, B, S) for ns in after]
        else:
            li = l // 2
            hp, route = _norm(x2, norm2, B, S)
            pos, n_rows, tile_expert, tile_valid = _moe_plan(route, T, MOE_TM)
            hs = _dispatch(hp, pos, n_rows)
            hid = _gmm(hs, [moe_w_gate, moe_w_up], li, mode="swiglu", tm=MOE_TM, tn=512,
                       n_out=d_ff, out_dtype=BF16, tile_expert=tile_expert, tile_valid=tile_valid,
                       a_packed=True)
            ys = _gmm(hid, [moe_w_down], li, mode="cast", tm=MOE_TM, tn=1024, n_out=D,
                      out_dtype=F32, tile_expert=tile_expert, tile_valid=tile_valid)
            x2, *normed = _combine(x2, ys, pos, route, (mod, l, 5), S, norms=after)
        h = normed[0]

        if l == n_a - 1:
            hkv = normed[1]
            w_kv = nsa_w_kv[None, None]
            kc = _gmm(hkv, [w_kv], 0, mode="cast", tm=1024, tn=512, n_out=2 * G * DK,
                      out_dtype=F32)
            kvb = _gmm(hkv, [w_kv], 0, mode="cast", tm=1024, tn=512, n_out=4 * G * DK,
                       out_dtype=BF16, n_off=(2 * G * DK) // 512)
            kvc = _nsa_compress(kc, jnp.stack([cmp_pos_k, cmp_pos_v]),
                                jnp.stack([cmp_k_w1, cmp_v_w1]), jnp.stack([cmp_k_w2, cmp_v_w2]),
                                B, S)
            shared = (kvb, kvc)

    return h.reshape(B, S, D)
```

```python
import functools

import numpy as np
import jax
import jax.numpy as jnp
from jax import lax
from jax.experimental import pallas as pl
from jax.experimental.pallas import tpu as pltpu

F32 = jnp.float32
BF16 = jnp.bfloat16

EPS = 1e-6
NEG = -1e30
FORCED_SCORE = 1e6
LOG2E = 1.4426950408889634

MLA_HEADS = 16
QK_NOPE = 128
QK_ROPE = 64
V_HEAD = 128
ROPE_THETA = 10000.0

NSA_HEADS = 16
NSA_GROUPS = 4
NSA_HPG = NSA_HEADS // NSA_GROUPS
NSA_DK = 128
CMP_BLOCK = 32
CMP_STRIDE = 16
SEL_BLOCK = 64
SEL_TOPN = 16
WINDOW = 512

N_EXPERTS = 8
LANES = 128
VMEM_LIMIT = 56 * 1024 * 1024

MOE_TM = 512

SEL_LANES = 32
POS_HI_LANE = 32
POS_LO_LANE = 35
MASK_BIG = 2.0 ** 30


def _cparams(n_axes):
    return pltpu.CompilerParams(
        dimension_semantics=("arbitrary",) * n_axes, vmem_limit_bytes=VMEM_LIMIT)


def _dot_nt(a, b):
    return lax.dot_general(a, b, (((1,), (1,)), ((), ())), preferred_element_type=F32)


def _mod_kernel(c_ref, w_ref, b_ref, o_ref):
    c = c_ref[...]
    ca = (c * jax.nn.sigmoid(c)).astype(BF16)
    o_ref[...] = jnp.dot(ca, w_ref[...].astype(BF16), preferred_element_type=F32) + b_ref[...]


def _modulation(c_pad, w, b, tn=1024):
    L, D, N = w.shape
    return pl.pallas_call(
        _mod_kernel,
        grid=(L, N // tn),
        in_specs=[
            pl.BlockSpec((8, D), lambda l, n: (0, 0)),
            pl.BlockSpec((None, D, tn), lambda l, n: (l, 0, n)),
            pl.BlockSpec((None, 1, tn), lambda l, n: (l, 0, n)),
        ],
        out_specs=pl.BlockSpec((None, 8, tn), lambda l, n: (l, 0, n)),
        out_shape=jax.ShapeDtypeStruct((L, 8, N), F32),
        compiler_params=_cparams(2),
        name="adaln_mod",
    )(c_pad, w, b.reshape(L, 1, N))


def _rms(x, g):
    return x * lax.rsqrt(jnp.mean(x * x, axis=-1, keepdims=True) + EPS) * g


_HI16 = 0xFFFF0000


def _pack_bf16_halves(h):
    half = h.shape[1] // 2
    bits = lax.bitcast_convert_type(h.astype(BF16).astype(F32), jnp.uint32)
    return (bits[:, :half] >> 16) | (bits[:, half:] & jnp.uint32(_HI16))


def _unpack_bf16_halves(a):
    lo = lax.bitcast_convert_type(a << 16, F32).astype(BF16)
    hi = lax.bitcast_convert_type(a & jnp.uint32(_HI16), F32).astype(BF16)
    return jnp.concatenate([lo, hi], axis=1)


def _norm_spec(g, shift=None, scale=None, router=None, out_dtype=BF16):
    return dict(g=g, shift=shift, scale=scale, router=router, out_dtype=out_dtype)


def _norm_cfg(ns):
    return (ns["shift"] is not None, ns["router"] is not None)


def _norm_io(ns, T, D, tm, batch_of, row_of):
    modulate, route = _norm_cfg(ns)
    args = [ns["g"].reshape(1, D)]
    in_specs = [pl.BlockSpec((1, D), lambda *a: (0, 0))]
    if modulate:
        for arr, l, j in (ns["shift"], ns["scale"]):
            args.append(arr)
            in_specs.append(pl.BlockSpec((None, None, None, 1, D),
                                         lambda *a, l=l, j=j: (l, batch_of(*a), j, 0, 0)))
    if route:
        wr, br = ns["router"]
        args += [wr, br]
        in_specs += [pl.BlockSpec((D, LANES), lambda *a: (0, 0)),
                     pl.BlockSpec((1, LANES), lambda *a: (0, 0))]
        out_shape = [jax.ShapeDtypeStruct((T, D // 2), jnp.uint32),
                     jax.ShapeDtypeStruct((T, LANES), F32)]
        out_specs = [pl.BlockSpec((tm, D // 2), lambda *a: (row_of(*a), 0)),
                     pl.BlockSpec((tm, LANES), lambda *a: (row_of(*a), 0))]
    else:
        out_shape = [jax.ShapeDtypeStruct((T, D), ns["out_dtype"])]
        out_specs = [pl.BlockSpec((tm, D), lambda *a: (row_of(*a), 0))]
    return args, in_specs, out_shape, out_specs


def _norm_counts(cfg):
    modulate, route = cfg
    return 1 + 2 * modulate + 2 * route, 1 + route


def _norm_apply(x, in_refs, out_refs, cfg):
    modulate, route = cfg
    g_ref, *rest = in_refs
    h = _rms(x, g_ref[...])
    if modulate:
        sh_ref, sc_ref, *rest = rest
        h = h * (1.0 + sc_ref[...]) + sh_ref[...]
    if not route:
        out_refs[0][...] = h.astype(out_refs[0].dtype)
        return
    wr_ref, br_ref = rest
    out_refs[0][...] = _pack_bf16_halves(h)
    w = wr_ref[...]
    h_hi, w_hi = h.astype(BF16), w.astype(BF16)
    h_lo = (h - h_hi.astype(F32)).astype(BF16)
    w_lo = (w - w_hi.astype(F32)).astype(BF16)
    logits = (jnp.dot(h_hi, w_hi, preferred_element_type=F32)
              + jnp.dot(h_lo, w_hi, preferred_element_type=F32)
              + jnp.dot(h_hi, w_lo, preferred_element_type=F32)) + br_ref[...]
    lane = lax.broadcasted_iota(jnp.int32, logits.shape, 1).astype(F32)
    logits = jnp.where(lane < N_EXPERTS, logits, -jnp.inf)
    v1 = jnp.max(logits, axis=-1, keepdims=True)
    i1 = jnp.min(jnp.where(logits == v1, lane, float(LANES)), axis=-1, keepdims=True)
    others = jnp.where(lane == i1, -jnp.inf, logits)
    v2 = jnp.max(others, axis=-1, keepdims=True)
    i2 = jnp.min(jnp.where(others == v2, lane, float(LANES)), axis=-1, keepdims=True)
    e = jnp.exp(v2 - v1)
    w1 = 1.0 / (1.0 + e)
    w2 = e / (1.0 + e)
    out_refs[1][...] = jnp.where(lane == 0, i1, jnp.where(lane == 1, i2, jnp.where(
        lane == 2, w1, jnp.where(lane == 3, w2, 0.0))))


def _apply_norms(x, refs_in, refs_out, cfgs):
    for cfg in cfgs:
        n_in, n_out = _norm_counts(cfg)
        _norm_apply(x, refs_in[:n_in], refs_out[:n_out], cfg)
        refs_in, refs_out = refs_in[n_in:], refs_out[n_out:]


def _norm_kernel(x_ref, *refs, cfg):
    n_in, _ = _norm_counts(cfg)
    _norm_apply(x_ref[...], refs[:n_in], refs[n_in:], cfg)


def _norm(x2, ns, B, S, ts=512):
    T, D = x2.shape
    nst = S // ts
    args, in_specs, out_shape, out_specs = _norm_io(
        ns, T, D, ts, lambda b, s: b, lambda b, s: b * nst + s)
    outs = pl.pallas_call(
        functools.partial(_norm_kernel, cfg=_norm_cfg(ns)),
        grid=(B, nst),
        in_specs=[pl.BlockSpec((ts, D), lambda b, s: (b * nst + s, 0))] + in_specs,
        out_specs=out_specs,
        out_shape=out_shape,
        compiler_params=_cparams(2),
        name="norm_mod",
    )(x2, *args)
    return outs if len(outs) > 1 else outs[0]


def _gmm_kernel(te_ref, tv_ref, nx_ref, a_ref, *refs, mode, out_scale, a_packed, n_w, layer,
                n_off, tn, n_n, norm_cfgs):
    n = pl.program_id(0)
    m = pl.program_id(1)
    w_hbm, refs = refs[:n_w], refs[n_w:]
    if mode == "residual":
        x_ref, gate_ref, *refs = refs
    n_norm_in = sum(_norm_counts(c)[0] for c in norm_cfgs)
    n_norm_out = sum(_norm_counts(c)[1] for c in norm_cfgs)
    norm_in, refs = refs[:n_norm_in], refs[n_norm_in:]
    o_ref, *refs = refs
    norm_out, refs = refs[:n_norm_out], refs[n_norm_out:]
    wst, wb, sem = refs[:n_w], refs[n_w:2 * n_w], refs[2 * n_w]
    first = jnp.logical_or(m == 0, te_ref[m] != te_ref[jnp.maximum(m - 1, 0)])
    valid = tv_ref[m] != 0

    def w_copy(i, e, nn):
        col = pl.multiple_of((nn + n_off) * tn, tn)
        return pltpu.make_async_copy(w_hbm[i].at[layer, e, :, pl.ds(col, tn)], wst[i], sem.at[i])

    @pl.when(jnp.logical_and(n == 0, m == 0))
    def _():
        for i in range(n_w):
            w_copy(i, te_ref[0], 0).start()

    @pl.when(first)
    def _():
        for i in range(n_w):
            w_copy(i, te_ref[m], n).wait()
            wb[i][...] = wst[i][...].astype(BF16)
        nm = nx_ref[m]
        same_sweep = nm >= 0
        e_next = jnp.where(same_sweep, te_ref[jnp.maximum(nm, 0)], te_ref[0])
        n_next = jnp.where(same_sweep, n, n + 1)

        @pl.when(jnp.logical_or(same_sweep, n + 1 < n_n))
        def _():
            for i in range(n_w):
                w_copy(i, e_next, n_next).start()

    @pl.when(valid)
    def _():
        a = _unpack_bf16_halves(a_ref[...]) if a_packed else a_ref[...]
        if mode == "swiglu":
            g = jnp.dot(a, wb[0][...], preferred_element_type=F32)
            u = jnp.dot(a, wb[1][...], preferred_element_type=F32)
            acc = g * jax.nn.sigmoid(g) * u
        else:
            acc = jnp.dot(a, wb[0][...], preferred_element_type=F32)
        if out_scale is not None:
            acc = acc * out_scale
        if mode == "residual":
            acc = x_ref[...] + gate_ref[...] * acc
        o_ref[...] = acc.astype(o_ref.dtype)
        _apply_norms(acc, norm_in, norm_out, norm_cfgs)

    @pl.when(jnp.logical_not(valid))
    def _():
        o_ref[...] = jnp.zeros_like(o_ref)


def _gmm(a, ws, l, *, mode, tm, tn, n_out, out_dtype, tile_expert=None, tile_valid=None,
         n_off=0, xres=None, gate=None, seq=None, out_scale=None, a_packed=False,
         alias_x=True, norms=None):
    M, ka = a.shape
    K = ws[0].shape[2]
    mt = M // tm
    n_w = len(ws)
    n_n = n_out // tn
    if tile_expert is None:
        tile_expert = jnp.zeros((mt,), jnp.int32)
        tile_valid = jnp.ones((mt,), jnp.int32)
    idx = jnp.arange(mt, dtype=jnp.int32)
    later = ((idx[None, :] > idx[:, None]) & (tile_valid[None, :] != 0)
             & (tile_expert[None, :] != tile_expert[:, None]))
    next_run = jnp.where(jnp.any(later, axis=1), jnp.argmax(later, axis=1), -1).astype(jnp.int32)
    args = [a] + list(ws)
    in_specs = [pl.BlockSpec((tm, ka), lambda n, m, te, tv, nx: (m, 0))]
    in_specs += [pl.BlockSpec(memory_space=pl.ANY)] * n_w
    aliases = {}
    if mode == "residual":
        garr, gl, gj = gate
        per_b = seq // tm
        if alias_x:
            aliases = {3 + len(args): 0}
        args += [xres, garr]
        in_specs += [
            pl.BlockSpec((tm, tn), lambda n, m, te, tv, nx: (m, n)),
            pl.BlockSpec((None, None, None, 1, tn),
                         lambda n, m, te, tv, nx: (gl, m // per_b, gj, 0, n)),
        ]
    out_shape = [jax.ShapeDtypeStruct((M, n_out), out_dtype)]
    out_specs = [pl.BlockSpec((tm, tn), lambda n, m, te, tv, nx: (m, n))]
    bare = norms is None
    norms = norms or []
    for ns in norms:
        assert tn == n_out
        n_args, n_in, n_shape, n_out_specs = _norm_io(
            ns, M, n_out, tm, lambda n, m, *_: m // (seq // tm), lambda n, m, *_: m)
        args += n_args
        in_specs += n_in
        out_shape += n_shape
        out_specs += n_out_specs
    outs = pl.pallas_call(
        functools.partial(_gmm_kernel, mode=mode, out_scale=out_scale, a_packed=a_packed,
                          n_w=n_w, layer=l, n_off=n_off, tn=tn, n_n=n_n,
                          norm_cfgs=tuple(_norm_cfg(ns) for ns in norms)),
        grid_spec=pltpu.PrefetchScalarGridSpec(
            num_scalar_prefetch=3,
            grid=(n_n, mt),
            in_specs=in_specs,
            out_specs=out_specs,
            scratch_shapes=([pltpu.VMEM((K, tn), F32) for _ in ws]
                            + [pltpu.VMEM((K, tn), BF16) for _ in ws]
                            + [pltpu.SemaphoreType.DMA((n_w,))]),
        ),
        out_shape=out_shape,
        input_output_aliases=aliases,
        compiler_params=_cparams(2),
        name="gmm_" + mode,
    )(tile_expert, tile_valid, next_run, *args)
    return outs[0] if bare else outs


def _rope_pairs(v, cos, s1, s2):
    return v * cos + pltpu.roll(v, LANES - QK_ROPE // 2, 1) * s1 + pltpu.roll(v, QK_ROPE // 2, 1) * s2


def _mla_in_kernel(a_ref, w_ref, gq_ref, gkv_ref, cos_ref, s1_ref, s2_ref,
                   cq_ref, ckv_ref, kpe_ref, wb, *, q_lora, kv_lora):
    @pl.when(pl.program_id(0) == 0)
    def _():
        wb[...] = w_ref[...].astype(BF16)

    acc = jnp.dot(a_ref[...], wb[...], preferred_element_type=F32)
    cq_ref[...] = _rms(acc[:, :q_lora], gq_ref[...]).astype(BF16)
    ckv_ref[...] = _rms(acc[:, q_lora:q_lora + kv_lora], gkv_ref[...]).astype(BF16)
    v = acc[:, q_lora + kv_lora:]
    kpe_ref[...] = _rope_pairs(v, cos_ref[...], s1_ref[...], s2_ref[...]).astype(BF16)


def _mla_in(h, w_pad, g_q, g_kv, rope_tabs, S, tm=512):
    T, D = h.shape
    q_lora, kv_lora = g_q.shape[0], g_kv.shape[0]
    n_all = w_pad.shape[1]
    nst = S // tm
    tab_spec = pl.BlockSpec((tm, LANES), lambda i: (i % nst, 0))
    return pl.pallas_call(
        functools.partial(_mla_in_kernel, q_lora=q_lora, kv_lora=kv_lora),
        grid=(T // tm,),
        in_specs=[
            pl.BlockSpec((tm, D), lambda i: (i, 0)),
            pl.BlockSpec((D, n_all), lambda i: (0, 0)),
            pl.BlockSpec((1, q_lora), lambda i: (0, 0)),
            pl.BlockSpec((1, kv_lora), lambda i: (0, 0)),
            tab_spec, tab_spec, tab_spec,
        ],
        out_specs=[
            pl.BlockSpec((tm, q_lora), lambda i: (i, 0)),
            pl.BlockSpec((tm, kv_lora), lambda i: (i, 0)),
            pl.BlockSpec((tm, LANES), lambda i: (i, 0)),
        ],
        out_shape=[
            jax.ShapeDtypeStruct((T, q_lora), BF16),
            jax.ShapeDtypeStruct((T, kv_lora), BF16),
            jax.ShapeDtypeStruct((T, LANES), BF16),
        ],
        scratch_shapes=[pltpu.VMEM((D, n_all), BF16)],
        compiler_params=_cparams(1),
        name="mla_in",
    )(h, w_pad, g_q.reshape(1, -1), g_kv.reshape(1, -1), *rope_tabs)


def _mla_attn_kernel(qn_ref, qp_ref, kv_ref, kpe_ref, cos_ref, s1_ref, s2_ref, o_ref,
                     m_sc, l_sc, acc_sc, *, tq):
    qi = pl.program_id(2)
    head_w = QK_NOPE + V_HEAD
    qp = _rope_pairs(qp_ref[...].astype(F32), cos_ref[...], s1_ref[...], s2_ref[...])
    lane = lax.broadcasted_iota(jnp.int32, qp.shape, 1)
    qs = []
    for hh in range(2):
        keep = (lane < QK_ROPE) if hh == 0 else (lane >= QK_ROPE)
        qs.append(jnp.concatenate(
            [qn_ref[:, hh * QK_NOPE:(hh + 1) * QK_NOPE], jnp.where(keep, qp, 0.0).astype(BF16)],
            axis=1))
    m_sc[...] = jnp.full_like(m_sc, NEG)
    l_sc[...] = jnp.zeros_like(l_sc)
    acc_sc[...] = jnp.zeros_like(acc_sc)

    def tile(ks, nk, r0, diag_off):
        kpe = kpe_ref[pl.ds(ks, nk), :]
        for hh in range(2):
            c0 = hh * head_w
            k = jnp.concatenate([kv_ref[pl.ds(ks, nk), c0:c0 + QK_NOPE], kpe], axis=1)
            s = _dot_nt(qs[hh][r0:], k)
            if diag_off is not None:
                row = lax.broadcasted_iota(jnp.int32, s.shape, 0) + r0
                col = lax.broadcasted_iota(jnp.int32, s.shape, 1) + diag_off
                s = jnp.where(col <= row, s, NEG)
            m_old = m_sc[hh, r0:]
            m_new = jnp.maximum(m_old, jnp.max(s, axis=-1, keepdims=True))
            alpha = jnp.exp2(m_old - m_new)
            p = jnp.exp2(s - jnp.concatenate([m_new] * (nk // LANES), axis=1))
            l_sc[hh, r0:] = alpha * l_sc[hh, r0:] + jnp.sum(p, axis=-1, keepdims=True)
            vv = kv_ref[pl.ds(ks, nk), c0 + QK_NOPE:c0 + head_w]
            acc_sc[hh, r0:] = alpha * acc_sc[hh, r0:] + jnp.dot(p.astype(BF16), vv,
                                                                preferred_element_type=F32)
            m_sc[hh, r0:] = m_new

    def body(j, carry):
        tile(pl.multiple_of(j * tq, tq), tq, 0, None)
        return carry

    lax.fori_loop(0, qi, body, 0)
    tile(pl.multiple_of(qi * tq, tq), tq, 0, 0)
    for hh in range(2):
        o_ref[:, hh * V_HEAD:(hh + 1) * V_HEAD] = (acc_sc[hh] / l_sc[hh]).astype(o_ref.dtype)


def _mla_attn(q_all, kv_all, kpe, rope_tabs, B, S, tq=512):
    T = q_all.shape[0]
    H = MLA_HEADS
    nqt = S // tq
    pair_w = 2 * QK_NOPE
    kv_pair_w = 2 * (QK_NOPE + V_HEAD)
    tab_spec = pl.BlockSpec((tq, LANES), lambda b, p, i: (i, 0))
    return pl.pallas_call(
        functools.partial(_mla_attn_kernel, tq=tq),
        grid=(B, H // 2, nqt),
        in_specs=[
            pl.BlockSpec((tq, pair_w), lambda b, p, i: (b * nqt + i, p)),
            pl.BlockSpec((tq, LANES), lambda b, p, i: (b * nqt + i, (H * QK_NOPE) // LANES + p)),
            pl.BlockSpec((S, kv_pair_w), lambda b, p, i: (b, p)),
            pl.BlockSpec((S, LANES), lambda b, p, i: (b, 0)),
            tab_spec, tab_spec, tab_spec,
        ],
        out_specs=pl.BlockSpec((tq, pair_w), lambda b, p, i: (b * nqt + i, p)),
        out_shape=jax.ShapeDtypeStruct((T, H * V_HEAD), BF16),
        scratch_shapes=[pltpu.VMEM((2, tq, LANES), F32), pltpu.VMEM((2, tq, LANES), F32),
                        pltpu.VMEM((2, tq, V_HEAD), F32)],
        compiler_params=_cparams(3),
        name="mla_attn",
    )(q_all, q_all, kv_all, kpe, *rope_tabs)


def _nsa_compress_kernel(t_ref, pe_ref, w1_ref, w2_ref, o_ref, *, n_chunk):
    half = CMP_BLOCK // 2
    pe = pe_ref[...]

    def chunk_rows(off):
        cols = []
        for l in range(half):
            x = t_ref[pl.ds(l, n_chunk, stride=CMP_STRIDE), :] + pe[off + l:off + l + 1, :]
            cols.append(x.astype(BF16))
        return jnp.concatenate(cols, axis=1)

    w1 = w1_ref[...].astype(BF16)
    kdim = half * NSA_DK
    p0 = jnp.dot(chunk_rows(0), w1[:kdim], preferred_element_type=F32)
    p1 = jnp.dot(chunk_rows(half), w1[kdim:], preferred_element_type=F32)
    pre = p0 + pltpu.roll(p1, n_chunk - 1, 0)
    hid = (pre * jax.nn.sigmoid(pre)).astype(BF16)
    out = jnp.dot(hid, w2_ref[...].astype(BF16), preferred_element_type=F32)
    rown = lax.broadcasted_iota(jnp.int32, out.shape, 0)
    o_ref[...] = jnp.where(rown < n_chunk - 1, out, 0.0).astype(o_ref.dtype)


def _nsa_compress(kc, pe, w1, w2, B, S):
    G = NSA_GROUPS
    n_chunk = S // CMP_STRIDE
    return pl.pallas_call(
        functools.partial(_nsa_compress_kernel, n_chunk=n_chunk),
        grid=(B, 2, G),
        in_specs=[
            pl.BlockSpec((S, NSA_DK), lambda b, i, g: (b, i * G + g)),
            pl.BlockSpec((None, CMP_BLOCK, NSA_DK), lambda b, i, g: (i, 0, 0)),
            pl.BlockSpec((None, CMP_BLOCK * NSA_DK, NSA_DK), lambda b, i, g: (i, 0, 0)),
            pl.BlockSpec((None, NSA_DK, NSA_DK), lambda b, i, g: (i, 0, 0)),
        ],
        out_specs=pl.BlockSpec((None, None, None, n_chunk, NSA_DK), lambda b, i, g: (b, i, g, 0, 0)),
        out_shape=jax.ShapeDtypeStruct((B, 2, G, n_chunk, NSA_DK), BF16),
        compiler_params=_cparams(3),
        name="nsa_compress",
    )(kc, pe, w1, w2)


def _nsa_attn_kernel(q_ref, gt_ref, ks_ref, vs_ref, kw_ref, vw_ref, kc_ref, vc_ref, qtab_ref,
                     ktab_ref, ctab_ref, mt_ref, cbias_ref, wbias_ref, o_ref, m_sc, acc_sc,
                     *, tq, tk, n_sel):
    qi = pl.program_id(2)
    t0 = qi * tq
    HP = NSA_HPG
    R = HP * tq
    qb = q_ref[...]
    qs = jnp.concatenate([qb[:, h * NSA_DK:(h + 1) * NSA_DK] for h in range(HP)], axis=0)
    qtab = qtab_ref[...]
    qa = jnp.concatenate([qs, qtab], axis=1)
    rowi = lax.broadcasted_iota(jnp.int32, (R, 1), 0)
    tcol = (t0 + rowi % tq).astype(F32)

    kc = jnp.concatenate([kc_ref[...], ctab_ref[...]], axis=1)
    s = _dot_nt(qa, kc) + jnp.concatenate([cbias_ref[...]] * HP, axis=0)
    p = jnp.exp2(s - jnp.max(s, axis=-1, keepdims=True))
    l = jnp.sum(p, axis=-1, keepdims=True)
    p_cmp = jnp.where(tcol >= CMP_BLOCK - 1, p / l, 0.0)
    o_cmp = jnp.dot(p_cmp.astype(BF16), vc_ref[...], preferred_element_type=F32)

    ps = p_cmp[0:tq]
    for h in range(1, HP):
        ps = ps + p_cmp[h * tq:(h + 1) * tq]
    ps_hi = ps.astype(BF16)
    ps_lo = (ps - ps_hi.astype(F32)).astype(BF16)
    imp = _dot_nt(mt_ref[...], ps_hi) + _dot_nt(mt_ref[...], ps_lo)
    jrow = lax.broadcasted_iota(jnp.int32, (n_sel, tq), 0)
    blk_t = (t0 + lax.broadcasted_iota(jnp.int32, (n_sel, tq), 1)) // SEL_BLOCK
    forced = (jrow == 0) | (jrow == blk_t) | (jrow == blk_t - 1)
    imp = jnp.where(forced, FORCED_SCORE, imp)
    imp = jnp.where(jrow > blk_t, -1.0, imp)
    rank = jnp.zeros((n_sel, tq), F32)
    for k in range(n_sel):
        rk = imp[k:k + 1, :]
        beats = (rk > imp) | ((rk == imp) & (jrow > k))
        rank = rank + jnp.where(beats, 1.0, 0.0)
    sel_t = jnp.where(rank < SEL_TOPN, 1.0, 0.0)
    sel_t = jnp.concatenate([sel_t, jnp.zeros((LANES - n_sel, tq), F32)], axis=0)
    sel_neg = ((sel_t.T - 1.0) * MASK_BIG).astype(BF16)
    lane = lax.broadcasted_iota(jnp.int32, (R, LANES), 1)
    qx = jnp.where(lane < SEL_LANES, jnp.concatenate([sel_neg] * HP, axis=0), qtab)
    qsel = jnp.concatenate([qs, qx], axis=1)

    m_sc[...] = jnp.full_like(m_sc, NEG)
    acc_sc[...] = jnp.zeros_like(acc_sc)
    ones = jnp.ones((tk, LANES), BF16)

    def tile(j, causal):
        ks = pl.multiple_of(j * tk, tk)
        k = jnp.concatenate([ks_ref[pl.ds(ks, tk), :], ktab_ref[pl.ds(ks, tk), :]], axis=1)
        sc = _dot_nt(qsel, k)
        if causal:
            kpos = (ks + lax.broadcasted_iota(jnp.int32, (1, tk), 1)).astype(F32)
            sc = jnp.where(kpos <= tcol, sc, NEG)
        m_old = m_sc[...]
        m_new = jnp.maximum(m_old, jnp.max(sc, axis=-1, keepdims=True))
        alpha = jnp.exp2(m_old - m_new)
        pp = jnp.exp2(sc - jnp.concatenate([m_new] * (tk // LANES), axis=1))
        vv = jnp.concatenate([vs_ref[pl.ds(ks, tk), :], ones], axis=1)
        acc_sc[...] = (jnp.concatenate([alpha, alpha], axis=1) * acc_sc[...]
                       + jnp.dot(pp.astype(BF16), vv, preferred_element_type=F32))
        m_sc[...] = m_new

    def body(j, carry):
        tile(j, False)
        return carry

    n_full = t0 // tk
    lax.fori_loop(0, n_full, body, 0)
    for c in range(tq // tk):
        tile(n_full + c, True)
    acc = acc_sc[...]
    o_sel = acc[:, :NSA_DK] / acc[:, NSA_DK:]

    span = WINDOW + tq
    ws = pl.multiple_of(jnp.maximum(t0 - WINDOW, 0), tq)
    wbias = wbias_ref[(t0 - ws) // tq]
    kw = jnp.concatenate([kw_ref[pl.ds(ws, span), :], ktab_ref[pl.ds(ws, span), :]], axis=1)
    sw = _dot_nt(qa, kw) + jnp.concatenate([wbias] * HP, axis=0)
    pw = jnp.exp2(sw - jnp.max(sw, axis=-1, keepdims=True))
    vw = jnp.concatenate([vw_ref[pl.ds(ws, span), :], jnp.ones((span, LANES), BF16)], axis=1)
    rw = jnp.dot(pw.astype(BF16), vw, preferred_element_type=F32)
    o_win = rw[:, :NSA_DK] / rw[:, NSA_DK:]

    gt = jax.nn.sigmoid(gt_ref[...])

    def gcol(i):
        return jnp.concatenate([gt[:, i * HP + h:i * HP + h + 1] for h in range(HP)], axis=0)

    o = gcol(0) * o_cmp + gcol(1) * o_sel + gcol(2) * o_win
    o_ref[...] = jnp.concatenate([o[h * tq:(h + 1) * tq] for h in range(HP)], axis=1).astype(o_ref.dtype)


def _nsa_attn(q, gates, kvb, kvc, tabs, B, S, tq=128, tk=256):
    T = q.shape[0]
    G, DK, HP = NSA_GROUPS, NSA_DK, NSA_HPG
    qtab, ktab, ctab, mt, cbias, wbias = tabs
    nqt = S // tq
    n_chunk = kvc.shape[3]
    n_sel = S // SEL_BLOCK

    def kv_spec(i):
        return pl.BlockSpec((S, DK), lambda b, g, t, i=i: (b, i * G + g))

    def cmp_spec(i):
        return pl.BlockSpec((None, None, None, n_chunk, DK), lambda b, g, t, i=i: (b, i, g, 0, 0))

    return pl.pallas_call(
        functools.partial(_nsa_attn_kernel, tq=tq, tk=tk, n_sel=n_sel),
        grid=(B, G, nqt),
        in_specs=[
            pl.BlockSpec((tq, HP * DK), lambda b, g, t: (b * nqt + t, g)),
            pl.BlockSpec((tq, LANES), lambda b, g, t: (b * nqt + t, g)),
            kv_spec(0), kv_spec(1), kv_spec(2), kv_spec(3),
            cmp_spec(0), cmp_spec(1),
            pl.BlockSpec((None, HP * tq, LANES), lambda b, g, t: (g, 0, 0)),
            pl.BlockSpec(ktab.shape, lambda b, g, t: (0, 0)),
            pl.BlockSpec(ctab.shape, lambda b, g, t: (0, 0)),
            pl.BlockSpec(mt.shape, lambda b, g, t: (0, 0)),
            pl.BlockSpec((tq, n_chunk), lambda b, g, t: (t, 0)),
            pl.BlockSpec(wbias.shape, lambda b, g, t: (0, 0, 0)),
        ],
        out_specs=pl.BlockSpec((tq, HP * DK), lambda b, g, t: (b * nqt + t, g)),
        out_shape=jax.ShapeDtypeStruct((T, NSA_HEADS * DK), BF16),
        scratch_shapes=[pltpu.VMEM((HP * tq, LANES), F32),
                        pltpu.VMEM((HP * tq, DK + LANES), F32)],
        compiler_params=_cparams(3),
        name="nsa_attn",
    )(q, gates, kvb, kvb, kvb, kvb, kvc, kvc, qtab, ktab, ctab, mt, cbias, wbias)


def _row_copy(src_hbm, dst_vmem, sem, src_row, dst_row):
    return pltpu.make_async_copy(src_hbm.at[pl.ds(src_row, 1), :], dst_vmem.at[pl.ds(dst_row, 1), :], sem)


def _dispatch_kernel(pos_ref, src_ref, init_hbm, out_hbm, sem, *, chunk, n_tok):
    del init_hbm
    base = pl.program_id(0) * chunk

    def start(r, c):
        t = base + r
        _row_copy(src_ref, out_hbm, sem, r, pos_ref[t]).start()
        _row_copy(src_ref, out_hbm, sem, r, pos_ref[n_tok + t]).start()
        return c

    def wait(r, c):
        _row_copy(src_ref, out_hbm, sem, 0, 0).wait()
        return c

    lax.fori_loop(0, chunk, start, 0, unroll=8)
    lax.fori_loop(0, 2 * chunk, wait, 0, unroll=8)


def _dispatch(h_packed, pos, n_rows, chunk=512):
    T, W = h_packed.shape
    n_steps = T // chunk
    init = jnp.zeros((n_rows, W), h_packed.dtype)
    return pl.pallas_call(
        functools.partial(_dispatch_kernel, chunk=chunk, n_tok=T),
        grid_spec=pltpu.PrefetchScalarGridSpec(
            num_scalar_prefetch=1,
            grid=(n_steps,),
            in_specs=[pl.BlockSpec((chunk, W), lambda i, pos: (i, 0)),
                      pl.BlockSpec(memory_space=pl.ANY)],
            out_specs=pl.BlockSpec(memory_space=pl.ANY),
            scratch_shapes=[pltpu.SemaphoreType.DMA(())],
        ),
        out_shape=jax.ShapeDtypeStruct((n_rows, W), h_packed.dtype),
        input_output_aliases={2: 0},
        compiler_params=_cparams(1),
        name="moe_dispatch",
    )(pos, h_packed, init)


def _combine_kernel(pos_ref, y_hbm, x_ref, gate_ref, rt_ref, *refs, tg, n_tok, n_steps,
                    norm_cfgs):
    n_norm_in = sum(_norm_counts(c)[0] for c in norm_cfgs)
    n_norm_out = sum(_norm_counts(c)[1] for c in norm_cfgs)
    norm_in, (o_ref, *refs) = refs[:n_norm_in], refs[n_norm_in:]
    norm_out, (buf, sem) = refs[:n_norm_out], refs[n_norm_out:]
    i = pl.program_id(0)
    slot = i % 2

    def issue(step, slot_):
        def start(r, c):
            t = step * tg + r
            _row_copy(y_hbm, buf.at[slot_, 0], sem.at[slot_], pos_ref[t], r).start()
            _row_copy(y_hbm, buf.at[slot_, 1], sem.at[slot_], pos_ref[n_tok + t], r).start()
            return c

        lax.fori_loop(0, tg, start, 0, unroll=8)

    @pl.when(i == 0)
    def _():
        issue(0, 0)

    @pl.when(i + 1 < n_steps)
    def _():
        issue(i + 1, 1 - slot)

    def wait(r, c):
        _row_copy(y_hbm, buf.at[slot, 0], sem.at[slot], 0, 0).wait()
        return c

    lax.fori_loop(0, 2 * tg, wait, 0, unroll=8)
    rt = rt_ref[...]
    y = rt[:, 2:3] * buf[slot, 0] + rt[:, 3:4] * buf[slot, 1]
    x_new = x_ref[...] + gate_ref[...] * y
    o_ref[...] = x_new
    _apply_norms(x_new, norm_in, norm_out, norm_cfgs)


def _combine(x2, y_sorted, pos, route, gate, S, norms=(), tg=256):
    T, D = x2.shape
    garr, gl, gj = gate
    per_b = S // tg
    n_steps = T // tg
    args = [y_sorted, x2, garr, route]
    in_specs = [
        pl.BlockSpec(memory_space=pl.ANY),
        pl.BlockSpec((tg, D), lambda i, pos: (i, 0)),
        pl.BlockSpec((None, None, None, 1, D), lambda i, pos: (gl, i // per_b, gj, 0, 0)),
        pl.BlockSpec((tg, LANES), lambda i, pos: (i, 0)),
    ]
    out_shape = [jax.ShapeDtypeStruct((T, D), F32)]
    out_specs = [pl.BlockSpec((tg, D), lambda i, pos: (i, 0))]
    for ns in norms:
        n_args, n_in, n_shape, n_out_specs = _norm_io(
            ns, T, D, tg, lambda i, *_: i // per_b, lambda i, *_: i)
        args += n_args
        in_specs += n_in
        out_shape += n_shape
        out_specs += n_out_specs
    outs = pl.pallas_call(
        functools.partial(_combine_kernel, tg=tg, n_tok=T, n_steps=n_steps,
                          norm_cfgs=tuple(_norm_cfg(ns) for ns in norms)),
        grid_spec=pltpu.PrefetchScalarGridSpec(
            num_scalar_prefetch=1,
            grid=(n_steps,),
            in_specs=in_specs,
            out_specs=out_specs,
            scratch_shapes=[pltpu.VMEM((2, 2, tg, D), F32), pltpu.SemaphoreType.DMA((2,))],
        ),
        out_shape=out_shape,
        input_output_aliases={2: 0},
        compiler_params=_cparams(1),
        name="moe_combine",
    )(pos, *args)
    return outs if norms else outs[0]


def _moe_plan(route, T, tm):
    E = N_EXPERTS
    n_tiles = (2 * T) // tm + E
    P = n_tiles * tm
    e_pair = jnp.concatenate([route[:, 0], route[:, 1]]).astype(jnp.int32)
    onehot = (e_pair[:, None] == jnp.arange(E, dtype=jnp.int32)[None, :]).astype(jnp.int32)
    csum = jnp.cumsum(onehot, axis=0)
    rank = jnp.sum(onehot * (csum - 1), axis=1)
    counts = csum[-1]
    padded = ((counts + tm - 1) // tm) * tm
    ends = jnp.cumsum(padded)
    pos = (ends - padded)[e_pair] + rank
    tile_start = jnp.arange(n_tiles, dtype=jnp.int32) * tm
    tile_valid = (tile_start < ends[-1]).astype(jnp.int32)
    tile_expert = jnp.sum((tile_start[:, None] >= ends[None, :]).astype(jnp.int32), axis=1)
    last_valid = tile_expert[jnp.maximum(ends[-1] // tm - 1, 0)]
    tile_expert = jnp.where(tile_valid == 1, tile_expert, last_valid).astype(jnp.int32)
    return pos.astype(jnp.int32), P, tile_expert, tile_valid


def _rope_tables(S):
    d = QK_ROPE
    inv = ROPE_THETA ** (-jnp.arange(0, d, 2, dtype=F32) / d)
    ang = jnp.arange(S).astype(F32)[:, None] * inv[None, :]
    cos, sin = jnp.cos(ang), jnp.sin(ang)
    z = jnp.zeros_like(sin)
    return (jnp.concatenate([cos, cos, cos, cos], axis=1),
            jnp.concatenate([-sin, z, -sin, z], axis=1),
            jnp.concatenate([z, sin, z, sin], axis=1))


def _pos_columns(pos):
    tab = np.zeros((pos.shape[0], LANES), np.float32)
    tab[:, POS_HI_LANE:POS_HI_LANE + 3] = (LANES * (pos // LANES))[:, None]
    tab[:, POS_LO_LANE:POS_LO_LANE + 3] = (pos % LANES)[:, None]
    return tab


def _nsa_tables(S, tq):
    n_cmp = (S - CMP_BLOCK) // CMP_STRIDE + 1
    n_sel = S // SEL_BLOCK
    n_chunk = S // CMP_STRIDE
    tok = np.arange(n_cmp)[:, None] * CMP_STRIDE + np.arange(CMP_BLOCK)[None, :]
    blk = tok // SEL_BLOCK
    m = (blk[:, :, None] == np.arange(n_sel)[None, None, :]).sum(axis=1) / CMP_BLOCK
    mt = np.zeros((n_sel, n_chunk), np.float32)
    mt[:, :n_cmp] = m.T
    keys = np.arange(S)
    ktab = _pos_columns(keys)
    ktab[:, :SEL_LANES] = (keys[:, None] // SEL_BLOCK == np.arange(SEL_LANES)[None, :])
    ctab = _pos_columns(np.arange(n_chunk) * CMP_STRIDE + (CMP_BLOCK - 1))
    slopes = jnp.asarray(2.0 ** (-8.0 * np.arange(1, NSA_HEADS + 1) / NSA_HEADS), F32)
    a = slopes * LOG2E
    a_hi = a.astype(BF16)
    r1 = a - a_hi.astype(F32)
    a_mid = r1.astype(BF16)
    a_lo = (r1 - a_mid.astype(F32)).astype(BF16)
    pieces = jnp.stack([a_hi, a_mid, a_lo], axis=-1)
    qrow = jnp.zeros((NSA_HEADS, LANES), BF16)
    qrow = qrow.at[:, POS_HI_LANE:POS_HI_LANE + 3].set(pieces)
    qrow = qrow.at[:, POS_LO_LANE:POS_LO_LANE + 3].set(pieces)
    qtab = jnp.repeat(qrow.reshape(NSA_GROUPS, NSA_HPG, LANES), tq, axis=1)
    cend = np.arange(n_chunk) * CMP_STRIDE + (CMP_BLOCK - 1)
    cvis = (keys[:, None] >= cend[None, :]) & (np.arange(n_chunk)[None, :] < n_cmp)
    cbias = np.where(cvis, 0.0, NEG).astype(np.float32)
    span = WINDOW + tq
    d = (np.arange(WINDOW // tq + 1)[:, None, None] * tq + np.arange(tq)[None, :, None]
         - np.arange(span)[None, None, :])
    wbias = np.where((d >= 0) & (d < WINDOW), 0.0, NEG).astype(np.float32)
    return (qtab, jnp.asarray(ktab, BF16), jnp.asarray(ctab, BF16), jnp.asarray(mt, BF16),
            jnp.asarray(cbias), jnp.asarray(wbias))


def kernel(x, c, ada_w, ada_b, norm1_g, norm2_g, mla_w_in, mla_g_q, mla_g_kv, mla_w_uq, mla_w_ukv, mla_w_o, kv_ada_w, kv_ada_b, kv_norm_g, nsa_w_kv, cmp_pos_k, cmp_pos_v, cmp_k_w1, cmp_k_w2, cmp_v_w1, cmp_v_w2, nsa_w_in, nsa_w_o, ffn_w_gate, ffn_w_up, ffn_w_down, moe_w_router, moe_b_router, moe_w_gate, moe_w_up, moe_w_down, final_g):
    B, S, D = x.shape
    T = B * S
    depth = ada_w.shape[0]
    n_a = mla_w_in.shape[0]
    H = MLA_HEADS
    G, HP, DK = NSA_GROUPS, NSA_HPG, NSA_DK
    d_ff = ffn_w_gate.shape[-1]
    nsa_tq = 512

    c_pad = jnp.zeros((8, D), F32).at[:B].set(c)
    mod = _modulation(c_pad, ada_w, ada_b)[:, :B].reshape(depth, B, 6, 1, D)
    kv_mod = _modulation(c_pad, kv_ada_w[None], kv_ada_b[None])[:, :B].reshape(1, B, 2, 1, D)

    rope_tabs = _rope_tables(S)
    nsa_tabs = _nsa_tables(S, nsa_tq)

    ffn_wg = ffn_w_gate[:, None]
    ffn_wu = ffn_w_up[:, None]
    ffn_wd = ffn_w_down[:, None]

    def norm1_spec(l):
        return _norm_spec(norm1_g[l], (mod, l, 0), (mod, l, 1))

    kv_spec = _norm_spec(kv_norm_g, (kv_mod, 0, 0), (kv_mod, 0, 1))
    final_spec = _norm_spec(final_g, out_dtype=F32)

    x2 = x.reshape(T, D)
    shared = None
    h = _norm(x2, norm1_spec(0), B, S)
    for l in range(depth):
        dense = l % 2 == 0
        if dense:
            norm2 = _norm_spec(norm2_g[l], (mod, l, 3), (mod, l, 4))
        else:
            wr = jnp.pad(moe_w_router[l // 2], ((0, 0), (0, LANES - N_EXPERTS)))
            br = jnp.pad(moe_b_router[l // 2], (0, LANES - N_EXPERTS)).reshape(1, LANES)
            norm2 = _norm_spec(norm2_g[l], (mod, l, 3), (mod, l, 4), router=(wr, br))
        fused2 = [norm2] if dense else []
        if l < n_a:
            w_in = mla_w_in[l]
            q_lora, kv_lora = mla_g_q.shape[1], mla_g_kv.shape[1]
            w_pad = jnp.concatenate([w_in, w_in[:, q_lora + kv_lora:]], axis=1)
            cq, ckv, kpe = _mla_in(h, w_pad, mla_g_q[l], mla_g_kv[l], rope_tabs, S)
            wq = mla_w_uq[l].reshape(q_lora, H, QK_NOPE + QK_ROPE)
            wq = jnp.concatenate([wq[:, :, :QK_NOPE].reshape(q_lora, H * QK_NOPE),
                                  wq[:, :, QK_NOPE:].reshape(q_lora, H * QK_ROPE)], axis=1)
            q_all = _gmm(cq, [wq[None, None]], 0, mode="cast", tm=2048, tn=1024,
                         n_out=wq.shape[1], out_dtype=BF16,
                         out_scale=(QK_NOPE + QK_ROPE) ** -0.5 * LOG2E)
            kv_all = _gmm(ckv, [mla_w_ukv[:, None]], l, mode="cast", tm=2048, tn=1024,
                          n_out=mla_w_ukv.shape[-1], out_dtype=BF16)
            o = _mla_attn(q_all, kv_all, kpe, rope_tabs, B, S)
            x2, *h2 = _gmm(o, [mla_w_o[:, None]], l, mode="residual", tm=512, tn=D, n_out=D,
                           out_dtype=F32, xres=x2, gate=(mod, l, 2), seq=S, norms=fused2,
                           alias_x=l > 0)
        else:
            jb = l - n_a
            w_in = nsa_w_in[jb]
            q = _gmm(h, [nsa_w_in[:, None]], jb, mode="cast", tm=1024, tn=512,
                     n_out=NSA_HEADS * DK, out_dtype=BF16, out_scale=DK ** -0.5 * LOG2E)
            wg = w_in[:, NSA_HEADS * DK:].reshape(D, G, HP, 3).transpose(0, 1, 3, 2)
            wg = jnp.pad(wg.reshape(D, G, 3 * HP), ((0, 0), (0, 0), (0, LANES - 3 * HP)))
            gates = _gmm(h, [wg.reshape(1, 1, D, G * LANES)], 0, mode="cast", tm=1024, tn=512,
                         n_out=G * LANES, out_dtype=F32)
            kvb, kvc = shared
            o = _nsa_attn(q, gates, kvb, kvc, nsa_tabs, B, S, tq=nsa_tq)
            x2, *h2 = _gmm(o, [nsa_w_o[:, None]], jb, mode="residual", tm=512, tn=D, n_out=D,
                           out_dtype=F32, xres=x2, gate=(mod, l, 2), seq=S, norms=fused2)

        after = [norm1_spec(l + 1) if l + 1 < depth else final_spec]
        if l == n_a - 1:
            after.append(kv_spec)
        if dense:
            hid = _gmm(h2[0], [ffn_wg, ffn_wu], l // 2, mode="swiglu", tm=1024, tn=512,
                       n_out=d_ff, out_dtype=BF16)
            x2 = _gmm(hid, [ffn_wd], l // 2, mode="residual", tm=1024, tn=512, n_out=D,
                      out_dtype=F32, xres=x2, gate=(mod, l, 5), seq=S)
            normed = [_norm(x2, ns, B, S) for ns in after]
        else:
            li = l // 2
            hp, route = _norm(x2, norm2, B, S)
            pos, n_rows, tile_expert, tile_valid = _moe_plan(route, T, MOE_TM)
            hs = _dispatch(hp, pos, n_rows)
            hid = _gmm(hs, [moe_w_gate, moe_w_up], li, mode="swiglu", tm=MOE_TM, tn=512,
                       n_out=d_ff, out_dtype=BF16, tile_expert=tile_expert, tile_valid=tile_valid,
                       a_packed=True)
            ys = _gmm(hid, [moe_w_down], li, mode="cast", tm=MOE_TM, tn=1024, n_out=D,
                      out_dtype=F32, tile_expert=tile_expert, tile_valid=tile_valid)
            x2, *normed = _combine(x2, ys, pos, route, (mod, l, 5), S, norms=after)
        h = normed[0]

        if l == n_a - 1:
            hkv = normed[1]
            w_kv = nsa_w_kv[None, None]
            kc = _gmm(hkv, [w_kv], 0, mode="cast", tm=1024, tn=512, n_out=2 * G * DK,
                      out_dtype=F32)
            kvb = _gmm(hkv, [w_kv], 0, mode="cast", tm=1024, tn=512, n_out=4 * G * DK,
                       out_dtype=BF16, n_off=(2 * G * DK) // 512)
            kvc = _nsa_compress(kc, jnp.stack([cmp_pos_k, cmp_pos_v]),
                                jnp.stack([cmp_k_w1, cmp_v_w1]), jnp.stack([cmp_k_w2, cmp_v_w2]),
                                B, S)
            shared = (kvb, kvc)

    return h.reshape(B, S, D)
```

```python
import functools

import numpy as np
import jax
import jax.numpy as jnp
from jax import lax
from jax.experimental import pallas as pl
from jax.experimental.pallas import tpu as pltpu

F32 = jnp.float32
BF16 = jnp.bfloat16

EPS = 1e-6
NEG = -1e30
FORCED_SCORE = 1e6
LOG2E = 1.4426950408889634

MLA_HEADS = 16
QK_NOPE = 128
QK_ROPE = 64
V_HEAD = 128
ROPE_THETA = 10000.0

NSA_HEADS = 16
NSA_GROUPS = 4
NSA_HPG = NSA_HEADS // NSA_GROUPS
NSA_DK = 128
CMP_BLOCK = 32
CMP_STRIDE = 16
SEL_BLOCK = 64
SEL_TOPN = 16
WINDOW = 512

N_EXPERTS = 8
LANES = 128
VMEM_LIMIT = 56 * 1024 * 1024

MOE_TM = 512

SEL_LANES = 32
POS_HI_LANE = 32
POS_LO_LANE = 35
MASK_BIG = 2.0 ** 30


def _cparams(n_axes):
    return pltpu.CompilerParams(
        dimension_semantics=("arbitrary",) * n_axes, vmem_limit_bytes=VMEM_LIMIT)


def _dot_nt(a, b):
    return lax.dot_general(a, b, (((1,), (1,)), ((), ())), preferred_element_type=F32)


def _mod_kernel(c_ref, w_ref, b_ref, o_ref):
    c = c_ref[...]
    ca = (c * jax.nn.sigmoid(c)).astype(BF16)
    o_ref[...] = jnp.dot(ca, w_ref[...].astype(BF16), preferred_element_type=F32) + b_ref[...]


def _modulation(c_pad, w, b, tn=1024):
    L, D, N = w.shape
    return pl.pallas_call(
        _mod_kernel,
        grid=(L, N // tn),
        in_specs=[
            pl.BlockSpec((8, D), lambda l, n: (0, 0)),
            pl.BlockSpec((None, D, tn), lambda l, n: (l, 0, n)),
            pl.BlockSpec((None, 1, tn), lambda l, n: (l, 0, n)),
        ],
        out_specs=pl.BlockSpec((None, 8, tn), lambda l, n: (l, 0, n)),
        out_shape=jax.ShapeDtypeStruct((L, 8, N), F32),
        compiler_params=_cparams(2),
        name="adaln_mod",
    )(c_pad, w, b.reshape(L, 1, N))


def _rms(x, g):
    return x * lax.rsqrt(jnp.mean(x * x, axis=-1, keepdims=True) + EPS) * g


_HI16 = 0xFFFF0000


def _pack_bf16_halves(h):
    half = h.shape[1] // 2
    bits = lax.bitcast_convert_type(h.astype(BF16).astype(F32), jnp.uint32)
    return (bits[:, :half] >> 16) | (bits[:, half:] & jnp.uint32(_HI16))


def _unpack_bf16_halves(a):
    lo = lax.bitcast_convert_type(a << 16, F32).astype(BF16)
    hi = lax.bitcast_convert_type(a & jnp.uint32(_HI16), F32).astype(BF16)
    return jnp.concatenate([lo, hi], axis=1)


def _norm_spec(g, shift=None, scale=None, router=None, out_dtype=BF16):
    return dict(g=g, shift=shift, scale=scale, router=router, out_dtype=out_dtype)


def _norm_cfg(ns):
    return (ns["shift"] is not None, ns["router"] is not None)


def _norm_io(ns, T, D, tm, batch_of, row_of):
    modulate, route = _norm_cfg(ns)
    args = [ns["g"].reshape(1, D)]
    in_specs = [pl.BlockSpec((1, D), lambda *a: (0, 0))]
    if modulate:
        for arr, l, j in (ns["shift"], ns["scale"]):
            args.append(arr)
            in_specs.append(pl.BlockSpec((None, None, None, 1, D),
                                         lambda *a, l=l, j=j: (l, batch_of(*a), j, 0, 0)))
    if route:
        wr, br = ns["router"]
        args += [wr, br]
        in_specs += [pl.BlockSpec((D, LANES), lambda *a: (0, 0)),
                     pl.BlockSpec((1, LANES), lambda *a: (0, 0))]
        out_shape = [jax.ShapeDtypeStruct((T, D // 2), jnp.uint32),
                     jax.ShapeDtypeStruct((T, LANES), F32)]
        out_specs = [pl.BlockSpec((tm, D // 2), lambda *a: (row_of(*a), 0)),
                     pl.BlockSpec((tm, LANES), lambda *a: (row_of(*a), 0))]
    else:
        out_shape = [jax.ShapeDtypeStruct((T, D), ns["out_dtype"])]
        out_specs = [pl.BlockSpec((tm, D), lambda *a: (row_of(*a), 0))]
    return args, in_specs, out_shape, out_specs


def _norm_counts(cfg):
    modulate, route = cfg
    return 1 + 2 * modulate + 2 * route, 1 + route


def _norm_apply(x, in_refs, out_refs, cfg):
    modulate, route = cfg
    g_ref, *rest = in_refs
    h = _rms(x, g_ref[...])
    if modulate:
        sh_ref, sc_ref, *rest = rest
        h = h * (1.0 + sc_ref[...]) + sh_ref[...]
    if not route:
        out_refs[0][...] = h.astype(out_refs[0].dtype)
        return
    wr_ref, br_ref = rest
    out_refs[0][...] = _pack_bf16_halves(h)
    w = wr_ref[...]
    h_hi, w_hi = h.astype(BF16), w.astype(BF16)
    h_lo = (h - h_hi.astype(F32)).astype(BF16)
    w_lo = (w - w_hi.astype(F32)).astype(BF16)
    logits = (jnp.dot(h_hi, w_hi, preferred_element_type=F32)
              + jnp.dot(h_lo, w_hi, preferred_element_type=F32)
              + jnp.dot(h_hi, w_lo, preferred_element_type=F32)) + br_ref[...]
    lane = lax.broadcasted_iota(jnp.int32, logits.shape, 1).astype(F32)
    logits = jnp.where(lane < N_EXPERTS, logits, -jnp.inf)
    v1 = jnp.max(logits, axis=-1, keepdims=True)
    i1 = jnp.min(jnp.where(logits == v1, lane, float(LANES)), axis=-1, keepdims=True)
    others = jnp.where(lane == i1, -jnp.inf, logits)
    v2 = jnp.max(others, axis=-1, keepdims=True)
    i2 = jnp.min(jnp.where(others == v2, lane, float(LANES)), axis=-1, keepdims=True)
    e = jnp.exp(v2 - v1)
    w1 = 1.0 / (1.0 + e)
    w2 = e / (1.0 + e)
    out_refs[1][...] = jnp.where(lane == 0, i1, jnp.where(lane == 1, i2, jnp.where(
        lane == 2, w1, jnp.where(lane == 3, w2, 0.0))))


def _apply_norms(x, refs_in, refs_out, cfgs):
    for cfg in cfgs:
        n_in, n_out = _norm_counts(cfg)
        _norm_apply(x, refs_in[:n_in], refs_out[:n_out], cfg)
        refs_in, refs_out = refs_in[n_in:], refs_out[n_out:]


def _norm_kernel(x_ref, *refs, cfg):
    n_in, _ = _norm_counts(cfg)
    _norm_apply(x_ref[...], refs[:n_in], refs[n_in:], cfg)


def _norm(x2, ns, B, S, ts=512):
    T, D = x2.shape
    nst = S // ts
    args, in_specs, out_shape, out_specs = _norm_io(
        ns, T, D, ts, lambda b, s: b, lambda b, s: b * nst + s)
    outs = pl.pallas_call(
        functools.partial(_norm_kernel, cfg=_norm_cfg(ns)),
        grid=(B, nst),
        in_specs=[pl.BlockSpec((ts, D), lambda b, s: (b * nst + s, 0))] + in_specs,
        out_specs=out_specs,
        out_shape=out_shape,
        compiler_params=_cparams(2),
        name="norm_mod",
    )(x2, *args)
    return outs if len(outs) > 1 else outs[0]


def _gmm_kernel(te_ref, tv_ref, nx_ref, a_ref, *refs, mode, out_scale, a_packed, n_w, layer,
                n_off, tn, n_n, norm_cfgs, k_split):
    n = pl.program_id(0)
    m = pl.program_id(1)
    if k_split is not None:
        a2_ref, *refs = refs
    w_hbm, refs = refs[:n_w], refs[n_w:]
    if mode == "residual":
        x_ref, gate_ref, *refs = refs
    n_norm_in = sum(_norm_counts(c)[0] for c in norm_cfgs)
    n_norm_out = sum(_norm_counts(c)[1] for c in norm_cfgs)
    norm_in, refs = refs[:n_norm_in], refs[n_norm_in:]
    o_ref, *refs = refs
    norm_out, refs = refs[:n_norm_out], refs[n_norm_out:]
    wst, wb, sem = refs[:n_w], refs[n_w:2 * n_w], refs[2 * n_w]
    first = jnp.logical_or(m == 0, te_ref[m] != te_ref[jnp.maximum(m - 1, 0)])
    valid = tv_ref[m] != 0

    def w_copy(i, e, nn):
        col = pl.multiple_of((nn + n_off) * tn, tn)
        return pltpu.make_async_copy(w_hbm[i].at[layer, e, :, pl.ds(col, tn)], wst[i], sem.at[i])

    @pl.when(jnp.logical_and(n == 0, m == 0))
    def _():
        for i in range(n_w):
            w_copy(i, te_ref[0], 0).start()

    @pl.when(first)
    def _():
        for i in range(n_w):
            w_copy(i, te_ref[m], n).wait()
            wb[i][...] = wst[i][...].astype(BF16)
        nm = nx_ref[m]
        same_sweep = nm >= 0
        e_next = jnp.where(same_sweep, te_ref[jnp.maximum(nm, 0)], te_ref[0])
        n_next = jnp.where(same_sweep, n, n + 1)

        @pl.when(jnp.logical_or(same_sweep, n + 1 < n_n))
        def _():
            for i in range(n_w):
                w_copy(i, e_next, n_next).start()

    @pl.when(valid)
    def _():
        a = _unpack_bf16_halves(a_ref[...]) if a_packed else a_ref[...]
        if mode == "swiglu":
            g = jnp.dot(a, wb[0][...], preferred_element_type=F32)
            u = jnp.dot(a, wb[1][...], preferred_element_type=F32)
            acc = g * jax.nn.sigmoid(g) * u
        elif k_split is None:
            acc = jnp.dot(a, wb[0][...], preferred_element_type=F32)
        else:
            acc = (jnp.dot(a, wb[0][:k_split], preferred_element_type=F32)
                   + jnp.dot(a2_ref[...], wb[0][k_split:], preferred_element_type=F32))
        if out_scale is not None:
            acc = acc * out_scale
        if mode == "residual":
            acc = x_ref[...] + gate_ref[...] * acc
        o_ref[...] = acc.astype(o_ref.dtype)
        _apply_norms(acc, norm_in, norm_out, norm_cfgs)

    @pl.when(jnp.logical_not(valid))
    def _():
        o_ref[...] = jnp.zeros_like(o_ref)


def _gmm(a, ws, l, *, mode, tm, tn, n_out, out_dtype, tile_expert=None, tile_valid=None,
         n_off=0, xres=None, gate=None, seq=None, out_scale=None, a_packed=False,
         alias_x=True, norms=None, a2=None):
    M, ka = a.shape
    K = ws[0].shape[2]
    mt = M // tm
    n_w = len(ws)
    n_n = n_out // tn
    if tile_expert is None:
        tile_expert = jnp.zeros((mt,), jnp.int32)
        tile_valid = jnp.ones((mt,), jnp.int32)
    idx = jnp.arange(mt, dtype=jnp.int32)
    later = ((idx[None, :] > idx[:, None]) & (tile_valid[None, :] != 0)
             & (tile_expert[None, :] != tile_expert[:, None]))
    next_run = jnp.where(jnp.any(later, axis=1), jnp.argmax(later, axis=1), -1).astype(jnp.int32)
    lhs = [a] if a2 is None else [a, a2]
    args = lhs + list(ws)
    in_specs = [pl.BlockSpec((tm, t.shape[1]), lambda n, m, te, tv, nx: (m, 0)) for t in lhs]
    in_specs += [pl.BlockSpec(memory_space=pl.ANY)] * n_w
    aliases = {}
    if mode == "residual":
        garr, gl, gj = gate
        per_b = seq // tm
        if alias_x:
            aliases = {3 + len(args): 0}
        args += [xres, garr]
        in_specs += [
            pl.BlockSpec((tm, tn), lambda n, m, te, tv, nx: (m, n)),
            pl.BlockSpec((None, None, None, 1, tn),
                         lambda n, m, te, tv, nx: (gl, m // per_b, gj, 0, n)),
        ]
    out_shape = [jax.ShapeDtypeStruct((M, n_out), out_dtype)]
    out_specs = [pl.BlockSpec((tm, tn), lambda n, m, te, tv, nx: (m, n))]
    bare = norms is None
    norms = norms or []
    for ns in norms:
        assert tn == n_out
        n_args, n_in, n_shape, n_out_specs = _norm_io(
            ns, M, n_out, tm, lambda n, m, *_: m // (seq // tm), lambda n, m, *_: m)
        args += n_args
        in_specs += n_in
        out_shape += n_shape
        out_specs += n_out_specs
    outs = pl.pallas_call(
        functools.partial(_gmm_kernel, mode=mode, out_scale=out_scale, a_packed=a_packed,
                          n_w=n_w, layer=l, n_off=n_off, tn=tn, n_n=n_n,
                          norm_cfgs=tuple(_norm_cfg(ns) for ns in norms),
                          k_split=None if a2 is None else ka),
        grid_spec=pltpu.PrefetchScalarGridSpec(
            num_scalar_prefetch=3,
            grid=(n_n, mt),
            in_specs=in_specs,
            out_specs=out_specs,
            scratch_shapes=([pltpu.VMEM((K, tn), F32) for _ in ws]
                            + [pltpu.VMEM((K, tn), BF16) for _ in ws]
                            + [pltpu.SemaphoreType.DMA((n_w,))]),
        ),
        out_shape=out_shape,
        input_output_aliases=aliases,
        compiler_params=_cparams(2),
        name="gmm_" + mode,
    )(tile_expert, tile_valid, next_run, *args)
    return outs[0] if bare else outs


def _rope_pairs(v, cos, s1, s2):
    return v * cos + pltpu.roll(v, LANES - QK_ROPE // 2, 1) * s1 + pltpu.roll(v, QK_ROPE // 2, 1) * s2


def _mla_in_kernel(a_ref, w_ref, gq_ref, gkv_ref, cos_ref, s1_ref, s2_ref,
                   cq_ref, ckv_ref, kpe_ref, wb, *, q_lora, kv_lora):
    @pl.when(pl.program_id(0) == 0)
    def _():
        wb[...] = w_ref[...].astype(BF16)

    acc = jnp.dot(a_ref[...], wb[...], preferred_element_type=F32)
    cq_ref[...] = _rms(acc[:, :q_lora], gq_ref[...]).astype(BF16)
    ckv_ref[...] = _rms(acc[:, q_lora:q_lora + kv_lora], gkv_ref[...]).astype(BF16)
    v = acc[:, q_lora + kv_lora:]
    kpe_ref[...] = _rope_pairs(v, cos_ref[...], s1_ref[...], s2_ref[...]).astype(BF16)


def _mla_in(h, w_pad, g_q, g_kv, rope_tabs, S, tm=512):
    T, D = h.shape
    q_lora, kv_lora = g_q.shape[0], g_kv.shape[0]
    n_all = w_pad.shape[1]
    nst = S // tm
    tab_spec = pl.BlockSpec((tm, LANES), lambda i: (i % nst, 0))
    return pl.pallas_call(
        functools.partial(_mla_in_kernel, q_lora=q_lora, kv_lora=kv_lora),
        grid=(T // tm,),
        in_specs=[
            pl.BlockSpec((tm, D), lambda i: (i, 0)),
            pl.BlockSpec((D, n_all), lambda i: (0, 0)),
            pl.BlockSpec((1, q_lora), lambda i: (0, 0)),
            pl.BlockSpec((1, kv_lora), lambda i: (0, 0)),
            tab_spec, tab_spec, tab_spec,
        ],
        out_specs=[
            pl.BlockSpec((tm, q_lora), lambda i: (i, 0)),
            pl.BlockSpec((tm, kv_lora), lambda i: (i, 0)),
            pl.BlockSpec((tm, LANES), lambda i: (i, 0)),
        ],
        out_shape=[
            jax.ShapeDtypeStruct((T, q_lora), BF16),
            jax.ShapeDtypeStruct((T, kv_lora), BF16),
            jax.ShapeDtypeStruct((T, LANES), BF16),
        ],
        scratch_shapes=[pltpu.VMEM((D, n_all), BF16)],
        compiler_params=_cparams(1),
        name="mla_in",
    )(h, w_pad, g_q.reshape(1, -1), g_kv.reshape(1, -1), *rope_tabs)


def _mla_attn_kernel(qn_ref, qp_ref, kv_ref, kpe_ref, cos_ref, s1_ref, s2_ref, o_ref,
                     m_sc, l_sc, acc_sc, *, tq):
    qi = pl.program_id(2)
    head_w = QK_NOPE + V_HEAD
    qp = _rope_pairs(qp_ref[...].astype(F32), cos_ref[...], s1_ref[...], s2_ref[...])
    lane = lax.broadcasted_iota(jnp.int32, qp.shape, 1)
    qs = []
    for hh in range(2):
        keep = (lane < QK_ROPE) if hh == 0 else (lane >= QK_ROPE)
        qs.append(jnp.concatenate(
            [qn_ref[:, hh * QK_NOPE:(hh + 1) * QK_NOPE], jnp.where(keep, qp, 0.0).astype(BF16)],
            axis=1))
    m_sc[...] = jnp.full_like(m_sc, NEG)
    l_sc[...] = jnp.zeros_like(l_sc)
    acc_sc[...] = jnp.zeros_like(acc_sc)

    def tile(ks, nk, r0, diag_off):
        kpe = kpe_ref[pl.ds(ks, nk), :]
        for hh in range(2):
            c0 = hh * head_w
            k = jnp.concatenate([kv_ref[pl.ds(ks, nk), c0:c0 + QK_NOPE], kpe], axis=1)
            s = _dot_nt(qs[hh][r0:], k)
            if diag_off is not None:
                row = lax.broadcasted_iota(jnp.int32, s.shape, 0) + r0
                col = lax.broadcasted_iota(jnp.int32, s.shape, 1) + diag_off
                s = jnp.where(col <= row, s, NEG)
            m_old = m_sc[hh, r0:]
            m_new = jnp.maximum(m_old, jnp.max(s, axis=-1, keepdims=True))
            alpha = jnp.exp2(m_old - m_new)
            p = jnp.exp2(s - jnp.concatenate([m_new] * (nk // LANES), axis=1))
            l_sc[hh, r0:] = alpha * l_sc[hh, r0:] + jnp.sum(p, axis=-1, keepdims=True)
            vv = kv_ref[pl.ds(ks, nk), c0 + QK_NOPE:c0 + head_w]
            acc_sc[hh, r0:] = alpha * acc_sc[hh, r0:] + jnp.dot(p.astype(BF16), vv,
                                                                preferred_element_type=F32)
            m_sc[hh, r0:] = m_new

    def body(j, carry):
        tile(pl.multiple_of(j * tq, tq), tq, 0, None)
        return carry

    lax.fori_loop(0, qi, body, 0)
    tile(pl.multiple_of(qi * tq, tq), tq, 0, 0)
    for hh in range(2):
        o_ref[:, hh * V_HEAD:(hh + 1) * V_HEAD] = (acc_sc[hh] / l_sc[hh]).astype(o_ref.dtype)


def _mla_attn(q_all, kv_all, kpe, rope_tabs, B, S, tq=512):
    T = q_all.shape[0]
    H = MLA_HEADS
    nqt = S // tq
    pair_w = 2 * QK_NOPE
    kv_pair_w = 2 * (QK_NOPE + V_HEAD)
    tab_spec = pl.BlockSpec((tq, LANES), lambda b, p, i: (i, 0))
    return pl.pallas_call(
        functools.partial(_mla_attn_kernel, tq=tq),
        grid=(B, H // 2, nqt),
        in_specs=[
            pl.BlockSpec((tq, pair_w), lambda b, p, i: (b * nqt + i, p)),
            pl.BlockSpec((tq, LANES), lambda b, p, i: (b * nqt + i, (H * QK_NOPE) // LANES + p)),
            pl.BlockSpec((S, kv_pair_w), lambda b, p, i: (b, p)),
            pl.BlockSpec((S, LANES), lambda b, p, i: (b, 0)),
            tab_spec, tab_spec, tab_spec,
        ],
        out_specs=pl.BlockSpec((tq, pair_w), lambda b, p, i: (b * nqt + i, p)),
        out_shape=jax.ShapeDtypeStruct((T, H * V_HEAD), BF16),
        scratch_shapes=[pltpu.VMEM((2, tq, LANES), F32), pltpu.VMEM((2, tq, LANES), F32),
                        pltpu.VMEM((2, tq, V_HEAD), F32)],
        compiler_params=_cparams(3),
        name="mla_attn",
    )(q_all, q_all, kv_all, kpe, *rope_tabs)


def _nsa_compress_kernel(t_ref, pe_ref, w1_ref, w2_ref, o_ref, *, n_chunk):
    half = CMP_BLOCK // 2
    pe = pe_ref[...]

    def chunk_rows(off):
        cols = []
        for l in range(half):
            x = t_ref[pl.ds(l, n_chunk, stride=CMP_STRIDE), :] + pe[off + l:off + l + 1, :]
            cols.append(x.astype(BF16))
        return jnp.concatenate(cols, axis=1)

    w1 = w1_ref[...].astype(BF16)
    kdim = half * NSA_DK
    p0 = jnp.dot(chunk_rows(0), w1[:kdim], preferred_element_type=F32)
    p1 = jnp.dot(chunk_rows(half), w1[kdim:], preferred_element_type=F32)
    pre = p0 + pltpu.roll(p1, n_chunk - 1, 0)
    hid = (pre * jax.nn.sigmoid(pre)).astype(BF16)
    out = jnp.dot(hid, w2_ref[...].astype(BF16), preferred_element_type=F32)
    rown = lax.broadcasted_iota(jnp.int32, out.shape, 0)
    o_ref[...] = jnp.where(rown < n_chunk - 1, out, 0.0).astype(o_ref.dtype)


def _nsa_compress(kc, pe, w1, w2, B, S):
    G = NSA_GROUPS
    n_chunk = S // CMP_STRIDE
    return pl.pallas_call(
        functools.partial(_nsa_compress_kernel, n_chunk=n_chunk),
        grid=(B, 2, G),
        in_specs=[
            pl.BlockSpec((S, NSA_DK), lambda b, i, g: (b, i * G + g)),
            pl.BlockSpec((None, CMP_BLOCK, NSA_DK), lambda b, i, g: (i, 0, 0)),
            pl.BlockSpec((None, CMP_BLOCK * NSA_DK, NSA_DK), lambda b, i, g: (i, 0, 0)),
            pl.BlockSpec((None, NSA_DK, NSA_DK), lambda b, i, g: (i, 0, 0)),
        ],
        out_specs=pl.BlockSpec((None, None, None, n_chunk, NSA_DK), lambda b, i, g: (b, i, g, 0, 0)),
        out_shape=jax.ShapeDtypeStruct((B, 2, G, n_chunk, NSA_DK), BF16),
        compiler_params=_cparams(3),
        name="nsa_compress",
    )(kc, pe, w1, w2)


def _nsa_attn_kernel(q_ref, gt_ref, ks_ref, vs_ref, kw_ref, vw_ref, kc_ref, vc_ref, qtab_ref,
                     ktab_ref, ctab_ref, mt_ref, cbias_ref, wbias_ref, o_ref, m_sc, acc_sc,
                     *, tq, tk, n_sel):
    qi = pl.program_id(2)
    t0 = qi * tq
    HP = NSA_HPG
    R = HP * tq
    qb = q_ref[...]
    qs = jnp.concatenate([qb[:, h * NSA_DK:(h + 1) * NSA_DK] for h in range(HP)], axis=0)
    qtab = qtab_ref[...]
    qa = jnp.concatenate([qs, qtab], axis=1)
    rowi = lax.broadcasted_iota(jnp.int32, (R, 1), 0)
    tcol = (t0 + rowi % tq).astype(F32)

    kc = jnp.concatenate([kc_ref[...], ctab_ref[...]], axis=1)
    s = _dot_nt(qa, kc) + jnp.concatenate([cbias_ref[...]] * HP, axis=0)
    p = jnp.exp2(s - jnp.max(s, axis=-1, keepdims=True))
    l = jnp.sum(p, axis=-1, keepdims=True)
    p_cmp = jnp.where(tcol >= CMP_BLOCK - 1, p / l, 0.0)
    o_cmp = jnp.dot(p_cmp.astype(BF16), vc_ref[...], preferred_element_type=F32)

    ps = p_cmp[0:tq]
    for h in range(1, HP):
        ps = ps + p_cmp[h * tq:(h + 1) * tq]
    ps_hi = ps.astype(BF16)
    ps_lo = (ps - ps_hi.astype(F32)).astype(BF16)
    imp = _dot_nt(mt_ref[...], ps_hi) + _dot_nt(mt_ref[...], ps_lo)
    jrow = lax.broadcasted_iota(jnp.int32, (n_sel, tq), 0)
    blk_t = (t0 + lax.broadcasted_iota(jnp.int32, (n_sel, tq), 1)) // SEL_BLOCK
    forced = (jrow == 0) | (jrow == blk_t) | (jrow == blk_t - 1)
    imp = jnp.where(forced, FORCED_SCORE, imp)
    imp = jnp.where(jrow > blk_t, -1.0, imp)
    rank = jnp.zeros((n_sel, tq), F32)
    for k in range(n_sel):
        rk = imp[k:k + 1, :]
        beats = (rk > imp) | ((rk == imp) & (jrow > k))
        rank = rank + jnp.where(beats, 1.0, 0.0)
    sel_t = jnp.where(rank < SEL_TOPN, 1.0, 0.0)
    sel_t = jnp.concatenate([sel_t, jnp.zeros((LANES - n_sel, tq), F32)], axis=0)
    sel_neg = ((sel_t.T - 1.0) * MASK_BIG).astype(BF16)
    lane = lax.broadcasted_iota(jnp.int32, (R, LANES), 1)
    qx = jnp.where(lane < SEL_LANES, jnp.concatenate([sel_neg] * HP, axis=0), qtab)
    qsel = jnp.concatenate([qs, qx], axis=1)

    m_sc[...] = jnp.full_like(m_sc, NEG)
    acc_sc[...] = jnp.zeros_like(acc_sc)
    ones = jnp.ones((tk, LANES), BF16)

    def tile(j, causal):
        ks = pl.multiple_of(j * tk, tk)
        k = jnp.concatenate([ks_ref[pl.ds(ks, tk), :], ktab_ref[pl.ds(ks, tk), :]], axis=1)
        sc = _dot_nt(qsel, k)
        if causal:
            kpos = (ks + lax.broadcasted_iota(jnp.int32, (1, tk), 1)).astype(F32)
            sc = jnp.where(kpos <= tcol, sc, NEG)
        m_old = m_sc[...]
        m_new = jnp.maximum(m_old, jnp.max(sc, axis=-1, keepdims=True))
        alpha = jnp.exp2(m_old - m_new)
        pp = jnp.exp2(sc - jnp.concatenate([m_new] * (tk // LANES), axis=1))
        vv = jnp.concatenate([vs_ref[pl.ds(ks, tk), :], ones], axis=1)
        acc_sc[...] = (jnp.concatenate([alpha, alpha], axis=1) * acc_sc[...]
                       + jnp.dot(pp.astype(BF16), vv, preferred_element_type=F32))
        m_sc[...] = m_new

    def body(j, carry):
        tile(j, False)
        return carry

    n_full = t0 // tk
    lax.fori_loop(0, n_full, body, 0)
    for c in range(tq // tk):
        tile(n_full + c, True)
    acc = acc_sc[...]
    o_sel = acc[:, :NSA_DK] / acc[:, NSA_DK:]

    span = WINDOW + tq
    ws = pl.multiple_of(jnp.maximum(t0 - WINDOW, 0), tq)
    wbias = wbias_ref[(t0 - ws) // tq]
    kw = jnp.concatenate([kw_ref[pl.ds(ws, span), :], ktab_ref[pl.ds(ws, span), :]], axis=1)
    sw = _dot_nt(qa, kw) + jnp.concatenate([wbias] * HP, axis=0)
    pw = jnp.exp2(sw - jnp.max(sw, axis=-1, keepdims=True))
    vw = jnp.concatenate([vw_ref[pl.ds(ws, span), :], jnp.ones((span, LANES), BF16)], axis=1)
    rw = jnp.dot(pw.astype(BF16), vw, preferred_element_type=F32)
    o_win = rw[:, :NSA_DK] / rw[:, NSA_DK:]

    gt = jax.nn.sigmoid(gt_ref[...])

    def gcol(i):
        return jnp.concatenate([gt[:, i * HP + h:i * HP + h + 1] for h in range(HP)], axis=0)

    o = gcol(0) * o_cmp + gcol(1) * o_sel + gcol(2) * o_win
    o_ref[...] = jnp.concatenate([o[h * tq:(h + 1) * tq] for h in range(HP)], axis=1).astype(o_ref.dtype)


def _nsa_attn(q, gates, kvb, kvc, tabs, B, S, tq=128, tk=256):
    T = q.shape[0]
    G, DK, HP = NSA_GROUPS, NSA_DK, NSA_HPG
    qtab, ktab, ctab, mt, cbias, wbias = tabs
    nqt = S // tq
    n_chunk = kvc.shape[3]
    n_sel = S // SEL_BLOCK

    def kv_spec(i):
        return pl.BlockSpec((S, DK), lambda b, g, t, i=i: (b, i * G + g))

    def cmp_spec(i):
        return pl.BlockSpec((None, None, None, n_chunk, DK), lambda b, g, t, i=i: (b, i, g, 0, 0))

    return pl.pallas_call(
        functools.partial(_nsa_attn_kernel, tq=tq, tk=tk, n_sel=n_sel),
        grid=(B, G, nqt),
        in_specs=[
            pl.BlockSpec((tq, HP * DK), lambda b, g, t: (b * nqt + t, g)),
            pl.BlockSpec((tq, LANES), lambda b, g, t: (b * nqt + t, g)),
            kv_spec(0), kv_spec(1), kv_spec(2), kv_spec(3),
            cmp_spec(0), cmp_spec(1),
            pl.BlockSpec((None, HP * tq, LANES), lambda b, g, t: (g, 0, 0)),
            pl.BlockSpec(ktab.shape, lambda b, g, t: (0, 0)),
            pl.BlockSpec(ctab.shape, lambda b, g, t: (0, 0)),
            pl.BlockSpec(mt.shape, lambda b, g, t: (0, 0)),
            pl.BlockSpec((tq, n_chunk), lambda b, g, t: (t, 0)),
            pl.BlockSpec(wbias.shape, lambda b, g, t: (0, 0, 0)),
        ],
        out_specs=pl.BlockSpec((tq, HP * DK), lambda b, g, t: (b * nqt + t, g)),
        out_shape=jax.ShapeDtypeStruct((T, NSA_HEADS * DK), BF16),
        scratch_shapes=[pltpu.VMEM((HP * tq, LANES), F32),
                        pltpu.VMEM((HP * tq, DK + LANES), F32)],
        compiler_params=_cparams(3),
        name="nsa_attn",
    )(q, gates, kvb, kvb, kvb, kvb, kvc, kvc, qtab, ktab, ctab, mt, cbias, wbias)


def _row_copy(src_hbm, dst_vmem, sem, src_row, dst_row):
    return pltpu.make_async_copy(src_hbm.at[pl.ds(src_row, 1), :], dst_vmem.at[pl.ds(dst_row, 1), :], sem)


def _dispatch_kernel(pos_ref, src_ref, init_hbm, out_hbm, sem, *, chunk, n_tok):
    del init_hbm
    base = pl.program_id(0) * chunk

    def start(r, c):
        t = base + r
        _row_copy(src_ref, out_hbm, sem, r, pos_ref[t]).start()
        _row_copy(src_ref, out_hbm, sem, r, pos_ref[n_tok + t]).start()
        return c

    def wait(r, c):
        _row_copy(src_ref, out_hbm, sem, 0, 0).wait()
        return c

    lax.fori_loop(0, chunk, start, 0, unroll=8)
    lax.fori_loop(0, 2 * chunk, wait, 0, unroll=8)


def _dispatch(h_packed, pos, n_rows, chunk=512):
    T, W = h_packed.shape
    n_steps = T // chunk
    init = jnp.zeros((n_rows, W), h_packed.dtype)
    return pl.pallas_call(
        functools.partial(_dispatch_kernel, chunk=chunk, n_tok=T),
        grid_spec=pltpu.PrefetchScalarGridSpec(
            num_scalar_prefetch=1,
            grid=(n_steps,),
            in_specs=[pl.BlockSpec((chunk, W), lambda i, pos: (i, 0)),
                      pl.BlockSpec(memory_space=pl.ANY)],
            out_specs=pl.BlockSpec(memory_space=pl.ANY),
            scratch_shapes=[pltpu.SemaphoreType.DMA(())],
        ),
        out_shape=jax.ShapeDtypeStruct((n_rows, W), h_packed.dtype),
        input_output_aliases={2: 0},
        compiler_params=_cparams(1),
        name="moe_dispatch",
    )(pos, h_packed, init)


def _combine_kernel(pos_ref, y_hbm, x_ref, gate_ref, rt_ref, *refs, tg, n_tok, n_steps,
                    norm_cfgs):
    n_norm_in = sum(_norm_counts(c)[0] for c in norm_cfgs)
    n_norm_out = sum(_norm_counts(c)[1] for c in norm_cfgs)
    norm_in, (o_ref, *refs) = refs[:n_norm_in], refs[n_norm_in:]
    norm_out, (buf, sem) = refs[:n_norm_out], refs[n_norm_out:]
    i = pl.program_id(0)
    slot = i % 2

    def issue(step, slot_):
        def start(r, c):
            t = step * tg + r
            _row_copy(y_hbm, buf.at[slot_, 0], sem.at[slot_], pos_ref[t], r).start()
            _row_copy(y_hbm, buf.at[slot_, 1], sem.at[slot_], pos_ref[n_tok + t], r).start()
            return c

        lax.fori_loop(0, tg, start, 0, unroll=8)

    @pl.when(i == 0)
    def _():
        issue(0, 0)

    @pl.when(i + 1 < n_steps)
    def _():
        issue(i + 1, 1 - slot)

    def wait(r, c):
        _row_copy(y_hbm, buf.at[slot, 0], sem.at[slot], 0, 0).wait()
        return c

    lax.fori_loop(0, 2 * tg, wait, 0, unroll=8)
    rt = rt_ref[...]
    y = rt[:, 2:3] * buf[slot, 0] + rt[:, 3:4] * buf[slot, 1]
    x_new = x_ref[...] + gate_ref[...] * y
    o_ref[...] = x_new
    _apply_norms(x_new, norm_in, norm_out, norm_cfgs)


def _combine(x2, y_sorted, pos, route, gate, S, norms=(), tg=256):
    T, D = x2.shape
    garr, gl, gj = gate
    per_b = S // tg
    n_steps = T // tg
    args = [y_sorted, x2, garr, route]
    in_specs = [
        pl.BlockSpec(memory_space=pl.ANY),
        pl.BlockSpec((tg, D), lambda i, pos: (i, 0)),
        pl.BlockSpec((None, None, None, 1, D), lambda i, pos: (gl, i // per_b, gj, 0, 0)),
        pl.BlockSpec((tg, LANES), lambda i, pos: (i, 0)),
    ]
    out_shape = [jax.ShapeDtypeStruct((T, D), F32)]
    out_specs = [pl.BlockSpec((tg, D), lambda i, pos: (i, 0))]
    for ns in norms:
        n_args, n_in, n_shape, n_out_specs = _norm_io(
            ns, T, D, tg, lambda i, *_: i // per_b, lambda i, *_: i)
        args += n_args
        in_specs += n_in
        out_shape += n_shape
        out_specs += n_out_specs
    outs = pl.pallas_call(
        functools.partial(_combine_kernel, tg=tg, n_tok=T, n_steps=n_steps,
                          norm_cfgs=tuple(_norm_cfg(ns) for ns in norms)),
        grid_spec=pltpu.PrefetchScalarGridSpec(
            num_scalar_prefetch=1,
            grid=(n_steps,),
            in_specs=in_specs,
            out_specs=out_specs,
            scratch_shapes=[pltpu.VMEM((2, 2, tg, D), F32), pltpu.SemaphoreType.DMA((2,))],
        ),
        out_shape=out_shape,
        input_output_aliases={2: 0},
        compiler_params=_cparams(1),
        name="moe_combine",
    )(pos, *args)
    return outs if norms else outs[0]


def _moe_plan(route, T, tm):
    E = N_EXPERTS
    n_tiles = (2 * T) // tm + E
    P = n_tiles * tm
    e_pair = jnp.concatenate([route[:, 0], route[:, 1]]).astype(jnp.int32)
    onehot = (e_pair[:, None] == jnp.arange(E, dtype=jnp.int32)[None, :]).astype(jnp.int32)
    csum = jnp.cumsum(onehot, axis=0)
    rank = jnp.sum(onehot * (csum - 1), axis=1)
    counts = csum[-1]
    padded = ((counts + tm - 1) // tm) * tm
    ends = jnp.cumsum(padded)
    pos = (ends - padded)[e_pair] + rank
    tile_start = jnp.arange(n_tiles, dtype=jnp.int32) * tm
    tile_valid = (tile_start < ends[-1]).astype(jnp.int32)
    tile_expert = jnp.sum((tile_start[:, None] >= ends[None, :]).astype(jnp.int32), axis=1)
    last_valid = tile_expert[jnp.maximum(ends[-1] // tm - 1, 0)]
    tile_expert = jnp.where(tile_valid == 1, tile_expert, last_valid).astype(jnp.int32)
    return pos.astype(jnp.int32), P, tile_expert, tile_valid


def _rope_tables(S):
    d = QK_ROPE
    inv = ROPE_THETA ** (-jnp.arange(0, d, 2, dtype=F32) / d)
    ang = jnp.arange(S).astype(F32)[:, None] * inv[None, :]
    cos, sin = jnp.cos(ang), jnp.sin(ang)
    z = jnp.zeros_like(sin)
    return (jnp.concatenate([cos, cos, cos, cos], axis=1),
            jnp.concatenate([-sin, z, -sin, z], axis=1),
            jnp.concatenate([z, sin, z, sin], axis=1))


def _pos_columns(pos):
    tab = np.zeros((pos.shape[0], LANES), np.float32)
    tab[:, POS_HI_LANE:POS_HI_LANE + 3] = (LANES * (pos // LANES))[:, None]
    tab[:, POS_LO_LANE:POS_LO_LANE + 3] = (pos % LANES)[:, None]
    return tab


def _nsa_tables(S, tq):
    n_cmp = (S - CMP_BLOCK) // CMP_STRIDE + 1
    n_sel = S // SEL_BLOCK
    n_chunk = S // CMP_STRIDE
    tok = np.arange(n_cmp)[:, None] * CMP_STRIDE + np.arange(CMP_BLOCK)[None, :]
    blk = tok // SEL_BLOCK
    m = (blk[:, :, None] == np.arange(n_sel)[None, None, :]).sum(axis=1) / CMP_BLOCK
    mt = np.zeros((n_sel, n_chunk), np.float32)
    mt[:, :n_cmp] = m.T
    keys = np.arange(S)
    ktab = _pos_columns(keys)
    ktab[:, :SEL_LANES] = (keys[:, None] // SEL_BLOCK == np.arange(SEL_LANES)[None, :])
    ctab = _pos_columns(np.arange(n_chunk) * CMP_STRIDE + (CMP_BLOCK - 1))
    slopes = jnp.asarray(2.0 ** (-8.0 * np.arange(1, NSA_HEADS + 1) / NSA_HEADS), F32)
    a = slopes * LOG2E
    a_hi = a.astype(BF16)
    r1 = a - a_hi.astype(F32)
    a_mid = r1.astype(BF16)
    a_lo = (r1 - a_mid.astype(F32)).astype(BF16)
    pieces = jnp.stack([a_hi, a_mid, a_lo], axis=-1)
    qrow = jnp.zeros((NSA_HEADS, LANES), BF16)
    qrow = qrow.at[:, POS_HI_LANE:POS_HI_LANE + 3].set(pieces)
    qrow = qrow.at[:, POS_LO_LANE:POS_LO_LANE + 3].set(pieces)
    qtab = jnp.repeat(qrow.reshape(NSA_GROUPS, NSA_HPG, LANES), tq, axis=1)
    cend = np.arange(n_chunk) * CMP_STRIDE + (CMP_BLOCK - 1)
    cvis = (keys[:, None] >= cend[None, :]) & (np.arange(n_chunk)[None, :] < n_cmp)
    cbias = np.where(cvis, 0.0, NEG).astype(np.float32)
    span = WINDOW + tq
    d = (np.arange(WINDOW // tq + 1)[:, None, None] * tq + np.arange(tq)[None, :, None]
         - np.arange(span)[None, None, :])
    wbias = np.where((d >= 0) & (d < WINDOW), 0.0, NEG).astype(np.float32)
    return (qtab, jnp.asarray(ktab, BF16), jnp.asarray(ctab, BF16), jnp.asarray(mt, BF16),
            jnp.asarray(cbias), jnp.asarray(wbias))


def kernel(x, c, ada_w, ada_b, norm1_g, norm2_g, mla_w_in, mla_g_q, mla_g_kv, mla_w_uq, mla_w_ukv, mla_w_o, kv_ada_w, kv_ada_b, kv_norm_g, nsa_w_kv, cmp_pos_k, cmp_pos_v, cmp_k_w1, cmp_k_w2, cmp_v_w1, cmp_v_w2, nsa_w_in, nsa_w_o, ffn_w_gate, ffn_w_up, ffn_w_down, moe_w_router, moe_b_router, moe_w_gate, moe_w_up, moe_w_down, final_g):
    B, S, D = x.shape
    T = B * S
    depth = ada_w.shape[0]
    n_a = mla_w_in.shape[0]
    H = MLA_HEADS
    G, HP, DK = NSA_GROUPS, NSA_HPG, NSA_DK
    d_ff = ffn_w_gate.shape[-1]
    nsa_tq = 512

    c_pad = jnp.zeros((8, D), F32).at[:B].set(c)
    mod = _modulation(c_pad, ada_w, ada_b)[:, :B].reshape(depth, B, 6, 1, D)
    kv_mod = _modulation(c_pad, kv_ada_w[None], kv_ada_b[None])[:, :B].reshape(1, B, 2, 1, D)

    rope_tabs = _rope_tables(S)
    nsa_tabs = _nsa_tables(S, nsa_tq)

    ffn_wg = ffn_w_gate[:, None]
    ffn_wu = ffn_w_up[:, None]
    ffn_wd = ffn_w_down[:, None]

    def norm1_spec(l):
        return _norm_spec(norm1_g[l], (mod, l, 0), (mod, l, 1))

    def swiglu(a, ws, li, tm, **kw):
        wide_tn = 1024
        wide = (d_ff // wide_tn) * wide_tn
        tail = d_ff - wide
        main = _gmm(a, ws, li, mode="swiglu", tm=tm, tn=wide_tn, n_out=wide, out_dtype=BF16, **kw)
        rest = _gmm(a, ws, li, mode="swiglu", tm=tm, tn=tail, n_out=tail, out_dtype=BF16,
                    n_off=wide // tail, **kw)
        return main, rest

    kv_spec = _norm_spec(kv_norm_g, (kv_mod, 0, 0), (kv_mod, 0, 1))
    final_spec = _norm_spec(final_g, out_dtype=F32)

    x2 = x.reshape(T, D)
    shared = None
    h = _norm(x2, norm1_spec(0), B, S)
    for l in range(depth):
        dense = l % 2 == 0
        if dense:
            norm2 = _norm_spec(norm2_g[l], (mod, l, 3), (mod, l, 4))
        else:
            wr = jnp.pad(moe_w_router[l // 2], ((0, 0), (0, LANES - N_EXPERTS)))
            br = jnp.pad(moe_b_router[l // 2], (0, LANES - N_EXPERTS)).reshape(1, LANES)
            norm2 = _norm_spec(norm2_g[l], (mod, l, 3), (mod, l, 4), router=(wr, br))
        fused2 = [norm2] if dense else []
        if l < n_a:
            w_in = mla_w_in[l]
            q_lora, kv_lora = mla_g_q.shape[1], mla_g_kv.shape[1]
            w_pad = jnp.concatenate([w_in, w_in[:, q_lora + kv_lora:]], axis=1)
            cq, ckv, kpe = _mla_in(h, w_pad, mla_g_q[l], mla_g_kv[l], rope_tabs, S)
            wq = mla_w_uq[l].reshape(q_lora, H, QK_NOPE + QK_ROPE)
            wq = jnp.concatenate([wq[:, :, :QK_NOPE].reshape(q_lora, H * QK_NOPE),
                                  wq[:, :, QK_NOPE:].reshape(q_lora, H * QK_ROPE)], axis=1)
            q_all = _gmm(cq, [wq[None, None]], 0, mode="cast", tm=2048, tn=1024,
                         n_out=wq.shape[1], out_dtype=BF16,
                         out_scale=(QK_NOPE + QK_ROPE) ** -0.5 * LOG2E)
            kv_all = _gmm(ckv, [mla_w_ukv[:, None]], l, mode="cast", tm=2048, tn=1024,
                          n_out=mla_w_ukv.shape[-1], out_dtype=BF16)
            o = _mla_attn(q_all, kv_all, kpe, rope_tabs, B, S)
            x2, *h2 = _gmm(o, [mla_w_o[:, None]], l, mode="residual", tm=512, tn=D, n_out=D,
                           out_dtype=F32, xres=x2, gate=(mod, l, 2), seq=S, norms=fused2,
                           alias_x=l > 0)
        else:
            jb = l - n_a
            w_in = nsa_w_in[jb]
            q = _gmm(h, [nsa_w_in[:, None]], jb, mode="cast", tm=1024, tn=512,
                     n_out=NSA_HEADS * DK, out_dtype=BF16, out_scale=DK ** -0.5 * LOG2E)
            wg = w_in[:, NSA_HEADS * DK:].reshape(D, G, HP, 3).transpose(0, 1, 3, 2)
            wg = jnp.pad(wg.reshape(D, G, 3 * HP), ((0, 0), (0, 0), (0, LANES - 3 * HP)))
            gates = _gmm(h, [wg.reshape(1, 1, D, G * LANES)], 0, mode="cast", tm=1024, tn=512,
                         n_out=G * LANES, out_dtype=F32)
            kvb, kvc = shared
            o = _nsa_attn(q, gates, kvb, kvc, nsa_tabs, B, S, tq=nsa_tq)
            x2, *h2 = _gmm(o, [nsa_w_o[:, None]], jb, mode="residual", tm=512, tn=D, n_out=D,
                           out_dtype=F32, xres=x2, gate=(mod, l, 2), seq=S, norms=fused2)

        after = [norm1_spec(l + 1) if l + 1 < depth else final_spec]
        if l == n_a - 1:
            after.append(kv_spec)
        if dense:
            hid, hid_tail = swiglu(h2[0], [ffn_wg, ffn_wu], l // 2, 1024)
            x2 = _gmm(hid, [ffn_wd], l // 2, mode="residual", tm=1024, tn=512, n_out=D,
                      out_dtype=F32, xres=x2, gate=(mod, l, 5), seq=S, a2=hid_tail)
            normed = [_norm(x2, ns, B, S) for ns in after]
        else:
            li = l // 2
            hp, route = _norm(x2, norm2, B, S)
            pos, n_rows, tile_expert, tile_valid = _moe_plan(route, T, MOE_TM)
            hs = _dispatch(hp, pos, n_rows)
            hid, hid_tail = swiglu(hs, [moe_w_gate, moe_w_up], li, MOE_TM, tile_expert=tile_expert,
                                   tile_valid=tile_valid, a_packed=True)
            ys = _gmm(hid, [moe_w_down], li, mode="cast", tm=MOE_TM, tn=1024, n_out=D,
                      out_dtype=F32, tile_expert=tile_expert, tile_valid=tile_valid, a2=hid_tail)
            x2, *normed = _combine(x2, ys, pos, route, (mod, l, 5), S, norms=after)
        h = normed[0]

        if l == n_a - 1:
            hkv = normed[1]
            w_kv = nsa_w_kv[None, None]
            kc = _gmm(hkv, [w_kv], 0, mode="cast", tm=1024, tn=512, n_out=2 * G * DK,
                      out_dtype=F32)
            kvb = _gmm(hkv, [w_kv], 0, mode="cast", tm=1024, tn=512, n_out=4 * G * DK,
                       out_dtype=BF16, n_off=(2 * G * DK) // 512)
            kvc = _nsa_compress(kc, jnp.stack([cmp_pos_k, cmp_pos_v]),
                                jnp.stack([cmp_k_w1, cmp_v_w1]), jnp.stack([cmp_k_w2, cmp_v_w2]),
                                B, S)
            shared = (kvb, kvc)

    return h.reshape(B, S, D)
```

```python
import functools

import numpy as np
import jax
import jax.numpy as jnp
from jax import lax
from jax.experimental import pallas as pl
from jax.experimental.pallas import tpu as pltpu

F32 = jnp.float32
BF16 = jnp.bfloat16

EPS = 1e-6
NEG = -1e30
FORCED_SCORE = 1e6
LOG2E = 1.4426950408889634

MLA_HEADS = 16
QK_NOPE = 128
QK_ROPE = 64
V_HEAD = 128
ROPE_THETA = 10000.0

NSA_HEADS = 16
NSA_GROUPS = 4
NSA_HPG = NSA_HEADS // NSA_GROUPS
NSA_DK = 128
CMP_BLOCK = 32
CMP_STRIDE = 16
SEL_BLOCK = 64
SEL_TOPN = 16
WINDOW = 512

N_EXPERTS = 8
LANES = 128
VMEM_LIMIT = 56 * 1024 * 1024

MOE_TM = 512

SEL_LANES = 32
POS_HI_LANE = 32
POS_LO_LANE = 35
MASK_BIG = 2.0 ** 30


def _cparams(n_axes):
    return pltpu.CompilerParams(
        dimension_semantics=("arbitrary",) * n_axes, vmem_limit_bytes=VMEM_LIMIT)


def _dot_nt(a, b):
    return lax.dot_general(a, b, (((1,), (1,)), ((), ())), preferred_element_type=F32)


def _mod_kernel(c_ref, w_ref, b_ref, o_ref):
    c = c_ref[...]
    ca = (c * jax.nn.sigmoid(c)).astype(BF16)
    o_ref[...] = jnp.dot(ca, w_ref[...].astype(BF16), preferred_element_type=F32) + b_ref[...]


def _modulation(c_pad, w, b, tn=2048):
    L, D, N = w.shape
    return pl.pallas_call(
        _mod_kernel,
        grid=(L, N // tn),
        in_specs=[
            pl.BlockSpec((8, D), lambda l, n: (0, 0)),
            pl.BlockSpec((None, D, tn), lambda l, n: (l, 0, n)),
            pl.BlockSpec((None, 1, tn), lambda l, n: (l, 0, n)),
        ],
        out_specs=pl.BlockSpec((None, 8, tn), lambda l, n: (l, 0, n)),
        out_shape=jax.ShapeDtypeStruct((L, 8, N), F32),
        compiler_params=_cparams(2),
        name="adaln_mod",
    )(c_pad, w, b.reshape(L, 1, N))


def _rms(x, g):
    return x * lax.rsqrt(jnp.mean(x * x, axis=-1, keepdims=True) + EPS) * g


_HI16 = 0xFFFF0000


def _pack_bf16_halves(h):
    half = h.shape[1] // 2
    bits = lax.bitcast_convert_type(h.astype(BF16).astype(F32), jnp.uint32)
    return (bits[:, :half] >> 16) | (bits[:, half:] & jnp.uint32(_HI16))


def _unpack_bf16_halves(a):
    lo = lax.bitcast_convert_type(a << 16, F32).astype(BF16)
    hi = lax.bitcast_convert_type(a & jnp.uint32(_HI16), F32).astype(BF16)
    return jnp.concatenate([lo, hi], axis=1)


def _norm_spec(g, shift=None, scale=None, router=None, out_dtype=BF16):
    return dict(g=g, shift=shift, scale=scale, router=router, out_dtype=out_dtype)


def _norm_cfg(ns):
    return (ns["shift"] is not None, ns["router"] is not None)


def _norm_io(ns, T, D, tm, batch_of, row_of):
    modulate, route = _norm_cfg(ns)
    args = [ns["g"].reshape(1, D)]
    in_specs = [pl.BlockSpec((1, D), lambda *a: (0, 0))]
    if modulate:
        for arr, l, j in (ns["shift"], ns["scale"]):
            args.append(arr)
            in_specs.append(pl.BlockSpec((None, None, None, 1, D),
                                         lambda *a, l=l, j=j: (l, batch_of(*a), j, 0, 0)))
    if route:
        wr, br = ns["router"]
        args += [wr, br]
        in_specs += [pl.BlockSpec((D, LANES), lambda *a: (0, 0)),
                     pl.BlockSpec((1, LANES), lambda *a: (0, 0))]
        out_shape = [jax.ShapeDtypeStruct((T, D // 2), jnp.uint32),
                     jax.ShapeDtypeStruct((T, LANES), F32)]
        out_specs = [pl.BlockSpec((tm, D // 2), lambda *a: (row_of(*a), 0)),
                     pl.BlockSpec((tm, LANES), lambda *a: (row_of(*a), 0))]
    else:
        out_shape = [jax.ShapeDtypeStruct((T, D), ns["out_dtype"])]
        out_specs = [pl.BlockSpec((tm, D), lambda *a: (row_of(*a), 0))]
    return args, in_specs, out_shape, out_specs


def _norm_counts(cfg):
    modulate, route = cfg
    return 1 + 2 * modulate + 2 * route, 1 + route


def _norm_apply(x, in_refs, out_refs, cfg):
    modulate, route = cfg
    g_ref, *rest = in_refs
    h = _rms(x, g_ref[...])
    if modulate:
        sh_ref, sc_ref, *rest = rest
        h = h * (1.0 + sc_ref[...]) + sh_ref[...]
    if not route:
        out_refs[0][...] = h.astype(out_refs[0].dtype)
        return
    wr_ref, br_ref = rest
    out_refs[0][...] = _pack_bf16_halves(h)
    w = wr_ref[...]
    h_hi, w_hi = h.astype(BF16), w.astype(BF16)
    h_lo = (h - h_hi.astype(F32)).astype(BF16)
    w_lo = (w - w_hi.astype(F32)).astype(BF16)
    logits = (jnp.dot(h_hi, w_hi, preferred_element_type=F32)
              + jnp.dot(h_lo, w_hi, preferred_element_type=F32)
              + jnp.dot(h_hi, w_lo, preferred_element_type=F32)) + br_ref[...]
    lane = lax.broadcasted_iota(jnp.int32, logits.shape, 1).astype(F32)
    logits = jnp.where(lane < N_EXPERTS, logits, -jnp.inf)
    v1 = jnp.max(logits, axis=-1, keepdims=True)
    i1 = jnp.min(jnp.where(logits == v1, lane, float(LANES)), axis=-1, keepdims=True)
    others = jnp.where(lane == i1, -jnp.inf, logits)
    v2 = jnp.max(others, axis=-1, keepdims=True)
    i2 = jnp.min(jnp.where(others == v2, lane, float(LANES)), axis=-1, keepdims=True)
    e = jnp.exp(v2 - v1)
    w1 = 1.0 / (1.0 + e)
    w2 = e / (1.0 + e)
    out_refs[1][...] = jnp.where(lane == 0, i1, jnp.where(lane == 1, i2, jnp.where(
        lane == 2, w1, jnp.where(lane == 3, w2, 0.0))))


def _apply_norms(x, refs_in, refs_out, cfgs):
    for cfg in cfgs:
        n_in, n_out = _norm_counts(cfg)
        _norm_apply(x, refs_in[:n_in], refs_out[:n_out], cfg)
        refs_in, refs_out = refs_in[n_in:], refs_out[n_out:]


def _norm_kernel(x_ref, *refs, cfg):
    n_in, _ = _norm_counts(cfg)
    _norm_apply(x_ref[...], refs[:n_in], refs[n_in:], cfg)


def _norm(x2, ns, B, S, ts=512):
    T, D = x2.shape
    nst = S // ts
    args, in_specs, out_shape, out_specs = _norm_io(
        ns, T, D, ts, lambda b, s: b, lambda b, s: b * nst + s)
    outs = pl.pallas_call(
        functools.partial(_norm_kernel, cfg=_norm_cfg(ns)),
        grid=(B, nst),
        in_specs=[pl.BlockSpec((ts, D), lambda b, s: (b * nst + s, 0))] + in_specs,
        out_specs=out_specs,
        out_shape=out_shape,
        compiler_params=_cparams(2),
        name="norm_mod",
    )(x2, *args)
    return outs if len(outs) > 1 else outs[0]


def _gmm_kernel(te_ref, tv_ref, nx_ref, a_ref, *refs, mode, out_scale, a_packed, n_w, layer,
                n_off, tn, n_n, norm_cfgs, k_split):
    n = pl.program_id(0)
    m = pl.program_id(1)
    if k_split is not None:
        a2_ref, *refs = refs
    w_hbm, refs = refs[:n_w], refs[n_w:]
    if mode == "residual":
        x_ref, gate_ref, *refs = refs
    n_norm_in = sum(_norm_counts(c)[0] for c in norm_cfgs)
    n_norm_out = sum(_norm_counts(c)[1] for c in norm_cfgs)
    norm_in, refs = refs[:n_norm_in], refs[n_norm_in:]
    o_ref, *refs = refs
    norm_out, refs = refs[:n_norm_out], refs[n_norm_out:]
    wst, wb, sem = refs[:n_w], refs[n_w:2 * n_w], refs[2 * n_w]
    first = jnp.logical_or(m == 0, te_ref[m] != te_ref[jnp.maximum(m - 1, 0)])
    valid = tv_ref[m] != 0

    def w_copy(i, e, nn):
        col = pl.multiple_of((nn + n_off) * tn, tn)
        return pltpu.make_async_copy(w_hbm[i].at[layer, e, :, pl.ds(col, tn)], wst[i], sem.at[i])

    @pl.when(jnp.logical_and(n == 0, m == 0))
    def _():
        for i in range(n_w):
            w_copy(i, te_ref[0], 0).start()

    @pl.when(first)
    def _():
        for i in range(n_w):
            w_copy(i, te_ref[m], n).wait()
            wb[i][...] = wst[i][...].astype(BF16)
        nm = nx_ref[m]
        same_sweep = nm >= 0
        e_next = jnp.where(same_sweep, te_ref[jnp.maximum(nm, 0)], te_ref[0])
        n_next = jnp.where(same_sweep, n, n + 1)

        @pl.when(jnp.logical_or(same_sweep, n + 1 < n_n))
        def _():
            for i in range(n_w):
                w_copy(i, e_next, n_next).start()

    @pl.when(valid)
    def _():
        a = _unpack_bf16_halves(a_ref[...]) if a_packed else a_ref[...]
        if mode == "swiglu":
            g = jnp.dot(a, wb[0][...], preferred_element_type=F32)
            u = jnp.dot(a, wb[1][...], preferred_element_type=F32)
            acc = g * jax.nn.sigmoid(g) * u
        elif k_split is None:
            acc = jnp.dot(a, wb[0][...], preferred_element_type=F32)
        else:
            acc = (jnp.dot(a, wb[0][:k_split], preferred_element_type=F32)
                   + jnp.dot(a2_ref[...], wb[0][k_split:], preferred_element_type=F32))
        if out_scale is not None:
            acc = acc * out_scale
        if mode == "residual":
            acc = x_ref[...] + gate_ref[...] * acc
        o_ref[...] = acc.astype(o_ref.dtype)
        _apply_norms(acc, norm_in, norm_out, norm_cfgs)

    @pl.when(jnp.logical_not(valid))
    def _():
        o_ref[...] = jnp.zeros_like(o_ref)


def _gmm(a, ws, l, *, mode, tm, tn, n_out, out_dtype, tile_expert=None, tile_valid=None,
         n_off=0, xres=None, gate=None, seq=None, out_scale=None, a_packed=False,
         alias_x=True, norms=None, a2=None):
    M, ka = a.shape
    K = ws[0].shape[2]
    mt = M // tm
    n_w = len(ws)
    n_n = n_out // tn
    if tile_expert is None:
        tile_expert = jnp.zeros((mt,), jnp.int32)
        tile_valid = jnp.ones((mt,), jnp.int32)
    idx = jnp.arange(mt, dtype=jnp.int32)
    later = ((idx[None, :] > idx[:, None]) & (tile_valid[None, :] != 0)
             & (tile_expert[None, :] != tile_expert[:, None]))
    next_run = jnp.where(jnp.any(later, axis=1), jnp.argmax(later, axis=1), -1).astype(jnp.int32)
    lhs = [a] if a2 is None else [a, a2]
    args = lhs + list(ws)
    in_specs = [pl.BlockSpec((tm, t.shape[1]), lambda n, m, te, tv, nx: (m, 0)) for t in lhs]
    in_specs += [pl.BlockSpec(memory_space=pl.ANY)] * n_w
    aliases = {}
    if mode == "residual":
        garr, gl, gj = gate
        per_b = seq // tm
        if alias_x:
            aliases = {3 + len(args): 0}
        args += [xres, garr]
        in_specs += [
            pl.BlockSpec((tm, tn), lambda n, m, te, tv, nx: (m, n)),
            pl.BlockSpec((None, None, None, 1, tn),
                         lambda n, m, te, tv, nx: (gl, m // per_b, gj, 0, n)),
        ]
    out_shape = [jax.ShapeDtypeStruct((M, n_out), out_dtype)]
    out_specs = [pl.BlockSpec((tm, tn), lambda n, m, te, tv, nx: (m, n))]
    bare = norms is None
    norms = norms or []
    for ns in norms:
        assert tn == n_out
        n_args, n_in, n_shape, n_out_specs = _norm_io(
            ns, M, n_out, tm, lambda n, m, *_: m // (seq // tm), lambda n, m, *_: m)
        args += n_args
        in_specs += n_in
        out_shape += n_shape
        out_specs += n_out_specs
    outs = pl.pallas_call(
        functools.partial(_gmm_kernel, mode=mode, out_scale=out_scale, a_packed=a_packed,
                          n_w=n_w, layer=l, n_off=n_off, tn=tn, n_n=n_n,
                          norm_cfgs=tuple(_norm_cfg(ns) for ns in norms),
                          k_split=None if a2 is None else ka),
        grid_spec=pltpu.PrefetchScalarGridSpec(
            num_scalar_prefetch=3,
            grid=(n_n, mt),
            in_specs=in_specs,
            out_specs=out_specs,
            scratch_shapes=([pltpu.VMEM((K, tn), F32) for _ in ws]
                            + [pltpu.VMEM((K, tn), BF16) for _ in ws]
                            + [pltpu.SemaphoreType.DMA((n_w,))]),
        ),
        out_shape=out_shape,
        input_output_aliases=aliases,
        compiler_params=_cparams(2),
        name="gmm_" + mode,
    )(tile_expert, tile_valid, next_run, *args)
    return outs[0] if bare else outs


def _rope_pairs(v, cos, s1, s2):
    return v * cos + pltpu.roll(v, LANES - QK_ROPE // 2, 1) * s1 + pltpu.roll(v, QK_ROPE // 2, 1) * s2


def _mla_in_kernel(a_ref, w_ref, gq_ref, gkv_ref, cos_ref, s1_ref, s2_ref,
                   cq_ref, ckv_ref, kpe_ref, wb, *, q_lora, kv_lora):
    @pl.when(pl.program_id(0) == 0)
    def _():
        wb[...] = w_ref[...].astype(BF16)

    acc = jnp.dot(a_ref[...], wb[...], preferred_element_type=F32)
    cq_ref[...] = _rms(acc[:, :q_lora], gq_ref[...]).astype(BF16)
    ckv_ref[...] = _rms(acc[:, q_lora:q_lora + kv_lora], gkv_ref[...]).astype(BF16)
    v = acc[:, q_lora + kv_lora:]
    kpe_ref[...] = _rope_pairs(v, cos_ref[...], s1_ref[...], s2_ref[...]).astype(BF16)


def _mla_in(h, w_pad, g_q, g_kv, rope_tabs, S, tm=1024):
    T, D = h.shape
    q_lora, kv_lora = g_q.shape[0], g_kv.shape[0]
    n_all = w_pad.shape[1]
    nst = S // tm
    tab_spec = pl.BlockSpec((tm, LANES), lambda i: (i % nst, 0))
    return pl.pallas_call(
        functools.partial(_mla_in_kernel, q_lora=q_lora, kv_lora=kv_lora),
        grid=(T // tm,),
        in_specs=[
            pl.BlockSpec((tm, D), lambda i: (i, 0)),
            pl.BlockSpec((D, n_all), lambda i: (0, 0)),
            pl.BlockSpec((1, q_lora), lambda i: (0, 0)),
            pl.BlockSpec((1, kv_lora), lambda i: (0, 0)),
            tab_spec, tab_spec, tab_spec,
        ],
        out_specs=[
            pl.BlockSpec((tm, q_lora), lambda i: (i, 0)),
            pl.BlockSpec((tm, kv_lora), lambda i: (i, 0)),
            pl.BlockSpec((tm, LANES), lambda i: (i, 0)),
        ],
        out_shape=[
            jax.ShapeDtypeStruct((T, q_lora), BF16),
            jax.ShapeDtypeStruct((T, kv_lora), BF16),
            jax.ShapeDtypeStruct((T, LANES), BF16),
        ],
        scratch_shapes=[pltpu.VMEM((D, n_all), BF16)],
        compiler_params=_cparams(1),
        name="mla_in",
    )(h, w_pad, g_q.reshape(1, -1), g_kv.reshape(1, -1), *rope_tabs)


def _mla_attn_kernel(qn_ref, qp_ref, kv_ref, kpe_ref, cos_ref, s1_ref, s2_ref, o_ref,
                     m_sc, l_sc, acc_sc, *, tq, hps):
    qi = pl.program_id(2)
    head_w = QK_NOPE + V_HEAD
    lane = lax.broadcasted_iota(jnp.int32, (tq, LANES), 1)
    qs = []
    for pair in range(hps // 2):
        qp = _rope_pairs(qp_ref[:, pair * LANES:(pair + 1) * LANES].astype(F32),
                         cos_ref[...], s1_ref[...], s2_ref[...])
        for sub in range(2):
            hh = 2 * pair + sub
            keep = (lane < QK_ROPE) if sub == 0 else (lane >= QK_ROPE)
            qs.append(jnp.concatenate(
                [qn_ref[:, hh * QK_NOPE:(hh + 1) * QK_NOPE],
                 jnp.where(keep, qp, 0.0).astype(BF16)], axis=1))
    m_sc[...] = jnp.full_like(m_sc, NEG)
    l_sc[...] = jnp.zeros_like(l_sc)
    acc_sc[...] = jnp.zeros_like(acc_sc)

    def tile(j, diagonal):
        ks = pl.multiple_of(j * tq, tq)
        kpe = kpe_ref[pl.ds(ks, tq), :]
        for hh in range(hps):
            c0 = hh * head_w
            k = jnp.concatenate([kv_ref[pl.ds(ks, tq), c0:c0 + QK_NOPE], kpe], axis=1)
            s = _dot_nt(qs[hh], k)
            if diagonal:
                row = lax.broadcasted_iota(jnp.int32, s.shape, 0)
                col = lax.broadcasted_iota(jnp.int32, s.shape, 1)
                s = jnp.where(col <= row, s, NEG)
            m_old = m_sc[hh]
            m_new = jnp.maximum(m_old, jnp.max(s, axis=-1, keepdims=True))
            alpha = jnp.exp2(m_old - m_new)
            p = jnp.exp2(s - jnp.concatenate([m_new] * (tq // LANES), axis=1))
            l_sc[hh] = alpha * l_sc[hh] + jnp.sum(p, axis=-1, keepdims=True)
            vv = kv_ref[pl.ds(ks, tq), c0 + QK_NOPE:c0 + head_w]
            acc_sc[hh] = alpha * acc_sc[hh] + jnp.dot(p.astype(BF16), vv,
                                                      preferred_element_type=F32)
            m_sc[hh] = m_new

    def body(j, carry):
        tile(j, False)
        return carry

    lax.fori_loop(0, qi, body, 0)
    tile(qi, True)
    for hh in range(hps):
        o_ref[:, hh * V_HEAD:(hh + 1) * V_HEAD] = (acc_sc[hh] / l_sc[hh]).astype(o_ref.dtype)


def _mla_attn(q_all, kv_all, kpe, rope_tabs, B, S, tq=512, hps=4):
    T = q_all.shape[0]
    H = MLA_HEADS
    nqt = S // tq
    qn_w, qp_w, kv_w = hps * QK_NOPE, hps * QK_ROPE, hps * (QK_NOPE + V_HEAD)
    tab_spec = pl.BlockSpec((tq, LANES), lambda b, p, i: (i, 0))
    return pl.pallas_call(
        functools.partial(_mla_attn_kernel, tq=tq, hps=hps),
        grid=(B, H // hps, nqt),
        in_specs=[
            pl.BlockSpec((tq, qn_w), lambda b, p, i: (b * nqt + i, p)),
            pl.BlockSpec((tq, qp_w), lambda b, p, i: (b * nqt + i, (H * QK_NOPE) // qp_w + p)),
            pl.BlockSpec((S, kv_w), lambda b, p, i: (b, p)),
            pl.BlockSpec((S, LANES), lambda b, p, i: (b, 0)),
            tab_spec, tab_spec, tab_spec,
        ],
        out_specs=pl.BlockSpec((tq, hps * V_HEAD), lambda b, p, i: (b * nqt + i, p)),
        out_shape=jax.ShapeDtypeStruct((T, H * V_HEAD), BF16),
        scratch_shapes=[pltpu.VMEM((hps, tq, LANES), F32), pltpu.VMEM((hps, tq, LANES), F32),
                        pltpu.VMEM((hps, tq, V_HEAD), F32)],
        compiler_params=_cparams(3),
        name="mla_attn",
    )(q_all, q_all, kv_all, kpe, *rope_tabs)


def _nsa_compress_kernel(t_ref, pe_ref, w1_ref, w2_ref, o_ref, *, n_chunk):
    half = CMP_BLOCK // 2
    pe = pe_ref[...]

    def chunk_rows(off):
        cols = []
        for l in range(half):
            x = t_ref[pl.ds(l, n_chunk, stride=CMP_STRIDE), :] + pe[off + l:off + l + 1, :]
            cols.append(x.astype(BF16))
        return jnp.concatenate(cols, axis=1)

    w1 = w1_ref[...].astype(BF16)
    kdim = half * NSA_DK
    p0 = jnp.dot(chunk_rows(0), w1[:kdim], preferred_element_type=F32)
    p1 = jnp.dot(chunk_rows(half), w1[kdim:], preferred_element_type=F32)
    pre = p0 + pltpu.roll(p1, n_chunk - 1, 0)
    hid = (pre * jax.nn.sigmoid(pre)).astype(BF16)
    out = jnp.dot(hid, w2_ref[...].astype(BF16), preferred_element_type=F32)
    rown = lax.broadcasted_iota(jnp.int32, out.shape, 0)
    o_ref[...] = jnp.where(rown < n_chunk - 1, out, 0.0).astype(o_ref.dtype)


def _nsa_compress(kc, pe, w1, w2, B, S):
    G = NSA_GROUPS
    n_chunk = S // CMP_STRIDE
    return pl.pallas_call(
        functools.partial(_nsa_compress_kernel, n_chunk=n_chunk),
        grid=(B, 2, G),
        in_specs=[
            pl.BlockSpec((S, NSA_DK), lambda b, i, g: (b, i * G + g)),
            pl.BlockSpec((None, CMP_BLOCK, NSA_DK), lambda b, i, g: (i, 0, 0)),
            pl.BlockSpec((None, CMP_BLOCK * NSA_DK, NSA_DK), lambda b, i, g: (i, 0, 0)),
            pl.BlockSpec((None, NSA_DK, NSA_DK), lambda b, i, g: (i, 0, 0)),
        ],
        out_specs=pl.BlockSpec((None, None, None, n_chunk, NSA_DK), lambda b, i, g: (b, i, g, 0, 0)),
        out_shape=jax.ShapeDtypeStruct((B, 2, G, n_chunk, NSA_DK), BF16),
        compiler_params=_cparams(3),
        name="nsa_compress",
    )(kc, pe, w1, w2)


def _nsa_attn_kernel(q_ref, gt_ref, ks_ref, vs_ref, kw_ref, vw_ref, kc_ref, vc_ref, qtab_ref,
                     ktab_ref, ctab_ref, mt_ref, cbias_ref, wbias_ref, o_ref, m_sc, acc_sc,
                     *, tq, tk, n_sel):
    qi = pl.program_id(2)
    t0 = qi * tq
    HP = NSA_HPG
    R = HP * tq
    qb = q_ref[...]
    qs = jnp.concatenate([qb[:, h * NSA_DK:(h + 1) * NSA_DK] for h in range(HP)], axis=0)
    qtab = qtab_ref[...]
    qa = jnp.concatenate([qs, qtab], axis=1)
    rowi = lax.broadcasted_iota(jnp.int32, (R, 1), 0)
    tcol = (t0 + rowi % tq).astype(F32)

    kc = jnp.concatenate([kc_ref[...], ctab_ref[...]], axis=1)
    s = _dot_nt(qa, kc) + jnp.concatenate([cbias_ref[...]] * HP, axis=0)
    p = jnp.exp2(s - jnp.max(s, axis=-1, keepdims=True))
    l = jnp.sum(p, axis=-1, keepdims=True)
    p_cmp = jnp.where(tcol >= CMP_BLOCK - 1, p / l, 0.0)
    o_cmp = jnp.dot(p_cmp.astype(BF16), vc_ref[...], preferred_element_type=F32)

    ps = p_cmp[0:tq]
    for h in range(1, HP):
        ps = ps + p_cmp[h * tq:(h + 1) * tq]
    ps_hi = ps.astype(BF16)
    ps_lo = (ps - ps_hi.astype(F32)).astype(BF16)
    imp = _dot_nt(mt_ref[...], ps_hi) + _dot_nt(mt_ref[...], ps_lo)
    jrow = lax.broadcasted_iota(jnp.int32, (n_sel, tq), 0)
    blk_t = (t0 + lax.broadcasted_iota(jnp.int32, (n_sel, tq), 1)) // SEL_BLOCK
    forced = (jrow == 0) | (jrow == blk_t) | (jrow == blk_t - 1)
    imp = jnp.where(forced, FORCED_SCORE, imp)
    imp = jnp.where(jrow > blk_t, -1.0, imp)
    rank = jnp.zeros((n_sel, tq), F32)
    for k in range(n_sel):
        rk = imp[k:k + 1, :]
        beats = (rk > imp) | ((rk == imp) & (jrow > k))
        rank = rank + jnp.where(beats, 1.0, 0.0)
    sel_t = jnp.where(rank < SEL_TOPN, 1.0, 0.0)
    sel_t = jnp.concatenate([sel_t, jnp.zeros((LANES - n_sel, tq), F32)], axis=0)
    sel_neg = ((sel_t.T - 1.0) * MASK_BIG).astype(BF16)
    lane = lax.broadcasted_iota(jnp.int32, (R, LANES), 1)
    qx = jnp.where(lane < SEL_LANES, jnp.concatenate([sel_neg] * HP, axis=0), qtab)
    qsel = jnp.concatenate([qs, qx], axis=1)

    m_sc[...] = jnp.full_like(m_sc, NEG)
    acc_sc[...] = jnp.zeros_like(acc_sc)
    ones = jnp.ones((tk, LANES), BF16)

    def tile(j, causal):
        ks = pl.multiple_of(j * tk, tk)
        k = jnp.concatenate([ks_ref[pl.ds(ks, tk), :], ktab_ref[pl.ds(ks, tk), :]], axis=1)
        sc = _dot_nt(qsel, k)
        if causal:
            kpos = (ks + lax.broadcasted_iota(jnp.int32, (1, tk), 1)).astype(F32)
            sc = jnp.where(kpos <= tcol, sc, NEG)
        m_old = m_sc[...]
        m_new = jnp.maximum(m_old, jnp.max(sc, axis=-1, keepdims=True))
        alpha = jnp.exp2(m_old - m_new)
        pp = jnp.exp2(sc - jnp.concatenate([m_new] * (tk // LANES), axis=1))
        vv = jnp.concatenate([vs_ref[pl.ds(ks, tk), :], ones], axis=1)
        acc_sc[...] = (jnp.concatenate([alpha, alpha], axis=1) * acc_sc[...]
                       + jnp.dot(pp.astype(BF16), vv, preferred_element_type=F32))
        m_sc[...] = m_new

    def body(j, carry):
        tile(j, False)
        return carry

    n_full = t0 // tk
    lax.fori_loop(0, n_full, body, 0)
    for c in range(tq // tk):
        tile(n_full + c, True)
    acc = acc_sc[...]
    o_sel = acc[:, :NSA_DK] / acc[:, NSA_DK:]

    span = WINDOW + tq
    ws = pl.multiple_of(jnp.maximum(t0 - WINDOW, 0), tq)
    wbias = wbias_ref[(t0 - ws) // tq]
    kw = jnp.concatenate([kw_ref[pl.ds(ws, span), :], ktab_ref[pl.ds(ws, span), :]], axis=1)
    sw = _dot_nt(qa, kw) + jnp.concatenate([wbias] * HP, axis=0)
    pw = jnp.exp2(sw - jnp.max(sw, axis=-1, keepdims=True))
    vw = jnp.concatenate([vw_ref[pl.ds(ws, span), :], jnp.ones((span, LANES), BF16)], axis=1)
    rw = jnp.dot(pw.astype(BF16), vw, preferred_element_type=F32)
    o_win = rw[:, :NSA_DK] / rw[:, NSA_DK:]

    gt = jax.nn.sigmoid(gt_ref[...])

    def gcol(i):
        return jnp.concatenate([gt[:, i * HP + h:i * HP + h + 1] for h in range(HP)], axis=0)

    o = gcol(0) * o_cmp + gcol(1) * o_sel + gcol(2) * o_win
    o_ref[...] = jnp.concatenate([o[h * tq:(h + 1) * tq] for h in range(HP)], axis=1).astype(o_ref.dtype)


def _nsa_attn(q, gates, kvb, kvc, tabs, B, S, tq=128, tk=256):
    T = q.shape[0]
    G, DK, HP = NSA_GROUPS, NSA_DK, NSA_HPG
    qtab, ktab, ctab, mt, cbias, wbias = tabs
    nqt = S // tq
    n_chunk = kvc.shape[3]
    n_sel = S // SEL_BLOCK

    def kv_spec(i):
        return pl.BlockSpec((S, DK), lambda b, g, t, i=i: (b, i * G + g))

    def cmp_spec(i):
        return pl.BlockSpec((None, None, None, n_chunk, DK), lambda b, g, t, i=i: (b, i, g, 0, 0))

    return pl.pallas_call(
        functools.partial(_nsa_attn_kernel, tq=tq, tk=tk, n_sel=n_sel),
        grid=(B, G, nqt),
        in_specs=[
            pl.BlockSpec((tq, HP * DK), lambda b, g, t: (b * nqt + t, g)),
            pl.BlockSpec((tq, LANES), lambda b, g, t: (b * nqt + t, g)),
            kv_spec(0), kv_spec(1), kv_spec(2), kv_spec(3),
            cmp_spec(0), cmp_spec(1),
            pl.BlockSpec((None, HP * tq, LANES), lambda b, g, t: (g, 0, 0)),
            pl.BlockSpec(ktab.shape, lambda b, g, t: (0, 0)),
            pl.BlockSpec(ctab.shape, lambda b, g, t: (0, 0)),
            pl.BlockSpec(mt.shape, lambda b, g, t: (0, 0)),
            pl.BlockSpec((tq, n_chunk), lambda b, g, t: (t, 0)),
            pl.BlockSpec(wbias.shape, lambda b, g, t: (0, 0, 0)),
        ],
        out_specs=pl.BlockSpec((tq, HP * DK), lambda b, g, t: (b * nqt + t, g)),
        out_shape=jax.ShapeDtypeStruct((T, NSA_HEADS * DK), BF16),
        scratch_shapes=[pltpu.VMEM((HP * tq, LANES), F32),
                        pltpu.VMEM((HP * tq, DK + LANES), F32)],
        compiler_params=_cparams(3),
        name="nsa_attn",
    )(q, gates, kvb, kvb, kvb, kvb, kvc, kvc, qtab, ktab, ctab, mt, cbias, wbias)


def _row_copy(src_hbm, dst_vmem, sem, src_row, dst_row):
    return pltpu.make_async_copy(src_hbm.at[pl.ds(src_row, 1), :], dst_vmem.at[pl.ds(dst_row, 1), :], sem)


def _dispatch_kernel(pos_ref, src_ref, init_hbm, out_hbm, sem, *, chunk, n_tok):
    del init_hbm
    base = pl.program_id(0) * chunk

    def start(r, c):
        t = base + r
        _row_copy(src_ref, out_hbm, sem, r, pos_ref[t]).start()
        _row_copy(src_ref, out_hbm, sem, r, pos_ref[n_tok + t]).start()
        return c

    def wait(r, c):
        _row_copy(src_ref, out_hbm, sem, 0, 0).wait()
        return c

    lax.fori_loop(0, chunk, start, 0, unroll=8)
    lax.fori_loop(0, 2 * chunk, wait, 0, unroll=8)


def _dispatch(h_packed, pos, n_rows, chunk=512):
    T, W = h_packed.shape
    n_steps = T // chunk
    init = jnp.zeros((n_rows, W), h_packed.dtype)
    return pl.pallas_call(
        functools.partial(_dispatch_kernel, chunk=chunk, n_tok=T),
        grid_spec=pltpu.PrefetchScalarGridSpec(
            num_scalar_prefetch=1,
            grid=(n_steps,),
            in_specs=[pl.BlockSpec((chunk, W), lambda i, pos: (i, 0)),
                      pl.BlockSpec(memory_space=pl.ANY)],
            out_specs=pl.BlockSpec(memory_space=pl.ANY),
            scratch_shapes=[pltpu.SemaphoreType.DMA(())],
        ),
        out_shape=jax.ShapeDtypeStruct((n_rows, W), h_packed.dtype),
        input_output_aliases={2: 0},
        compiler_params=_cparams(1),
        name="moe_dispatch",
    )(pos, h_packed, init)


def _combine_kernel(pos_ref, y_hbm, x_ref, gate_ref, rt_ref, *refs, tg, n_tok, n_steps,
                    norm_cfgs):
    n_norm_in = sum(_norm_counts(c)[0] for c in norm_cfgs)
    n_norm_out = sum(_norm_counts(c)[1] for c in norm_cfgs)
    norm_in, (o_ref, *refs) = refs[:n_norm_in], refs[n_norm_in:]
    norm_out, (buf, sem) = refs[:n_norm_out], refs[n_norm_out:]
    i = pl.program_id(0)
    slot = i % 2

    def issue(step, slot_):
        def start(r, c):
            t = step * tg + r
            _row_copy(y_hbm, buf.at[slot_, 0], sem.at[slot_], pos_ref[t], r).start()
            _row_copy(y_hbm, buf.at[slot_, 1], sem.at[slot_], pos_ref[n_tok + t], r).start()
            return c

        lax.fori_loop(0, tg, start, 0, unroll=8)

    @pl.when(i == 0)
    def _():
        issue(0, 0)

    @pl.when(i + 1 < n_steps)
    def _():
        issue(i + 1, 1 - slot)

    def wait(r, c):
        _row_copy(y_hbm, buf.at[slot, 0], sem.at[slot], 0, 0).wait()
        return c

    lax.fori_loop(0, 2 * tg, wait, 0, unroll=8)
    rt = rt_ref[...]
    y = rt[:, 2:3] * buf[slot, 0] + rt[:, 3:4] * buf[slot, 1]
    x_new = x_ref[...] + gate_ref[...] * y
    o_ref[...] = x_new
    _apply_norms(x_new, norm_in, norm_out, norm_cfgs)


def _combine(x2, y_sorted, pos, route, gate, S, norms=(), tg=256):
    T, D = x2.shape
    garr, gl, gj = gate
    per_b = S // tg
    n_steps = T // tg
    args = [y_sorted, x2, garr, route]
    in_specs = [
        pl.BlockSpec(memory_space=pl.ANY),
        pl.BlockSpec((tg, D), lambda i, pos: (i, 0)),
        pl.BlockSpec((None, None, None, 1, D), lambda i, pos: (gl, i // per_b, gj, 0, 0)),
        pl.BlockSpec((tg, LANES), lambda i, pos: (i, 0)),
    ]
    out_shape = [jax.ShapeDtypeStruct((T, D), F32)]
    out_specs = [pl.BlockSpec((tg, D), lambda i, pos: (i, 0))]
    for ns in norms:
        n_args, n_in, n_shape, n_out_specs = _norm_io(
            ns, T, D, tg, lambda i, *_: i // per_b, lambda i, *_: i)
        args += n_args
        in_specs += n_in
        out_shape += n_shape
        out_specs += n_out_specs
    outs = pl.pallas_call(
        functools.partial(_combine_kernel, tg=tg, n_tok=T, n_steps=n_steps,
                          norm_cfgs=tuple(_norm_cfg(ns) for ns in norms)),
        grid_spec=pltpu.PrefetchScalarGridSpec(
            num_scalar_prefetch=1,
            grid=(n_steps,),
            in_specs=in_specs,
            out_specs=out_specs,
            scratch_shapes=[pltpu.VMEM((2, 2, tg, D), F32), pltpu.SemaphoreType.DMA((2,))],
        ),
        out_shape=out_shape,
        input_output_aliases={2: 0},
        compiler_params=_cparams(1),
        name="moe_combine",
    )(pos, *args)
    return outs if norms else outs[0]


def _moe_plan(route, T, tm):
    E = N_EXPERTS
    n_tiles = (2 * T) // tm + E
    P = n_tiles * tm
    e_pair = jnp.concatenate([route[:, 0], route[:, 1]]).astype(jnp.int32)
    onehot = (e_pair[:, None] == jnp.arange(E, dtype=jnp.int32)[None, :]).astype(jnp.int32)
    csum = jnp.cumsum(onehot, axis=0)
    rank = jnp.sum(onehot * (csum - 1), axis=1)
    counts = csum[-1]
    padded = ((counts + tm - 1) // tm) * tm
    ends = jnp.cumsum(padded)
    pos = (ends - padded)[e_pair] + rank
    tile_start = jnp.arange(n_tiles, dtype=jnp.int32) * tm
    tile_valid = (tile_start < ends[-1]).astype(jnp.int32)
    tile_expert = jnp.sum((tile_start[:, None] >= ends[None, :]).astype(jnp.int32), axis=1)
    last_valid = tile_expert[jnp.maximum(ends[-1] // tm - 1, 0)]
    tile_expert = jnp.where(tile_valid == 1, tile_expert, last_valid).astype(jnp.int32)
    return pos.astype(jnp.int32), P, tile_expert, tile_valid


def _rope_tables(S):
    d = QK_ROPE
    inv = ROPE_THETA ** (-jnp.arange(0, d, 2, dtype=F32) / d)
    ang = jnp.arange(S).astype(F32)[:, None] * inv[None, :]
    cos, sin = jnp.cos(ang), jnp.sin(ang)
    z = jnp.zeros_like(sin)
    return (jnp.concatenate([cos, cos, cos, cos], axis=1),
            jnp.concatenate([-sin, z, -sin, z], axis=1),
            jnp.concatenate([z, sin, z, sin], axis=1))


def _pos_columns(pos):
    tab = np.zeros((pos.shape[0], LANES), np.float32)
    tab[:, POS_HI_LANE:POS_HI_LANE + 3] = (LANES * (pos // LANES))[:, None]
    tab[:, POS_LO_LANE:POS_LO_LANE + 3] = (pos % LANES)[:, None]
    return tab


def _nsa_tables(S, tq):
    n_cmp = (S - CMP_BLOCK) // CMP_STRIDE + 1
    n_sel = S // SEL_BLOCK
    n_chunk = S // CMP_STRIDE
    tok = np.arange(n_cmp)[:, None] * CMP_STRIDE + np.arange(CMP_BLOCK)[None, :]
    blk = tok // SEL_BLOCK
    m = (blk[:, :, None] == np.arange(n_sel)[None, None, :]).sum(axis=1) / CMP_BLOCK
    mt = np.zeros((n_sel, n_chunk), np.float32)
    mt[:, :n_cmp] = m.T
    keys = np.arange(S)
    ktab = _pos_columns(keys)
    ktab[:, :SEL_LANES] = (keys[:, None] // SEL_BLOCK == np.arange(SEL_LANES)[None, :])
    ctab = _pos_columns(np.arange(n_chunk) * CMP_STRIDE + (CMP_BLOCK - 1))
    slopes = jnp.asarray(2.0 ** (-8.0 * np.arange(1, NSA_HEADS + 1) / NSA_HEADS), F32)
    a = slopes * LOG2E
    a_hi = a.astype(BF16)
    r1 = a - a_hi.astype(F32)
    a_mid = r1.astype(BF16)
    a_lo = (r1 - a_mid.astype(F32)).astype(BF16)
    pieces = jnp.stack([a_hi, a_mid, a_lo], axis=-1)
    qrow = jnp.zeros((NSA_HEADS, LANES), BF16)
    qrow = qrow.at[:, POS_HI_LANE:POS_HI_LANE + 3].set(pieces)
    qrow = qrow.at[:, POS_LO_LANE:POS_LO_LANE + 3].set(pieces)
    qtab = jnp.repeat(qrow.reshape(NSA_GROUPS, NSA_HPG, LANES), tq, axis=1)
    cend = np.arange(n_chunk) * CMP_STRIDE + (CMP_BLOCK - 1)
    cvis = (keys[:, None] >= cend[None, :]) & (np.arange(n_chunk)[None, :] < n_cmp)
    cbias = np.where(cvis, 0.0, NEG).astype(np.float32)
    span = WINDOW + tq
    d = (np.arange(WINDOW // tq + 1)[:, None, None] * tq + np.arange(tq)[None, :, None]
         - np.arange(span)[None, None, :])
    wbias = np.where((d >= 0) & (d < WINDOW), 0.0, NEG).astype(np.float32)
    return (qtab, jnp.asarray(ktab, BF16), jnp.asarray(ctab, BF16), jnp.asarray(mt, BF16),
            jnp.asarray(cbias), jnp.asarray(wbias))


def kernel(x, c, ada_w, ada_b, norm1_g, norm2_g, mla_w_in, mla_g_q, mla_g_kv, mla_w_uq, mla_w_ukv, mla_w_o, kv_ada_w, kv_ada_b, kv_norm_g, nsa_w_kv, cmp_pos_k, cmp_pos_v, cmp_k_w1, cmp_k_w2, cmp_v_w1, cmp_v_w2, nsa_w_in, nsa_w_o, ffn_w_gate, ffn_w_up, ffn_w_down, moe_w_router, moe_b_router, moe_w_gate, moe_w_up, moe_w_down, final_g):
    B, S, D = x.shape
    T = B * S
    depth = ada_w.shape[0]
    n_a = mla_w_in.shape[0]
    H = MLA_HEADS
    G, HP, DK = NSA_GROUPS, NSA_HPG, NSA_DK
    d_ff = ffn_w_gate.shape[-1]
    nsa_tq = 512

    c_pad = jnp.zeros((8, D), F32).at[:B].set(c)
    mod = _modulation(c_pad, ada_w, ada_b)[:, :B].reshape(depth, B, 6, 1, D)
    kv_mod = _modulation(c_pad, kv_ada_w[None], kv_ada_b[None])[:, :B].reshape(1, B, 2, 1, D)

    rope_tabs = _rope_tables(S)
    nsa_tabs = _nsa_tables(S, nsa_tq)

    ffn_wg = ffn_w_gate[:, None]
    ffn_wu = ffn_w_up[:, None]
    ffn_wd = ffn_w_down[:, None]

    def norm1_spec(l):
        return _norm_spec(norm1_g[l], (mod, l, 0), (mod, l, 1))

    def swiglu(a, ws, li, tm, **kw):
        wide_tn = 1024
        wide = (d_ff // wide_tn) * wide_tn
        tail = d_ff - wide
        main = _gmm(a, ws, li, mode="swiglu", tm=tm, tn=wide_tn, n_out=wide, out_dtype=BF16, **kw)
        rest = _gmm(a, ws, li, mode="swiglu", tm=tm, tn=tail, n_out=tail, out_dtype=BF16,
                    n_off=wide // tail, **kw)
        return main, rest

    kv_spec = _norm_spec(kv_norm_g, (kv_mod, 0, 0), (kv_mod, 0, 1))
    final_spec = _norm_spec(final_g, out_dtype=F32)

    x2 = x.reshape(T, D)
    shared = None
    h = _norm(x2, norm1_spec(0), B, S)
    for l in range(depth):
        dense = l % 2 == 0
        if dense:
            norm2 = _norm_spec(norm2_g[l], (mod, l, 3), (mod, l, 4))
        else:
            wr = jnp.pad(moe_w_router[l // 2], ((0, 0), (0, LANES - N_EXPERTS)))
            br = jnp.pad(moe_b_router[l // 2], (0, LANES - N_EXPERTS)).reshape(1, LANES)
            norm2 = _norm_spec(norm2_g[l], (mod, l, 3), (mod, l, 4), router=(wr, br))
        fused2 = [norm2] if dense else []
        if l < n_a:
            w_in = mla_w_in[l]
            q_lora, kv_lora = mla_g_q.shape[1], mla_g_kv.shape[1]
            w_pad = jnp.concatenate([w_in, w_in[:, q_lora + kv_lora:]], axis=1)
            cq, ckv, kpe = _mla_in(h, w_pad, mla_g_q[l], mla_g_kv[l], rope_tabs, S)
            wq = mla_w_uq[l].reshape(q_lora, H, QK_NOPE + QK_ROPE)
            wq = jnp.concatenate([wq[:, :, :QK_NOPE].reshape(q_lora, H * QK_NOPE),
                                  wq[:, :, QK_NOPE:].reshape(q_lora, H * QK_ROPE)], axis=1)
            q_all = _gmm(cq, [wq[None, None]], 0, mode="cast", tm=2048, tn=1024,
                         n_out=wq.shape[1], out_dtype=BF16,
                         out_scale=(QK_NOPE + QK_ROPE) ** -0.5 * LOG2E)
            kv_all = _gmm(ckv, [mla_w_ukv[:, None]], l, mode="cast", tm=2048, tn=1024,
                          n_out=mla_w_ukv.shape[-1], out_dtype=BF16)
            o = _mla_attn(q_all, kv_all, kpe, rope_tabs, B, S)
            x2, *h2 = _gmm(o, [mla_w_o[:, None]], l, mode="residual", tm=512, tn=D, n_out=D,
                           out_dtype=F32, xres=x2, gate=(mod, l, 2), seq=S, norms=fused2,
                           alias_x=l > 0)
        else:
            jb = l - n_a
            w_in = nsa_w_in[jb]
            q = _gmm(h, [nsa_w_in[:, None]], jb, mode="cast", tm=1024, tn=1024,
                     n_out=NSA_HEADS * DK, out_dtype=BF16, out_scale=DK ** -0.5 * LOG2E)
            wg = w_in[:, NSA_HEADS * DK:].reshape(D, G, HP, 3).transpose(0, 1, 3, 2)
            wg = jnp.pad(wg.reshape(D, G, 3 * HP), ((0, 0), (0, 0), (0, LANES - 3 * HP)))
            gates = _gmm(h, [wg.reshape(1, 1, D, G * LANES)], 0, mode="cast", tm=1024, tn=512,
                         n_out=G * LANES, out_dtype=F32)
            kvb, kvc = shared
            o = _nsa_attn(q, gates, kvb, kvc, nsa_tabs, B, S, tq=nsa_tq)
            x2, *h2 = _gmm(o, [nsa_w_o[:, None]], jb, mode="residual", tm=512, tn=D, n_out=D,
                           out_dtype=F32, xres=x2, gate=(mod, l, 2), seq=S, norms=fused2)

        after = [norm1_spec(l + 1) if l + 1 < depth else final_spec]
        if l == n_a - 1:
            after.append(kv_spec)
        if dense:
            hid, hid_tail = swiglu(h2[0], [ffn_wg, ffn_wu], l // 2, 1024)
            x2 = _gmm(hid, [ffn_wd], l // 2, mode="residual", tm=1024, tn=512, n_out=D,
                      out_dtype=F32, xres=x2, gate=(mod, l, 5), seq=S, a2=hid_tail)
            normed = [_norm(x2, ns, B, S) for ns in after]
        else:
            li = l // 2
            hp, route = _norm(x2, norm2, B, S)
            pos, n_rows, tile_expert, tile_valid = _moe_plan(route, T, MOE_TM)
            hs = _dispatch(hp, pos, n_rows)
            hid, hid_tail = swiglu(hs, [moe_w_gate, moe_w_up], li, MOE_TM, tile_expert=tile_expert,
                                   tile_valid=tile_valid, a_packed=True)
            ys = _gmm(hid, [moe_w_down], li, mode="cast", tm=MOE_TM, tn=1024, n_out=D,
                      out_dtype=F32, tile_expert=tile_expert, tile_valid=tile_valid, a2=hid_tail)
            x2, *normed = _combine(x2, ys, pos, route, (mod, l, 5), S, norms=after)
        h = normed[0]

        if l == n_a - 1:
            hkv = normed[1]
            w_kv = nsa_w_kv[None, None]
            kc = _gmm(hkv, [w_kv], 0, mode="cast", tm=1024, tn=512, n_out=2 * G * DK,
                      out_dtype=F32)
            kvb = _gmm(hkv, [w_kv], 0, mode="cast", tm=1024, tn=1024, n_out=4 * G * DK,
                       out_dtype=BF16, n_off=(2 * G * DK) // 1024)
            kvc = _nsa_compress(kc, jnp.stack([cmp_pos_k, cmp_pos_v]),
                                jnp.stack([cmp_k_w1, cmp_v_w1]), jnp.stack([cmp_k_w2, cmp_v_w2]),
                                B, S)
            shared = (kvb, kvc)

    return h.reshape(B, S, D)
```

```python
import functools

import numpy as np
import jax
import jax.numpy as jnp
from jax import lax
from jax.experimental import pallas as pl
from jax.experimental.pallas import tpu as pltpu

F32 = jnp.float32
BF16 = jnp.bfloat16

EPS = 1e-6
NEG = -1e30
FORCED_SCORE = 1e6
LOG2E = 1.4426950408889634

MLA_HEADS = 16
QK_NOPE = 128
QK_ROPE = 64
V_HEAD = 128
ROPE_THETA = 10000.0

NSA_HEADS = 16
NSA_GROUPS = 4
NSA_HPG = NSA_HEADS // NSA_GROUPS
NSA_DK = 128
CMP_BLOCK = 32
CMP_STRIDE = 16
SEL_BLOCK = 64
SEL_TOPN = 16
WINDOW = 512

N_EXPERTS = 8
LANES = 128
VMEM_LIMIT = 56 * 1024 * 1024

MOE_TM = 512
DENSE_TM = 1024
NARROW_TN = 512
WIDE_TN = 1024
LORA_TM = 2048
OUT_PROJ_TM = 512
NSA_TQ = 512

SEL_LANES = 32
POS_HI_LANE = 32
POS_LO_LANE = 35
MASK_BIG = 2.0 ** 30


def _cparams(n_axes):
    return pltpu.CompilerParams(
        dimension_semantics=("arbitrary",) * n_axes, vmem_limit_bytes=VMEM_LIMIT)


def _dot_nt(a, b):
    return lax.dot_general(a, b, (((1,), (1,)), ((), ())), preferred_element_type=F32)


def _mod_kernel(c_ref, w_ref, b_ref, o_ref):
    c = c_ref[...]
    ca = (c * jax.nn.sigmoid(c)).astype(BF16)
    o_ref[...] = jnp.dot(ca, w_ref[...].astype(BF16), preferred_element_type=F32) + b_ref[...]


def _modulation(c_pad, w, b, tn=2048):
    L, D, N = w.shape
    return pl.pallas_call(
        _mod_kernel,
        grid=(L, N // tn),
        in_specs=[
            pl.BlockSpec((8, D), lambda l, n: (0, 0)),
            pl.BlockSpec((None, D, tn), lambda l, n: (l, 0, n)),
            pl.BlockSpec((None, 1, tn), lambda l, n: (l, 0, n)),
        ],
        out_specs=pl.BlockSpec((None, 8, tn), lambda l, n: (l, 0, n)),
        out_shape=jax.ShapeDtypeStruct((L, 8, N), F32),
        compiler_params=_cparams(2),
        name="adaln_mod",
    )(c_pad, w, b.reshape(L, 1, N))


def _rms(x, g):
    return x * lax.rsqrt(jnp.mean(x * x, axis=-1, keepdims=True) + EPS) * g


_HI16 = 0xFFFF0000


def _pack_bf16_halves(h):
    half = h.shape[1] // 2
    bits = lax.bitcast_convert_type(h.astype(BF16).astype(F32), jnp.uint32)
    return (bits[:, :half] >> 16) | (bits[:, half:] & jnp.uint32(_HI16))


def _unpack_bf16_halves(a):
    lo = lax.bitcast_convert_type(a << 16, F32).astype(BF16)
    hi = lax.bitcast_convert_type(a & jnp.uint32(_HI16), F32).astype(BF16)
    return jnp.concatenate([lo, hi], axis=1)


def _norm_spec(g, shift=None, scale=None, router=None, out_dtype=BF16):
    return dict(g=g, shift=shift, scale=scale, router=router, out_dtype=out_dtype)


def _norm_cfg(ns):
    return (ns["shift"] is not None, ns["router"] is not None)


def _norm_io(ns, T, D, tm, batch_of, row_of):
    modulate, route = _norm_cfg(ns)
    args = [ns["g"].reshape(1, D)]
    in_specs = [pl.BlockSpec((1, D), lambda *a: (0, 0))]
    if modulate:
        for arr, l, j in (ns["shift"], ns["scale"]):
            args.append(arr)
            in_specs.append(pl.BlockSpec((None, None, None, 1, D),
                                         lambda *a, l=l, j=j: (l, batch_of(*a), j, 0, 0)))
    if route:
        wr, br = ns["router"]
        args += [wr, br]
        in_specs += [pl.BlockSpec((D, LANES), lambda *a: (0, 0)),
                     pl.BlockSpec((1, LANES), lambda *a: (0, 0))]
        out_shape = [jax.ShapeDtypeStruct((T, D // 2), jnp.uint32),
                     jax.ShapeDtypeStruct((T, LANES), F32)]
        out_specs = [pl.BlockSpec((tm, D // 2), lambda *a: (row_of(*a), 0)),
                     pl.BlockSpec((tm, LANES), lambda *a: (row_of(*a), 0))]
    else:
        out_shape = [jax.ShapeDtypeStruct((T, D), ns["out_dtype"])]
        out_specs = [pl.BlockSpec((tm, D), lambda *a: (row_of(*a), 0))]
    return args, in_specs, out_shape, out_specs


def _norm_counts(cfg):
    modulate, route = cfg
    return 1 + 2 * modulate + 2 * route, 1 + route


def _norm_apply(x, in_refs, out_refs, cfg):
    modulate, route = cfg
    g_ref, *rest = in_refs
    h = _rms(x, g_ref[...])
    if modulate:
        sh_ref, sc_ref, *rest = rest
        h = h * (1.0 + sc_ref[...]) + sh_ref[...]
    if not route:
        out_refs[0][...] = h.astype(out_refs[0].dtype)
        return
    wr_ref, br_ref = rest
    out_refs[0][...] = _pack_bf16_halves(h)
    w = wr_ref[...]
    h_hi, w_hi = h.astype(BF16), w.astype(BF16)
    h_lo = (h - h_hi.astype(F32)).astype(BF16)
    w_lo = (w - w_hi.astype(F32)).astype(BF16)
    logits = (jnp.dot(h_hi, w_hi, preferred_element_type=F32)
              + jnp.dot(h_lo, w_hi, preferred_element_type=F32)
              + jnp.dot(h_hi, w_lo, preferred_element_type=F32)) + br_ref[...]
    lane = lax.broadcasted_iota(jnp.int32, logits.shape, 1).astype(F32)
    logits = jnp.where(lane < N_EXPERTS, logits, -jnp.inf)
    v1 = jnp.max(logits, axis=-1, keepdims=True)
    i1 = jnp.min(jnp.where(logits == v1, lane, float(LANES)), axis=-1, keepdims=True)
    others = jnp.where(lane == i1, -jnp.inf, logits)
    v2 = jnp.max(others, axis=-1, keepdims=True)
    i2 = jnp.min(jnp.where(others == v2, lane, float(LANES)), axis=-1, keepdims=True)
    e = jnp.exp(v2 - v1)
    w1 = 1.0 / (1.0 + e)
    w2 = e / (1.0 + e)
    out_refs[1][...] = jnp.where(lane == 0, i1, jnp.where(lane == 1, i2, jnp.where(
        lane == 2, w1, jnp.where(lane == 3, w2, 0.0))))


def _apply_norms(x, refs_in, refs_out, cfgs):
    for cfg in cfgs:
        n_in, n_out = _norm_counts(cfg)
        _norm_apply(x, refs_in[:n_in], refs_out[:n_out], cfg)
        refs_in, refs_out = refs_in[n_in:], refs_out[n_out:]


def _norm_kernel(x_ref, *refs, cfg):
    n_in, _ = _norm_counts(cfg)
    _norm_apply(x_ref[...], refs[:n_in], refs[n_in:], cfg)


def _norm(x2, ns, B, S, ts=512):
    T, D = x2.shape
    nst = S // ts
    args, in_specs, out_shape, out_specs = _norm_io(
        ns, T, D, ts, lambda b, s: b, lambda b, s: b * nst + s)
    outs = pl.pallas_call(
        functools.partial(_norm_kernel, cfg=_norm_cfg(ns)),
        grid=(B, nst),
        in_specs=[pl.BlockSpec((ts, D), lambda b, s: (b * nst + s, 0))] + in_specs,
        out_specs=out_specs,
        out_shape=out_shape,
        compiler_params=_cparams(2),
        name="norm_mod",
    )(x2, *args)
    return outs if len(outs) > 1 else outs[0]


def _gmm_kernel(te_ref, tv_ref, nx_ref, a_ref, *refs, mode, out_scale, a_packed, n_w, layer,
                n_off, tn, n_n, norm_cfgs, k_split):
    n = pl.program_id(0)
    m = pl.program_id(1)
    if k_split is not None:
        a2_ref, *refs = refs
    w_hbm, refs = refs[:n_w], refs[n_w:]
    if mode == "residual":
        x_ref, gate_ref, *refs = refs
    n_norm_in = sum(_norm_counts(c)[0] for c in norm_cfgs)
    n_norm_out = sum(_norm_counts(c)[1] for c in norm_cfgs)
    norm_in, refs = refs[:n_norm_in], refs[n_norm_in:]
    o_ref, *refs = refs
    norm_out, refs = refs[:n_norm_out], refs[n_norm_out:]
    wst, wb, sem = refs[:n_w], refs[n_w:2 * n_w], refs[2 * n_w]
    first = jnp.logical_or(m == 0, te_ref[m] != te_ref[jnp.maximum(m - 1, 0)])
    valid = tv_ref[m] != 0

    def w_copy(i, e, nn):
        col = pl.multiple_of((nn + n_off) * tn, tn)
        return pltpu.make_async_copy(w_hbm[i].at[layer, e, :, pl.ds(col, tn)], wst[i], sem.at[i])

    @pl.when(jnp.logical_and(n == 0, m == 0))
    def _():
        for i in range(n_w):
            w_copy(i, te_ref[0], 0).start()

    @pl.when(first)
    def _():
        for i in range(n_w):
            w_copy(i, te_ref[m], n).wait()
            wb[i][...] = wst[i][...].astype(BF16)
        nm = nx_ref[m]
        same_sweep = nm >= 0
        e_next = jnp.where(same_sweep, te_ref[jnp.maximum(nm, 0)], te_ref[0])
        n_next = jnp.where(same_sweep, n, n + 1)

        @pl.when(jnp.logical_or(same_sweep, n + 1 < n_n))
        def _():
            for i in range(n_w):
                w_copy(i, e_next, n_next).start()

    @pl.when(valid)
    def _():
        a = _unpack_bf16_halves(a_ref[...]) if a_packed else a_ref[...]
        if mode == "swiglu":
            g = jnp.dot(a, wb[0][...], preferred_element_type=F32)
            u = jnp.dot(a, wb[1][...], preferred_element_type=F32)
            acc = g * jax.nn.sigmoid(g) * u
        elif k_split is None:
            acc = jnp.dot(a, wb[0][...], preferred_element_type=F32)
        else:
            acc = (jnp.dot(a, wb[0][:k_split], preferred_element_type=F32)
                   + jnp.dot(a2_ref[...], wb[0][k_split:], preferred_element_type=F32))
        if out_scale is not None:
            acc = acc * out_scale
        if mode == "residual":
            acc = x_ref[...] + gate_ref[...] * acc
        o_ref[...] = acc.astype(o_ref.dtype)
        _apply_norms(acc, norm_in, norm_out, norm_cfgs)

    @pl.when(jnp.logical_not(valid))
    def _():
        o_ref[...] = jnp.zeros_like(o_ref)


def _gmm(a, ws, l, *, mode, tm, tn, n_out, out_dtype, tile_expert=None, tile_valid=None,
         n_off=0, xres=None, gate=None, seq=None, out_scale=None, a_packed=False,
         alias_x=True, norms=None, a2=None):
    M, ka = a.shape
    K = ws[0].shape[2]
    mt = M // tm
    n_w = len(ws)
    n_n = n_out // tn
    if tile_expert is None:
        tile_expert = jnp.zeros((mt,), jnp.int32)
        tile_valid = jnp.ones((mt,), jnp.int32)
    idx = jnp.arange(mt, dtype=jnp.int32)
    later = ((idx[None, :] > idx[:, None]) & (tile_valid[None, :] != 0)
             & (tile_expert[None, :] != tile_expert[:, None]))
    next_run = jnp.where(jnp.any(later, axis=1), jnp.argmax(later, axis=1), -1).astype(jnp.int32)
    lhs = [a] if a2 is None else [a, a2]
    args = lhs + list(ws)
    in_specs = [pl.BlockSpec((tm, t.shape[1]), lambda n, m, te, tv, nx: (m, 0)) for t in lhs]
    in_specs += [pl.BlockSpec(memory_space=pl.ANY)] * n_w
    aliases = {}
    if mode == "residual":
        garr, gl, gj = gate
        per_b = seq // tm
        if alias_x:
            aliases = {3 + len(args): 0}
        args += [xres, garr]
        in_specs += [
            pl.BlockSpec((tm, tn), lambda n, m, te, tv, nx: (m, n)),
            pl.BlockSpec((None, None, None, 1, tn),
                         lambda n, m, te, tv, nx: (gl, m // per_b, gj, 0, n)),
        ]
    out_shape = [jax.ShapeDtypeStruct((M, n_out), out_dtype)]
    out_specs = [pl.BlockSpec((tm, tn), lambda n, m, te, tv, nx: (m, n))]
    bare = norms is None
    norms = norms or []
    for ns in norms:
        assert tn == n_out
        n_args, n_in, n_shape, n_out_specs = _norm_io(
            ns, M, n_out, tm, lambda n, m, *_: m // (seq // tm), lambda n, m, *_: m)
        args += n_args
        in_specs += n_in
        out_shape += n_shape
        out_specs += n_out_specs
    outs = pl.pallas_call(
        functools.partial(_gmm_kernel, mode=mode, out_scale=out_scale, a_packed=a_packed,
                          n_w=n_w, layer=l, n_off=n_off, tn=tn, n_n=n_n,
                          norm_cfgs=tuple(_norm_cfg(ns) for ns in norms),
                          k_split=None if a2 is None else ka),
        grid_spec=pltpu.PrefetchScalarGridSpec(
            num_scalar_prefetch=3,
            grid=(n_n, mt),
            in_specs=in_specs,
            out_specs=out_specs,
            scratch_shapes=([pltpu.VMEM((K, tn), F32) for _ in ws]
                            + [pltpu.VMEM((K, tn), BF16) for _ in ws]
                            + [pltpu.SemaphoreType.DMA((n_w,))]),
        ),
        out_shape=out_shape,
        input_output_aliases=aliases,
        compiler_params=_cparams(2),
        name="gmm_" + mode,
    )(tile_expert, tile_valid, next_run, *args)
    return outs[0] if bare else outs


def _rope_pairs(v, cos, s1, s2):
    return v * cos + pltpu.roll(v, LANES - QK_ROPE // 2, 1) * s1 + pltpu.roll(v, QK_ROPE // 2, 1) * s2


def _mla_in_kernel(a_ref, w_ref, gq_ref, gkv_ref, cos_ref, s1_ref, s2_ref,
                   cq_ref, ckv_ref, kpe_ref, wb, *, q_lora, kv_lora):
    @pl.when(pl.program_id(0) == 0)
    def _():
        wb[...] = w_ref[...].astype(BF16)

    acc = jnp.dot(a_ref[...], wb[...], preferred_element_type=F32)
    cq_ref[...] = _rms(acc[:, :q_lora], gq_ref[...]).astype(BF16)
    ckv_ref[...] = _rms(acc[:, q_lora:q_lora + kv_lora], gkv_ref[...]).astype(BF16)
    v = acc[:, q_lora + kv_lora:]
    kpe_ref[...] = _rope_pairs(v, cos_ref[...], s1_ref[...], s2_ref[...]).astype(BF16)


def _mla_in(h, w_pad, g_q, g_kv, rope_tabs, S, tm=1024):
    T, D = h.shape
    q_lora, kv_lora = g_q.shape[0], g_kv.shape[0]
    n_all = w_pad.shape[1]
    nst = S // tm
    tab_spec = pl.BlockSpec((tm, LANES), lambda i: (i % nst, 0))
    return pl.pallas_call(
        functools.partial(_mla_in_kernel, q_lora=q_lora, kv_lora=kv_lora),
        grid=(T // tm,),
        in_specs=[
            pl.BlockSpec((tm, D), lambda i: (i, 0)),
            pl.BlockSpec((D, n_all), lambda i: (0, 0)),
            pl.BlockSpec((1, q_lora), lambda i: (0, 0)),
            pl.BlockSpec((1, kv_lora), lambda i: (0, 0)),
            tab_spec, tab_spec, tab_spec,
        ],
        out_specs=[
            pl.BlockSpec((tm, q_lora), lambda i: (i, 0)),
            pl.BlockSpec((tm, kv_lora), lambda i: (i, 0)),
            pl.BlockSpec((tm, LANES), lambda i: (i, 0)),
        ],
        out_shape=[
            jax.ShapeDtypeStruct((T, q_lora), BF16),
            jax.ShapeDtypeStruct((T, kv_lora), BF16),
            jax.ShapeDtypeStruct((T, LANES), BF16),
        ],
        scratch_shapes=[pltpu.VMEM((D, n_all), BF16)],
        compiler_params=_cparams(1),
        name="mla_in",
    )(h, w_pad, g_q.reshape(1, -1), g_kv.reshape(1, -1), *rope_tabs)


def _mla_attn_kernel(qn_ref, qp_ref, kv_ref, kpe_ref, cos_ref, s1_ref, s2_ref, o_ref,
                     m_sc, l_sc, acc_sc, *, tq, hps):
    qi = pl.program_id(2)
    head_w = QK_NOPE + V_HEAD
    lane = lax.broadcasted_iota(jnp.int32, (tq, LANES), 1)
    qs = []
    for pair in range(hps // 2):
        qp = _rope_pairs(qp_ref[:, pair * LANES:(pair + 1) * LANES].astype(F32),
                         cos_ref[...], s1_ref[...], s2_ref[...])
        for sub in range(2):
            hh = 2 * pair + sub
            keep = (lane < QK_ROPE) if sub == 0 else (lane >= QK_ROPE)
            qs.append(jnp.concatenate(
                [qn_ref[:, hh * QK_NOPE:(hh + 1) * QK_NOPE],
                 jnp.where(keep, qp, 0.0).astype(BF16)], axis=1))
    m_sc[...] = jnp.full_like(m_sc, NEG)
    l_sc[...] = jnp.zeros_like(l_sc)
    acc_sc[...] = jnp.zeros_like(acc_sc)

    def tile(j, diagonal):
        ks = pl.multiple_of(j * tq, tq)
        kpe = kpe_ref[pl.ds(ks, tq), :]
        for hh in range(hps):
            c0 = hh * head_w
            k = jnp.concatenate([kv_ref[pl.ds(ks, tq), c0:c0 + QK_NOPE], kpe], axis=1)
            s = _dot_nt(qs[hh], k)
            if diagonal:
                row = lax.broadcasted_iota(jnp.int32, s.shape, 0)
                col = lax.broadcasted_iota(jnp.int32, s.shape, 1)
                s = jnp.where(col <= row, s, NEG)
            m_old = m_sc[hh]
            m_new = jnp.maximum(m_old, jnp.max(s, axis=-1, keepdims=True))
            alpha = jnp.exp2(m_old - m_new)
            p = jnp.exp2(s - jnp.concatenate([m_new] * (tq // LANES), axis=1))
            l_sc[hh] = alpha * l_sc[hh] + jnp.sum(p, axis=-1, keepdims=True)
            vv = kv_ref[pl.ds(ks, tq), c0 + QK_NOPE:c0 + head_w]
            acc_sc[hh] = alpha * acc_sc[hh] + jnp.dot(p.astype(BF16), vv,
                                                      preferred_element_type=F32)
            m_sc[hh] = m_new

    def body(j, carry):
        tile(j, False)
        return carry

    lax.fori_loop(0, qi, body, 0)
    tile(qi, True)
    for hh in range(hps):
        o_ref[:, hh * V_HEAD:(hh + 1) * V_HEAD] = (acc_sc[hh] / l_sc[hh]).astype(o_ref.dtype)


def _mla_attn(q_all, kv_all, kpe, rope_tabs, B, S, tq=512, hps=8):
    T = q_all.shape[0]
    H = MLA_HEADS
    nqt = S // tq
    qn_w, qp_w, kv_w = hps * QK_NOPE, hps * QK_ROPE, hps * (QK_NOPE + V_HEAD)
    tab_spec = pl.BlockSpec((tq, LANES), lambda b, p, i: (i, 0))
    return pl.pallas_call(
        functools.partial(_mla_attn_kernel, tq=tq, hps=hps),
        grid=(B, H // hps, nqt),
        in_specs=[
            pl.BlockSpec((tq, qn_w), lambda b, p, i: (b * nqt + i, p)),
            pl.BlockSpec((tq, qp_w), lambda b, p, i: (b * nqt + i, (H * QK_NOPE) // qp_w + p)),
            pl.BlockSpec((S, kv_w), lambda b, p, i: (b, p)),
            pl.BlockSpec((S, LANES), lambda b, p, i: (b, 0)),
            tab_spec, tab_spec, tab_spec,
        ],
        out_specs=pl.BlockSpec((tq, hps * V_HEAD), lambda b, p, i: (b * nqt + i, p)),
        out_shape=jax.ShapeDtypeStruct((T, H * V_HEAD), BF16),
        scratch_shapes=[pltpu.VMEM((hps, tq, LANES), F32), pltpu.VMEM((hps, tq, LANES), F32),
                        pltpu.VMEM((hps, tq, V_HEAD), F32)],
        compiler_params=_cparams(3),
        name="mla_attn",
    )(q_all, q_all, kv_all, kpe, *rope_tabs)


def _nsa_compress_kernel(t_ref, pe_ref, w1_ref, w2_ref, o_ref, *, n_chunk):
    half = CMP_BLOCK // 2
    pe = pe_ref[...]

    def chunk_rows(off):
        cols = []
        for l in range(half):
            x = t_ref[pl.ds(l, n_chunk, stride=CMP_STRIDE), :] + pe[off + l:off + l + 1, :]
            cols.append(x.astype(BF16))
        return jnp.concatenate(cols, axis=1)

    w1 = w1_ref[...].astype(BF16)
    kdim = half * NSA_DK
    p0 = jnp.dot(chunk_rows(0), w1[:kdim], preferred_element_type=F32)
    p1 = jnp.dot(chunk_rows(half), w1[kdim:], preferred_element_type=F32)
    pre = p0 + pltpu.roll(p1, n_chunk - 1, 0)
    hid = (pre * jax.nn.sigmoid(pre)).astype(BF16)
    out = jnp.dot(hid, w2_ref[...].astype(BF16), preferred_element_type=F32)
    rown = lax.broadcasted_iota(jnp.int32, out.shape, 0)
    o_ref[...] = jnp.where(rown < n_chunk - 1, out, 0.0).astype(o_ref.dtype)


def _nsa_compress(kc, pe, w1, w2, B, S):
    G = NSA_GROUPS
    n_chunk = S // CMP_STRIDE
    return pl.pallas_call(
        functools.partial(_nsa_compress_kernel, n_chunk=n_chunk),
        grid=(B, 2, G),
        in_specs=[
            pl.BlockSpec((S, NSA_DK), lambda b, i, g: (b, i * G + g)),
            pl.BlockSpec((None, CMP_BLOCK, NSA_DK), lambda b, i, g: (i, 0, 0)),
            pl.BlockSpec((None, CMP_BLOCK * NSA_DK, NSA_DK), lambda b, i, g: (i, 0, 0)),
            pl.BlockSpec((None, NSA_DK, NSA_DK), lambda b, i, g: (i, 0, 0)),
        ],
        out_specs=pl.BlockSpec((None, None, None, n_chunk, NSA_DK), lambda b, i, g: (b, i, g, 0, 0)),
        out_shape=jax.ShapeDtypeStruct((B, 2, G, n_chunk, NSA_DK), BF16),
        compiler_params=_cparams(3),
        name="nsa_compress",
    )(kc, pe, w1, w2)


def _nsa_attn_kernel(q_ref, gt_ref, ks_ref, vs_ref, kw_ref, vw_ref, kc_ref, vc_ref, qtab_ref,
                     ktab_ref, ctab_ref, mt_ref, cbias_ref, wbias_ref, o_ref, m_sc, acc_sc,
                     *, tq, tk, n_sel):
    qi = pl.program_id(2)
    t0 = qi * tq
    HP = NSA_HPG
    R = HP * tq
    qb = q_ref[...]
    qs = jnp.concatenate([qb[:, h * NSA_DK:(h + 1) * NSA_DK] for h in range(HP)], axis=0)
    qtab = qtab_ref[...]
    qa = jnp.concatenate([qs, qtab], axis=1)
    rowi = lax.broadcasted_iota(jnp.int32, (R, 1), 0)
    tcol = (t0 + rowi % tq).astype(F32)

    kc = jnp.concatenate([kc_ref[...], ctab_ref[...]], axis=1)
    s = _dot_nt(qa, kc) + jnp.concatenate([cbias_ref[...]] * HP, axis=0)
    p = jnp.exp2(s - jnp.max(s, axis=-1, keepdims=True))
    l = jnp.sum(p, axis=-1, keepdims=True)
    p_cmp = jnp.where(tcol >= CMP_BLOCK - 1, p / l, 0.0)
    o_cmp = jnp.dot(p_cmp.astype(BF16), vc_ref[...], preferred_element_type=F32)

    ps = p_cmp[0:tq]
    for h in range(1, HP):
        ps = ps + p_cmp[h * tq:(h + 1) * tq]
    ps_hi = ps.astype(BF16)
    ps_lo = (ps - ps_hi.astype(F32)).astype(BF16)
    imp = _dot_nt(mt_ref[...], ps_hi) + _dot_nt(mt_ref[...], ps_lo)
    jrow = lax.broadcasted_iota(jnp.int32, (n_sel, tq), 0)
    blk_t = (t0 + lax.broadcasted_iota(jnp.int32, (n_sel, tq), 1)) // SEL_BLOCK
    forced = (jrow == 0) | (jrow == blk_t) | (jrow == blk_t - 1)
    imp = jnp.where(forced, FORCED_SCORE, imp)
    imp = jnp.where(jrow > blk_t, -1.0, imp)
    rank = jnp.zeros((n_sel, tq), F32)
    for k in range(n_sel):
        rk = imp[k:k + 1, :]
        beats = (rk > imp) | ((rk == imp) & (jrow > k))
        rank = rank + jnp.where(beats, 1.0, 0.0)
    sel_t = jnp.where(rank < SEL_TOPN, 1.0, 0.0)
    sel_t = jnp.concatenate([sel_t, jnp.zeros((LANES - n_sel, tq), F32)], axis=0)
    sel_neg = ((sel_t.T - 1.0) * MASK_BIG).astype(BF16)
    lane = lax.broadcasted_iota(jnp.int32, (R, LANES), 1)
    qx = jnp.where(lane < SEL_LANES, jnp.concatenate([sel_neg] * HP, axis=0), qtab)
    qsel = jnp.concatenate([qs, qx], axis=1)

    m_sc[...] = jnp.full_like(m_sc, NEG)
    acc_sc[...] = jnp.zeros_like(acc_sc)
    ones = jnp.ones((tk, LANES), BF16)

    def tile(j, causal):
        ks = pl.multiple_of(j * tk, tk)
        k = jnp.concatenate([ks_ref[pl.ds(ks, tk), :], ktab_ref[pl.ds(ks, tk), :]], axis=1)
        sc = _dot_nt(qsel, k)
        if causal:
            kpos = (ks + lax.broadcasted_iota(jnp.int32, (1, tk), 1)).astype(F32)
            sc = jnp.where(kpos <= tcol, sc, NEG)
        m_old = m_sc[...]
        m_new = jnp.maximum(m_old, jnp.max(sc, axis=-1, keepdims=True))
        alpha = jnp.exp2(m_old - m_new)
        pp = jnp.exp2(sc - jnp.concatenate([m_new] * (tk // LANES), axis=1))
        vv = jnp.concatenate([vs_ref[pl.ds(ks, tk), :], ones], axis=1)
        acc_sc[...] = (jnp.concatenate([alpha, alpha], axis=1) * acc_sc[...]
                       + jnp.dot(pp.astype(BF16), vv, preferred_element_type=F32))
        m_sc[...] = m_new

    def body(j, carry):
        tile(j, False)
        return carry

    n_full = t0 // tk
    lax.fori_loop(0, n_full, body, 0)
    for c in range(tq // tk):
        tile(n_full + c, True)
    acc = acc_sc[...]
    o_sel = acc[:, :NSA_DK] / acc[:, NSA_DK:]

    span = WINDOW + tq
    ws = pl.multiple_of(jnp.maximum(t0 - WINDOW, 0), tq)
    wbias = wbias_ref[(t0 - ws) // tq]
    kw = jnp.concatenate([kw_ref[pl.ds(ws, span), :], ktab_ref[pl.ds(ws, span), :]], axis=1)
    sw = _dot_nt(qa, kw) + jnp.concatenate([wbias] * HP, axis=0)
    pw = jnp.exp2(sw - jnp.max(sw, axis=-1, keepdims=True))
    vw = jnp.concatenate([vw_ref[pl.ds(ws, span), :], jnp.ones((span, LANES), BF16)], axis=1)
    rw = jnp.dot(pw.astype(BF16), vw, preferred_element_type=F32)
    o_win = rw[:, :NSA_DK] / rw[:, NSA_DK:]

    gt = jax.nn.sigmoid(gt_ref[...])

    def gcol(i):
        return jnp.concatenate([gt[:, i * HP + h:i * HP + h + 1] for h in range(HP)], axis=0)

    o = gcol(0) * o_cmp + gcol(1) * o_sel + gcol(2) * o_win
    o_ref[...] = jnp.concatenate([o[h * tq:(h + 1) * tq] for h in range(HP)], axis=1).astype(o_ref.dtype)


def _nsa_attn(q, gates, kvb, kvc, tabs, B, S, tq=128, tk=256):
    T = q.shape[0]
    G, DK, HP = NSA_GROUPS, NSA_DK, NSA_HPG
    qtab, ktab, ctab, mt, cbias, wbias = tabs
    nqt = S // tq
    n_chunk = kvc.shape[3]
    n_sel = S // SEL_BLOCK

    def kv_spec(i):
        return pl.BlockSpec((S, DK), lambda b, g, t, i=i: (b, i * G + g))

    def cmp_spec(i):
        return pl.BlockSpec((None, None, None, n_chunk, DK), lambda b, g, t, i=i: (b, i, g, 0, 0))

    return pl.pallas_call(
        functools.partial(_nsa_attn_kernel, tq=tq, tk=tk, n_sel=n_sel),
        grid=(B, G, nqt),
        in_specs=[
            pl.BlockSpec((tq, HP * DK), lambda b, g, t: (b * nqt + t, g)),
            pl.BlockSpec((tq, LANES), lambda b, g, t: (b * nqt + t, g)),
            kv_spec(0), kv_spec(1), kv_spec(2), kv_spec(3),
            cmp_spec(0), cmp_spec(1),
            pl.BlockSpec((None, HP * tq, LANES), lambda b, g, t: (g, 0, 0)),
            pl.BlockSpec(ktab.shape, lambda b, g, t: (0, 0)),
            pl.BlockSpec(ctab.shape, lambda b, g, t: (0, 0)),
            pl.BlockSpec(mt.shape, lambda b, g, t: (0, 0)),
            pl.BlockSpec((tq, n_chunk), lambda b, g, t: (t, 0)),
            pl.BlockSpec(wbias.shape, lambda b, g, t: (0, 0, 0)),
        ],
        out_specs=pl.BlockSpec((tq, HP * DK), lambda b, g, t: (b * nqt + t, g)),
        out_shape=jax.ShapeDtypeStruct((T, NSA_HEADS * DK), BF16),
        scratch_shapes=[pltpu.VMEM((HP * tq, LANES), F32),
                        pltpu.VMEM((HP * tq, DK + LANES), F32)],
        compiler_params=_cparams(3),
        name="nsa_attn",
    )(q, gates, kvb, kvb, kvb, kvb, kvc, kvc, qtab, ktab, ctab, mt, cbias, wbias)


def _row_copy(src_hbm, dst_vmem, sem, src_row, dst_row):
    return pltpu.make_async_copy(src_hbm.at[pl.ds(src_row, 1), :], dst_vmem.at[pl.ds(dst_row, 1), :], sem)


def _dispatch_kernel(pos_ref, src_ref, init_hbm, out_hbm, sem, *, chunk, n_tok):
    del init_hbm
    base = pl.program_id(0) * chunk

    def start(r, c):
        t = base + r
        _row_copy(src_ref, out_hbm, sem, r, pos_ref[t]).start()
        _row_copy(src_ref, out_hbm, sem, r, pos_ref[n_tok + t]).start()
        return c

    def wait(r, c):
        _row_copy(src_ref, out_hbm, sem, 0, 0).wait()
        return c

    lax.fori_loop(0, chunk, start, 0, unroll=8)
    lax.fori_loop(0, 2 * chunk, wait, 0, unroll=8)


def _dispatch(h_packed, pos, n_rows, chunk=512):
    T, W = h_packed.shape
    n_steps = T // chunk
    init = jnp.zeros((n_rows, W), h_packed.dtype)
    return pl.pallas_call(
        functools.partial(_dispatch_kernel, chunk=chunk, n_tok=T),
        grid_spec=pltpu.PrefetchScalarGridSpec(
            num_scalar_prefetch=1,
            grid=(n_steps,),
            in_specs=[pl.BlockSpec((chunk, W), lambda i, pos: (i, 0)),
                      pl.BlockSpec(memory_space=pl.ANY)],
            out_specs=pl.BlockSpec(memory_space=pl.ANY),
            scratch_shapes=[pltpu.SemaphoreType.DMA(())],
        ),
        out_shape=jax.ShapeDtypeStruct((n_rows, W), h_packed.dtype),
        input_output_aliases={2: 0},
        compiler_params=_cparams(1),
        name="moe_dispatch",
    )(pos, h_packed, init)


def _combine_kernel(pos_ref, y_hbm, x_ref, gate_ref, rt_ref, *refs, tg, n_tok, n_steps,
                    norm_cfgs):
    n_norm_in = sum(_norm_counts(c)[0] for c in norm_cfgs)
    n_norm_out = sum(_norm_counts(c)[1] for c in norm_cfgs)
    norm_in, (o_ref, *refs) = refs[:n_norm_in], refs[n_norm_in:]
    norm_out, (buf, sem) = refs[:n_norm_out], refs[n_norm_out:]
    i = pl.program_id(0)
    slot = i % 2

    def issue(step, slot_):
        def start(r, c):
            t = step * tg + r
            _row_copy(y_hbm, buf.at[slot_, 0], sem.at[slot_], pos_ref[t], r).start()
            _row_copy(y_hbm, buf.at[slot_, 1], sem.at[slot_], pos_ref[n_tok + t], r).start()
            return c

        lax.fori_loop(0, tg, start, 0, unroll=8)

    @pl.when(i == 0)
    def _():
        issue(0, 0)

    @pl.when(i + 1 < n_steps)
    def _():
        issue(i + 1, 1 - slot)

    def wait(r, c):
        _row_copy(y_hbm, buf.at[slot, 0], sem.at[slot], 0, 0).wait()
        return c

    lax.fori_loop(0, 2 * tg, wait, 0, unroll=8)
    rt = rt_ref[...]
    y = rt[:, 2:3] * buf[slot, 0] + rt[:, 3:4] * buf[slot, 1]
    x_new = x_ref[...] + gate_ref[...] * y
    o_ref[...] = x_new
    _apply_norms(x_new, norm_in, norm_out, norm_cfgs)


def _combine(x2, y_sorted, pos, route, gate, S, norms=(), tg=256):
    T, D = x2.shape
    garr, gl, gj = gate
    per_b = S // tg
    n_steps = T // tg
    args = [y_sorted, x2, garr, route]
    in_specs = [
        pl.BlockSpec(memory_space=pl.ANY),
        pl.BlockSpec((tg, D), lambda i, pos: (i, 0)),
        pl.BlockSpec((None, None, None, 1, D), lambda i, pos: (gl, i // per_b, gj, 0, 0)),
        pl.BlockSpec((tg, LANES), lambda i, pos: (i, 0)),
    ]
    out_shape = [jax.ShapeDtypeStruct((T, D), F32)]
    out_specs = [pl.BlockSpec((tg, D), lambda i, pos: (i, 0))]
    for ns in norms:
        n_args, n_in, n_shape, n_out_specs = _norm_io(
            ns, T, D, tg, lambda i, *_: i // per_b, lambda i, *_: i)
        args += n_args
        in_specs += n_in
        out_shape += n_shape
        out_specs += n_out_specs
    outs = pl.pallas_call(
        functools.partial(_combine_kernel, tg=tg, n_tok=T, n_steps=n_steps,
                          norm_cfgs=tuple(_norm_cfg(ns) for ns in norms)),
        grid_spec=pltpu.PrefetchScalarGridSpec(
            num_scalar_prefetch=1,
            grid=(n_steps,),
            in_specs=in_specs,
            out_specs=out_specs,
            scratch_shapes=[pltpu.VMEM((2, 2, tg, D), F32), pltpu.SemaphoreType.DMA((2,))],
        ),
        out_shape=out_shape,
        input_output_aliases={2: 0},
        compiler_params=_cparams(1),
        name="moe_combine",
    )(pos, *args)
    return outs if norms else outs[0]


def _moe_plan(route, T, tm):
    E = N_EXPERTS
    n_tiles = (2 * T) // tm + E
    P = n_tiles * tm
    e_pair = jnp.concatenate([route[:, 0], route[:, 1]]).astype(jnp.int32)
    onehot = (e_pair[:, None] == jnp.arange(E, dtype=jnp.int32)[None, :]).astype(jnp.int32)
    csum = jnp.cumsum(onehot, axis=0)
    rank = jnp.sum(onehot * (csum - 1), axis=1)
    counts = csum[-1]
    padded = ((counts + tm - 1) // tm) * tm
    ends = jnp.cumsum(padded)
    pos = (ends - padded)[e_pair] + rank
    tile_start = jnp.arange(n_tiles, dtype=jnp.int32) * tm
    tile_valid = (tile_start < ends[-1]).astype(jnp.int32)
    tile_expert = jnp.sum((tile_start[:, None] >= ends[None, :]).astype(jnp.int32), axis=1)
    last_valid = tile_expert[jnp.maximum(ends[-1] // tm - 1, 0)]
    tile_expert = jnp.where(tile_valid == 1, tile_expert, last_valid).astype(jnp.int32)
    return pos.astype(jnp.int32), P, tile_expert, tile_valid


def _rope_tables(S):
    d = QK_ROPE
    inv = ROPE_THETA ** (-jnp.arange(0, d, 2, dtype=F32) / d)
    ang = jnp.arange(S).astype(F32)[:, None] * inv[None, :]
    cos, sin = jnp.cos(ang), jnp.sin(ang)
    z = jnp.zeros_like(sin)
    return (jnp.concatenate([cos, cos, cos, cos], axis=1),
            jnp.concatenate([-sin, z, -sin, z], axis=1),
            jnp.concatenate([z, sin, z, sin], axis=1))


def _pos_columns(pos):
    tab = np.zeros((pos.shape[0], LANES), np.float32)
    tab[:, POS_HI_LANE:POS_HI_LANE + 3] = (LANES * (pos // LANES))[:, None]
    tab[:, POS_LO_LANE:POS_LO_LANE + 3] = (pos % LANES)[:, None]
    return tab


def _nsa_tables(S, tq):
    n_cmp = (S - CMP_BLOCK) // CMP_STRIDE + 1
    n_sel = S // SEL_BLOCK
    n_chunk = S // CMP_STRIDE
    tok = np.arange(n_cmp)[:, None] * CMP_STRIDE + np.arange(CMP_BLOCK)[None, :]
    blk = tok // SEL_BLOCK
    m = (blk[:, :, None] == np.arange(n_sel)[None, None, :]).sum(axis=1) / CMP_BLOCK
    mt = np.zeros((n_sel, n_chunk), np.float32)
    mt[:, :n_cmp] = m.T
    keys = np.arange(S)
    ktab = _pos_columns(keys)
    ktab[:, :SEL_LANES] = (keys[:, None] // SEL_BLOCK == np.arange(SEL_LANES)[None, :])
    ctab = _pos_columns(np.arange(n_chunk) * CMP_STRIDE + (CMP_BLOCK - 1))
    slopes = jnp.asarray(2.0 ** (-8.0 * np.arange(1, NSA_HEADS + 1) / NSA_HEADS), F32)
    a = slopes * LOG2E
    a_hi = a.astype(BF16)
    r1 = a - a_hi.astype(F32)
    a_mid = r1.astype(BF16)
    a_lo = (r1 - a_mid.astype(F32)).astype(BF16)
    pieces = jnp.stack([a_hi, a_mid, a_lo], axis=-1)
    qrow = jnp.zeros((NSA_HEADS, LANES), BF16)
    qrow = qrow.at[:, POS_HI_LANE:POS_HI_LANE + 3].set(pieces)
    qrow = qrow.at[:, POS_LO_LANE:POS_LO_LANE + 3].set(pieces)
    qtab = jnp.repeat(qrow.reshape(NSA_GROUPS, NSA_HPG, LANES), tq, axis=1)
    cend = np.arange(n_chunk) * CMP_STRIDE + (CMP_BLOCK - 1)
    cvis = (keys[:, None] >= cend[None, :]) & (np.arange(n_chunk)[None, :] < n_cmp)
    cbias = np.where(cvis, 0.0, NEG).astype(np.float32)
    span = WINDOW + tq
    d = (np.arange(WINDOW // tq + 1)[:, None, None] * tq + np.arange(tq)[None, :, None]
         - np.arange(span)[None, None, :])
    wbias = np.where((d >= 0) & (d < WINDOW), 0.0, NEG).astype(np.float32)
    return (qtab, jnp.asarray(ktab, BF16), jnp.asarray(ctab, BF16), jnp.asarray(mt, BF16),
            jnp.asarray(cbias), jnp.asarray(wbias))


def kernel(x, c, ada_w, ada_b, norm1_g, norm2_g, mla_w_in, mla_g_q, mla_g_kv, mla_w_uq, mla_w_ukv, mla_w_o, kv_ada_w, kv_ada_b, kv_norm_g, nsa_w_kv, cmp_pos_k, cmp_pos_v, cmp_k_w1, cmp_k_w2, cmp_v_w1, cmp_v_w2, nsa_w_in, nsa_w_o, ffn_w_gate, ffn_w_up, ffn_w_down, moe_w_router, moe_b_router, moe_w_gate, moe_w_up, moe_w_down, final_g):
    B, S, D = x.shape
    T = B * S
    depth = ada_w.shape[0]
    n_a = mla_w_in.shape[0]
    H = MLA_HEADS
    G, HP, DK = NSA_GROUPS, NSA_HPG, NSA_DK
    d_ff = ffn_w_gate.shape[-1]

    c_pad = jnp.zeros((8, D), F32).at[:B].set(c)
    mod = _modulation(c_pad, ada_w, ada_b)[:, :B].reshape(depth, B, 6, 1, D)
    kv_mod = _modulation(c_pad, kv_ada_w[None], kv_ada_b[None])[:, :B].reshape(1, B, 2, 1, D)

    rope_tabs = _rope_tables(S)
    nsa_tabs = _nsa_tables(S, NSA_TQ)

    ffn_wg = ffn_w_gate[:, None]
    ffn_wu = ffn_w_up[:, None]
    ffn_wd = ffn_w_down[:, None]

    def norm1_spec(l):
        return _norm_spec(norm1_g[l], (mod, l, 0), (mod, l, 1))

    def swiglu(a, ws, li, tm, **kw):
        wide = (d_ff // WIDE_TN) * WIDE_TN
        tail = d_ff - wide
        main = _gmm(a, ws, li, mode="swiglu", tm=tm, tn=WIDE_TN, n_out=wide, out_dtype=BF16, **kw)
        rest = _gmm(a, ws, li, mode="swiglu", tm=tm, tn=tail, n_out=tail, out_dtype=BF16,
                    n_off=wide // tail, **kw)
        return main, rest

    kv_spec = _norm_spec(kv_norm_g, (kv_mod, 0, 0), (kv_mod, 0, 1))
    final_spec = _norm_spec(final_g, out_dtype=F32)

    x2 = x.reshape(T, D)
    shared = None
    h = _norm(x2, norm1_spec(0), B, S)
    for l in range(depth):
        dense = l % 2 == 0
        if dense:
            norm2 = _norm_spec(norm2_g[l], (mod, l, 3), (mod, l, 4))
        else:
            wr = jnp.pad(moe_w_router[l // 2], ((0, 0), (0, LANES - N_EXPERTS)))
            br = jnp.pad(moe_b_router[l // 2], (0, LANES - N_EXPERTS)).reshape(1, LANES)
            norm2 = _norm_spec(norm2_g[l], (mod, l, 3), (mod, l, 4), router=(wr, br))
        fused2 = [norm2] if dense else []
        if l < n_a:
            w_in = mla_w_in[l]
            q_lora, kv_lora = mla_g_q.shape[1], mla_g_kv.shape[1]
            w_pad = jnp.concatenate([w_in, w_in[:, q_lora + kv_lora:]], axis=1)
            cq, ckv, kpe = _mla_in(h, w_pad, mla_g_q[l], mla_g_kv[l], rope_tabs, S)
            wq = mla_w_uq[l].reshape(q_lora, H, QK_NOPE + QK_ROPE)
            wq = jnp.concatenate([wq[:, :, :QK_NOPE].reshape(q_lora, H * QK_NOPE),
                                  wq[:, :, QK_NOPE:].reshape(q_lora, H * QK_ROPE)], axis=1)
            q_all = _gmm(cq, [wq[None, None]], 0, mode="cast", tm=LORA_TM, tn=WIDE_TN,
                         n_out=wq.shape[1], out_dtype=BF16,
                         out_scale=(QK_NOPE + QK_ROPE) ** -0.5 * LOG2E)
            kv_all = _gmm(ckv, [mla_w_ukv[:, None]], l, mode="cast", tm=LORA_TM, tn=WIDE_TN,
                          n_out=mla_w_ukv.shape[-1], out_dtype=BF16)
            o = _mla_attn(q_all, kv_all, kpe, rope_tabs, B, S)
            x2, *h2 = _gmm(o, [mla_w_o[:, None]], l, mode="residual", tm=OUT_PROJ_TM, tn=D, n_out=D,
                           out_dtype=F32, xres=x2, gate=(mod, l, 2), seq=S, norms=fused2,
                           alias_x=l > 0)
        else:
            jb = l - n_a
            w_in = nsa_w_in[jb]
            q = _gmm(h, [nsa_w_in[:, None]], jb, mode="cast", tm=DENSE_TM, tn=WIDE_TN,
                     n_out=NSA_HEADS * DK, out_dtype=BF16, out_scale=DK ** -0.5 * LOG2E)
            wg = w_in[:, NSA_HEADS * DK:].reshape(D, G, HP, 3).transpose(0, 1, 3, 2)
            wg = jnp.pad(wg.reshape(D, G, 3 * HP), ((0, 0), (0, 0), (0, LANES - 3 * HP)))
            gates = _gmm(h, [wg.reshape(1, 1, D, G * LANES)], 0, mode="cast", tm=DENSE_TM,
                         tn=NARROW_TN,
                         n_out=G * LANES, out_dtype=F32)
            kvb, kvc = shared
            o = _nsa_attn(q, gates, kvb, kvc, nsa_tabs, B, S, tq=NSA_TQ)
            x2, *h2 = _gmm(o, [nsa_w_o[:, None]], jb, mode="residual", tm=OUT_PROJ_TM, tn=D, n_out=D,
                           out_dtype=F32, xres=x2, gate=(mod, l, 2), seq=S, norms=fused2)

        after = [norm1_spec(l + 1) if l + 1 < depth else final_spec]
        if l == n_a - 1:
            after.append(kv_spec)
        if dense:
            hid, hid_tail = swiglu(h2[0], [ffn_wg, ffn_wu], l // 2, DENSE_TM)
            x2 = _gmm(hid, [ffn_wd], l // 2, mode="residual", tm=DENSE_TM, tn=NARROW_TN, n_out=D,
                      out_dtype=F32, xres=x2, gate=(mod, l, 5), seq=S, a2=hid_tail)
            normed = [_norm(x2, ns, B, S) for ns in after]
        else:
            li = l // 2
            hp, route = _norm(x2, norm2, B, S)
            pos, n_rows, tile_expert, tile_valid = _moe_plan(route, T, MOE_TM)
            hs = _dispatch(hp, pos, n_rows)
            hid, hid_tail = swiglu(hs, [moe_w_gate, moe_w_up], li, MOE_TM, tile_expert=tile_expert,
                                   tile_valid=tile_valid, a_packed=True)
            ys = _gmm(hid, [moe_w_down], li, mode="cast", tm=MOE_TM, tn=WIDE_TN, n_out=D,
                      out_dtype=F32, tile_expert=tile_expert, tile_valid=tile_valid, a2=hid_tail)
            x2, *normed = _combine(x2, ys, pos, route, (mod, l, 5), S, norms=after)
        h = normed[0]

        if l == n_a - 1:
            hkv = normed[1]
            w_kv = nsa_w_kv[None, None]
            kc = _gmm(hkv, [w_kv], 0, mode="cast", tm=DENSE_TM, tn=NARROW_TN, n_out=2 * G * DK,
                      out_dtype=F32)
            kvb = _gmm(hkv, [w_kv], 0, mode="cast", tm=DENSE_TM, tn=WIDE_TN, n_out=4 * G * DK,
                       out_dtype=BF16, n_off=(2 * G * DK) // WIDE_TN)
            kvc = _nsa_compress(kc, jnp.stack([cmp_pos_k, cmp_pos_v]),
                                jnp.stack([cmp_k_w1, cmp_v_w1]), jnp.stack([cmp_k_w2, cmp_v_w2]),
                                B, S)
            shared = (kvb, kvc)

    return h.reshape(B, S, D)
```

```python
import functools

import numpy as np
import jax
import jax.numpy as jnp
from jax import lax
from jax.experimental import pallas as pl
from jax.experimental.pallas import tpu as pltpu

F32 = jnp.float32
BF16 = jnp.bfloat16

EPS = 1e-6
NEG = -1e30
FORCED_SCORE = 1e6
LOG2E = 1.4426950408889634

MLA_HEADS = 16
QK_NOPE = 128
QK_ROPE = 64
V_HEAD = 128
ROPE_THETA = 10000.0

NSA_HEADS = 16
NSA_GROUPS = 4
NSA_HPG = NSA_HEADS // NSA_GROUPS
NSA_DK = 128
CMP_BLOCK = 32
CMP_STRIDE = 16
SEL_BLOCK = 64
SEL_TOPN = 16
WINDOW = 512

N_EXPERTS = 8
LANES = 128
VMEM_LIMIT = 56 * 1024 * 1024

MOE_TM = 512
DENSE_TM = 1024
NARROW_TN = 512
WIDE_TN = 1024
LORA_TM = 2048
OUT_PROJ_TM = 512
NSA_TQ = 512
WIN_ROWS = 128

SEL_LANES = 32
POS_HI_LANE = 32
POS_LO_LANE = 35
MASK_BIG = 2.0 ** 30


def _cparams(n_axes):
    return pltpu.CompilerParams(
        dimension_semantics=("arbitrary",) * n_axes, vmem_limit_bytes=VMEM_LIMIT)


def _dot_nt(a, b):
    return lax.dot_general(a, b, (((1,), (1,)), ((), ())), preferred_element_type=F32)


def _mod_kernel(c_ref, w_ref, b_ref, o_ref):
    c = c_ref[...]
    ca = (c * jax.nn.sigmoid(c)).astype(BF16)
    o_ref[...] = jnp.dot(ca, w_ref[...].astype(BF16), preferred_element_type=F32) + b_ref[...]


def _modulation(c_pad, w, b, tn=2048):
    L, D, N = w.shape
    return pl.pallas_call(
        _mod_kernel,
        grid=(L, N // tn),
        in_specs=[
            pl.BlockSpec((8, D), lambda l, n: (0, 0)),
            pl.BlockSpec((None, D, tn), lambda l, n: (l, 0, n)),
            pl.BlockSpec((None, 1, tn), lambda l, n: (l, 0, n)),
        ],
        out_specs=pl.BlockSpec((None, 8, tn), lambda l, n: (l, 0, n)),
        out_shape=jax.ShapeDtypeStruct((L, 8, N), F32),
        compiler_params=_cparams(2),
        name="adaln_mod",
    )(c_pad, w, b.reshape(L, 1, N))


def _rms(x, g):
    return x * lax.rsqrt(jnp.mean(x * x, axis=-1, keepdims=True) + EPS) * g


_HI16 = 0xFFFF0000


def _pack_bf16_halves(h):
    half = h.shape[1] // 2
    bits = lax.bitcast_convert_type(h.astype(BF16).astype(F32), jnp.uint32)
    return (bits[:, :half] >> 16) | (bits[:, half:] & jnp.uint32(_HI16))


def _unpack_bf16_halves(a):
    lo = lax.bitcast_convert_type(a << 16, F32).astype(BF16)
    hi = lax.bitcast_convert_type(a & jnp.uint32(_HI16), F32).astype(BF16)
    return jnp.concatenate([lo, hi], axis=1)


def _norm_spec(g, shift=None, scale=None, router=None, out_dtype=BF16):
    return dict(g=g, shift=shift, scale=scale, router=router, out_dtype=out_dtype)


def _norm_cfg(ns):
    return (ns["shift"] is not None, ns["router"] is not None)


def _norm_io(ns, T, D, tm, batch_of, row_of):
    modulate, route = _norm_cfg(ns)
    args = [ns["g"].reshape(1, D)]
    in_specs = [pl.BlockSpec((1, D), lambda *a: (0, 0))]
    if modulate:
        for arr, l, j in (ns["shift"], ns["scale"]):
            args.append(arr)
            in_specs.append(pl.BlockSpec((None, None, None, 1, D),
                                         lambda *a, l=l, j=j: (l, batch_of(*a), j, 0, 0)))
    if route:
        wr, br = ns["router"]
        args += [wr, br]
        in_specs += [pl.BlockSpec((D, LANES), lambda *a: (0, 0)),
                     pl.BlockSpec((1, LANES), lambda *a: (0, 0))]
        out_shape = [jax.ShapeDtypeStruct((T, D // 2), jnp.uint32),
                     jax.ShapeDtypeStruct((T, LANES), F32)]
        out_specs = [pl.BlockSpec((tm, D // 2), lambda *a: (row_of(*a), 0)),
                     pl.BlockSpec((tm, LANES), lambda *a: (row_of(*a), 0))]
    else:
        out_shape = [jax.ShapeDtypeStruct((T, D), ns["out_dtype"])]
        out_specs = [pl.BlockSpec((tm, D), lambda *a: (row_of(*a), 0))]
    return args, in_specs, out_shape, out_specs


def _norm_counts(cfg):
    modulate, route = cfg
    return 1 + 2 * modulate + 2 * route, 1 + route


def _norm_apply(x, in_refs, out_refs, cfg):
    modulate, route = cfg
    g_ref, *rest = in_refs
    h = _rms(x, g_ref[...])
    if modulate:
        sh_ref, sc_ref, *rest = rest
        h = h * (1.0 + sc_ref[...]) + sh_ref[...]
    if not route:
        out_refs[0][...] = h.astype(out_refs[0].dtype)
        return
    wr_ref, br_ref = rest
    out_refs[0][...] = _pack_bf16_halves(h)
    w = wr_ref[...]
    h_hi, w_hi = h.astype(BF16), w.astype(BF16)
    h_lo = (h - h_hi.astype(F32)).astype(BF16)
    w_lo = (w - w_hi.astype(F32)).astype(BF16)
    logits = (jnp.dot(h_hi, w_hi, preferred_element_type=F32)
              + jnp.dot(h_lo, w_hi, preferred_element_type=F32)
              + jnp.dot(h_hi, w_lo, preferred_element_type=F32)) + br_ref[...]
    lane = lax.broadcasted_iota(jnp.int32, logits.shape, 1).astype(F32)
    logits = jnp.where(lane < N_EXPERTS, logits, -jnp.inf)
    v1 = jnp.max(logits, axis=-1, keepdims=True)
    i1 = jnp.min(jnp.where(logits == v1, lane, float(LANES)), axis=-1, keepdims=True)
    others = jnp.where(lane == i1, -jnp.inf, logits)
    v2 = jnp.max(others, axis=-1, keepdims=True)
    i2 = jnp.min(jnp.where(others == v2, lane, float(LANES)), axis=-1, keepdims=True)
    e = jnp.exp(v2 - v1)
    w1 = 1.0 / (1.0 + e)
    w2 = e / (1.0 + e)
    out_refs[1][...] = jnp.where(lane == 0, i1, jnp.where(lane == 1, i2, jnp.where(
        lane == 2, w1, jnp.where(lane == 3, w2, 0.0))))


def _apply_norms(x, refs_in, refs_out, cfgs):
    for cfg in cfgs:
        n_in, n_out = _norm_counts(cfg)
        _norm_apply(x, refs_in[:n_in], refs_out[:n_out], cfg)
        refs_in, refs_out = refs_in[n_in:], refs_out[n_out:]


def _norm_kernel(x_ref, *refs, cfg):
    n_in, _ = _norm_counts(cfg)
    _norm_apply(x_ref[...], refs[:n_in], refs[n_in:], cfg)


def _norm(x2, ns, B, S, ts=512):
    T, D = x2.shape
    nst = S // ts
    args, in_specs, out_shape, out_specs = _norm_io(
        ns, T, D, ts, lambda b, s: b, lambda b, s: b * nst + s)
    outs = pl.pallas_call(
        functools.partial(_norm_kernel, cfg=_norm_cfg(ns)),
        grid=(B, nst),
        in_specs=[pl.BlockSpec((ts, D), lambda b, s: (b * nst + s, 0))] + in_specs,
        out_specs=out_specs,
        out_shape=out_shape,
        compiler_params=_cparams(2),
        name="norm_mod",
    )(x2, *args)
    return outs if len(outs) > 1 else outs[0]


def _gmm_kernel(te_ref, tv_ref, nx_ref, a_ref, *refs, mode, out_scale, a_packed, n_w, layer,
                n_off, tn, n_n, norm_cfgs, k_split):
    n = pl.program_id(0)
    m = pl.program_id(1)
    if k_split is not None:
        a2_ref, *refs = refs
    w_hbm, refs = refs[:n_w], refs[n_w:]
    if mode == "residual":
        x_ref, gate_ref, *refs = refs
    n_norm_in = sum(_norm_counts(c)[0] for c in norm_cfgs)
    n_norm_out = sum(_norm_counts(c)[1] for c in norm_cfgs)
    norm_in, refs = refs[:n_norm_in], refs[n_norm_in:]
    o_ref, *refs = refs
    norm_out, refs = refs[:n_norm_out], refs[n_norm_out:]
    wst, wb, sem = refs[:n_w], refs[n_w:2 * n_w], refs[2 * n_w]
    first = jnp.logical_or(m == 0, te_ref[m] != te_ref[jnp.maximum(m - 1, 0)])
    valid = tv_ref[m] != 0

    def w_copy(i, e, nn):
        col = pl.multiple_of((nn + n_off) * tn, tn)
        return pltpu.make_async_copy(w_hbm[i].at[layer, e, :, pl.ds(col, tn)], wst[i], sem.at[i])

    @pl.when(jnp.logical_and(n == 0, m == 0))
    def _():
        for i in range(n_w):
            w_copy(i, te_ref[0], 0).start()

    @pl.when(first)
    def _():
        for i in range(n_w):
            w_copy(i, te_ref[m], n).wait()
            wb[i][...] = wst[i][...].astype(BF16)
        nm = nx_ref[m]
        same_sweep = nm >= 0
        e_next = jnp.where(same_sweep, te_ref[jnp.maximum(nm, 0)], te_ref[0])
        n_next = jnp.where(same_sweep, n, n + 1)

        @pl.when(jnp.logical_or(same_sweep, n + 1 < n_n))
        def _():
            for i in range(n_w):
                w_copy(i, e_next, n_next).start()

    @pl.when(valid)
    def _():
        a = _unpack_bf16_halves(a_ref[...]) if a_packed else a_ref[...]
        if mode == "swiglu":
            g = jnp.dot(a, wb[0][...], preferred_element_type=F32)
            u = jnp.dot(a, wb[1][...], preferred_element_type=F32)
            acc = g * jax.nn.sigmoid(g) * u
        elif k_split is None:
            acc = jnp.dot(a, wb[0][...], preferred_element_type=F32)
        else:
            acc = (jnp.dot(a, wb[0][:k_split], preferred_element_type=F32)
                   + jnp.dot(a2_ref[...], wb[0][k_split:], preferred_element_type=F32))
        if out_scale is not None:
            acc = acc * out_scale
        if mode == "residual":
            acc = x_ref[...] + gate_ref[...] * acc
        o_ref[...] = acc.astype(o_ref.dtype)
        _apply_norms(acc, norm_in, norm_out, norm_cfgs)

    @pl.when(jnp.logical_not(valid))
    def _():
        o_ref[...] = jnp.zeros_like(o_ref)


def _gmm(a, ws, l, *, mode, tm, tn, n_out, out_dtype, tile_expert=None, tile_valid=None,
         n_off=0, xres=None, gate=None, seq=None, out_scale=None, a_packed=False,
         alias_x=True, norms=None, a2=None):
    M, ka = a.shape
    K = ws[0].shape[2]
    mt = M // tm
    n_w = len(ws)
    n_n = n_out // tn
    if tile_expert is None:
        tile_expert = jnp.zeros((mt,), jnp.int32)
        tile_valid = jnp.ones((mt,), jnp.int32)
    idx = jnp.arange(mt, dtype=jnp.int32)
    later = ((idx[None, :] > idx[:, None]) & (tile_valid[None, :] != 0)
             & (tile_expert[None, :] != tile_expert[:, None]))
    next_run = jnp.where(jnp.any(later, axis=1), jnp.argmax(later, axis=1), -1).astype(jnp.int32)
    lhs = [a] if a2 is None else [a, a2]
    args = lhs + list(ws)
    in_specs = [pl.BlockSpec((tm, t.shape[1]), lambda n, m, te, tv, nx: (m, 0)) for t in lhs]
    in_specs += [pl.BlockSpec(memory_space=pl.ANY)] * n_w
    aliases = {}
    if mode == "residual":
        garr, gl, gj = gate
        per_b = seq // tm
        if alias_x:
            aliases = {3 + len(args): 0}
        args += [xres, garr]
        in_specs += [
            pl.BlockSpec((tm, tn), lambda n, m, te, tv, nx: (m, n)),
            pl.BlockSpec((None, None, None, 1, tn),
                         lambda n, m, te, tv, nx: (gl, m // per_b, gj, 0, n)),
        ]
    out_shape = [jax.ShapeDtypeStruct((M, n_out), out_dtype)]
    out_specs = [pl.BlockSpec((tm, tn), lambda n, m, te, tv, nx: (m, n))]
    bare = norms is None
    norms = norms or []
    for ns in norms:
        assert tn == n_out
        n_args, n_in, n_shape, n_out_specs = _norm_io(
            ns, M, n_out, tm, lambda n, m, *_: m // (seq // tm), lambda n, m, *_: m)
        args += n_args
        in_specs += n_in
        out_shape += n_shape
        out_specs += n_out_specs
    outs = pl.pallas_call(
        functools.partial(_gmm_kernel, mode=mode, out_scale=out_scale, a_packed=a_packed,
                          n_w=n_w, layer=l, n_off=n_off, tn=tn, n_n=n_n,
                          norm_cfgs=tuple(_norm_cfg(ns) for ns in norms),
                          k_split=None if a2 is None else ka),
        grid_spec=pltpu.PrefetchScalarGridSpec(
            num_scalar_prefetch=3,
            grid=(n_n, mt),
            in_specs=in_specs,
            out_specs=out_specs,
            scratch_shapes=([pltpu.VMEM((K, tn), F32) for _ in ws]
                            + [pltpu.VMEM((K, tn), BF16) for _ in ws]
                            + [pltpu.SemaphoreType.DMA((n_w,))]),
        ),
        out_shape=out_shape,
        input_output_aliases=aliases,
        compiler_params=_cparams(2),
        name="gmm_" + mode,
    )(tile_expert, tile_valid, next_run, *args)
    return outs[0] if bare else outs


def _rope_pairs(v, cos, s1, s2):
    return v * cos + pltpu.roll(v, LANES - QK_ROPE // 2, 1) * s1 + pltpu.roll(v, QK_ROPE // 2, 1) * s2


def _mla_in_kernel(a_ref, w_ref, gq_ref, gkv_ref, cos_ref, s1_ref, s2_ref,
                   cq_ref, ckv_ref, kpe_ref, wb, *, q_lora, kv_lora):
    @pl.when(pl.program_id(0) == 0)
    def _():
        wb[...] = w_ref[...].astype(BF16)

    acc = jnp.dot(a_ref[...], wb[...], preferred_element_type=F32)
    cq_ref[...] = _rms(acc[:, :q_lora], gq_ref[...]).astype(BF16)
    ckv_ref[...] = _rms(acc[:, q_lora:q_lora + kv_lora], gkv_ref[...]).astype(BF16)
    v = acc[:, q_lora + kv_lora:]
    kpe_ref[...] = _rope_pairs(v, cos_ref[...], s1_ref[...], s2_ref[...]).astype(BF16)


def _mla_in(h, w_pad, g_q, g_kv, rope_tabs, S, tm=1024):
    T, D = h.shape
    q_lora, kv_lora = g_q.shape[0], g_kv.shape[0]
    n_all = w_pad.shape[1]
    nst = S // tm
    tab_spec = pl.BlockSpec((tm, LANES), lambda i: (i % nst, 0))
    return pl.pallas_call(
        functools.partial(_mla_in_kernel, q_lora=q_lora, kv_lora=kv_lora),
        grid=(T // tm,),
        in_specs=[
            pl.BlockSpec((tm, D), lambda i: (i, 0)),
            pl.BlockSpec((D, n_all), lambda i: (0, 0)),
            pl.BlockSpec((1, q_lora), lambda i: (0, 0)),
            pl.BlockSpec((1, kv_lora), lambda i: (0, 0)),
            tab_spec, tab_spec, tab_spec,
        ],
        out_specs=[
            pl.BlockSpec((tm, q_lora), lambda i: (i, 0)),
            pl.BlockSpec((tm, kv_lora), lambda i: (i, 0)),
            pl.BlockSpec((tm, LANES), lambda i: (i, 0)),
        ],
        out_shape=[
            jax.ShapeDtypeStruct((T, q_lora), BF16),
            jax.ShapeDtypeStruct((T, kv_lora), BF16),
            jax.ShapeDtypeStruct((T, LANES), BF16),
        ],
        scratch_shapes=[pltpu.VMEM((D, n_all), BF16)],
        compiler_params=_cparams(1),
        name="mla_in",
    )(h, w_pad, g_q.reshape(1, -1), g_kv.reshape(1, -1), *rope_tabs)


def _mla_attn_kernel(qn_ref, qp_ref, kv_ref, kpe_ref, cos_ref, s1_ref, s2_ref, o_ref,
                     m_sc, l_sc, acc_sc, *, tq, hps):
    qi = pl.program_id(2)
    head_w = QK_NOPE + V_HEAD
    lane = lax.broadcasted_iota(jnp.int32, (tq, LANES), 1)
    qs = []
    for pair in range(hps // 2):
        qp = _rope_pairs(qp_ref[:, pair * LANES:(pair + 1) * LANES].astype(F32),
                         cos_ref[...], s1_ref[...], s2_ref[...])
        for sub in range(2):
            hh = 2 * pair + sub
            keep = (lane < QK_ROPE) if sub == 0 else (lane >= QK_ROPE)
            qs.append(jnp.concatenate(
                [qn_ref[:, hh * QK_NOPE:(hh + 1) * QK_NOPE],
                 jnp.where(keep, qp, 0.0).astype(BF16)], axis=1))
    m_sc[...] = jnp.full_like(m_sc, NEG)
    l_sc[...] = jnp.zeros_like(l_sc)
    acc_sc[...] = jnp.zeros_like(acc_sc)

    def tile(j, diagonal):
        ks = pl.multiple_of(j * tq, tq)
        kpe = kpe_ref[pl.ds(ks, tq), :]
        for hh in range(hps):
            c0 = hh * head_w
            k = jnp.concatenate([kv_ref[pl.ds(ks, tq), c0:c0 + QK_NOPE], kpe], axis=1)
            s = _dot_nt(qs[hh], k)
            if diagonal:
                row = lax.broadcasted_iota(jnp.int32, s.shape, 0)
                col = lax.broadcasted_iota(jnp.int32, s.shape, 1)
                s = jnp.where(col <= row, s, NEG)
            m_old = m_sc[hh]
            m_new = jnp.maximum(m_old, jnp.max(s, axis=-1, keepdims=True))
            alpha = jnp.exp2(m_old - m_new)
            p = jnp.exp2(s - jnp.concatenate([m_new] * (tq // LANES), axis=1))
            l_sc[hh] = alpha * l_sc[hh] + jnp.sum(p, axis=-1, keepdims=True)
            vv = kv_ref[pl.ds(ks, tq), c0 + QK_NOPE:c0 + head_w]
            acc_sc[hh] = alpha * acc_sc[hh] + jnp.dot(p.astype(BF16), vv,
                                                      preferred_element_type=F32)
            m_sc[hh] = m_new

    def body(j, carry):
        tile(j, False)
        return carry

    lax.fori_loop(0, qi, body, 0)
    tile(qi, True)
    for hh in range(hps):
        o_ref[:, hh * V_HEAD:(hh + 1) * V_HEAD] = (acc_sc[hh] / l_sc[hh]).astype(o_ref.dtype)


def _mla_attn(q_all, kv_all, kpe, rope_tabs, B, S, tq=512, hps=8):
    T = q_all.shape[0]
    H = MLA_HEADS
    nqt = S // tq
    qn_w, qp_w, kv_w = hps * QK_NOPE, hps * QK_ROPE, hps * (QK_NOPE + V_HEAD)
    tab_spec = pl.BlockSpec((tq, LANES), lambda b, p, i: (i, 0))
    return pl.pallas_call(
        functools.partial(_mla_attn_kernel, tq=tq, hps=hps),
        grid=(B, H // hps, nqt),
        in_specs=[
            pl.BlockSpec((tq, qn_w), lambda b, p, i: (b * nqt + i, p)),
            pl.BlockSpec((tq, qp_w), lambda b, p, i: (b * nqt + i, (H * QK_NOPE) // qp_w + p)),
            pl.BlockSpec((S, kv_w), lambda b, p, i: (b, p)),
            pl.BlockSpec((S, LANES), lambda b, p, i: (b, 0)),
            tab_spec, tab_spec, tab_spec,
        ],
        out_specs=pl.BlockSpec((tq, hps * V_HEAD), lambda b, p, i: (b * nqt + i, p)),
        out_shape=jax.ShapeDtypeStruct((T, H * V_HEAD), BF16),
        scratch_shapes=[pltpu.VMEM((hps, tq, LANES), F32), pltpu.VMEM((hps, tq, LANES), F32),
                        pltpu.VMEM((hps, tq, V_HEAD), F32)],
        compiler_params=_cparams(3),
        name="mla_attn",
    )(q_all, q_all, kv_all, kpe, *rope_tabs)


def _nsa_compress_kernel(t_ref, pe_ref, w1_ref, w2_ref, o_ref, *, n_chunk):
    half = CMP_BLOCK // 2
    pe = pe_ref[...]

    def chunk_rows(off):
        cols = []
        for l in range(half):
            x = t_ref[pl.ds(l, n_chunk, stride=CMP_STRIDE), :] + pe[off + l:off + l + 1, :]
            cols.append(x.astype(BF16))
        return jnp.concatenate(cols, axis=1)

    w1 = w1_ref[...].astype(BF16)
    kdim = half * NSA_DK
    p0 = jnp.dot(chunk_rows(0), w1[:kdim], preferred_element_type=F32)
    p1 = jnp.dot(chunk_rows(half), w1[kdim:], preferred_element_type=F32)
    pre = p0 + pltpu.roll(p1, n_chunk - 1, 0)
    hid = (pre * jax.nn.sigmoid(pre)).astype(BF16)
    out = jnp.dot(hid, w2_ref[...].astype(BF16), preferred_element_type=F32)
    rown = lax.broadcasted_iota(jnp.int32, out.shape, 0)
    o_ref[...] = jnp.where(rown < n_chunk - 1, out, 0.0).astype(o_ref.dtype)


def _nsa_compress(kc, pe, w1, w2, B, S):
    G = NSA_GROUPS
    n_chunk = S // CMP_STRIDE
    return pl.pallas_call(
        functools.partial(_nsa_compress_kernel, n_chunk=n_chunk),
        grid=(B, 2, G),
        in_specs=[
            pl.BlockSpec((S, NSA_DK), lambda b, i, g: (b, i * G + g)),
            pl.BlockSpec((None, CMP_BLOCK, NSA_DK), lambda b, i, g: (i, 0, 0)),
            pl.BlockSpec((None, CMP_BLOCK * NSA_DK, NSA_DK), lambda b, i, g: (i, 0, 0)),
            pl.BlockSpec((None, NSA_DK, NSA_DK), lambda b, i, g: (i, 0, 0)),
        ],
        out_specs=pl.BlockSpec((None, None, None, n_chunk, NSA_DK), lambda b, i, g: (b, i, g, 0, 0)),
        out_shape=jax.ShapeDtypeStruct((B, 2, G, n_chunk, NSA_DK), BF16),
        compiler_params=_cparams(3),
        name="nsa_compress",
    )(kc, pe, w1, w2)


def _nsa_attn_kernel(q_ref, gt_ref, ks_ref, vs_ref, kw_ref, vw_ref, kc_ref, vc_ref, qtab_ref,
                     ktab_ref, ctab_ref, mt_ref, cbias_ref, wbias_ref, o_ref, m_sc, acc_sc,
                     *, tq, tk, n_sel):
    qi = pl.program_id(2)
    t0 = qi * tq
    HP = NSA_HPG
    R = HP * tq
    qb = q_ref[...]
    qs = jnp.concatenate([qb[:, h * NSA_DK:(h + 1) * NSA_DK] for h in range(HP)], axis=0)
    qtab = qtab_ref[...]
    qa = jnp.concatenate([qs, qtab], axis=1)
    rowi = lax.broadcasted_iota(jnp.int32, (R, 1), 0)
    tcol = (t0 + rowi % tq).astype(F32)

    kc = jnp.concatenate([kc_ref[...], ctab_ref[...]], axis=1)
    s = _dot_nt(qa, kc) + jnp.concatenate([cbias_ref[...]] * HP, axis=0)
    p = jnp.exp2(s - jnp.max(s, axis=-1, keepdims=True))
    l = jnp.sum(p, axis=-1, keepdims=True)
    p_cmp = jnp.where(tcol >= CMP_BLOCK - 1, p / l, 0.0)
    o_cmp = jnp.dot(p_cmp.astype(BF16), vc_ref[...], preferred_element_type=F32)

    ps = p_cmp[0:tq]
    for h in range(1, HP):
        ps = ps + p_cmp[h * tq:(h + 1) * tq]
    ps_hi = ps.astype(BF16)
    ps_lo = (ps - ps_hi.astype(F32)).astype(BF16)
    imp = _dot_nt(mt_ref[...], ps_hi) + _dot_nt(mt_ref[...], ps_lo)
    jrow = lax.broadcasted_iota(jnp.int32, (n_sel, tq), 0)
    blk_t = (t0 + lax.broadcasted_iota(jnp.int32, (n_sel, tq), 1)) // SEL_BLOCK
    forced = (jrow == 0) | (jrow == blk_t) | (jrow == blk_t - 1)
    imp = jnp.where(forced, FORCED_SCORE, imp)
    imp = jnp.where(jrow > blk_t, -1.0, imp)
    rank = jnp.zeros((n_sel, tq), F32)
    for k in range(n_sel):
        rk = imp[k:k + 1, :]
        beats = (rk > imp) | ((rk == imp) & (jrow > k))
        rank = rank + jnp.where(beats, 1.0, 0.0)
    sel_t = jnp.where(rank < SEL_TOPN, 1.0, 0.0)
    sel_t = jnp.concatenate([sel_t, jnp.zeros((LANES - n_sel, tq), F32)], axis=0)
    sel_neg = ((sel_t.T - 1.0) * MASK_BIG).astype(BF16)
    lane = lax.broadcasted_iota(jnp.int32, (R, LANES), 1)
    qx = jnp.where(lane < SEL_LANES, jnp.concatenate([sel_neg] * HP, axis=0), qtab)
    qsel = jnp.concatenate([qs, qx], axis=1)

    m_sc[...] = jnp.full_like(m_sc, NEG)
    acc_sc[...] = jnp.zeros_like(acc_sc)
    ones = jnp.ones((tk, LANES), BF16)

    def tile(j, causal):
        ks = pl.multiple_of(j * tk, tk)
        k = jnp.concatenate([ks_ref[pl.ds(ks, tk), :], ktab_ref[pl.ds(ks, tk), :]], axis=1)
        sc = _dot_nt(qsel, k)
        if causal:
            kpos = (ks + lax.broadcasted_iota(jnp.int32, (1, tk), 1)).astype(F32)
            sc = jnp.where(kpos <= tcol, sc, NEG)
        m_old = m_sc[...]
        m_new = jnp.maximum(m_old, jnp.max(sc, axis=-1, keepdims=True))
        alpha = jnp.exp2(m_old - m_new)
        pp = jnp.exp2(sc - jnp.concatenate([m_new] * (tk // LANES), axis=1))
        vv = jnp.concatenate([vs_ref[pl.ds(ks, tk), :], ones], axis=1)
        acc_sc[...] = (jnp.concatenate([alpha, alpha], axis=1) * acc_sc[...]
                       + jnp.dot(pp.astype(BF16), vv, preferred_element_type=F32))
        m_sc[...] = m_new

    def body(j, carry):
        tile(j, False)
        return carry

    n_full = t0 // tk
    lax.fori_loop(0, n_full, body, 0)
    for c in range(tq // tk):
        tile(n_full + c, True)
    acc = acc_sc[...]
    o_sel = acc[:, :NSA_DK] / acc[:, NSA_DK:]

    span = WINDOW + WIN_ROWS
    ones_w = jnp.ones((span, LANES), BF16)
    o_blocks = []
    for i in range(tq // WIN_ROWS):
        b0 = t0 + i * WIN_ROWS
        ws = pl.multiple_of(jnp.maximum(b0 - WINDOW, 0), WIN_ROWS)
        qi_rows = jnp.concatenate(
            [qa[h * tq + i * WIN_ROWS:h * tq + (i + 1) * WIN_ROWS] for h in range(HP)], axis=0)
        kw = jnp.concatenate([kw_ref[pl.ds(ws, span), :], ktab_ref[pl.ds(ws, span), :]], axis=1)
        sw = _dot_nt(qi_rows, kw) + jnp.concatenate([wbias_ref[(b0 - ws) // WIN_ROWS]] * HP, axis=0)
        pw = jnp.exp2(sw - jnp.max(sw, axis=-1, keepdims=True))
        vw = jnp.concatenate([vw_ref[pl.ds(ws, span), :], ones_w], axis=1)
        rw = jnp.dot(pw.astype(BF16), vw, preferred_element_type=F32)
        o_blocks.append(rw[:, :NSA_DK] / rw[:, NSA_DK:])
    o_win = jnp.concatenate(
        [o_blocks[i][h * WIN_ROWS:(h + 1) * WIN_ROWS]
         for h in range(HP) for i in range(tq // WIN_ROWS)], axis=0)

    gt = jax.nn.sigmoid(gt_ref[...])

    def gcol(i):
        return jnp.concatenate([gt[:, i * HP + h:i * HP + h + 1] for h in range(HP)], axis=0)

    o = gcol(0) * o_cmp + gcol(1) * o_sel + gcol(2) * o_win
    o_ref[...] = jnp.concatenate([o[h * tq:(h + 1) * tq] for h in range(HP)], axis=1).astype(o_ref.dtype)


def _nsa_attn(q, gates, kvb, kvc, tabs, B, S, tq=128, tk=256):
    T = q.shape[0]
    G, DK, HP = NSA_GROUPS, NSA_DK, NSA_HPG
    qtab, ktab, ctab, mt, cbias, wbias = tabs
    nqt = S // tq
    n_chunk = kvc.shape[3]
    n_sel = S // SEL_BLOCK

    def kv_spec(i):
        return pl.BlockSpec((S, DK), lambda b, g, t, i=i: (b, i * G + g))

    def cmp_spec(i):
        return pl.BlockSpec((None, None, None, n_chunk, DK), lambda b, g, t, i=i: (b, i, g, 0, 0))

    return pl.pallas_call(
        functools.partial(_nsa_attn_kernel, tq=tq, tk=tk, n_sel=n_sel),
        grid=(B, G, nqt),
        in_specs=[
            pl.BlockSpec((tq, HP * DK), lambda b, g, t: (b * nqt + t, g)),
            pl.BlockSpec((tq, LANES), lambda b, g, t: (b * nqt + t, g)),
            kv_spec(0), kv_spec(1), kv_spec(2), kv_spec(3),
            cmp_spec(0), cmp_spec(1),
            pl.BlockSpec((None, HP * tq, LANES), lambda b, g, t: (g, 0, 0)),
            pl.BlockSpec(ktab.shape, lambda b, g, t: (0, 0)),
            pl.BlockSpec(ctab.shape, lambda b, g, t: (0, 0)),
            pl.BlockSpec(mt.shape, lambda b, g, t: (0, 0)),
            pl.BlockSpec((tq, n_chunk), lambda b, g, t: (t, 0)),
            pl.BlockSpec(wbias.shape, lambda b, g, t: (0, 0, 0)),
        ],
        out_specs=pl.BlockSpec((tq, HP * DK), lambda b, g, t: (b * nqt + t, g)),
        out_shape=jax.ShapeDtypeStruct((T, NSA_HEADS * DK), BF16),
        scratch_shapes=[pltpu.VMEM((HP * tq, LANES), F32),
                        pltpu.VMEM((HP * tq, DK + LANES), F32)],
        compiler_params=_cparams(3),
        name="nsa_attn",
    )(q, gates, kvb, kvb, kvb, kvb, kvc, kvc, qtab, ktab, ctab, mt, cbias, wbias)


def _row_copy(src_hbm, dst_vmem, sem, src_row, dst_row):
    return pltpu.make_async_copy(src_hbm.at[pl.ds(src_row, 1), :], dst_vmem.at[pl.ds(dst_row, 1), :], sem)


def _dispatch_kernel(pos_ref, src_ref, init_hbm, out_hbm, sem, *, chunk, n_tok):
    del init_hbm
    base = pl.program_id(0) * chunk

    def start(r, c):
        t = base + r
        _row_copy(src_ref, out_hbm, sem, r, pos_ref[t]).start()
        _row_copy(src_ref, out_hbm, sem, r, pos_ref[n_tok + t]).start()
        return c

    def wait(r, c):
        _row_copy(src_ref, out_hbm, sem, 0, 0).wait()
        return c

    lax.fori_loop(0, chunk, start, 0, unroll=8)
    lax.fori_loop(0, 2 * chunk, wait, 0, unroll=8)


def _dispatch(h_packed, pos, n_rows, chunk=512):
    T, W = h_packed.shape
    n_steps = T // chunk
    init = jnp.zeros((n_rows, W), h_packed.dtype)
    return pl.pallas_call(
        functools.partial(_dispatch_kernel, chunk=chunk, n_tok=T),
        grid_spec=pltpu.PrefetchScalarGridSpec(
            num_scalar_prefetch=1,
            grid=(n_steps,),
            in_specs=[pl.BlockSpec((chunk, W), lambda i, pos: (i, 0)),
                      pl.BlockSpec(memory_space=pl.ANY)],
            out_specs=pl.BlockSpec(memory_space=pl.ANY),
            scratch_shapes=[pltpu.SemaphoreType.DMA(())],
        ),
        out_shape=jax.ShapeDtypeStruct((n_rows, W), h_packed.dtype),
        input_output_aliases={2: 0},
        compiler_params=_cparams(1),
        name="moe_dispatch",
    )(pos, h_packed, init)


def _combine_kernel(pos_ref, y_hbm, x_ref, gate_ref, rt_ref, *refs, tg, n_tok, n_steps,
                    norm_cfgs):
    n_norm_in = sum(_norm_counts(c)[0] for c in norm_cfgs)
    n_norm_out = sum(_norm_counts(c)[1] for c in norm_cfgs)
    norm_in, (o_ref, *refs) = refs[:n_norm_in], refs[n_norm_in:]
    norm_out, (buf, sem) = refs[:n_norm_out], refs[n_norm_out:]
    i = pl.program_id(0)
    slot = i % 2

    def issue(step, slot_):
        def start(r, c):
            t = step * tg + r
            _row_copy(y_hbm, buf.at[slot_, 0], sem.at[slot_], pos_ref[t], r).start()
            _row_copy(y_hbm, buf.at[slot_, 1], sem.at[slot_], pos_ref[n_tok + t], r).start()
            return c

        lax.fori_loop(0, tg, start, 0, unroll=8)

    @pl.when(i == 0)
    def _():
        issue(0, 0)

    @pl.when(i + 1 < n_steps)
    def _():
        issue(i + 1, 1 - slot)

    def wait(r, c):
        _row_copy(y_hbm, buf.at[slot, 0], sem.at[slot], 0, 0).wait()
        return c

    lax.fori_loop(0, 2 * tg, wait, 0, unroll=8)
    rt = rt_ref[...]
    y = rt[:, 2:3] * buf[slot, 0] + rt[:, 3:4] * buf[slot, 1]
    x_new = x_ref[...] + gate_ref[...] * y
    o_ref[...] = x_new
    _apply_norms(x_new, norm_in, norm_out, norm_cfgs)


def _combine(x2, y_sorted, pos, route, gate, S, norms=(), tg=256):
    T, D = x2.shape
    garr, gl, gj = gate
    per_b = S // tg
    n_steps = T // tg
    args = [y_sorted, x2, garr, route]
    in_specs = [
        pl.BlockSpec(memory_space=pl.ANY),
        pl.BlockSpec((tg, D), lambda i, pos: (i, 0)),
        pl.BlockSpec((None, None, None, 1, D), lambda i, pos: (gl, i // per_b, gj, 0, 0)),
        pl.BlockSpec((tg, LANES), lambda i, pos: (i, 0)),
    ]
    out_shape = [jax.ShapeDtypeStruct((T, D), F32)]
    out_specs = [pl.BlockSpec((tg, D), lambda i, pos: (i, 0))]
    for ns in norms:
        n_args, n_in, n_shape, n_out_specs = _norm_io(
            ns, T, D, tg, lambda i, *_: i // per_b, lambda i, *_: i)
        args += n_args
        in_specs += n_in
        out_shape += n_shape
        out_specs += n_out_specs
    outs = pl.pallas_call(
        functools.partial(_combine_kernel, tg=tg, n_tok=T, n_steps=n_steps,
                          norm_cfgs=tuple(_norm_cfg(ns) for ns in norms)),
        grid_spec=pltpu.PrefetchScalarGridSpec(
            num_scalar_prefetch=1,
            grid=(n_steps,),
            in_specs=in_specs,
            out_specs=out_specs,
            scratch_shapes=[pltpu.VMEM((2, 2, tg, D), F32), pltpu.SemaphoreType.DMA((2,))],
        ),
        out_shape=out_shape,
        input_output_aliases={2: 0},
        compiler_params=_cparams(1),
        name="moe_combine",
    )(pos, *args)
    return outs if norms else outs[0]


def _moe_plan(route, T, tm):
    E = N_EXPERTS
    n_tiles = (2 * T) // tm + E
    P = n_tiles * tm
    e_pair = jnp.concatenate([route[:, 0], route[:, 1]]).astype(jnp.int32)
    onehot = (e_pair[:, None] == jnp.arange(E, dtype=jnp.int32)[None, :]).astype(jnp.int32)
    csum = jnp.cumsum(onehot, axis=0)
    rank = jnp.sum(onehot * (csum - 1), axis=1)
    counts = csum[-1]
    padded = ((counts + tm - 1) // tm) * tm
    ends = jnp.cumsum(padded)
    pos = (ends - padded)[e_pair] + rank
    tile_start = jnp.arange(n_tiles, dtype=jnp.int32) * tm
    tile_valid = (tile_start < ends[-1]).astype(jnp.int32)
    tile_expert = jnp.sum((tile_start[:, None] >= ends[None, :]).astype(jnp.int32), axis=1)
    last_valid = tile_expert[jnp.maximum(ends[-1] // tm - 1, 0)]
    tile_expert = jnp.where(tile_valid == 1, tile_expert, last_valid).astype(jnp.int32)
    return pos.astype(jnp.int32), P, tile_expert, tile_valid


def _rope_tables(S):
    d = QK_ROPE
    inv = ROPE_THETA ** (-jnp.arange(0, d, 2, dtype=F32) / d)
    ang = jnp.arange(S).astype(F32)[:, None] * inv[None, :]
    cos, sin = jnp.cos(ang), jnp.sin(ang)
    z = jnp.zeros_like(sin)
    return (jnp.concatenate([cos, cos, cos, cos], axis=1),
            jnp.concatenate([-sin, z, -sin, z], axis=1),
            jnp.concatenate([z, sin, z, sin], axis=1))


def _pos_columns(pos):
    tab = np.zeros((pos.shape[0], LANES), np.float32)
    tab[:, POS_HI_LANE:POS_HI_LANE + 3] = (LANES * (pos // LANES))[:, None]
    tab[:, POS_LO_LANE:POS_LO_LANE + 3] = (pos % LANES)[:, None]
    return tab


def _nsa_tables(S, tq):
    n_cmp = (S - CMP_BLOCK) // CMP_STRIDE + 1
    n_sel = S // SEL_BLOCK
    n_chunk = S // CMP_STRIDE
    tok = np.arange(n_cmp)[:, None] * CMP_STRIDE + np.arange(CMP_BLOCK)[None, :]
    blk = tok // SEL_BLOCK
    m = (blk[:, :, None] == np.arange(n_sel)[None, None, :]).sum(axis=1) / CMP_BLOCK
    mt = np.zeros((n_sel, n_chunk), np.float32)
    mt[:, :n_cmp] = m.T
    keys = np.arange(S)
    ktab = _pos_columns(keys)
    ktab[:, :SEL_LANES] = (keys[:, None] // SEL_BLOCK == np.arange(SEL_LANES)[None, :])
    ctab = _pos_columns(np.arange(n_chunk) * CMP_STRIDE + (CMP_BLOCK - 1))
    slopes = jnp.asarray(2.0 ** (-8.0 * np.arange(1, NSA_HEADS + 1) / NSA_HEADS), F32)
    a = slopes * LOG2E
    a_hi = a.astype(BF16)
    r1 = a - a_hi.astype(F32)
    a_mid = r1.astype(BF16)
    a_lo = (r1 - a_mid.astype(F32)).astype(BF16)
    pieces = jnp.stack([a_hi, a_mid, a_lo], axis=-1)
    qrow = jnp.zeros((NSA_HEADS, LANES), BF16)
    qrow = qrow.at[:, POS_HI_LANE:POS_HI_LANE + 3].set(pieces)
    qrow = qrow.at[:, POS_LO_LANE:POS_LO_LANE + 3].set(pieces)
    qtab = jnp.repeat(qrow.reshape(NSA_GROUPS, NSA_HPG, LANES), tq, axis=1)
    cend = np.arange(n_chunk) * CMP_STRIDE + (CMP_BLOCK - 1)
    cvis = (keys[:, None] >= cend[None, :]) & (np.arange(n_chunk)[None, :] < n_cmp)
    cbias = np.where(cvis, 0.0, NEG).astype(np.float32)
    span = WINDOW + WIN_ROWS
    d = (np.arange(WINDOW // WIN_ROWS + 1)[:, None, None] * WIN_ROWS
         + np.arange(WIN_ROWS)[None, :, None] - np.arange(span)[None, None, :])
    wbias = np.where((d >= 0) & (d < WINDOW), 0.0, NEG).astype(np.float32)
    return (qtab, jnp.asarray(ktab, BF16), jnp.asarray(ctab, BF16), jnp.asarray(mt, BF16),
            jnp.asarray(cbias), jnp.asarray(wbias))


def kernel(x, c, ada_w, ada_b, norm1_g, norm2_g, mla_w_in, mla_g_q, mla_g_kv, mla_w_uq, mla_w_ukv, mla_w_o, kv_ada_w, kv_ada_b, kv_norm_g, nsa_w_kv, cmp_pos_k, cmp_pos_v, cmp_k_w1, cmp_k_w2, cmp_v_w1, cmp_v_w2, nsa_w_in, nsa_w_o, ffn_w_gate, ffn_w_up, ffn_w_down, moe_w_router, moe_b_router, moe_w_gate, moe_w_up, moe_w_down, final_g):
    B, S, D = x.shape
    T = B * S
    depth = ada_w.shape[0]
    n_a = mla_w_in.shape[0]
    H = MLA_HEADS
    G, HP, DK = NSA_GROUPS, NSA_HPG, NSA_DK
    d_ff = ffn_w_gate.shape[-1]

    c_pad = jnp.zeros((8, D), F32).at[:B].set(c)
    mod = _modulation(c_pad, ada_w, ada_b)[:, :B].reshape(depth, B, 6, 1, D)
    kv_mod = _modulation(c_pad, kv_ada_w[None], kv_ada_b[None])[:, :B].reshape(1, B, 2, 1, D)

    rope_tabs = _rope_tables(S)
    nsa_tabs = _nsa_tables(S, NSA_TQ)

    ffn_wg = ffn_w_gate[:, None]
    ffn_wu = ffn_w_up[:, None]
    ffn_wd = ffn_w_down[:, None]

    def norm1_spec(l):
        return _norm_spec(norm1_g[l], (mod, l, 0), (mod, l, 1))

    def swiglu(a, ws, li, tm, **kw):
        wide = (d_ff // WIDE_TN) * WIDE_TN
        tail = d_ff - wide
        main = _gmm(a, ws, li, mode="swiglu", tm=tm, tn=WIDE_TN, n_out=wide, out_dtype=BF16, **kw)
        rest = _gmm(a, ws, li, mode="swiglu", tm=tm, tn=tail, n_out=tail, out_dtype=BF16,
                    n_off=wide // tail, **kw)
        return main, rest

    kv_spec = _norm_spec(kv_norm_g, (kv_mod, 0, 0), (kv_mod, 0, 1))
    final_spec = _norm_spec(final_g, out_dtype=F32)

    x2 = x.reshape(T, D)
    shared = None
    h = _norm(x2, norm1_spec(0), B, S)
    for l in range(depth):
        dense = l % 2 == 0
        if dense:
            norm2 = _norm_spec(norm2_g[l], (mod, l, 3), (mod, l, 4))
        else:
            wr = jnp.pad(moe_w_router[l // 2], ((0, 0), (0, LANES - N_EXPERTS)))
            br = jnp.pad(moe_b_router[l // 2], (0, LANES - N_EXPERTS)).reshape(1, LANES)
            norm2 = _norm_spec(norm2_g[l], (mod, l, 3), (mod, l, 4), router=(wr, br))
        fused2 = [norm2] if dense else []
        if l < n_a:
            w_in = mla_w_in[l]
            q_lora, kv_lora = mla_g_q.shape[1], mla_g_kv.shape[1]
            w_pad = jnp.concatenate([w_in, w_in[:, q_lora + kv_lora:]], axis=1)
            cq, ckv, kpe = _mla_in(h, w_pad, mla_g_q[l], mla_g_kv[l], rope_tabs, S)
            wq = mla_w_uq[l].reshape(q_lora, H, QK_NOPE + QK_ROPE)
            wq = jnp.concatenate([wq[:, :, :QK_NOPE].reshape(q_lora, H * QK_NOPE),
                                  wq[:, :, QK_NOPE:].reshape(q_lora, H * QK_ROPE)], axis=1)
            q_all = _gmm(cq, [wq[None, None]], 0, mode="cast", tm=LORA_TM, tn=WIDE_TN,
                         n_out=wq.shape[1], out_dtype=BF16,
                         out_scale=(QK_NOPE + QK_ROPE) ** -0.5 * LOG2E)
            kv_all = _gmm(ckv, [mla_w_ukv[:, None]], l, mode="cast", tm=LORA_TM, tn=WIDE_TN,
                          n_out=mla_w_ukv.shape[-1], out_dtype=BF16)
            o = _mla_attn(q_all, kv_all, kpe, rope_tabs, B, S)
            x2, *h2 = _gmm(o, [mla_w_o[:, None]], l, mode="residual", tm=OUT_PROJ_TM, tn=D, n_out=D,
                           out_dtype=F32, xres=x2, gate=(mod, l, 2), seq=S, norms=fused2,
                           alias_x=l > 0)
        else:
            jb = l - n_a
            w_in = nsa_w_in[jb]
            q = _gmm(h, [nsa_w_in[:, None]], jb, mode="cast", tm=DENSE_TM, tn=WIDE_TN,
                     n_out=NSA_HEADS * DK, out_dtype=BF16, out_scale=DK ** -0.5 * LOG2E)
            wg = w_in[:, NSA_HEADS * DK:].reshape(D, G, HP, 3).transpose(0, 1, 3, 2)
            wg = jnp.pad(wg.reshape(D, G, 3 * HP), ((0, 0), (0, 0), (0, LANES - 3 * HP)))
            gates = _gmm(h, [wg.reshape(1, 1, D, G * LANES)], 0, mode="cast", tm=DENSE_TM,
                         tn=NARROW_TN,
                         n_out=G * LANES, out_dtype=F32)
            kvb, kvc = shared
            o = _nsa_attn(q, gates, kvb, kvc, nsa_tabs, B, S, tq=NSA_TQ)
            x2, *h2 = _gmm(o, [nsa_w_o[:, None]], jb, mode="residual", tm=OUT_PROJ_TM, tn=D, n_out=D,
                           out_dtype=F32, xres=x2, gate=(mod, l, 2), seq=S, norms=fused2)

        after = [norm1_spec(l + 1) if l + 1 < depth else final_spec]
        if l == n_a - 1:
            after.append(kv_spec)
        if dense:
            hid, hid_tail = swiglu(h2[0], [ffn_wg, ffn_wu], l // 2, DENSE_TM)
            x2 = _gmm(hid, [ffn_wd], l // 2, mode="residual", tm=DENSE_TM, tn=NARROW_TN, n_out=D,
                      out_dtype=F32, xres=x2, gate=(mod, l, 5), seq=S, a2=hid_tail)
            normed = [_norm(x2, ns, B, S) for ns in after]
        else:
            li = l // 2
            hp, route = _norm(x2, norm2, B, S)
            pos, n_rows, tile_expert, tile_valid = _moe_plan(route, T, MOE_TM)
            hs = _dispatch(hp, pos, n_rows)
            hid, hid_tail = swiglu(hs, [moe_w_gate, moe_w_up], li, MOE_TM, tile_expert=tile_expert,
                                   tile_valid=tile_valid, a_packed=True)
            ys = _gmm(hid, [moe_w_down], li, mode="cast", tm=MOE_TM, tn=WIDE_TN, n_out=D,
                      out_dtype=F32, tile_expert=tile_expert, tile_valid=tile_valid, a2=hid_tail)
            x2, *normed = _combine(x2, ys, pos, route, (mod, l, 5), S, norms=after)
        h = normed[0]

        if l == n_a - 1:
            hkv = normed[1]
            w_kv = nsa_w_kv[None, None]
            kc = _gmm(hkv, [w_kv], 0, mode="cast", tm=DENSE_TM, tn=NARROW_TN, n_out=2 * G * DK,
                      out_dtype=F32)
            kvb = _gmm(hkv, [w_kv], 0, mode="cast", tm=DENSE_TM, tn=WIDE_TN, n_out=4 * G * DK,
                       out_dtype=BF16, n_off=(2 * G * DK) // WIDE_TN)
            kvc = _nsa_compress(kc, jnp.stack([cmp_pos_k, cmp_pos_v]),
                                jnp.stack([cmp_k_w1, cmp_v_w1]), jnp.stack([cmp_k_w2, cmp_v_w2]),
                                B, S)
            shared = (kvb, kvc)

    return h.reshape(B, S, D)
```

```python
import functools

import numpy as np
import jax
import jax.numpy as jnp
from jax import lax
from jax.experimental import pallas as pl
from jax.experimental.pallas import tpu as pltpu

F32 = jnp.float32
BF16 = jnp.bfloat16

EPS = 1e-6
NEG = -1e30
FORCED_SCORE = 1e6
LOG2E = 1.4426950408889634

MLA_HEADS = 16
QK_NOPE = 128
QK_ROPE = 64
V_HEAD = 128
ROPE_THETA = 10000.0

NSA_HEADS = 16
NSA_GROUPS = 4
NSA_HPG = NSA_HEADS // NSA_GROUPS
NSA_DK = 128
CMP_BLOCK = 32
CMP_STRIDE = 16
SEL_BLOCK = 64
SEL_TOPN = 16
WINDOW = 512

N_EXPERTS = 8
LANES = 128
VMEM_LIMIT = 56 * 1024 * 1024

MOE_TM = 512
DENSE_TM = 1024
NARROW_TN = 512
WIDE_TN = 1024
LORA_TM = 2048
OUT_PROJ_TM = 512
NSA_TQ = 512
WIN_ROWS = 128

SEL_LANES = 32
POS_HI_LANE = 32
POS_LO_LANE = 35
MASK_BIG = 2.0 ** 30


def _cparams(n_axes):
    return pltpu.CompilerParams(
        dimension_semantics=("arbitrary",) * n_axes, vmem_limit_bytes=VMEM_LIMIT)


def _dot_nt(a, b):
    return lax.dot_general(a, b, (((1,), (1,)), ((), ())), preferred_element_type=F32)


def _mod_kernel(c_ref, w_ref, b_ref, o_ref):
    c = c_ref[...]
    ca = (c * jax.nn.sigmoid(c)).astype(BF16)
    o_ref[...] = jnp.dot(ca, w_ref[...].astype(BF16), preferred_element_type=F32) + b_ref[...]


def _modulation(c_pad, w, b, tn=2048):
    L, D, N = w.shape
    return pl.pallas_call(
        _mod_kernel,
        grid=(L, N // tn),
        in_specs=[
            pl.BlockSpec((8, D), lambda l, n: (0, 0)),
            pl.BlockSpec((None, D, tn), lambda l, n: (l, 0, n)),
            pl.BlockSpec((None, 1, tn), lambda l, n: (l, 0, n)),
        ],
        out_specs=pl.BlockSpec((None, 8, tn), lambda l, n: (l, 0, n)),
        out_shape=jax.ShapeDtypeStruct((L, 8, N), F32),
        compiler_params=_cparams(2),
        name="adaln_mod",
    )(c_pad, w, b.reshape(L, 1, N))


def _rms(x, g):
    return x * lax.rsqrt(jnp.mean(x * x, axis=-1, keepdims=True) + EPS) * g


_HI16 = 0xFFFF0000


def _pack_bf16_halves(h):
    half = h.shape[1] // 2
    bits = lax.bitcast_convert_type(h.astype(BF16).astype(F32), jnp.uint32)
    return (bits[:, :half] >> 16) | (bits[:, half:] & jnp.uint32(_HI16))


def _unpack_bf16_halves(a):
    lo = lax.bitcast_convert_type(a << 16, F32).astype(BF16)
    hi = lax.bitcast_convert_type(a & jnp.uint32(_HI16), F32).astype(BF16)
    return jnp.concatenate([lo, hi], axis=1)


def _norm_spec(g, shift=None, scale=None, router=None, out_dtype=BF16):
    return dict(g=g, shift=shift, scale=scale, router=router, out_dtype=out_dtype)


def _norm_cfg(ns):
    return (ns["shift"] is not None, ns["router"] is not None)


def _norm_io(ns, T, D, tm, batch_of, row_of):
    modulate, route = _norm_cfg(ns)
    args = [ns["g"].reshape(1, D)]
    in_specs = [pl.BlockSpec((1, D), lambda *a: (0, 0))]
    if modulate:
        for arr, l, j in (ns["shift"], ns["scale"]):
            args.append(arr)
            in_specs.append(pl.BlockSpec((None, None, None, 1, D),
                                         lambda *a, l=l, j=j: (l, batch_of(*a), j, 0, 0)))
    if route:
        wr, br = ns["router"]
        args += [wr, br]
        in_specs += [pl.BlockSpec((D, LANES), lambda *a: (0, 0)),
                     pl.BlockSpec((1, LANES), lambda *a: (0, 0))]
        out_shape = [jax.ShapeDtypeStruct((T, D // 2), jnp.uint32),
                     jax.ShapeDtypeStruct((T, LANES), F32)]
        out_specs = [pl.BlockSpec((tm, D // 2), lambda *a: (row_of(*a), 0)),
                     pl.BlockSpec((tm, LANES), lambda *a: (row_of(*a), 0))]
    else:
        out_shape = [jax.ShapeDtypeStruct((T, D), ns["out_dtype"])]
        out_specs = [pl.BlockSpec((tm, D), lambda *a: (row_of(*a), 0))]
    return args, in_specs, out_shape, out_specs


def _norm_counts(cfg):
    modulate, route = cfg
    return 1 + 2 * modulate + 2 * route, 1 + route


def _norm_apply(x, in_refs, out_refs, cfg):
    modulate, route = cfg
    g_ref, *rest = in_refs
    h = _rms(x, g_ref[...])
    if modulate:
        sh_ref, sc_ref, *rest = rest
        h = h * (1.0 + sc_ref[...]) + sh_ref[...]
    if not route:
        out_refs[0][...] = h.astype(out_refs[0].dtype)
        return
    wr_ref, br_ref = rest
    out_refs[0][...] = _pack_bf16_halves(h)
    w = wr_ref[...]
    h_hi, w_hi = h.astype(BF16), w.astype(BF16)
    h_lo = (h - h_hi.astype(F32)).astype(BF16)
    w_lo = (w - w_hi.astype(F32)).astype(BF16)
    logits = (jnp.dot(h_hi, w_hi, preferred_element_type=F32)
              + jnp.dot(h_lo, w_hi, preferred_element_type=F32)
              + jnp.dot(h_hi, w_lo, preferred_element_type=F32)) + br_ref[...]
    lane = lax.broadcasted_iota(jnp.int32, logits.shape, 1).astype(F32)
    logits = jnp.where(lane < N_EXPERTS, logits, -jnp.inf)
    v1 = jnp.max(logits, axis=-1, keepdims=True)
    i1 = jnp.min(jnp.where(logits == v1, lane, float(LANES)), axis=-1, keepdims=True)
    others = jnp.where(lane == i1, -jnp.inf, logits)
    v2 = jnp.max(others, axis=-1, keepdims=True)
    i2 = jnp.min(jnp.where(others == v2, lane, float(LANES)), axis=-1, keepdims=True)
    e = jnp.exp(v2 - v1)
    w1 = 1.0 / (1.0 + e)
    w2 = e / (1.0 + e)
    out_refs[1][...] = jnp.where(lane == 0, i1, jnp.where(lane == 1, i2, jnp.where(
        lane == 2, w1, jnp.where(lane == 3, w2, 0.0))))


def _apply_norms(x, refs_in, refs_out, cfgs):
    for cfg in cfgs:
        n_in, n_out = _norm_counts(cfg)
        _norm_apply(x, refs_in[:n_in], refs_out[:n_out], cfg)
        refs_in, refs_out = refs_in[n_in:], refs_out[n_out:]


def _norm_kernel(x_ref, *refs, cfg):
    n_in, _ = _norm_counts(cfg)
    _norm_apply(x_ref[...], refs[:n_in], refs[n_in:], cfg)


def _norm(x2, ns, B, S, ts=512):
    T, D = x2.shape
    nst = S // ts
    args, in_specs, out_shape, out_specs = _norm_io(
        ns, T, D, ts, lambda b, s: b, lambda b, s: b * nst + s)
    outs = pl.pallas_call(
        functools.partial(_norm_kernel, cfg=_norm_cfg(ns)),
        grid=(B, nst),
        in_specs=[pl.BlockSpec((ts, D), lambda b, s: (b * nst + s, 0))] + in_specs,
        out_specs=out_specs,
        out_shape=out_shape,
        compiler_params=_cparams(2),
        name="norm_mod",
    )(x2, *args)
    return outs if len(outs) > 1 else outs[0]


def _gmm_kernel(te_ref, tv_ref, nx_ref, a_ref, *refs, mode, out_scale, a_packed, n_w, layer,
                n_off, tn, n_n, norm_cfgs, k_split):
    n = pl.program_id(0)
    m = pl.program_id(1)
    if k_split is not None:
        a2_ref, *refs = refs
    w_hbm, refs = refs[:n_w], refs[n_w:]
    if mode == "residual":
        x_ref, gate_ref, *refs = refs
    n_norm_in = sum(_norm_counts(c)[0] for c in norm_cfgs)
    n_norm_out = sum(_norm_counts(c)[1] for c in norm_cfgs)
    norm_in, refs = refs[:n_norm_in], refs[n_norm_in:]
    o_ref, *refs = refs
    norm_out, refs = refs[:n_norm_out], refs[n_norm_out:]
    wst, wb, sem = refs[:n_w], refs[n_w:2 * n_w], refs[2 * n_w]
    first = jnp.logical_or(m == 0, te_ref[m] != te_ref[jnp.maximum(m - 1, 0)])
    valid = tv_ref[m] != 0

    def w_copy(i, e, nn):
        col = pl.multiple_of((nn + n_off) * tn, tn)
        return pltpu.make_async_copy(w_hbm[i].at[layer, e, :, pl.ds(col, tn)], wst[i], sem.at[i])

    @pl.when(jnp.logical_and(n == 0, m == 0))
    def _():
        for i in range(n_w):
            w_copy(i, te_ref[0], 0).start()

    @pl.when(first)
    def _():
        for i in range(n_w):
            w_copy(i, te_ref[m], n).wait()
            wb[i][...] = wst[i][...].astype(BF16)
        nm = nx_ref[m]
        same_sweep = nm >= 0
        e_next = jnp.where(same_sweep, te_ref[jnp.maximum(nm, 0)], te_ref[0])
        n_next = jnp.where(same_sweep, n, n + 1)

        @pl.when(jnp.logical_or(same_sweep, n + 1 < n_n))
        def _():
            for i in range(n_w):
                w_copy(i, e_next, n_next).start()

    @pl.when(valid)
    def _():
        a = _unpack_bf16_halves(a_ref[...]) if a_packed else a_ref[...]
        if mode == "swiglu":
            g = jnp.dot(a, wb[0][...], preferred_element_type=F32)
            u = jnp.dot(a, wb[1][...], preferred_element_type=F32)
            acc = g * jax.nn.sigmoid(g) * u
        elif k_split is None:
            acc = jnp.dot(a, wb[0][...], preferred_element_type=F32)
        else:
            acc = (jnp.dot(a, wb[0][:k_split], preferred_element_type=F32)
                   + jnp.dot(a2_ref[...], wb[0][k_split:], preferred_element_type=F32))
        if out_scale is not None:
            acc = acc * out_scale
        if mode == "residual":
            acc = x_ref[...] + gate_ref[...] * acc
        o_ref[...] = acc.astype(o_ref.dtype)
        _apply_norms(acc, norm_in, norm_out, norm_cfgs)

    @pl.when(jnp.logical_not(valid))
    def _():
        o_ref[...] = jnp.zeros_like(o_ref)


def _gmm(a, ws, l, *, mode, tm, tn, n_out, out_dtype, tile_expert=None, tile_valid=None,
         n_off=0, xres=None, gate=None, seq=None, out_scale=None, a_packed=False,
         alias_x=True, norms=None, a2=None):
    M, ka = a.shape
    K = ws[0].shape[2]
    mt = M // tm
    n_w = len(ws)
    n_n = n_out // tn
    if tile_expert is None:
        tile_expert = jnp.zeros((mt,), jnp.int32)
        tile_valid = jnp.ones((mt,), jnp.int32)
    idx = jnp.arange(mt, dtype=jnp.int32)
    later = ((idx[None, :] > idx[:, None]) & (tile_valid[None, :] != 0)
             & (tile_expert[None, :] != tile_expert[:, None]))
    next_run = jnp.where(jnp.any(later, axis=1), jnp.argmax(later, axis=1), -1).astype(jnp.int32)
    lhs = [a] if a2 is None else [a, a2]
    args = lhs + list(ws)
    in_specs = [pl.BlockSpec((tm, t.shape[1]), lambda n, m, te, tv, nx: (m, 0)) for t in lhs]
    in_specs += [pl.BlockSpec(memory_space=pl.ANY)] * n_w
    aliases = {}
    if mode == "residual":
        garr, gl, gj = gate
        per_b = seq // tm
        if alias_x:
            aliases = {3 + len(args): 0}
        args += [xres, garr]
        in_specs += [
            pl.BlockSpec((tm, tn), lambda n, m, te, tv, nx: (m, n)),
            pl.BlockSpec((None, None, None, 1, tn),
                         lambda n, m, te, tv, nx: (gl, m // per_b, gj, 0, n)),
        ]
    out_shape = [jax.ShapeDtypeStruct((M, n_out), out_dtype)]
    out_specs = [pl.BlockSpec((tm, tn), lambda n, m, te, tv, nx: (m, n))]
    bare = norms is None
    norms = norms or []
    for ns in norms:
        assert tn == n_out
        n_args, n_in, n_shape, n_out_specs = _norm_io(
            ns, M, n_out, tm, lambda n, m, *_: m // (seq // tm), lambda n, m, *_: m)
        args += n_args
        in_specs += n_in
        out_shape += n_shape
        out_specs += n_out_specs
    outs = pl.pallas_call(
        functools.partial(_gmm_kernel, mode=mode, out_scale=out_scale, a_packed=a_packed,
                          n_w=n_w, layer=l, n_off=n_off, tn=tn, n_n=n_n,
                          norm_cfgs=tuple(_norm_cfg(ns) for ns in norms),
                          k_split=None if a2 is None else ka),
        grid_spec=pltpu.PrefetchScalarGridSpec(
            num_scalar_prefetch=3,
            grid=(n_n, mt),
            in_specs=in_specs,
            out_specs=out_specs,
            scratch_shapes=([pltpu.VMEM((K, tn), F32) for _ in ws]
                            + [pltpu.VMEM((K, tn), BF16) for _ in ws]
                            + [pltpu.SemaphoreType.DMA((n_w,))]),
        ),
        out_shape=out_shape,
        input_output_aliases=aliases,
        compiler_params=_cparams(2),
        name="gmm_" + mode,
    )(tile_expert, tile_valid, next_run, *args)
    return outs[0] if bare else outs


def _rope_pairs(v, cos, s1, s2):
    return v * cos + pltpu.roll(v, LANES - QK_ROPE // 2, 1) * s1 + pltpu.roll(v, QK_ROPE // 2, 1) * s2


def _mla_in_kernel(a_ref, w_ref, gq_ref, gkv_ref, cos_ref, s1_ref, s2_ref,
                   cq_ref, ckv_ref, kpe_ref, wb, *, q_lora, kv_lora):
    @pl.when(pl.program_id(0) == 0)
    def _():
        wb[...] = w_ref[...].astype(BF16)

    acc = jnp.dot(a_ref[...], wb[...], preferred_element_type=F32)
    cq_ref[...] = _rms(acc[:, :q_lora], gq_ref[...]).astype(BF16)
    ckv_ref[...] = _rms(acc[:, q_lora:q_lora + kv_lora], gkv_ref[...]).astype(BF16)
    v = acc[:, q_lora + kv_lora:]
    kpe_ref[...] = _rope_pairs(v, cos_ref[...], s1_ref[...], s2_ref[...]).astype(BF16)


def _mla_in(h, w_pad, g_q, g_kv, rope_tabs, S, tm=1024):
    T, D = h.shape
    q_lora, kv_lora = g_q.shape[0], g_kv.shape[0]
    n_all = w_pad.shape[1]
    nst = S // tm
    tab_spec = pl.BlockSpec((tm, LANES), lambda i: (i % nst, 0))
    return pl.pallas_call(
        functools.partial(_mla_in_kernel, q_lora=q_lora, kv_lora=kv_lora),
        grid=(T // tm,),
        in_specs=[
            pl.BlockSpec((tm, D), lambda i: (i, 0)),
            pl.BlockSpec((D, n_all), lambda i: (0, 0)),
            pl.BlockSpec((1, q_lora), lambda i: (0, 0)),
            pl.BlockSpec((1, kv_lora), lambda i: (0, 0)),
            tab_spec, tab_spec, tab_spec,
        ],
        out_specs=[
            pl.BlockSpec((tm, q_lora), lambda i: (i, 0)),
            pl.BlockSpec((tm, kv_lora), lambda i: (i, 0)),
            pl.BlockSpec((tm, LANES), lambda i: (i, 0)),
        ],
        out_shape=[
            jax.ShapeDtypeStruct((T, q_lora), BF16),
            jax.ShapeDtypeStruct((T, kv_lora), BF16),
            jax.ShapeDtypeStruct((T, LANES), BF16),
        ],
        scratch_shapes=[pltpu.VMEM((D, n_all), BF16)],
        compiler_params=_cparams(1),
        name="mla_in",
    )(h, w_pad, g_q.reshape(1, -1), g_kv.reshape(1, -1), *rope_tabs)


def _mla_attn_kernel(qn_ref, qp_ref, kv_ref, kpe_ref, cos_ref, s1_ref, s2_ref, o_ref,
                     m_sc, l_sc, acc_sc, *, tq, hps):
    qi = pl.program_id(2)
    head_w = QK_NOPE + V_HEAD
    lane = lax.broadcasted_iota(jnp.int32, (tq, LANES), 1)
    qs = []
    for pair in range(hps // 2):
        qp = _rope_pairs(qp_ref[:, pair * LANES:(pair + 1) * LANES].astype(F32),
                         cos_ref[...], s1_ref[...], s2_ref[...])
        for sub in range(2):
            hh = 2 * pair + sub
            keep = (lane < QK_ROPE) if sub == 0 else (lane >= QK_ROPE)
            qs.append(jnp.concatenate(
                [qn_ref[:, hh * QK_NOPE:(hh + 1) * QK_NOPE],
                 jnp.where(keep, qp, 0.0).astype(BF16)], axis=1))
    m_sc[...] = jnp.full_like(m_sc, NEG)
    l_sc[...] = jnp.zeros_like(l_sc)
    acc_sc[...] = jnp.zeros_like(acc_sc)

    def tile(j, diagonal):
        ks = pl.multiple_of(j * tq, tq)
        kpe = kpe_ref[pl.ds(ks, tq), :]
        for hh in range(hps):
            c0 = hh * head_w
            k = jnp.concatenate([kv_ref[pl.ds(ks, tq), c0:c0 + QK_NOPE], kpe], axis=1)
            s = _dot_nt(qs[hh], k)
            if diagonal:
                row = lax.broadcasted_iota(jnp.int32, s.shape, 0)
                col = lax.broadcasted_iota(jnp.int32, s.shape, 1)
                s = jnp.where(col <= row, s, NEG)
            m_old = m_sc[hh]
            m_new = jnp.maximum(m_old, jnp.max(s, axis=-1, keepdims=True))
            alpha = jnp.exp2(m_old - m_new)
            p = jnp.exp2(s - jnp.concatenate([m_new] * (tq // LANES), axis=1))
            l_sc[hh] = alpha * l_sc[hh] + jnp.sum(p, axis=-1, keepdims=True)
            vv = kv_ref[pl.ds(ks, tq), c0 + QK_NOPE:c0 + head_w]
            acc_sc[hh] = alpha * acc_sc[hh] + jnp.dot(p.astype(BF16), vv,
                                                      preferred_element_type=F32)
            m_sc[hh] = m_new

    def body(j, carry):
        tile(j, False)
        return carry

    lax.fori_loop(0, qi, body, 0)
    tile(qi, True)
    for hh in range(hps):
        o_ref[:, hh * V_HEAD:(hh + 1) * V_HEAD] = (acc_sc[hh] / l_sc[hh]).astype(o_ref.dtype)


def _mla_attn(q_all, kv_all, kpe, rope_tabs, B, S, tq=512, hps=8):
    T = q_all.shape[0]
    H = MLA_HEADS
    nqt = S // tq
    qn_w, qp_w, kv_w = hps * QK_NOPE, hps * QK_ROPE, hps * (QK_NOPE + V_HEAD)
    tab_spec = pl.BlockSpec((tq, LANES), lambda b, p, i: (i, 0))
    return pl.pallas_call(
        functools.partial(_mla_attn_kernel, tq=tq, hps=hps),
        grid=(B, H // hps, nqt),
        in_specs=[
            pl.BlockSpec((tq, qn_w), lambda b, p, i: (b * nqt + i, p)),
            pl.BlockSpec((tq, qp_w), lambda b, p, i: (b * nqt + i, (H * QK_NOPE) // qp_w + p)),
            pl.BlockSpec((S, kv_w), lambda b, p, i: (b, p)),
            pl.BlockSpec((S, LANES), lambda b, p, i: (b, 0)),
            tab_spec, tab_spec, tab_spec,
        ],
        out_specs=pl.BlockSpec((tq, hps * V_HEAD), lambda b, p, i: (b * nqt + i, p)),
        out_shape=jax.ShapeDtypeStruct((T, H * V_HEAD), BF16),
        scratch_shapes=[pltpu.VMEM((hps, tq, LANES), F32), pltpu.VMEM((hps, tq, LANES), F32),
                        pltpu.VMEM((hps, tq, V_HEAD), F32)],
        compiler_params=_cparams(3),
        name="mla_attn",
    )(q_all, q_all, kv_all, kpe, *rope_tabs)


def _nsa_compress_kernel(t_ref, pe_ref, w1_ref, w2_ref, o_ref, *, n_chunk):
    half = CMP_BLOCK // 2
    pe = pe_ref[...]

    def chunk_rows(off):
        cols = []
        for l in range(half):
            x = t_ref[pl.ds(l, n_chunk, stride=CMP_STRIDE), :] + pe[off + l:off + l + 1, :]
            cols.append(x.astype(BF16))
        return jnp.concatenate(cols, axis=1)

    w1 = w1_ref[...].astype(BF16)
    kdim = half * NSA_DK
    p0 = jnp.dot(chunk_rows(0), w1[:kdim], preferred_element_type=F32)
    p1 = jnp.dot(chunk_rows(half), w1[kdim:], preferred_element_type=F32)
    pre = p0 + pltpu.roll(p1, n_chunk - 1, 0)
    hid = (pre * jax.nn.sigmoid(pre)).astype(BF16)
    out = jnp.dot(hid, w2_ref[...].astype(BF16), preferred_element_type=F32)
    rown = lax.broadcasted_iota(jnp.int32, out.shape, 0)
    o_ref[...] = jnp.where(rown < n_chunk - 1, out, 0.0).astype(o_ref.dtype)


def _nsa_compress(kc, pe, w1, w2, B, S):
    G = NSA_GROUPS
    n_chunk = S // CMP_STRIDE
    return pl.pallas_call(
        functools.partial(_nsa_compress_kernel, n_chunk=n_chunk),
        grid=(B, 2, G),
        in_specs=[
            pl.BlockSpec((S, NSA_DK), lambda b, i, g: (b, i * G + g)),
            pl.BlockSpec((None, CMP_BLOCK, NSA_DK), lambda b, i, g: (i, 0, 0)),
            pl.BlockSpec((None, CMP_BLOCK * NSA_DK, NSA_DK), lambda b, i, g: (i, 0, 0)),
            pl.BlockSpec((None, NSA_DK, NSA_DK), lambda b, i, g: (i, 0, 0)),
        ],
        out_specs=pl.BlockSpec((None, None, None, n_chunk, NSA_DK), lambda b, i, g: (b, i, g, 0, 0)),
        out_shape=jax.ShapeDtypeStruct((B, 2, G, n_chunk, NSA_DK), BF16),
        compiler_params=_cparams(3),
        name="nsa_compress",
    )(kc, pe, w1, w2)


def _nsa_attn_kernel(q_ref, gt_ref, ks_ref, vs_ref, kw_ref, vw_ref, kc_ref, vc_ref, qtab_ref,
                     ktab_ref, ctab_ref, mt_ref, cbias_ref, wbias_ref, o_ref, m_sc, acc_sc,
                     *, tq, tk, n_sel):
    qi = pl.program_id(2)
    t0 = qi * tq
    HP = NSA_HPG
    R = HP * tq
    qb = q_ref[...]
    qs = jnp.concatenate([qb[:, h * NSA_DK:(h + 1) * NSA_DK] for h in range(HP)], axis=0)
    qtab = qtab_ref[...]
    qa = jnp.concatenate([qs, qtab], axis=1)
    rowi = lax.broadcasted_iota(jnp.int32, (R, 1), 0)
    tcol = (t0 + rowi % tq).astype(F32)

    kc = jnp.concatenate([kc_ref[...], ctab_ref[...]], axis=1)
    s = _dot_nt(qa, kc) + jnp.concatenate([cbias_ref[...]] * HP, axis=0)
    p = jnp.exp2(s - jnp.max(s, axis=-1, keepdims=True))
    l = jnp.sum(p, axis=-1, keepdims=True)
    p_cmp = jnp.where(tcol >= CMP_BLOCK - 1, p / l, 0.0)
    o_cmp = jnp.dot(p_cmp.astype(BF16), vc_ref[...], preferred_element_type=F32)

    ps = p_cmp[0:tq]
    for h in range(1, HP):
        ps = ps + p_cmp[h * tq:(h + 1) * tq]
    ps_hi = ps.astype(BF16)
    ps_lo = (ps - ps_hi.astype(F32)).astype(BF16)
    imp = _dot_nt(mt_ref[...], ps_hi) + _dot_nt(mt_ref[...], ps_lo)
    jrow = lax.broadcasted_iota(jnp.int32, (n_sel, tq), 0)
    blk_t = (t0 + lax.broadcasted_iota(jnp.int32, (n_sel, tq), 1)) // SEL_BLOCK
    forced = (jrow == 0) | (jrow == blk_t) | (jrow == blk_t - 1)
    imp = jnp.where(forced, FORCED_SCORE, imp)
    imp = jnp.where(jrow > blk_t, -1.0, imp)
    rank = jnp.zeros((n_sel, tq), F32)
    for k in range(n_sel):
        rk = imp[k:k + 1, :]
        beats = (rk > imp) | ((rk == imp) & (jrow > k))
        rank = rank + jnp.where(beats, 1.0, 0.0)
    sel_t = jnp.where(rank < SEL_TOPN, 1.0, 0.0)
    sel_t = jnp.concatenate([sel_t, jnp.zeros((LANES - n_sel, tq), F32)], axis=0)
    sel_neg = ((sel_t.T - 1.0) * MASK_BIG).astype(BF16)
    lane = lax.broadcasted_iota(jnp.int32, (R, LANES), 1)
    qx = jnp.where(lane < SEL_LANES, jnp.concatenate([sel_neg] * HP, axis=0), qtab)
    qsel = jnp.concatenate([qs, qx], axis=1)

    m_sc[...] = jnp.full_like(m_sc, NEG)
    acc_sc[...] = jnp.zeros_like(acc_sc)
    ones = jnp.ones((tk, LANES), BF16)

    def tile(j, causal, r0=0):
        ks = pl.multiple_of(j * tk, tk)
        k = jnp.concatenate([ks_ref[pl.ds(ks, tk), :], ktab_ref[pl.ds(ks, tk), :]], axis=1)
        spans = [slice(h * tq + r0, (h + 1) * tq) for h in range(HP)]

        def rows(x):
            return x if r0 == 0 else jnp.concatenate([x[s] for s in spans], axis=0)

        sc = _dot_nt(rows(qsel), k)
        if causal:
            kpos = (ks + lax.broadcasted_iota(jnp.int32, (1, tk), 1)).astype(F32)
            sc = jnp.where(kpos <= rows(tcol), sc, NEG)
        m_old = rows(m_sc[...])
        m_new = jnp.maximum(m_old, jnp.max(sc, axis=-1, keepdims=True))
        alpha = jnp.exp2(m_old - m_new)
        pp = jnp.exp2(sc - jnp.concatenate([m_new] * (tk // LANES), axis=1))
        vv = jnp.concatenate([vs_ref[pl.ds(ks, tk), :], ones], axis=1)
        acc_new = (jnp.concatenate([alpha, alpha], axis=1) * rows(acc_sc[...])
                   + jnp.dot(pp.astype(BF16), vv, preferred_element_type=F32))
        if r0 == 0:
            acc_sc[...] = acc_new
            m_sc[...] = m_new
        else:
            n = tq - r0
            for h, s in enumerate(spans):
                acc_sc[s] = acc_new[h * n:(h + 1) * n]
                m_sc[s] = m_new[h * n:(h + 1) * n]

    def body(j, carry):
        tile(j, False)
        return carry

    n_full = t0 // tk
    lax.fori_loop(0, n_full, body, 0)
    for c in range(tq // tk):
        tile(n_full + c, True, r0=c * tk)
    acc = acc_sc[...]
    o_sel = acc[:, :NSA_DK] / acc[:, NSA_DK:]

    span = WINDOW + WIN_ROWS
    ones_w = jnp.ones((span, LANES), BF16)
    o_blocks = []
    for i in range(tq // WIN_ROWS):
        b0 = t0 + i * WIN_ROWS
        ws = pl.multiple_of(jnp.maximum(b0 - WINDOW, 0), WIN_ROWS)
        qi_rows = jnp.concatenate(
            [qa[h * tq + i * WIN_ROWS:h * tq + (i + 1) * WIN_ROWS] for h in range(HP)], axis=0)
        kw = jnp.concatenate([kw_ref[pl.ds(ws, span), :], ktab_ref[pl.ds(ws, span), :]], axis=1)
        sw = _dot_nt(qi_rows, kw) + jnp.concatenate([wbias_ref[(b0 - ws) // WIN_ROWS]] * HP, axis=0)
        pw = jnp.exp2(sw - jnp.max(sw, axis=-1, keepdims=True))
        vw = jnp.concatenate([vw_ref[pl.ds(ws, span), :], ones_w], axis=1)
        rw = jnp.dot(pw.astype(BF16), vw, preferred_element_type=F32)
        o_blocks.append(rw[:, :NSA_DK] / rw[:, NSA_DK:])
    o_win = jnp.concatenate(
        [o_blocks[i][h * WIN_ROWS:(h + 1) * WIN_ROWS]
         for h in range(HP) for i in range(tq // WIN_ROWS)], axis=0)

    gt = jax.nn.sigmoid(gt_ref[...])

    def gcol(i):
        return jnp.concatenate([gt[:, i * HP + h:i * HP + h + 1] for h in range(HP)], axis=0)

    o = gcol(0) * o_cmp + gcol(1) * o_sel + gcol(2) * o_win
    o_ref[...] = jnp.concatenate([o[h * tq:(h + 1) * tq] for h in range(HP)], axis=1).astype(o_ref.dtype)


def _nsa_attn(q, gates, kvb, kvc, tabs, B, S, tq=128, tk=256):
    T = q.shape[0]
    G, DK, HP = NSA_GROUPS, NSA_DK, NSA_HPG
    qtab, ktab, ctab, mt, cbias, wbias = tabs
    nqt = S // tq
    n_chunk = kvc.shape[3]
    n_sel = S // SEL_BLOCK

    def kv_spec(i):
        return pl.BlockSpec((S, DK), lambda b, g, t, i=i: (b, i * G + g))

    def cmp_spec(i):
        return pl.BlockSpec((None, None, None, n_chunk, DK), lambda b, g, t, i=i: (b, i, g, 0, 0))

    return pl.pallas_call(
        functools.partial(_nsa_attn_kernel, tq=tq, tk=tk, n_sel=n_sel),
        grid=(B, G, nqt),
        in_specs=[
            pl.BlockSpec((tq, HP * DK), lambda b, g, t: (b * nqt + t, g)),
            pl.BlockSpec((tq, LANES), lambda b, g, t: (b * nqt + t, g)),
            kv_spec(0), kv_spec(1), kv_spec(2), kv_spec(3),
            cmp_spec(0), cmp_spec(1),
            pl.BlockSpec((None, HP * tq, LANES), lambda b, g, t: (g, 0, 0)),
            pl.BlockSpec(ktab.shape, lambda b, g, t: (0, 0)),
            pl.BlockSpec(ctab.shape, lambda b, g, t: (0, 0)),
            pl.BlockSpec(mt.shape, lambda b, g, t: (0, 0)),
            pl.BlockSpec((tq, n_chunk), lambda b, g, t: (t, 0)),
            pl.BlockSpec(wbias.shape, lambda b, g, t: (0, 0, 0)),
        ],
        out_specs=pl.BlockSpec((tq, HP * DK), lambda b, g, t: (b * nqt + t, g)),
        out_shape=jax.ShapeDtypeStruct((T, NSA_HEADS * DK), BF16),
        scratch_shapes=[pltpu.VMEM((HP * tq, LANES), F32),
                        pltpu.VMEM((HP * tq, DK + LANES), F32)],
        compiler_params=_cparams(3),
        name="nsa_attn",
    )(q, gates, kvb, kvb, kvb, kvb, kvc, kvc, qtab, ktab, ctab, mt, cbias, wbias)


def _row_copy(src_hbm, dst_vmem, sem, src_row, dst_row):
    return pltpu.make_async_copy(src_hbm.at[pl.ds(src_row, 1), :], dst_vmem.at[pl.ds(dst_row, 1), :], sem)


def _dispatch_kernel(pos_ref, src_ref, init_hbm, out_hbm, sem, *, chunk, n_tok):
    del init_hbm
    base = pl.program_id(0) * chunk

    def start(r, c):
        t = base + r
        _row_copy(src_ref, out_hbm, sem, r, pos_ref[t]).start()
        _row_copy(src_ref, out_hbm, sem, r, pos_ref[n_tok + t]).start()
        return c

    def wait(r, c):
        _row_copy(src_ref, out_hbm, sem, 0, 0).wait()
        return c

    lax.fori_loop(0, chunk, start, 0, unroll=8)
    lax.fori_loop(0, 2 * chunk, wait, 0, unroll=8)


def _dispatch(h_packed, pos, n_rows, chunk=512):
    T, W = h_packed.shape
    n_steps = T // chunk
    init = jnp.zeros((n_rows, W), h_packed.dtype)
    return pl.pallas_call(
        functools.partial(_dispatch_kernel, chunk=chunk, n_tok=T),
        grid_spec=pltpu.PrefetchScalarGridSpec(
            num_scalar_prefetch=1,
            grid=(n_steps,),
            in_specs=[pl.BlockSpec((chunk, W), lambda i, pos: (i, 0)),
                      pl.BlockSpec(memory_space=pl.ANY)],
            out_specs=pl.BlockSpec(memory_space=pl.ANY),
            scratch_shapes=[pltpu.SemaphoreType.DMA(())],
        ),
        out_shape=jax.ShapeDtypeStruct((n_rows, W), h_packed.dtype),
        input_output_aliases={2: 0},
        compiler_params=_cparams(1),
        name="moe_dispatch",
    )(pos, h_packed, init)


def _combine_kernel(pos_ref, y_hbm, x_ref, gate_ref, rt_ref, *refs, tg, n_tok, n_steps,
                    norm_cfgs):
    n_norm_in = sum(_norm_counts(c)[0] for c in norm_cfgs)
    n_norm_out = sum(_norm_counts(c)[1] for c in norm_cfgs)
    norm_in, (o_ref, *refs) = refs[:n_norm_in], refs[n_norm_in:]
    norm_out, (buf, sem) = refs[:n_norm_out], refs[n_norm_out:]
    i = pl.program_id(0)
    slot = i % 2

    def issue(step, slot_):
        def start(r, c):
            t = step * tg + r
            _row_copy(y_hbm, buf.at[slot_, 0], sem.at[slot_], pos_ref[t], r).start()
            _row_copy(y_hbm, buf.at[slot_, 1], sem.at[slot_], pos_ref[n_tok + t], r).start()
            return c

        lax.fori_loop(0, tg, start, 0, unroll=8)

    @pl.when(i == 0)
    def _():
        issue(0, 0)

    @pl.when(i + 1 < n_steps)
    def _():
        issue(i + 1, 1 - slot)

    def wait(r, c):
        _row_copy(y_hbm, buf.at[slot, 0], sem.at[slot], 0, 0).wait()
        return c

    lax.fori_loop(0, 2 * tg, wait, 0, unroll=8)
    rt = rt_ref[...]
    y = rt[:, 2:3] * buf[slot, 0] + rt[:, 3:4] * buf[slot, 1]
    x_new = x_ref[...] + gate_ref[...] * y
    o_ref[...] = x_new
    _apply_norms(x_new, norm_in, norm_out, norm_cfgs)


def _combine(x2, y_sorted, pos, route, gate, S, norms=(), tg=256):
    T, D = x2.shape
    garr, gl, gj = gate
    per_b = S // tg
    n_steps = T // tg
    args = [y_sorted, x2, garr, route]
    in_specs = [
        pl.BlockSpec(memory_space=pl.ANY),
        pl.BlockSpec((tg, D), lambda i, pos: (i, 0)),
        pl.BlockSpec((None, None, None, 1, D), lambda i, pos: (gl, i // per_b, gj, 0, 0)),
        pl.BlockSpec((tg, LANES), lambda i, pos: (i, 0)),
    ]
    out_shape = [jax.ShapeDtypeStruct((T, D), F32)]
    out_specs = [pl.BlockSpec((tg, D), lambda i, pos: (i, 0))]
    for ns in norms:
        n_args, n_in, n_shape, n_out_specs = _norm_io(
            ns, T, D, tg, lambda i, *_: i // per_b, lambda i, *_: i)
        args += n_args
        in_specs += n_in
        out_shape += n_shape
        out_specs += n_out_specs
    outs = pl.pallas_call(
        functools.partial(_combine_kernel, tg=tg, n_tok=T, n_steps=n_steps,
                          norm_cfgs=tuple(_norm_cfg(ns) for ns in norms)),
        grid_spec=pltpu.PrefetchScalarGridSpec(
            num_scalar_prefetch=1,
            grid=(n_steps,),
            in_specs=in_specs,
            out_specs=out_specs,
            scratch_shapes=[pltpu.VMEM((2, 2, tg, D), F32), pltpu.SemaphoreType.DMA((2,))],
        ),
        out_shape=out_shape,
        input_output_aliases={2: 0},
        compiler_params=_cparams(1),
        name="moe_combine",
    )(pos, *args)
    return outs if norms else outs[0]


def _moe_plan(route, T, tm):
    E = N_EXPERTS
    n_tiles = (2 * T) // tm + E
    P = n_tiles * tm
    e_pair = jnp.concatenate([route[:, 0], route[:, 1]]).astype(jnp.int32)
    onehot = (e_pair[:, None] == jnp.arange(E, dtype=jnp.int32)[None, :]).astype(jnp.int32)
    csum = jnp.cumsum(onehot, axis=0)
    rank = jnp.sum(onehot * (csum - 1), axis=1)
    counts = csum[-1]
    padded = ((counts + tm - 1) // tm) * tm
    ends = jnp.cumsum(padded)
    pos = (ends - padded)[e_pair] + rank
    tile_start = jnp.arange(n_tiles, dtype=jnp.int32) * tm
    tile_valid = (tile_start < ends[-1]).astype(jnp.int32)
    tile_expert = jnp.sum((tile_start[:, None] >= ends[None, :]).astype(jnp.int32), axis=1)
    last_valid = tile_expert[jnp.maximum(ends[-1] // tm - 1, 0)]
    tile_expert = jnp.where(tile_valid == 1, tile_expert, last_valid).astype(jnp.int32)
    return pos.astype(jnp.int32), P, tile_expert, tile_valid


def _rope_tables(S):
    d = QK_ROPE
    inv = ROPE_THETA ** (-jnp.arange(0, d, 2, dtype=F32) / d)
    ang = jnp.arange(S).astype(F32)[:, None] * inv[None, :]
    cos, sin = jnp.cos(ang), jnp.sin(ang)
    z = jnp.zeros_like(sin)
    return (jnp.concatenate([cos, cos, cos, cos], axis=1),
            jnp.concatenate([-sin, z, -sin, z], axis=1),
            jnp.concatenate([z, sin, z, sin], axis=1))


def _pos_columns(pos):
    tab = np.zeros((pos.shape[0], LANES), np.float32)
    tab[:, POS_HI_LANE:POS_HI_LANE + 3] = (LANES * (pos // LANES))[:, None]
    tab[:, POS_LO_LANE:POS_LO_LANE + 3] = (pos % LANES)[:, None]
    return tab


def _nsa_tables(S, tq):
    n_cmp = (S - CMP_BLOCK) // CMP_STRIDE + 1
    n_sel = S // SEL_BLOCK
    n_chunk = S // CMP_STRIDE
    tok = np.arange(n_cmp)[:, None] * CMP_STRIDE + np.arange(CMP_BLOCK)[None, :]
    blk = tok // SEL_BLOCK
    m = (blk[:, :, None] == np.arange(n_sel)[None, None, :]).sum(axis=1) / CMP_BLOCK
    mt = np.zeros((n_sel, n_chunk), np.float32)
    mt[:, :n_cmp] = m.T
    keys = np.arange(S)
    ktab = _pos_columns(keys)
    ktab[:, :SEL_LANES] = (keys[:, None] // SEL_BLOCK == np.arange(SEL_LANES)[None, :])
    ctab = _pos_columns(np.arange(n_chunk) * CMP_STRIDE + (CMP_BLOCK - 1))
    slopes = jnp.asarray(2.0 ** (-8.0 * np.arange(1, NSA_HEADS + 1) / NSA_HEADS), F32)
    a = slopes * LOG2E
    a_hi = a.astype(BF16)
    r1 = a - a_hi.astype(F32)
    a_mid = r1.astype(BF16)
    a_lo = (r1 - a_mid.astype(F32)).astype(BF16)
    pieces = jnp.stack([a_hi, a_mid, a_lo], axis=-1)
    qrow = jnp.zeros((NSA_HEADS, LANES), BF16)
    qrow = qrow.at[:, POS_HI_LANE:POS_HI_LANE + 3].set(pieces)
    qrow = qrow.at[:, POS_LO_LANE:POS_LO_LANE + 3].set(pieces)
    qtab = jnp.repeat(qrow.reshape(NSA_GROUPS, NSA_HPG, LANES), tq, axis=1)
    cend = np.arange(n_chunk) * CMP_STRIDE + (CMP_BLOCK - 1)
    cvis = (keys[:, None] >= cend[None, :]) & (np.arange(n_chunk)[None, :] < n_cmp)
    cbias = np.where(cvis, 0.0, NEG).astype(np.float32)
    span = WINDOW + WIN_ROWS
    d = (np.arange(WINDOW // WIN_ROWS + 1)[:, None, None] * WIN_ROWS
         + np.arange(WIN_ROWS)[None, :, None] - np.arange(span)[None, None, :])
    wbias = np.where((d >= 0) & (d < WINDOW), 0.0, NEG).astype(np.float32)
    return (qtab, jnp.asarray(ktab, BF16), jnp.asarray(ctab, BF16), jnp.asarray(mt, BF16),
            jnp.asarray(cbias), jnp.asarray(wbias))


def kernel(x, c, ada_w, ada_b, norm1_g, norm2_g, mla_w_in, mla_g_q, mla_g_kv, mla_w_uq, mla_w_ukv, mla_w_o, kv_ada_w, kv_ada_b, kv_norm_g, nsa_w_kv, cmp_pos_k, cmp_pos_v, cmp_k_w1, cmp_k_w2, cmp_v_w1, cmp_v_w2, nsa_w_in, nsa_w_o, ffn_w_gate, ffn_w_up, ffn_w_down, moe_w_router, moe_b_router, moe_w_gate, moe_w_up, moe_w_down, final_g):
    B, S, D = x.shape
    T = B * S
    depth = ada_w.shape[0]
    n_a = mla_w_in.shape[0]
    H = MLA_HEADS
    G, HP, DK = NSA_GROUPS, NSA_HPG, NSA_DK
    d_ff = ffn_w_gate.shape[-1]

    c_pad = jnp.zeros((8, D), F32).at[:B].set(c)
    mod = _modulation(c_pad, ada_w, ada_b)[:, :B].reshape(depth, B, 6, 1, D)
    kv_mod = _modulation(c_pad, kv_ada_w[None], kv_ada_b[None])[:, :B].reshape(1, B, 2, 1, D)

    rope_tabs = _rope_tables(S)
    nsa_tabs = _nsa_tables(S, NSA_TQ)

    ffn_wg = ffn_w_gate[:, None]
    ffn_wu = ffn_w_up[:, None]
    ffn_wd = ffn_w_down[:, None]

    def norm1_spec(l):
        return _norm_spec(norm1_g[l], (mod, l, 0), (mod, l, 1))

    def swiglu(a, ws, li, tm, **kw):
        wide = (d_ff // WIDE_TN) * WIDE_TN
        tail = d_ff - wide
        main = _gmm(a, ws, li, mode="swiglu", tm=tm, tn=WIDE_TN, n_out=wide, out_dtype=BF16, **kw)
        rest = _gmm(a, ws, li, mode="swiglu", tm=tm, tn=tail, n_out=tail, out_dtype=BF16,
                    n_off=wide // tail, **kw)
        return main, rest

    kv_spec = _norm_spec(kv_norm_g, (kv_mod, 0, 0), (kv_mod, 0, 1))
    final_spec = _norm_spec(final_g, out_dtype=F32)

    x2 = x.reshape(T, D)
    shared = None
    h = _norm(x2, norm1_spec(0), B, S)
    for l in range(depth):
        dense = l % 2 == 0
        if dense:
            norm2 = _norm_spec(norm2_g[l], (mod, l, 3), (mod, l, 4))
        else:
            wr = jnp.pad(moe_w_router[l // 2], ((0, 0), (0, LANES - N_EXPERTS)))
            br = jnp.pad(moe_b_router[l // 2], (0, LANES - N_EXPERTS)).reshape(1, LANES)
            norm2 = _norm_spec(norm2_g[l], (mod, l, 3), (mod, l, 4), router=(wr, br))
        fused2 = [norm2] if dense else []
        if l < n_a:
            w_in = mla_w_in[l]
            q_lora, kv_lora = mla_g_q.shape[1], mla_g_kv.shape[1]
            w_pad = jnp.concatenate([w_in, w_in[:, q_lora + kv_lora:]], axis=1)
            cq, ckv, kpe = _mla_in(h, w_pad, mla_g_q[l], mla_g_kv[l], rope_tabs, S)
            wq = mla_w_uq[l].reshape(q_lora, H, QK_NOPE + QK_ROPE)
            wq = jnp.concatenate([wq[:, :, :QK_NOPE].reshape(q_lora, H * QK_NOPE),
                                  wq[:, :, QK_NOPE:].reshape(q_lora, H * QK_ROPE)], axis=1)
            q_all = _gmm(cq, [wq[None, None]], 0, mode="cast", tm=LORA_TM, tn=WIDE_TN,
                         n_out=wq.shape[1], out_dtype=BF16,
                         out_scale=(QK_NOPE + QK_ROPE) ** -0.5 * LOG2E)
            kv_all = _gmm(ckv, [mla_w_ukv[:, None]], l, mode="cast", tm=LORA_TM, tn=WIDE_TN,
                          n_out=mla_w_ukv.shape[-1], out_dtype=BF16)
            o = _mla_attn(q_all, kv_all, kpe, rope_tabs, B, S)
            x2, *h2 = _gmm(o, [mla_w_o[:, None]], l, mode="residual", tm=OUT_PROJ_TM, tn=D, n_out=D,
                           out_dtype=F32, xres=x2, gate=(mod, l, 2), seq=S, norms=fused2,
                           alias_x=l > 0)
        else:
            jb = l - n_a
            w_in = nsa_w_in[jb]
            q = _gmm(h, [nsa_w_in[:, None]], jb, mode="cast", tm=DENSE_TM, tn=WIDE_TN,
                     n_out=NSA_HEADS * DK, out_dtype=BF16, out_scale=DK ** -0.5 * LOG2E)
            wg = w_in[:, NSA_HEADS * DK:].reshape(D, G, HP, 3).transpose(0, 1, 3, 2)
            wg = jnp.pad(wg.reshape(D, G, 3 * HP), ((0, 0), (0, 0), (0, LANES - 3 * HP)))
            gates = _gmm(h, [wg.reshape(1, 1, D, G * LANES)], 0, mode="cast", tm=DENSE_TM,
                         tn=NARROW_TN,
                         n_out=G * LANES, out_dtype=F32)
            kvb, kvc = shared
            o = _nsa_attn(q, gates, kvb, kvc, nsa_tabs, B, S, tq=NSA_TQ)
            x2, *h2 = _gmm(o, [nsa_w_o[:, None]], jb, mode="residual", tm=OUT_PROJ_TM, tn=D, n_out=D,
                           out_dtype=F32, xres=x2, gate=(mod, l, 2), seq=S, norms=fused2)

        after = [norm1_spec(l + 1) if l + 1 < depth else final_spec]
        if l == n_a - 1:
            after.append(kv_spec)
        if dense:
            hid, hid_tail = swiglu(h2[0], [ffn_wg, ffn_wu], l // 2, DENSE_TM)
            x2 = _gmm(hid, [ffn_wd], l // 2, mode="residual", tm=DENSE_TM, tn=NARROW_TN, n_out=D,
                      out_dtype=F32, xres=x2, gate=(mod, l, 5), seq=S, a2=hid_tail)
            normed = [_norm(x2, ns, B, S) for ns in after]
        else:
            li = l // 2
            hp, route = _norm(x2, norm2, B, S)
            pos, n_rows, tile_expert, tile_valid = _moe_plan(route, T, MOE_TM)
            hs = _dispatch(hp, pos, n_rows)
            hid, hid_tail = swiglu(hs, [moe_w_gate, moe_w_up], li, MOE_TM, tile_expert=tile_expert,
                                   tile_valid=tile_valid, a_packed=True)
            ys = _gmm(hid, [moe_w_down], li, mode="cast", tm=MOE_TM, tn=WIDE_TN, n_out=D,
                      out_dtype=F32, tile_expert=tile_expert, tile_valid=tile_valid, a2=hid_tail)
            x2, *normed = _combine(x2, ys, pos, route, (mod, l, 5), S, norms=after)
        h = normed[0]

        if l == n_a - 1:
            hkv = normed[1]
            w_kv = nsa_w_kv[None, None]
            kc = _gmm(hkv, [w_kv], 0, mode="cast", tm=DENSE_TM, tn=NARROW_TN, n_out=2 * G * DK,
                      out_dtype=F32)
            kvb = _gmm(hkv, [w_kv], 0, mode="cast", tm=DENSE_TM, tn=WIDE_TN, n_out=4 * G * DK,
                       out_dtype=BF16, n_off=(2 * G * DK) // WIDE_TN)
            kvc = _nsa_compress(kc, jnp.stack([cmp_pos_k, cmp_pos_v]),
                                jnp.stack([cmp_k_w1, cmp_v_w1]), jnp.stack([cmp_k_w2, cmp_v_w2]),
                                B, S)
            shared = (kvb, kvc)

    return h.reshape(B, S, D)
```

```python
import functools

import numpy as np
import jax
import jax.numpy as jnp
from jax import lax
from jax.experimental import pallas as pl
from jax.experimental.pallas import tpu as pltpu

F32 = jnp.float32
BF16 = jnp.bfloat16

EPS = 1e-6
NEG = -1e30
FORCED_SCORE = 1e6
LOG2E = 1.4426950408889634

MLA_HEADS = 16
QK_NOPE = 128
QK_ROPE = 64
V_HEAD = 128
ROPE_THETA = 10000.0

NSA_HEADS = 16
NSA_GROUPS = 4
NSA_HPG = NSA_HEADS // NSA_GROUPS
NSA_DK = 128
CMP_BLOCK = 32
CMP_STRIDE = 16
SEL_BLOCK = 64
SEL_TOPN = 16
WINDOW = 512

N_EXPERTS = 8
LANES = 128
VMEM_LIMIT = 56 * 1024 * 1024

MOE_TM = 512
DENSE_TM = 1024
NARROW_TN = 512
WIDE_TN = 1024
LORA_TM = 2048
OUT_PROJ_TM = 512
NSA_TQ = 512
WIN_ROWS = 128

SEL_LANES = 32
POS_HI_LANE = 32
POS_LO_LANE = 35
MASK_BIG = 2.0 ** 30


def _cparams(n_axes):
    return pltpu.CompilerParams(
        dimension_semantics=("arbitrary",) * n_axes, vmem_limit_bytes=VMEM_LIMIT)


def _dot_nt(a, b):
    return lax.dot_general(a, b, (((1,), (1,)), ((), ())), preferred_element_type=F32)


def _mod_kernel(c_ref, w_ref, b_ref, o_ref):
    c = c_ref[...]
    ca = (c * jax.nn.sigmoid(c)).astype(BF16)
    o_ref[...] = jnp.dot(ca, w_ref[...].astype(BF16), preferred_element_type=F32) + b_ref[...]


def _modulation(c_pad, w, b, tn=2048):
    L, D, N = w.shape
    return pl.pallas_call(
        _mod_kernel,
        grid=(L, N // tn),
        in_specs=[
            pl.BlockSpec((8, D), lambda l, n: (0, 0)),
            pl.BlockSpec((None, D, tn), lambda l, n: (l, 0, n)),
            pl.BlockSpec((None, 1, tn), lambda l, n: (l, 0, n)),
        ],
        out_specs=pl.BlockSpec((None, 8, tn), lambda l, n: (l, 0, n)),
        out_shape=jax.ShapeDtypeStruct((L, 8, N), F32),
        compiler_params=_cparams(2),
        name="adaln_mod",
    )(c_pad, w, b.reshape(L, 1, N))


def _rms(x, g):
    return x * lax.rsqrt(jnp.mean(x * x, axis=-1, keepdims=True) + EPS) * g


_HI16 = 0xFFFF0000


def _pack_bf16_halves(h):
    half = h.shape[1] // 2
    bits = lax.bitcast_convert_type(h.astype(BF16).astype(F32), jnp.uint32)
    return (bits[:, :half] >> 16) | (bits[:, half:] & jnp.uint32(_HI16))


def _unpack_bf16_halves(a):
    lo = lax.bitcast_convert_type(a << 16, F32).astype(BF16)
    hi = lax.bitcast_convert_type(a & jnp.uint32(_HI16), F32).astype(BF16)
    return jnp.concatenate([lo, hi], axis=1)


def _norm_spec(g, shift=None, scale=None, router=None, out_dtype=BF16):
    return dict(g=g, shift=shift, scale=scale, router=router, out_dtype=out_dtype)


def _norm_cfg(ns):
    return (ns["shift"] is not None, ns["router"] is not None)


def _norm_io(ns, T, D, tm, batch_of, row_of):
    modulate, route = _norm_cfg(ns)
    args = [ns["g"].reshape(1, D)]
    in_specs = [pl.BlockSpec((1, D), lambda *a: (0, 0))]
    if modulate:
        for arr, l, j in (ns["shift"], ns["scale"]):
            args.append(arr)
            in_specs.append(pl.BlockSpec((None, None, None, 1, D),
                                         lambda *a, l=l, j=j: (l, batch_of(*a), j, 0, 0)))
    if route:
        wr, br = ns["router"]
        args += [wr, br]
        in_specs += [pl.BlockSpec((D, LANES), lambda *a: (0, 0)),
                     pl.BlockSpec((1, LANES), lambda *a: (0, 0))]
        out_shape = [jax.ShapeDtypeStruct((T, D // 2), jnp.uint32),
                     jax.ShapeDtypeStruct((T, LANES), F32)]
        out_specs = [pl.BlockSpec((tm, D // 2), lambda *a: (row_of(*a), 0)),
                     pl.BlockSpec((tm, LANES), lambda *a: (row_of(*a), 0))]
    else:
        out_shape = [jax.ShapeDtypeStruct((T, D), ns["out_dtype"])]
        out_specs = [pl.BlockSpec((tm, D), lambda *a: (row_of(*a), 0))]
    return args, in_specs, out_shape, out_specs


def _norm_counts(cfg):
    modulate, route = cfg
    return 1 + 2 * modulate + 2 * route, 1 + route


def _norm_apply(x, in_refs, out_refs, cfg):
    modulate, route = cfg
    g_ref, *rest = in_refs
    h = _rms(x, g_ref[...])
    if modulate:
        sh_ref, sc_ref, *rest = rest
        h = h * (1.0 + sc_ref[...]) + sh_ref[...]
    if not route:
        out_refs[0][...] = h.astype(out_refs[0].dtype)
        return
    wr_ref, br_ref = rest
    out_refs[0][...] = _pack_bf16_halves(h)
    w = wr_ref[...]
    h_hi, w_hi = h.astype(BF16), w.astype(BF16)
    h_lo = (h - h_hi.astype(F32)).astype(BF16)
    w_lo = (w - w_hi.astype(F32)).astype(BF16)
    logits = (jnp.dot(h_hi, w_hi, preferred_element_type=F32)
              + jnp.dot(h_lo, w_hi, preferred_element_type=F32)
              + jnp.dot(h_hi, w_lo, preferred_element_type=F32)) + br_ref[...]
    lane = lax.broadcasted_iota(jnp.int32, logits.shape, 1).astype(F32)
    logits = jnp.where(lane < N_EXPERTS, logits, -jnp.inf)
    v1 = jnp.max(logits, axis=-1, keepdims=True)
    i1 = jnp.min(jnp.where(logits == v1, lane, float(LANES)), axis=-1, keepdims=True)
    others = jnp.where(lane == i1, -jnp.inf, logits)
    v2 = jnp.max(others, axis=-1, keepdims=True)
    i2 = jnp.min(jnp.where(others == v2, lane, float(LANES)), axis=-1, keepdims=True)
    e = jnp.exp(v2 - v1)
    w1 = 1.0 / (1.0 + e)
    w2 = e / (1.0 + e)
    out_refs[1][...] = jnp.where(lane == 0, i1, jnp.where(lane == 1, i2, jnp.where(
        lane == 2, w1, jnp.where(lane == 3, w2, 0.0))))


def _apply_norms(x, refs_in, refs_out, cfgs):
    for cfg in cfgs:
        n_in, n_out = _norm_counts(cfg)
        _norm_apply(x, refs_in[:n_in], refs_out[:n_out], cfg)
        refs_in, refs_out = refs_in[n_in:], refs_out[n_out:]


def _norm_kernel(x_ref, *refs, cfg):
    n_in, _ = _norm_counts(cfg)
    _norm_apply(x_ref[...], refs[:n_in], refs[n_in:], cfg)


def _norm(x2, ns, B, S, ts=512):
    T, D = x2.shape
    nst = S // ts
    args, in_specs, out_shape, out_specs = _norm_io(
        ns, T, D, ts, lambda b, s: b, lambda b, s: b * nst + s)
    outs = pl.pallas_call(
        functools.partial(_norm_kernel, cfg=_norm_cfg(ns)),
        grid=(B, nst),
        in_specs=[pl.BlockSpec((ts, D), lambda b, s: (b * nst + s, 0))] + in_specs,
        out_specs=out_specs,
        out_shape=out_shape,
        compiler_params=_cparams(2),
        name="norm_mod",
    )(x2, *args)
    return outs if len(outs) > 1 else outs[0]


def _gmm_kernel(te_ref, tv_ref, nx_ref, a_ref, *refs, mode, out_scale, a_packed, n_w, layer,
                n_off, tn, n_n, norm_cfgs, k_split):
    n = pl.program_id(0)
    m = pl.program_id(1)
    if k_split is not None:
        a2_ref, *refs = refs
    w_hbm, refs = refs[:n_w], refs[n_w:]
    if mode == "residual":
        x_ref, gate_ref, *refs = refs
    n_norm_in = sum(_norm_counts(c)[0] for c in norm_cfgs)
    n_norm_out = sum(_norm_counts(c)[1] for c in norm_cfgs)
    norm_in, refs = refs[:n_norm_in], refs[n_norm_in:]
    o_ref, *refs = refs
    norm_out, refs = refs[:n_norm_out], refs[n_norm_out:]
    wst, wb, sem = refs[:n_w], refs[n_w:2 * n_w], refs[2 * n_w]
    first = jnp.logical_or(m == 0, te_ref[m] != te_ref[jnp.maximum(m - 1, 0)])
    valid = tv_ref[m] != 0

    def w_copy(i, e, nn):
        col = pl.multiple_of((nn + n_off) * tn, tn)
        return pltpu.make_async_copy(w_hbm[i].at[layer, e, :, pl.ds(col, tn)], wst[i], sem.at[i])

    @pl.when(jnp.logical_and(n == 0, m == 0))
    def _():
        for i in range(n_w):
            w_copy(i, te_ref[0], 0).start()

    @pl.when(first)
    def _():
        for i in range(n_w):
            w_copy(i, te_ref[m], n).wait()
            wb[i][...] = wst[i][...].astype(BF16)
        nm = nx_ref[m]
        same_sweep = nm >= 0
        e_next = jnp.where(same_sweep, te_ref[jnp.maximum(nm, 0)], te_ref[0])
        n_next = jnp.where(same_sweep, n, n + 1)

        @pl.when(jnp.logical_or(same_sweep, n + 1 < n_n))
        def _():
            for i in range(n_w):
                w_copy(i, e_next, n_next).start()

    @pl.when(valid)
    def _():
        a = _unpack_bf16_halves(a_ref[...]) if a_packed else a_ref[...]
        if mode == "swiglu":
            g = jnp.dot(a, wb[0][...], preferred_element_type=F32)
            u = jnp.dot(a, wb[1][...], preferred_element_type=F32)
            acc = g * jax.nn.sigmoid(g) * u
        elif k_split is None:
            acc = jnp.dot(a, wb[0][...], preferred_element_type=F32)
        else:
            acc = (jnp.dot(a, wb[0][:k_split], preferred_element_type=F32)
                   + jnp.dot(a2_ref[...], wb[0][k_split:], preferred_element_type=F32))
        if out_scale is not None:
            acc = acc * out_scale
        if mode == "residual":
            acc = x_ref[...] + gate_ref[...] * acc
        o_ref[...] = acc.astype(o_ref.dtype)
        _apply_norms(acc, norm_in, norm_out, norm_cfgs)

    @pl.when(jnp.logical_not(valid))
    def _():
        o_ref[...] = jnp.zeros_like(o_ref)


def _gmm(a, ws, l, *, mode, tm, tn, n_out, out_dtype, tile_expert=None, tile_valid=None,
         n_off=0, xres=None, gate=None, seq=None, out_scale=None, a_packed=False,
         alias_x=True, norms=None, a2=None):
    M, ka = a.shape
    K = ws[0].shape[2]
    mt = M // tm
    n_w = len(ws)
    n_n = n_out // tn
    if tile_expert is None:
        tile_expert = jnp.zeros((mt,), jnp.int32)
        tile_valid = jnp.ones((mt,), jnp.int32)
    idx = jnp.arange(mt, dtype=jnp.int32)
    later = ((idx[None, :] > idx[:, None]) & (tile_valid[None, :] != 0)
             & (tile_expert[None, :] != tile_expert[:, None]))
    next_run = jnp.where(jnp.any(later, axis=1), jnp.argmax(later, axis=1), -1).astype(jnp.int32)
    lhs = [a] if a2 is None else [a, a2]
    args = lhs + list(ws)
    in_specs = [pl.BlockSpec((tm, t.shape[1]), lambda n, m, te, tv, nx: (m, 0)) for t in lhs]
    in_specs += [pl.BlockSpec(memory_space=pl.ANY)] * n_w
    aliases = {}
    if mode == "residual":
        garr, gl, gj = gate
        per_b = seq // tm
        if alias_x:
            aliases = {3 + len(args): 0}
        args += [xres, garr]
        in_specs += [
            pl.BlockSpec((tm, tn), lambda n, m, te, tv, nx: (m, n)),
            pl.BlockSpec((None, None, None, 1, tn),
                         lambda n, m, te, tv, nx: (gl, m // per_b, gj, 0, n)),
        ]
    out_shape = [jax.ShapeDtypeStruct((M, n_out), out_dtype)]
    out_specs = [pl.BlockSpec((tm, tn), lambda n, m, te, tv, nx: (m, n))]
    bare = norms is None
    norms = norms or []
    for ns in norms:
        assert tn == n_out
        n_args, n_in, n_shape, n_out_specs = _norm_io(
            ns, M, n_out, tm, lambda n, m, *_: m // (seq // tm), lambda n, m, *_: m)
        args += n_args
        in_specs += n_in
        out_shape += n_shape
        out_specs += n_out_specs
    outs = pl.pallas_call(
        functools.partial(_gmm_kernel, mode=mode, out_scale=out_scale, a_packed=a_packed,
                          n_w=n_w, layer=l, n_off=n_off, tn=tn, n_n=n_n,
                          norm_cfgs=tuple(_norm_cfg(ns) for ns in norms),
                          k_split=None if a2 is None else ka),
        grid_spec=pltpu.PrefetchScalarGridSpec(
            num_scalar_prefetch=3,
            grid=(n_n, mt),
            in_specs=in_specs,
            out_specs=out_specs,
            scratch_shapes=([pltpu.VMEM((K, tn), F32) for _ in ws]
                            + [pltpu.VMEM((K, tn), BF16) for _ in ws]
                            + [pltpu.SemaphoreType.DMA((n_w,))]),
        ),
        out_shape=out_shape,
        input_output_aliases=aliases,
        compiler_params=_cparams(2),
        name="gmm_" + mode,
    )(tile_expert, tile_valid, next_run, *args)
    return outs[0] if bare else outs


def _rope_pairs(v, cos, s1, s2):
    return v * cos + pltpu.roll(v, LANES - QK_ROPE // 2, 1) * s1 + pltpu.roll(v, QK_ROPE // 2, 1) * s2


def _mla_in_kernel(a_ref, w_ref, gq_ref, gkv_ref, cos_ref, s1_ref, s2_ref,
                   cq_ref, ckv_ref, kpe_ref, wb, *, q_lora, kv_lora):
    @pl.when(pl.program_id(0) == 0)
    def _():
        wb[...] = w_ref[...].astype(BF16)

    acc = jnp.dot(a_ref[...], wb[...], preferred_element_type=F32)
    cq_ref[...] = _rms(acc[:, :q_lora], gq_ref[...]).astype(BF16)
    ckv_ref[...] = _rms(acc[:, q_lora:q_lora + kv_lora], gkv_ref[...]).astype(BF16)
    v = acc[:, q_lora + kv_lora:]
    kpe_ref[...] = _rope_pairs(v, cos_ref[...], s1_ref[...], s2_ref[...]).astype(BF16)


def _mla_in(h, w_pad, g_q, g_kv, rope_tabs, S, tm=1024):
    T, D = h.shape
    q_lora, kv_lora = g_q.shape[0], g_kv.shape[0]
    n_all = w_pad.shape[1]
    nst = S // tm
    tab_spec = pl.BlockSpec((tm, LANES), lambda i: (i % nst, 0))
    return pl.pallas_call(
        functools.partial(_mla_in_kernel, q_lora=q_lora, kv_lora=kv_lora),
        grid=(T // tm,),
        in_specs=[
            pl.BlockSpec((tm, D), lambda i: (i, 0)),
            pl.BlockSpec((D, n_all), lambda i: (0, 0)),
            pl.BlockSpec((1, q_lora), lambda i: (0, 0)),
            pl.BlockSpec((1, kv_lora), lambda i: (0, 0)),
            tab_spec, tab_spec, tab_spec,
        ],
        out_specs=[
            pl.BlockSpec((tm, q_lora), lambda i: (i, 0)),
            pl.BlockSpec((tm, kv_lora), lambda i: (i, 0)),
            pl.BlockSpec((tm, LANES), lambda i: (i, 0)),
        ],
        out_shape=[
            jax.ShapeDtypeStruct((T, q_lora), BF16),
            jax.ShapeDtypeStruct((T, kv_lora), BF16),
            jax.ShapeDtypeStruct((T, LANES), BF16),
        ],
        scratch_shapes=[pltpu.VMEM((D, n_all), BF16)],
        compiler_params=_cparams(1),
        name="mla_in",
    )(h, w_pad, g_q.reshape(1, -1), g_kv.reshape(1, -1), *rope_tabs)


def _mla_attn_kernel(qn_ref, qp_ref, kv_ref, kpe_ref, cos_ref, s1_ref, s2_ref, o_ref,
                     m_sc, l_sc, acc_sc, *, tq, hps):
    qi = pl.program_id(2)
    head_w = QK_NOPE + V_HEAD
    lane = lax.broadcasted_iota(jnp.int32, (tq, LANES), 1)
    qs = []
    for pair in range(hps // 2):
        qp = _rope_pairs(qp_ref[:, pair * LANES:(pair + 1) * LANES].astype(F32),
                         cos_ref[...], s1_ref[...], s2_ref[...])
        for sub in range(2):
            hh = 2 * pair + sub
            keep = (lane < QK_ROPE) if sub == 0 else (lane >= QK_ROPE)
            qs.append(jnp.concatenate(
                [qn_ref[:, hh * QK_NOPE:(hh + 1) * QK_NOPE],
                 jnp.where(keep, qp, 0.0).astype(BF16)], axis=1))
    m_sc[...] = jnp.full_like(m_sc, NEG)
    l_sc[...] = jnp.zeros_like(l_sc)
    acc_sc[...] = jnp.zeros_like(acc_sc)

    def tile(ks, nk, r0, diag_off):
        kpe = kpe_ref[pl.ds(ks, nk), :]
        for hh in range(hps):
            c0 = hh * head_w
            k = jnp.concatenate([kv_ref[pl.ds(ks, nk), c0:c0 + QK_NOPE], kpe], axis=1)
            s = _dot_nt(qs[hh][r0:], k)
            if diag_off is not None:
                row = lax.broadcasted_iota(jnp.int32, s.shape, 0) + r0
                col = lax.broadcasted_iota(jnp.int32, s.shape, 1) + diag_off
                s = jnp.where(col <= row, s, NEG)
            m_old = m_sc[hh, r0:]
            m_new = jnp.maximum(m_old, jnp.max(s, axis=-1, keepdims=True))
            alpha = jnp.exp2(m_old - m_new)
            p = jnp.exp2(s - jnp.concatenate([m_new] * (nk // LANES), axis=1))
            l_sc[hh, r0:] = alpha * l_sc[hh, r0:] + jnp.sum(p, axis=-1, keepdims=True)
            vv = kv_ref[pl.ds(ks, nk), c0 + QK_NOPE:c0 + head_w]
            acc_sc[hh, r0:] = alpha * acc_sc[hh, r0:] + jnp.dot(p.astype(BF16), vv,
                                                                preferred_element_type=F32)
            m_sc[hh, r0:] = m_new

    def body(j, carry):
        tile(pl.multiple_of(j * tq, tq), tq, 0, None)
        return carry

    lax.fori_loop(0, qi, body, 0)
    half = tq // 2
    kd = pl.multiple_of(qi * tq, tq)
    tile(kd, half, 0, 0)
    tile(pl.multiple_of(kd + half, half), half, half, half)
    for hh in range(hps):
        o_ref[:, hh * V_HEAD:(hh + 1) * V_HEAD] = (acc_sc[hh] / l_sc[hh]).astype(o_ref.dtype)


def _mla_attn(q_all, kv_all, kpe, rope_tabs, B, S, tq=512, hps=8):
    T = q_all.shape[0]
    H = MLA_HEADS
    nqt = S // tq
    qn_w, qp_w, kv_w = hps * QK_NOPE, hps * QK_ROPE, hps * (QK_NOPE + V_HEAD)
    tab_spec = pl.BlockSpec((tq, LANES), lambda b, p, i: (i, 0))
    return pl.pallas_call(
        functools.partial(_mla_attn_kernel, tq=tq, hps=hps),
        grid=(B, H // hps, nqt),
        in_specs=[
            pl.BlockSpec((tq, qn_w), lambda b, p, i: (b * nqt + i, p)),
            pl.BlockSpec((tq, qp_w), lambda b, p, i: (b * nqt + i, (H * QK_NOPE) // qp_w + p)),
            pl.BlockSpec((S, kv_w), lambda b, p, i: (b, p)),
            pl.BlockSpec((S, LANES), lambda b, p, i: (b, 0)),
            tab_spec, tab_spec, tab_spec,
        ],
        out_specs=pl.BlockSpec((tq, hps * V_HEAD), lambda b, p, i: (b * nqt + i, p)),
        out_shape=jax.ShapeDtypeStruct((T, H * V_HEAD), BF16),
        scratch_shapes=[pltpu.VMEM((hps, tq, LANES), F32), pltpu.VMEM((hps, tq, LANES), F32),
                        pltpu.VMEM((hps, tq, V_HEAD), F32)],
        compiler_params=_cparams(3),
        name="mla_attn",
    )(q_all, q_all, kv_all, kpe, *rope_tabs)


def _nsa_compress_kernel(t_ref, pe_ref, w1_ref, w2_ref, o_ref, *, n_chunk):
    half = CMP_BLOCK // 2
    pe = pe_ref[...]

    def chunk_rows(off):
        cols = []
        for l in range(half):
            x = t_ref[pl.ds(l, n_chunk, stride=CMP_STRIDE), :] + pe[off + l:off + l + 1, :]
            cols.append(x.astype(BF16))
        return jnp.concatenate(cols, axis=1)

    w1 = w1_ref[...].astype(BF16)
    kdim = half * NSA_DK
    p0 = jnp.dot(chunk_rows(0), w1[:kdim], preferred_element_type=F32)
    p1 = jnp.dot(chunk_rows(half), w1[kdim:], preferred_element_type=F32)
    pre = p0 + pltpu.roll(p1, n_chunk - 1, 0)
    hid = (pre * jax.nn.sigmoid(pre)).astype(BF16)
    out = jnp.dot(hid, w2_ref[...].astype(BF16), preferred_element_type=F32)
    rown = lax.broadcasted_iota(jnp.int32, out.shape, 0)
    o_ref[...] = jnp.where(rown < n_chunk - 1, out, 0.0).astype(o_ref.dtype)


def _nsa_compress(kc, pe, w1, w2, B, S):
    G = NSA_GROUPS
    n_chunk = S // CMP_STRIDE
    return pl.pallas_call(
        functools.partial(_nsa_compress_kernel, n_chunk=n_chunk),
        grid=(B, 2, G),
        in_specs=[
            pl.BlockSpec((S, NSA_DK), lambda b, i, g: (b, i * G + g)),
            pl.BlockSpec((None, CMP_BLOCK, NSA_DK), lambda b, i, g: (i, 0, 0)),
            pl.BlockSpec((None, CMP_BLOCK * NSA_DK, NSA_DK), lambda b, i, g: (i, 0, 0)),
            pl.BlockSpec((None, NSA_DK, NSA_DK), lambda b, i, g: (i, 0, 0)),
        ],
        out_specs=pl.BlockSpec((None, None, None, n_chunk, NSA_DK), lambda b, i, g: (b, i, g, 0, 0)),
        out_shape=jax.ShapeDtypeStruct((B, 2, G, n_chunk, NSA_DK), BF16),
        compiler_params=_cparams(3),
        name="nsa_compress",
    )(kc, pe, w1, w2)


def _nsa_attn_kernel(q_ref, gt_ref, ks_ref, vs_ref, kw_ref, vw_ref, kc_ref, vc_ref, qtab_ref,
                     ktab_ref, ctab_ref, mt_ref, cbias_ref, wbias_ref, o_ref, m_sc, acc_sc,
                     *, tq, tk, n_sel):
    qi = pl.program_id(2)
    t0 = qi * tq
    HP = NSA_HPG
    R = HP * tq
    qb = q_ref[...]
    qs = jnp.concatenate([qb[:, h * NSA_DK:(h + 1) * NSA_DK] for h in range(HP)], axis=0)
    qtab = qtab_ref[...]
    qa = jnp.concatenate([qs, qtab], axis=1)
    rowi = lax.broadcasted_iota(jnp.int32, (R, 1), 0)
    tcol = (t0 + rowi % tq).astype(F32)

    kc = jnp.concatenate([kc_ref[...], ctab_ref[...]], axis=1)
    s = _dot_nt(qa, kc) + jnp.concatenate([cbias_ref[...]] * HP, axis=0)
    p = jnp.exp2(s - jnp.max(s, axis=-1, keepdims=True))
    l = jnp.sum(p, axis=-1, keepdims=True)
    p_cmp = jnp.where(tcol >= CMP_BLOCK - 1, p / l, 0.0)
    o_cmp = jnp.dot(p_cmp.astype(BF16), vc_ref[...], preferred_element_type=F32)

    ps = p_cmp[0:tq]
    for h in range(1, HP):
        ps = ps + p_cmp[h * tq:(h + 1) * tq]
    ps_hi = ps.astype(BF16)
    ps_lo = (ps - ps_hi.astype(F32)).astype(BF16)
    imp = _dot_nt(mt_ref[...], ps_hi) + _dot_nt(mt_ref[...], ps_lo)
    jrow = lax.broadcasted_iota(jnp.int32, (n_sel, tq), 0)
    blk_t = (t0 + lax.broadcasted_iota(jnp.int32, (n_sel, tq), 1)) // SEL_BLOCK
    forced = (jrow == 0) | (jrow == blk_t) | (jrow == blk_t - 1)
    imp = jnp.where(forced, FORCED_SCORE, imp)
    imp = jnp.where(jrow > blk_t, -1.0, imp)
    rank = jnp.zeros((n_sel, tq), F32)
    for k in range(n_sel):
        rk = imp[k:k + 1, :]
        beats = (rk > imp) | ((rk == imp) & (jrow > k))
        rank = rank + jnp.where(beats, 1.0, 0.0)
    sel_t = jnp.where(rank < SEL_TOPN, 1.0, 0.0)
    sel_t = jnp.concatenate([sel_t, jnp.zeros((LANES - n_sel, tq), F32)], axis=0)
    sel_neg = ((sel_t.T - 1.0) * MASK_BIG).astype(BF16)
    lane = lax.broadcasted_iota(jnp.int32, (R, LANES), 1)
    qx = jnp.where(lane < SEL_LANES, jnp.concatenate([sel_neg] * HP, axis=0), qtab)
    qsel = jnp.concatenate([qs, qx], axis=1)

    m_sc[...] = jnp.full_like(m_sc, NEG)
    acc_sc[...] = jnp.zeros_like(acc_sc)
    ones = jnp.ones((tk, LANES), BF16)

    def tile(j, causal, r0=0):
        ks = pl.multiple_of(j * tk, tk)
        k = jnp.concatenate([ks_ref[pl.ds(ks, tk), :], ktab_ref[pl.ds(ks, tk), :]], axis=1)
        spans = [slice(h * tq + r0, (h + 1) * tq) for h in range(HP)]

        def rows(x):
            return x if r0 == 0 else jnp.concatenate([x[s] for s in spans], axis=0)

        sc = _dot_nt(rows(qsel), k)
        if causal:
            kpos = (ks + lax.broadcasted_iota(jnp.int32, (1, tk), 1)).astype(F32)
            sc = jnp.where(kpos <= rows(tcol), sc, NEG)
        m_old = rows(m_sc[...])
        m_new = jnp.maximum(m_old, jnp.max(sc, axis=-1, keepdims=True))
        alpha = jnp.exp2(m_old - m_new)
        pp = jnp.exp2(sc - jnp.concatenate([m_new] * (tk // LANES), axis=1))
        vv = jnp.concatenate([vs_ref[pl.ds(ks, tk), :], ones], axis=1)
        acc_new = (jnp.concatenate([alpha, alpha], axis=1) * rows(acc_sc[...])
                   + jnp.dot(pp.astype(BF16), vv, preferred_element_type=F32))
        if r0 == 0:
            acc_sc[...] = acc_new
            m_sc[...] = m_new
        else:
            n = tq - r0
            for h, s in enumerate(spans):
                acc_sc[s] = acc_new[h * n:(h + 1) * n]
                m_sc[s] = m_new[h * n:(h + 1) * n]

    def body(j, carry):
        tile(j, False)
        return carry

    n_full = t0 // tk
    lax.fori_loop(0, n_full, body, 0)
    for c in range(tq // tk):
        tile(n_full + c, True, r0=c * tk)
    acc = acc_sc[...]
    o_sel = acc[:, :NSA_DK] / acc[:, NSA_DK:]

    span = WINDOW + WIN_ROWS
    ones_w = jnp.ones((span, LANES), BF16)
    o_blocks = []
    for i in range(tq // WIN_ROWS):
        b0 = t0 + i * WIN_ROWS
        ws = pl.multiple_of(jnp.maximum(b0 - WINDOW, 0), WIN_ROWS)
        qi_rows = jnp.concatenate(
            [qa[h * tq + i * WIN_ROWS:h * tq + (i + 1) * WIN_ROWS] for h in range(HP)], axis=0)
        kw = jnp.concatenate([kw_ref[pl.ds(ws, span), :], ktab_ref[pl.ds(ws, span), :]], axis=1)
        sw = _dot_nt(qi_rows, kw) + jnp.concatenate([wbias_ref[(b0 - ws) // WIN_ROWS]] * HP, axis=0)
        pw = jnp.exp2(sw - jnp.max(sw, axis=-1, keepdims=True))
        vw = jnp.concatenate([vw_ref[pl.ds(ws, span), :], ones_w], axis=1)
        rw = jnp.dot(pw.astype(BF16), vw, preferred_element_type=F32)
        o_blocks.append(rw[:, :NSA_DK] / rw[:, NSA_DK:])
    o_win = jnp.concatenate(
        [o_blocks[i][h * WIN_ROWS:(h + 1) * WIN_ROWS]
         for h in range(HP) for i in range(tq // WIN_ROWS)], axis=0)

    gt = jax.nn.sigmoid(gt_ref[...])

    def gcol(i):
        return jnp.concatenate([gt[:, i * HP + h:i * HP + h + 1] for h in range(HP)], axis=0)

    o = gcol(0) * o_cmp + gcol(1) * o_sel + gcol(2) * o_win
    o_ref[...] = jnp.concatenate([o[h * tq:(h + 1) * tq] for h in range(HP)], axis=1).astype(o_ref.dtype)


def _nsa_attn(q, gates, kvb, kvc, tabs, B, S, tq=128, tk=256):
    T = q.shape[0]
    G, DK, HP = NSA_GROUPS, NSA_DK, NSA_HPG
    qtab, ktab, ctab, mt, cbias, wbias = tabs
    nqt = S // tq
    n_chunk = kvc.shape[3]
    n_sel = S // SEL_BLOCK

    def kv_spec(i):
        return pl.BlockSpec((S, DK), lambda b, g, t, i=i: (b, i * G + g))

    def cmp_spec(i):
        return pl.BlockSpec((None, None, None, n_chunk, DK), lambda b, g, t, i=i: (b, i, g, 0, 0))

    return pl.pallas_call(
        functools.partial(_nsa_attn_kernel, tq=tq, tk=tk, n_sel=n_sel),
        grid=(B, G, nqt),
        in_specs=[
            pl.BlockSpec((tq, HP * DK), lambda b, g, t: (b * nqt + t, g)),
            pl.BlockSpec((tq, LANES), lambda b, g, t: (b * nqt + t, g)),
            kv_spec(0), kv_spec(1), kv_spec(2), kv_spec(3),
            cmp_spec(0), cmp_spec(1),
            pl.BlockSpec((None, HP * tq, LANES), lambda b, g, t: (g, 0, 0)),
            pl.BlockSpec(ktab.shape, lambda b, g, t: (0, 0)),
            pl.BlockSpec(ctab.shape, lambda b, g, t: (0, 0)),
            pl.BlockSpec(mt.shape, lambda b, g, t: (0, 0)),
            pl.BlockSpec((tq, n_chunk), lambda b, g, t: (t, 0)),
            pl.BlockSpec(wbias.shape, lambda b, g, t: (0, 0, 0)),
        ],
        out_specs=pl.BlockSpec((tq, HP * DK), lambda b, g, t: (b * nqt + t, g)),
        out_shape=jax.ShapeDtypeStruct((T, NSA_HEADS * DK), BF16),
        scratch_shapes=[pltpu.VMEM((HP * tq, LANES), F32),
                        pltpu.VMEM((HP * tq, DK + LANES), F32)],
        compiler_params=_cparams(3),
        name="nsa_attn",
    )(q, gates, kvb, kvb, kvb, kvb, kvc, kvc, qtab, ktab, ctab, mt, cbias, wbias)


def _row_copy(src_hbm, dst_vmem, sem, src_row, dst_row):
    return pltpu.make_async_copy(src_hbm.at[pl.ds(src_row, 1), :], dst_vmem.at[pl.ds(dst_row, 1), :], sem)


def _dispatch_kernel(pos_ref, src_ref, init_hbm, out_hbm, sem, *, chunk, n_tok):
    del init_hbm
    base = pl.program_id(0) * chunk

    def start(r, c):
        t = base + r
        _row_copy(src_ref, out_hbm, sem, r, pos_ref[t]).start()
        _row_copy(src_ref, out_hbm, sem, r, pos_ref[n_tok + t]).start()
        return c

    def wait(r, c):
        _row_copy(src_ref, out_hbm, sem, 0, 0).wait()
        return c

    lax.fori_loop(0, chunk, start, 0, unroll=8)
    lax.fori_loop(0, 2 * chunk, wait, 0, unroll=8)


def _dispatch(h_packed, pos, n_rows, chunk=512):
    T, W = h_packed.shape
    n_steps = T // chunk
    init = jnp.zeros((n_rows, W), h_packed.dtype)
    return pl.pallas_call(
        functools.partial(_dispatch_kernel, chunk=chunk, n_tok=T),
        grid_spec=pltpu.PrefetchScalarGridSpec(
            num_scalar_prefetch=1,
            grid=(n_steps,),
            in_specs=[pl.BlockSpec((chunk, W), lambda i, pos: (i, 0)),
                      pl.BlockSpec(memory_space=pl.ANY)],
            out_specs=pl.BlockSpec(memory_space=pl.ANY),
            scratch_shapes=[pltpu.SemaphoreType.DMA(())],
        ),
        out_shape=jax.ShapeDtypeStruct((n_rows, W), h_packed.dtype),
        input_output_aliases={2: 0},
        compiler_params=_cparams(1),
        name="moe_dispatch",
    )(pos, h_packed, init)


def _combine_kernel(pos_ref, y_hbm, x_ref, gate_ref, rt_ref, *refs, tg, n_tok, n_steps,
                    norm_cfgs):
    n_norm_in = sum(_norm_counts(c)[0] for c in norm_cfgs)
    n_norm_out = sum(_norm_counts(c)[1] for c in norm_cfgs)
    norm_in, (o_ref, *refs) = refs[:n_norm_in], refs[n_norm_in:]
    norm_out, (buf, sem) = refs[:n_norm_out], refs[n_norm_out:]
    i = pl.program_id(0)
    slot = i % 2

    def issue(step, slot_):
        def start(r, c):
            t = step * tg + r
            _row_copy(y_hbm, buf.at[slot_, 0], sem.at[slot_], pos_ref[t], r).start()
            _row_copy(y_hbm, buf.at[slot_, 1], sem.at[slot_], pos_ref[n_tok + t], r).start()
            return c

        lax.fori_loop(0, tg, start, 0, unroll=8)

    @pl.when(i == 0)
    def _():
        issue(0, 0)

    @pl.when(i + 1 < n_steps)
    def _():
        issue(i + 1, 1 - slot)

    def wait(r, c):
        _row_copy(y_hbm, buf.at[slot, 0], sem.at[slot], 0, 0).wait()
        return c

    lax.fori_loop(0, 2 * tg, wait, 0, unroll=8)
    rt = rt_ref[...]
    y = rt[:, 2:3] * buf[slot, 0] + rt[:, 3:4] * buf[slot, 1]
    x_new = x_ref[...] + gate_ref[...] * y
    o_ref[...] = x_new
    _apply_norms(x_new, norm_in, norm_out, norm_cfgs)


def _combine(x2, y_sorted, pos, route, gate, S, norms=(), tg=256):
    T, D = x2.shape
    garr, gl, gj = gate
    per_b = S // tg
    n_steps = T // tg
    args = [y_sorted, x2, garr, route]
    in_specs = [
        pl.BlockSpec(memory_space=pl.ANY),
        pl.BlockSpec((tg, D), lambda i, pos: (i, 0)),
        pl.BlockSpec((None, None, None, 1, D), lambda i, pos: (gl, i // per_b, gj, 0, 0)),
        pl.BlockSpec((tg, LANES), lambda i, pos: (i, 0)),
    ]
    out_shape = [jax.ShapeDtypeStruct((T, D), F32)]
    out_specs = [pl.BlockSpec((tg, D), lambda i, pos: (i, 0))]
    for ns in norms:
        n_args, n_in, n_shape, n_out_specs = _norm_io(
            ns, T, D, tg, lambda i, *_: i // per_b, lambda i, *_: i)
        args += n_args
        in_specs += n_in
        out_shape += n_shape
        out_specs += n_out_specs
    outs = pl.pallas_call(
        functools.partial(_combine_kernel, tg=tg, n_tok=T, n_steps=n_steps,
                          norm_cfgs=tuple(_norm_cfg(ns) for ns in norms)),
        grid_spec=pltpu.PrefetchScalarGridSpec(
            num_scalar_prefetch=1,
            grid=(n_steps,),
            in_specs=in_specs,
            out_specs=out_specs,
            scratch_shapes=[pltpu.VMEM((2, 2, tg, D), F32), pltpu.SemaphoreType.DMA((2,))],
        ),
        out_shape=out_shape,
        input_output_aliases={2: 0},
        compiler_params=_cparams(1),
        name="moe_combine",
    )(pos, *args)
    return outs if norms else outs[0]


def _moe_plan(route, T, tm):
    E = N_EXPERTS
    n_tiles = (2 * T) // tm + E
    P = n_tiles * tm
    e_pair = jnp.concatenate([route[:, 0], route[:, 1]]).astype(jnp.int32)
    onehot = (e_pair[:, None] == jnp.arange(E, dtype=jnp.int32)[None, :]).astype(jnp.int32)
    csum = jnp.cumsum(onehot, axis=0)
    rank = jnp.sum(onehot * (csum - 1), axis=1)
    counts = csum[-1]
    padded = ((counts + tm - 1) // tm) * tm
    ends = jnp.cumsum(padded)
    pos = (ends - padded)[e_pair] + rank
    tile_start = jnp.arange(n_tiles, dtype=jnp.int32) * tm
    tile_valid = (tile_start < ends[-1]).astype(jnp.int32)
    tile_expert = jnp.sum((tile_start[:, None] >= ends[None, :]).astype(jnp.int32), axis=1)
    last_valid = tile_expert[jnp.maximum(ends[-1] // tm - 1, 0)]
    tile_expert = jnp.where(tile_valid == 1, tile_expert, last_valid).astype(jnp.int32)
    return pos.astype(jnp.int32), P, tile_expert, tile_valid


def _rope_tables(S):
    d = QK_ROPE
    inv = ROPE_THETA ** (-jnp.arange(0, d, 2, dtype=F32) / d)
    ang = jnp.arange(S).astype(F32)[:, None] * inv[None, :]
    cos, sin = jnp.cos(ang), jnp.sin(ang)
    z = jnp.zeros_like(sin)
    return (jnp.concatenate([cos, cos, cos, cos], axis=1),
            jnp.concatenate([-sin, z, -sin, z], axis=1),
            jnp.concatenate([z, sin, z, sin], axis=1))


def _pos_columns(pos):
    tab = np.zeros((pos.shape[0], LANES), np.float32)
    tab[:, POS_HI_LANE:POS_HI_LANE + 3] = (LANES * (pos // LANES))[:, None]
    tab[:, POS_LO_LANE:POS_LO_LANE + 3] = (pos % LANES)[:, None]
    return tab


def _nsa_tables(S, tq):
    n_cmp = (S - CMP_BLOCK) // CMP_STRIDE + 1
    n_sel = S // SEL_BLOCK
    n_chunk = S // CMP_STRIDE
    tok = np.arange(n_cmp)[:, None] * CMP_STRIDE + np.arange(CMP_BLOCK)[None, :]
    blk = tok // SEL_BLOCK
    m = (blk[:, :, None] == np.arange(n_sel)[None, None, :]).sum(axis=1) / CMP_BLOCK
    mt = np.zeros((n_sel, n_chunk), np.float32)
    mt[:, :n_cmp] = m.T
    keys = np.arange(S)
    ktab = _pos_columns(keys)
    ktab[:, :SEL_LANES] = (keys[:, None] // SEL_BLOCK == np.arange(SEL_LANES)[None, :])
    ctab = _pos_columns(np.arange(n_chunk) * CMP_STRIDE + (CMP_BLOCK - 1))
    slopes = jnp.asarray(2.0 ** (-8.0 * np.arange(1, NSA_HEADS + 1) / NSA_HEADS), F32)
    a = slopes * LOG2E
    a_hi = a.astype(BF16)
    r1 = a - a_hi.astype(F32)
    a_mid = r1.astype(BF16)
    a_lo = (r1 - a_mid.astype(F32)).astype(BF16)
    pieces = jnp.stack([a_hi, a_mid, a_lo], axis=-1)
    qrow = jnp.zeros((NSA_HEADS, LANES), BF16)
    qrow = qrow.at[:, POS_HI_LANE:POS_HI_LANE + 3].set(pieces)
    qrow = qrow.at[:, POS_LO_LANE:POS_LO_LANE + 3].set(pieces)
    qtab = jnp.repeat(qrow.reshape(NSA_GROUPS, NSA_HPG, LANES), tq, axis=1)
    cend = np.arange(n_chunk) * CMP_STRIDE + (CMP_BLOCK - 1)
    cvis = (keys[:, None] >= cend[None, :]) & (np.arange(n_chunk)[None, :] < n_cmp)
    cbias = np.where(cvis, 0.0, NEG).astype(np.float32)
    span = WINDOW + WIN_ROWS
    d = (np.arange(WINDOW // WIN_ROWS + 1)[:, None, None] * WIN_ROWS
         + np.arange(WIN_ROWS)[None, :, None] - np.arange(span)[None, None, :])
    wbias = np.where((d >= 0) & (d < WINDOW), 0.0, NEG).astype(np.float32)
    return (qtab, jnp.asarray(ktab, BF16), jnp.asarray(ctab, BF16), jnp.asarray(mt, BF16),
            jnp.asarray(cbias), jnp.asarray(wbias))


def kernel(x, c, ada_w, ada_b, norm1_g, norm2_g, mla_w_in, mla_g_q, mla_g_kv, mla_w_uq, mla_w_ukv, mla_w_o, kv_ada_w, kv_ada_b, kv_norm_g, nsa_w_kv, cmp_pos_k, cmp_pos_v, cmp_k_w1, cmp_k_w2, cmp_v_w1, cmp_v_w2, nsa_w_in, nsa_w_o, ffn_w_gate, ffn_w_up, ffn_w_down, moe_w_router, moe_b_router, moe_w_gate, moe_w_up, moe_w_down, final_g):
    B, S, D = x.shape
    T = B * S
    depth = ada_w.shape[0]
    n_a = mla_w_in.shape[0]
    H = MLA_HEADS
    G, HP, DK = NSA_GROUPS, NSA_HPG, NSA_DK
    d_ff = ffn_w_gate.shape[-1]

    c_pad = jnp.zeros((8, D), F32).at[:B].set(c)
    mod = _modulation(c_pad, ada_w, ada_b)[:, :B].reshape(depth, B, 6, 1, D)
    kv_mod = _modulation(c_pad, kv_ada_w[None], kv_ada_b[None])[:, :B].reshape(1, B, 2, 1, D)

    rope_tabs = _rope_tables(S)
    nsa_tabs = _nsa_tables(S, NSA_TQ)

    ffn_wg = ffn_w_gate[:, None]
    ffn_wu = ffn_w_up[:, None]
    ffn_wd = ffn_w_down[:, None]

    def norm1_spec(l):
        return _norm_spec(norm1_g[l], (mod, l, 0), (mod, l, 1))

    def swiglu(a, ws, li, tm, **kw):
        wide = (d_ff // WIDE_TN) * WIDE_TN
        tail = d_ff - wide
        main = _gmm(a, ws, li, mode="swiglu", tm=tm, tn=WIDE_TN, n_out=wide, out_dtype=BF16, **kw)
        rest = _gmm(a, ws, li, mode="swiglu", tm=tm, tn=tail, n_out=tail, out_dtype=BF16,
                    n_off=wide // tail, **kw)
        return main, rest

    kv_spec = _norm_spec(kv_norm_g, (kv_mod, 0, 0), (kv_mod, 0, 1))
    final_spec = _norm_spec(final_g, out_dtype=F32)

    x2 = x.reshape(T, D)
    shared = None
    h = _norm(x2, norm1_spec(0), B, S)
    for l in range(depth):
        dense = l % 2 == 0
        if dense:
            norm2 = _norm_spec(norm2_g[l], (mod, l, 3), (mod, l, 4))
        else:
            wr = jnp.pad(moe_w_router[l // 2], ((0, 0), (0, LANES - N_EXPERTS)))
            br = jnp.pad(moe_b_router[l // 2], (0, LANES - N_EXPERTS)).reshape(1, LANES)
            norm2 = _norm_spec(norm2_g[l], (mod, l, 3), (mod, l, 4), router=(wr, br))
        fused2 = [norm2] if dense else []
        if l < n_a:
            w_in = mla_w_in[l]
            q_lora, kv_lora = mla_g_q.shape[1], mla_g_kv.shape[1]
            w_pad = jnp.concatenate([w_in, w_in[:, q_lora + kv_lora:]], axis=1)
            cq, ckv, kpe = _mla_in(h, w_pad, mla_g_q[l], mla_g_kv[l], rope_tabs, S)
            wq = mla_w_uq[l].reshape(q_lora, H, QK_NOPE + QK_ROPE)
            wq = jnp.concatenate([wq[:, :, :QK_NOPE].reshape(q_lora, H * QK_NOPE),
                                  wq[:, :, QK_NOPE:].reshape(q_lora, H * QK_ROPE)], axis=1)
            q_all = _gmm(cq, [wq[None, None]], 0, mode="cast", tm=LORA_TM, tn=WIDE_TN,
                         n_out=wq.shape[1], out_dtype=BF16,
                         out_scale=(QK_NOPE + QK_ROPE) ** -0.5 * LOG2E)
            kv_all = _gmm(ckv, [mla_w_ukv[:, None]], l, mode="cast", tm=LORA_TM, tn=WIDE_TN,
                          n_out=mla_w_ukv.shape[-1], out_dtype=BF16)
            o = _mla_attn(q_all, kv_all, kpe, rope_tabs, B, S)
            x2, *h2 = _gmm(o, [mla_w_o[:, None]], l, mode="residual", tm=OUT_PROJ_TM, tn=D, n_out=D,
                           out_dtype=F32, xres=x2, gate=(mod, l, 2), seq=S, norms=fused2,
                           alias_x=l > 0)
        else:
            jb = l - n_a
            w_in = nsa_w_in[jb]
            q = _gmm(h, [nsa_w_in[:, None]], jb, mode="cast", tm=DENSE_TM, tn=WIDE_TN,
                     n_out=NSA_HEADS * DK, out_dtype=BF16, out_scale=DK ** -0.5 * LOG2E)
            wg = w_in[:, NSA_HEADS * DK:].reshape(D, G, HP, 3).transpose(0, 1, 3, 2)
            wg = jnp.pad(wg.reshape(D, G, 3 * HP), ((0, 0), (0, 0), (0, LANES - 3 * HP)))
            gates = _gmm(h, [wg.reshape(1, 1, D, G * LANES)], 0, mode="cast", tm=DENSE_TM,
                         tn=NARROW_TN,
                         n_out=G * LANES, out_dtype=F32)
            kvb, kvc = shared
            o = _nsa_attn(q, gates, kvb, kvc, nsa_tabs, B, S, tq=NSA_TQ)
            x2, *h2 = _gmm(o, [nsa_w_o[:, None]], jb, mode="residual", tm=OUT_PROJ_TM, tn=D, n_out=D,
                           out_dtype=F32, xres=x2, gate=(mod, l, 2), seq=S, norms=fused2)

        after = [norm1_spec(l + 1) if l + 1 < depth else final_spec]
        if l == n_a - 1:
            after.append(kv_spec)
        if dense:
            hid, hid_tail = swiglu(h2[0], [ffn_wg, ffn_wu], l // 2, DENSE_TM)
            x2 = _gmm(hid, [ffn_wd], l // 2, mode="residual", tm=DENSE_TM, tn=NARROW_TN, n_out=D,
                      out_dtype=F32, xres=x2, gate=(mod, l, 5), seq=S, a2=hid_tail)
            normed = [_norm(x2, ns, B, S) for ns in after]
        else:
            li = l // 2
            hp, route = _norm(x2, norm2, B, S)
            pos, n_rows, tile_expert, tile_valid = _moe_plan(route, T, MOE_TM)
            hs = _dispatch(hp, pos, n_rows)
            hid, hid_tail = swiglu(hs, [moe_w_gate, moe_w_up], li, MOE_TM, tile_expert=tile_expert,
                                   tile_valid=tile_valid, a_packed=True)
            ys = _gmm(hid, [moe_w_down], li, mode="cast", tm=MOE_TM, tn=WIDE_TN, n_out=D,
                      out_dtype=F32, tile_expert=tile_expert, tile_valid=tile_valid, a2=hid_tail)
            x2, *normed = _combine(x2, ys, pos, route, (mod, l, 5), S, norms=after)
        h = normed[0]

        if l == n_a - 1:
            hkv = normed[1]
            w_kv = nsa_w_kv[None, None]
            kc = _gmm(hkv, [w_kv], 0, mode="cast", tm=DENSE_TM, tn=NARROW_TN, n_out=2 * G * DK,
                      out_dtype=F32)
            kvb = _gmm(hkv, [w_kv], 0, mode="cast", tm=DENSE_TM, tn=WIDE_TN, n_out=4 * G * DK,
                       out_dtype=BF16, n_off=(2 * G * DK) // WIDE_TN)
            kvc = _nsa_compress(kc, jnp.stack([cmp_pos_k, cmp_pos_v]),
                                jnp.stack([cmp_k_w1, cmp_v_w1]), jnp.stack([cmp_k_w2, cmp_v_w2]),
                                B, S)
            shared = (kvb, kvc)

    return h.reshape(B, S, D)
```
